```python
import math
import jax, jax.numpy as jnp
from jax import lax
import numpy as np

D_MODEL = 1024
BATCH = 16
SEQ = 2048
DEPTH = 4

GRID_W = 64
CTX_LEN = 256
HEAD_DIM = 64
QBLK = 128
WINDOW = 128
ROPE_THETA = 10000.0
ROPE_PAIRS = HEAD_DIM // 4
EPS = 1e-6
SUBLN_EPS = 1e-5
ATTN_SCALE = HEAD_DIM ** -0.5

A_HEADS = D_MODEL // 128
A_KV = A_HEADS // 4
A_G = A_HEADS // A_KV
A_W = A_HEADS * HEAD_DIM
A_KVW = A_KV * HEAD_DIM
B_HEADS = D_MODEL // 256
B_W = B_HEADS * 2 * HEAD_DIM
C_HEADS = D_MODEL // 128
C_KV = C_HEADS // 4
C_G = C_HEADS // C_KV
C_W = C_HEADS * HEAD_DIM
C_KVW = C_KV * HEAD_DIM

IN_SIZES = (A_W, A_KVW, A_KVW, A_W,
            B_W, B_W, B_W, B_W,
            C_W, C_KVW, C_KVW, C_W)
IN_WIDTH = sum(IN_SIZES)

kernel_name = "hybrid_parallel_gqa_diff_window_dit"


def rmsnorm(x, g, eps=EPS):
    xf = x.astype(jnp.float32)
    y = xf * lax.rsqrt(jnp.mean(xf * xf, axis=-1, keepdims=True) + eps)
    return (y * g.astype(jnp.float32)).astype(x.dtype)


def axial_rope_tables(rows, dtype):
    row = jnp.repeat(jnp.arange(rows), GRID_W).astype(jnp.float32)
    col = jnp.tile(jnp.arange(GRID_W), rows).astype(jnp.float32)
    freqs = ROPE_THETA ** (-jnp.arange(ROPE_PAIRS, dtype=jnp.float32) / ROPE_PAIRS)
    ang_r = row[:, None] * freqs
    ang_c = col[:, None] * freqs
    ang = jnp.concatenate([ang_r, ang_r, ang_c, ang_c], axis=-1)
    return jnp.cos(ang).astype(dtype), jnp.sin(ang).astype(dtype)


def rope(x, cos, sin):
    xs = x.reshape(x.shape[:-1] + (2, 2, ROPE_PAIRS))
    rot = jnp.concatenate([-xs[..., 1:, :], xs[..., :1, :]], axis=-2).reshape(x.shape)
    return x * cos[:, None, :] + rot * sin[:, None, :]


def project(h, w_in, qn_g, kn_g, cos, sin):
    B, T, _ = h.shape
    idx = [int(v) for v in np.cumsum(IN_SIZES)[:-1]]
    qa, ka, va, ga, qb, kb, vb, gb, qc, kc, vc, gc = jnp.split(h @ w_in, idx, axis=-1)
    qa = rmsnorm(qa.reshape(B, T, A_HEADS, HEAD_DIM), qn_g)
    ka = rmsnorm(ka.reshape(B, T, A_KV, HEAD_DIM), kn_g)
    qb = qb.reshape(B, T, 2 * B_HEADS, HEAD_DIM)
    kb = kb.reshape(B, T, 2 * B_HEADS, HEAD_DIM)
    qc = qc.reshape(B, T, C_HEADS, HEAD_DIM)
    kc = kc.reshape(B, T, C_KV, HEAD_DIM)
    if cos is not None:
        qa, ka, qb, kb, qc, kc = [rope(t, cos, sin) for t in (qa, ka, qb, kb, qc, kc)]
    return (qa.reshape(B, T, A_KV, A_G, HEAD_DIM), ka, va.reshape(B, T, A_KV, HEAD_DIM), ga,
            qb.reshape(B, T, B_HEADS, 2, HEAD_DIM), kb.reshape(B, T, B_HEADS, 2, HEAD_DIM),
            vb.reshape(B, T, B_HEADS, 2 * HEAD_DIM), gb,
            qc.reshape(B, T, C_KV, C_G, HEAD_DIM), kc, vc.reshape(B, T, C_KV, HEAD_DIM), gc)


def gqa_attend(q, k, v, sink=None):
    s = jnp.einsum('bqkgd,btkd->bkgqt', q, k).astype(jnp.float32) * ATTN_SCALE
    if sink is not None:
        kvh, g = q.shape[2], q.shape[3]
        col = jnp.broadcast_to(sink.reshape(kvh, g)[None, :, :, None, None].astype(jnp.float32),
                               s.shape[:-1] + (1,))
        p = jax.nn.softmax(jnp.concatenate([s, col], axis=-1), axis=-1)[..., :-1]
    else:
        p = jax.nn.softmax(s, axis=-1)
    return jnp.einsum('bkgqt,btkd->bqkgd', p.astype(v.dtype), v)


def diff_attend(q, k, v, lam):
    s = jnp.einsum('bqhcd,bthcd->bhcqt', q, k).astype(jnp.float32) * ATTN_SCALE
    p = jax.nn.softmax(s, axis=-1)
    a = p[:, :, 0] - lam * p[:, :, 1]
    return jnp.einsum('bhqt,bthe->bqhe', a.astype(v.dtype), v)


def diff_post(o, g, lam_init):
    B, T = o.shape[:2]
    return (rmsnorm(o, g, SUBLN_EPS) * (1.0 - lam_init)).reshape(B, T, B_W)


def query_blocked(fn, q):
    B, S = q.shape[:2]
    nblk = S // QBLK
    qb = jnp.moveaxis(q.reshape((B, nblk, QBLK) + q.shape[2:]), 1, 0)
    ob = lax.map(fn, qb)
    return jnp.moveaxis(ob, 0, 1).reshape((B, S) + ob.shape[3:])


def window_attend(q, k, v, kc, vc, sink):
    B, S = q.shape[:2]
    nblk = S // QBLK
    ncx = kc.shape[1]

    def band(t):
        tp = jnp.pad(t, ((0, 0), (QBLK, QBLK), (0, 0), (0, 0))).reshape((B, nblk + 2, QBLK) + t.shape[2:])
        return jnp.moveaxis(jnp.concatenate([tp[:, :-2], tp[:, 1:-1], tp[:, 2:]], axis=2), 1, 0)

    qpos = jnp.arange(S).reshape(nblk, QBLK)
    kpos = (jnp.arange(nblk)[:, None] - 1) * QBLK + jnp.arange(3 * QBLK)[None, :]
    mask = ((jnp.abs(qpos[:, :, None] - kpos[:, None, :]) <= WINDOW)
            & (kpos >= 0)[:, None, :] & (kpos < S)[:, None, :])
    qb = jnp.moveaxis(q.reshape((B, nblk, QBLK) + q.shape[2:]), 1, 0)
    sink_l = sink.reshape(C_KV, C_G).astype(jnp.float32)

    def f(args):
        qblk, kblk, vblk, mblk = args
        s_loc = jnp.einsum('bqkgd,btkd->bkgqt', qblk, kblk).astype(jnp.float32) * ATTN_SCALE
        s_loc = jnp.where(mblk[None, None, None], s_loc, -jnp.inf)
        s_ctx = jnp.einsum('bqkgd,btkd->bkgqt', qblk, kc).astype(jnp.float32) * ATTN_SCALE
        col = jnp.broadcast_to(sink_l[None, :, :, None, None], s_ctx.shape[:-1] + (1,))
        p = jax.nn.softmax(jnp.concatenate([s_ctx, s_loc, col], axis=-1), axis=-1)
        p_ctx = p[..., :ncx].astype(v.dtype)
        p_loc = p[..., ncx:-1].astype(v.dtype)
        return (jnp.einsum('bkgqt,btkd->bqkgd', p_ctx, vc)
                + jnp.einsum('bkgqt,btkd->bqkgd', p_loc, vblk))

    ob = lax.map(f, (qb, band(k), band(v), mask))
    return jnp.moveaxis(ob, 0, 1).reshape((B, S) + ob.shape[3:])


def merge_out(h, oa, ob, oc, ga, gb, gc, w_br_a, w_br_b, w_br_c, w_mg, b_mg, w_out):
    B, T, _ = h.shape
    pa = (oa.reshape(B, T, A_W) * jax.nn.silu(ga)) @ w_br_a
    pb = (ob * jax.nn.silu(gb)) @ w_br_b
    pc = (oc.reshape(B, T, C_W) * jax.nn.silu(gc)) @ w_br_c
    g_a, g_b, g_c = jnp.split(jax.nn.sigmoid(h @ w_mg + b_mg), 3, axis=-1)
    return (g_a * pa + g_b * pb + g_c * pc) @ w_out


def setup_inputs(seed: int = 0) -> dict:
    key = jax.random.key(seed)
    ks = jax.random.split(key, 24)
    f32 = jnp.float32
    n = lambda k, shape, s: jax.random.normal(k, shape, f32) * s
    D = D_MODEL
    return {
        "x": n(ks[0], (BATCH, SEQ, D), 1.0),
        "c": n(ks[1], (BATCH, D), 1.0),
        "ctx": n(ks[2], (BATCH, CTX_LEN, D), 1.0),
        "c_ctx": n(ks[3], (D,), 1.0),
        "w_ada": n(ks[4], (DEPTH, D, 3 * D), 0.5 * D ** -0.5),
        "b_ada": n(ks[5], (DEPTH, 3 * D), 0.02),
        "g_pre": 1.0 + n(ks[6], (DEPTH, D), 0.02),
        "g_post": 1.0 + n(ks[7], (DEPTH, D), 0.02),
        "w_in": n(ks[8], (DEPTH, D, IN_WIDTH), D ** -0.5),
        "q_norm": 1.0 + n(ks[9], (DEPTH, HEAD_DIM), 0.02),
        "k_norm": 1.0 + n(ks[10], (DEPTH, HEAD_DIM), 0.02),
        "lam_q1": n(ks[11], (DEPTH, HEAD_DIM), 0.1),
        "lam_k1": n(ks[12], (DEPTH, HEAD_DIM), 0.1),
        "lam_q2": n(ks[13], (DEPTH, HEAD_DIM), 0.1),
        "lam_k2": n(ks[14], (DEPTH, HEAD_DIM), 0.1),
        "subln": 1.0 + n(ks[15], (DEPTH, 2 * HEAD_DIM), 0.02),
        "sink": n(ks[16], (DEPTH, C_HEADS), 0.5),
        "w_br_a": n(ks[17], (DEPTH, A_W, D), A_W ** -0.5),
        "w_br_b": n(ks[18], (DEPTH, B_W, D), B_W ** -0.5),
        "w_br_c": n(ks[19], (DEPTH, C_W, D), C_W ** -0.5),
        "w_mg": n(ks[20], (DEPTH, D, 3 * D), D ** -0.5),
        "b_mg": n(ks[21], (DEPTH, 3 * D), 0.1),
        "w_out": n(ks[22], (DEPTH, D, D), D ** -0.5),
    }


def reference(x, c, ctx, c_ctx, w_ada, b_ada, g_pre, g_post, w_in, q_norm, k_norm,
              lam_q1, lam_k1, lam_q2, lam_k2, subln, sink, w_br_a, w_br_b, w_br_c,
              w_mg, b_mg, w_out):
    S = x.shape[1]
    ROWS = S // GRID_W
    cos, sin = axial_rope_tables(ROWS, x.dtype)
    sc = jax.nn.silu(c)
    scc = jax.nn.silu(c_ctx)
    cx = ctx
    for l in range(DEPTH):
        last = l == DEPTH - 1
        shift, scale, gate = jnp.split(sc @ w_ada[l] + b_ada[l], 3, axis=-1)
        shift_c, scale_c, gate_c = jnp.split(scc @ w_ada[l] + b_ada[l], 3, axis=-1)
        h = rmsnorm(x, g_pre[l]) * (1.0 + scale[:, None]) + shift[:, None]
        hc = rmsnorm(cx, g_pre[l]) * (1.0 + scale_c) + shift_c
        (qa, ka, va, ga, qb, kb, vb, gb, qc, kc, vc, gc) = project(h, w_in[l], q_norm[l], k_norm[l], cos, sin)
        (cqa, cka, cva, cga, cqb, ckb, cvb, cgb, cqc, ckc, cvc, cgc) = project(hc, w_in[l], q_norm[l], k_norm[l], None, None)
        lam_init = 0.8 - 0.6 * math.exp(-0.3 * l)
        lam = (jnp.exp(jnp.sum(lam_q1[l].astype(jnp.float32) * lam_k1[l].astype(jnp.float32)))
               - jnp.exp(jnp.sum(lam_q2[l].astype(jnp.float32) * lam_k2[l].astype(jnp.float32)))
               + lam_init)
        ka_all = jnp.concatenate([cka, ka], axis=1)
        va_all = jnp.concatenate([cva, va], axis=1)
        kb_all = jnp.concatenate([ckb, kb], axis=1)
        vb_all = jnp.concatenate([cvb, vb], axis=1)
        oa = query_blocked(lambda qblk: gqa_attend(qblk, ka_all, va_all), qa)
        ob = diff_post(query_blocked(lambda qblk: diff_attend(qblk, kb_all, vb_all, lam), qb), subln[l], lam_init)
        oc = window_attend(qc, kc, vc, ckc, cvc, sink[l])
        y = merge_out(h, oa, ob, oc, ga, gb, gc, w_br_a[l], w_br_b[l], w_br_c[l], w_mg[l], b_mg[l], w_out[l])
        if not last:
            coa = gqa_attend(cqa, cka, cva)
            cob = diff_post(diff_attend(cqb, ckb, cvb, lam), subln[l], lam_init)
            coc = gqa_attend(cqc, ckc, cvc, sink[l])
            yc = merge_out(hc, coa, cob, coc, cga, cgb, cgc, w_br_a[l], w_br_b[l], w_br_c[l], w_mg[l], b_mg[l], w_out[l])
            cx = cx + gate_c * rmsnorm(yc, g_post[l])
        x = x + gate[:, None] * rmsnorm(y, g_post[l])
    return x
```

```python
import functools
import math

import jax
import jax.numpy as jnp
import numpy as np
from jax import lax
from jax.experimental import pallas as pl
from jax.experimental.pallas import tpu as pltpu

F32 = jnp.float32
BF16 = jnp.bfloat16

D = 1024
HD = 64
GRID_W = 64
WINDOW = 128
ROPE_THETA = 10000.0
ROPE_PAIRS = HD // 4
EPS = 1e-6
SUBLN_EPS = 1e-5
ATTN_SCALE = HD ** -0.5

QW = 512
N_Q = 3 * QW
N_K = 128 + 512 + 128
N_V = 128 + 512 + 128
N_P = N_Q + N_K + N_V
N_G = 3 * QW
N_M = 3 * D

LANES = 128
VMEM_LIMIT = 56 * 1024 * 1024

_IN = dict(qa=(0, 512), ka=(512, 640), va=(640, 768), ga=(768, 1280),
           qb=(1280, 1792), kb=(1792, 2304), vb=(2304, 2816), gb=(2816, 3328),
           qc=(3328, 3840), kc=(3840, 3968), vc=(3968, 4096), gc=(4096, 4608))


def _sigmoid(v):
    return 1.0 / (1.0 + jnp.exp(-v))


def _cparams(n_axes):
    return pltpu.CompilerParams(dimension_semantics=("arbitrary",) * n_axes,
                                vmem_limit_bytes=VMEM_LIMIT)


def _ada_kernel(sc_ref, w_ref, b_ref, o_ref):
    v = sc_ref[...]
    s = (v * _sigmoid(v)).astype(BF16)
    o_ref[...] = jnp.dot(s, w_ref[...].astype(BF16), preferred_element_type=F32) + b_ref[...]


def _ada_call(sc_in, w_ada, b_ada):
    depth = w_ada.shape[0]
    rows = sc_in.shape[0]
    nblk = 3
    return pl.pallas_call(
        _ada_kernel,
        out_shape=jax.ShapeDtypeStruct((depth, rows, 3 * D), F32),
        grid=(depth, nblk),
        in_specs=[
            pl.BlockSpec((rows, D), lambda l, n: (0, 0)),
            pl.BlockSpec((None, D, D), lambda l, n: (l, 0, n)),
            pl.BlockSpec((None, 1, D), lambda l, n: (l, 0, n)),
        ],
        out_specs=pl.BlockSpec((None, rows, D), lambda l, n: (l, 0, n)),
        compiler_params=_cparams(2),
        name="adaln",
    )(sc_in, w_ada, b_ada.reshape(depth, 1, 3 * D))


def _modulated_norm(x, mod, gpre):
    shift = mod[:, :D]
    scale = mod[:, D:2 * D]
    ms = jnp.mean(x * x, axis=-1, keepdims=True)
    return x * lax.rsqrt(ms + EPS) * gpre * (1.0 + scale) + shift


def _proj_kernel(x_ref, mod_ref, gpre_ref, w_ref, cost_ref, sint_ref, cosn_ref, sina_ref, sinb_ref,
                 gq_ref, gk_ref, qt_ref, k_ref, vt_ref):
    tm = x_ref.shape[0]
    h = _modulated_norm(x_ref[...], mod_ref[...], gpre_ref[...])
    y = jnp.dot(h.astype(BF16), w_ref[...], preferred_element_type=F32)

    q3 = y[:, :N_Q].T.reshape(N_Q // HD, HD, tm)
    qa = q3[:8]
    ss = jnp.sum(qa * qa, axis=1, keepdims=True)
    qa = qa * lax.rsqrt(ss * (1.0 / HD) + EPS) * gq_ref[...][None]
    q3 = jnp.concatenate([qa, q3[8:]], axis=0)
    rot = jnp.concatenate([q3[:, 16:32], q3[:, 0:16], q3[:, 48:64], q3[:, 32:48]], axis=1)
    q3 = (q3 * cost_ref[...][None] + rot * sint_ref[...][None]) * ATTN_SCALE
    qt_ref[...] = q3.reshape(N_Q, tm).astype(BF16)

    ka = y[:, N_Q:N_Q + LANES]
    lane = lax.broadcasted_iota(jnp.int32, (1, LANES), 1)
    lo = lane < HD
    sq = ka * ka
    s_lo = jnp.sum(jnp.where(lo, sq, 0.0), axis=-1, keepdims=True)
    s_hi = jnp.sum(jnp.where(lo, 0.0, sq), axis=-1, keepdims=True)
    r = jnp.where(lo, lax.rsqrt(s_lo * (1.0 / HD) + EPS), lax.rsqrt(s_hi * (1.0 / HD) + EPS))
    ka = ka * r * gk_ref[...]
    cosn = cosn_ref[...]
    sina = sina_ref[...]
    sinb = sinb_ref[...]
    for i in range(N_K // LANES):
        t = ka if i == 0 else y[:, N_Q + i * LANES:N_Q + (i + 1) * LANES]
        t = t * cosn + pltpu.roll(t, LANES - 16, 1) * sina + pltpu.roll(t, 16, 1) * sinb
        k_ref[:, i * LANES:(i + 1) * LANES] = t.astype(BF16)

    vt_ref[...] = y[:, N_Q + N_K:].T.astype(BF16)


def _proj_call(xs, mod4, g_pre, wp, tabs, gq_t, gk_n, layer, n_lat_tiles, tm):
    nb, t_all, _ = xs.shape
    cost, sint, cosn, sina, sinb = tabs
    n_ctx_row = mod4.shape[1] - 1
    del n_ctx_row

    def mod_idx(b, t):
        return (layer, jnp.where(t < n_lat_tiles, b, nb), 0, 0)

    return pl.pallas_call(
        _proj_kernel,
        out_shape=(jax.ShapeDtypeStruct((nb, N_Q, t_all), BF16),
                   jax.ShapeDtypeStruct((nb, t_all, N_K), BF16),
                   jax.ShapeDtypeStruct((nb, N_V, t_all), BF16)),
        grid=(nb, t_all // tm),
        in_specs=[
            pl.BlockSpec((None, tm, D), lambda b, t: (b, t, 0)),
            pl.BlockSpec((None, None, 1, 3 * D), mod_idx),
            pl.BlockSpec((None, 1, D), lambda b, t: (layer, 0, 0)),
            pl.BlockSpec((None, D, N_P), lambda b, t: (layer, 0, 0)),
            pl.BlockSpec((HD, tm), lambda b, t: (0, t)),
            pl.BlockSpec((HD, tm), lambda b, t: (0, t)),
            pl.BlockSpec((tm, LANES), lambda b, t: (t, 0)),
            pl.BlockSpec((tm, LANES), lambda b, t: (t, 0)),
            pl.BlockSpec((tm, LANES), lambda b, t: (t, 0)),
            pl.BlockSpec((None, HD, tm), lambda b, t: (layer, 0, 0)),
            pl.BlockSpec((None, 1, LANES), lambda b, t: (layer, 0, 0)),
        ],
        out_specs=(pl.BlockSpec((None, N_Q, tm), lambda b, t: (b, 0, t)),
                   pl.BlockSpec((None, tm, N_K), lambda b, t: (b, t, 0)),
                   pl.BlockSpec((None, N_V, tm), lambda b, t: (b, 0, t))),
        compiler_params=_cparams(2),
        name="proj",
    )(xs, mod4, g_pre, wp, cost, sint, cosn, sina, sinb, gq_t, gk_n)


def _attend(k, rhs, vt, extra=None):
    s = jnp.dot(k, rhs, preferred_element_type=F32)
    m = jnp.max(s, axis=0, keepdims=True)
    if extra is not None:
        m = jnp.maximum(m, extra)
    p = jnp.exp(s - m)
    l = jnp.sum(p, axis=0, keepdims=True)
    if extra is not None:
        l = l + jnp.exp(extra - m)
    o = jnp.dot(vt, p.astype(BF16), preferred_element_type=F32)
    return o, l


def _gqa_rhs(qt, j, tq):
    z = jnp.zeros((HD, tq), qt.dtype)
    first = j == 0
    cols = []
    for g in range(4):
        qg = qt[g * HD:(g + 1) * HD, :]
        cols.append(jnp.concatenate([jnp.where(first, qg, z), jnp.where(first, z, qg)], axis=0))
    return jnp.concatenate(cols, axis=1)


def _store_heads(o_ref, ot, l, tq, nheads):
    ot = ot * (1.0 / l)
    o = jnp.concatenate([ot[:, g * tq:(g + 1) * tq] for g in range(nheads)], axis=0)
    o_ref[...] = o.T.astype(o_ref.dtype)


def _sink_row(sink_ref, base, tq):
    blk = lax.broadcasted_iota(jnp.int32, (1, 4 * tq), 1) // tq
    row = jnp.zeros((1, 4 * tq), F32)
    for g in range(4):
        row = jnp.where(blk == g, sink_ref[base + g], row)
    return row


def _attn_a_kernel(qt_ref, k_ref, vt_ref, o_ref, *, tq, n_lat, s_len):
    j = pl.program_id(1)
    qi = pl.program_id(2)
    rhs = _gqa_rhs(qt_ref[...], j, tq)

    @pl.when(qi < n_lat)
    def _():
        o, l = _attend(k_ref[...], rhs, vt_ref[...])
        _store_heads(o_ref, o, l, tq, 4)

    @pl.when(qi >= n_lat)
    def _():
        o, l = _attend(k_ref[s_len:, :], rhs, vt_ref[:, s_len:])
        _store_heads(o_ref, o, l, tq, 4)


def _attn_c_kernel(sink_ref, qt_ref, k_ref, vt_ref, o_ref, *, tq, n_lat, s_len):
    j = pl.program_id(1)
    qi = pl.program_id(2)
    rhs = _gqa_rhs(qt_ref[...], j, tq)
    n = 4 * tq
    wk = tq + 2 * WINDOW
    srow = _sink_row(sink_ref, j * 4, tq)

    @pl.when(qi < n_lat)
    def _():
        q0 = qi * tq
        start = pl.multiple_of(jnp.clip(q0 - WINDOW, 0, s_len - wk), LANES)
        k_loc = k_ref[pl.ds(start, wk), :]
        vt_loc = vt_ref[:, pl.ds(start, wk)]
        k_ctx = k_ref[s_len:, :]
        vt_ctx = vt_ref[:, s_len:]
        s_loc = jnp.dot(k_loc, rhs, preferred_element_type=F32)
        s_ctx = jnp.dot(k_ctx, rhs, preferred_element_type=F32)
        kpos = lax.broadcasted_iota(jnp.int32, (wk, n), 0)
        qpos = lax.broadcasted_iota(jnp.int32, (wk, n), 1) & (tq - 1)
        dist = jnp.abs(qpos - kpos + (q0 - start))
        s_loc = jnp.where(dist <= WINDOW, s_loc, -jnp.inf)
        m = jnp.maximum(jnp.maximum(jnp.max(s_loc, axis=0, keepdims=True),
                                    jnp.max(s_ctx, axis=0, keepdims=True)), srow)
        p_loc = jnp.exp(s_loc - m)
        p_ctx = jnp.exp(s_ctx - m)
        l = (jnp.sum(p_loc, axis=0, keepdims=True) + jnp.sum(p_ctx, axis=0, keepdims=True)
             + jnp.exp(srow - m))
        o = (jnp.dot(vt_loc, p_loc.astype(BF16), preferred_element_type=F32)
             + jnp.dot(vt_ctx, p_ctx.astype(BF16), preferred_element_type=F32))
        _store_heads(o_ref, o, l, tq, 4)

    @pl.when(qi >= n_lat)
    def _():
        o, l = _attend(k_ref[s_len:, :], rhs, vt_ref[:, s_len:], extra=srow)
        _store_heads(o_ref, o, l, tq, 4)


def _attn_b_kernel(lamv_ref, subln_ref, qt_ref, k_ref, vt_ref, o_ref, *, tq, n_lat, s_len, lam_init):
    qi = pl.program_id(2)
    qt = qt_ref[...]
    z = jnp.zeros((HD, tq), qt.dtype)
    rhs = jnp.concatenate([jnp.concatenate([qt[:HD], z], axis=0),
                           jnp.concatenate([z, qt[HD:]], axis=0)], axis=1)
    lv = lamv_ref[...]
    lam = (jnp.exp(jnp.sum(lv[0:1] * lv[1:2], axis=-1, keepdims=True))
           - jnp.exp(jnp.sum(lv[2:3] * lv[3:4], axis=-1, keepdims=True)) + lam_init)

    def finish(o2, l):
        o2 = o2 * (1.0 / l)
        o = o2[:, :tq] - lam * o2[:, tq:]
        ms = jnp.mean(o * o, axis=0, keepdims=True)
        o = o * lax.rsqrt(ms + SUBLN_EPS) * subln_ref[...] * (1.0 - lam_init)
        o_ref[...] = o.T.astype(o_ref.dtype)

    @pl.when(qi < n_lat)
    def _():
        finish(*_attend(k_ref[...], rhs, vt_ref[...]))

    @pl.when(qi >= n_lat)
    def _():
        finish(*_attend(k_ref[s_len:, :], rhs, vt_ref[:, s_len:]))


def _attn_calls(qt_all, k_all, vt_all, sink, lamv, subln_t, layer, with_ctx, s_len, lam_init,
                tq_a, tq_b, tq_c):
    nb, _, t_all = qt_all.shape
    c_len = t_all - s_len
    o_shape = jax.ShapeDtypeStruct((nb, t_all, QW), BF16)

    def steps(tq):
        n_lat = s_len // tq
        return n_lat, n_lat + (c_len // tq if with_ctx else 0)

    n_lat, n_all = steps(tq_a)
    oa = pl.pallas_call(
        functools.partial(_attn_a_kernel, tq=tq_a, n_lat=n_lat, s_len=s_len),
        out_shape=o_shape,
        grid=(nb, 2, n_all),
        in_specs=[
            pl.BlockSpec((None, 4 * HD, tq_a), lambda b, j, q: (b, j, q)),
            pl.BlockSpec((None, t_all, LANES), lambda b, j, q: (b, 0, 0)),
            pl.BlockSpec((None, HD, t_all), lambda b, j, q: (b, j, 0)),
        ],
        out_specs=pl.BlockSpec((None, tq_a, 4 * HD), lambda b, j, q: (b, q, j)),
        compiler_params=_cparams(3),
        name="attn_a",
    )(qt_all, k_all, vt_all)

    n_lat, n_all = steps(tq_b)
    ob = pl.pallas_call(
        functools.partial(_attn_b_kernel, tq=tq_b, n_lat=n_lat, s_len=s_len, lam_init=lam_init),
        out_shape=o_shape,
        grid=(nb, 4, n_all),
        in_specs=[
            pl.BlockSpec((None, 4, HD), lambda b, h, q: (layer, 0, 0)),
            pl.BlockSpec((None, 2 * HD, tq_b), lambda b, h, q: (layer, 0, 0)),
            pl.BlockSpec((None, 2 * HD, tq_b), lambda b, h, q: (b, 4 + h, q)),
            pl.BlockSpec((None, t_all, LANES), lambda b, h, q: (b, 0, 1 + h)),
            pl.BlockSpec((None, 2 * HD, t_all), lambda b, h, q: (b, 1 + h, 0)),
        ],
        out_specs=pl.BlockSpec((None, tq_b, 2 * HD), lambda b, h, q: (b, q, h)),
        compiler_params=_cparams(3),
        name="attn_b",
    )(lamv, subln_t, qt_all, k_all, vt_all)

    n_lat, n_all = steps(tq_c)
    oc = pl.pallas_call(
        functools.partial(_attn_c_kernel, tq=tq_c, n_lat=n_lat, s_len=s_len),
        out_shape=o_shape,
        grid=(nb, 2, n_all),
        in_specs=[
            pl.BlockSpec(memory_space=pltpu.SMEM),
            pl.BlockSpec((None, 4 * HD, tq_c), lambda b, j, q: (b, 4 + j, q)),
            pl.BlockSpec((None, t_all, LANES), lambda b, j, q: (b, 0, 5)),
            pl.BlockSpec((None, HD, t_all), lambda b, j, q: (b, 10 + j, 0)),
        ],
        out_specs=pl.BlockSpec((None, tq_c, 4 * HD), lambda b, j, q: (b, q, j)),
        compiler_params=_cparams(3),
        name="attn_c",
    )(sink[layer], qt_all, k_all, vt_all)
    return oa, ob, oc


def _merge_kernel(x_ref, oa_ref, ob_ref, oc_ref, mod_ref, gpre_ref, gpost_ref, wgm_ref, bmg_ref,
                  wbr_ref, wout_ref, out_ref):
    x = x_ref[...]
    mod = mod_ref[...]
    h = _modulated_norm(x, mod, gpre_ref[...])
    gm = jnp.dot(h.astype(BF16), wgm_ref[...], preferred_element_type=F32)
    z = None
    for i, o_ref in enumerate((oa_ref, ob_ref, oc_ref)):
        g = gm[:, i * QW:(i + 1) * QW]
        u = (o_ref[...].astype(F32) * (g * _sigmoid(g))).astype(BF16)
        p = jnp.dot(u, wbr_ref[i], preferred_element_type=F32)
        mg = _sigmoid(gm[:, N_G + i * D:N_G + (i + 1) * D] + bmg_ref[:, i * D:(i + 1) * D])
        z = mg * p if z is None else z + mg * p
    y = jnp.dot(z.astype(BF16), wout_ref[...], preferred_element_type=F32)
    ms = jnp.mean(y * y, axis=-1, keepdims=True)
    gate = mod[:, 2 * D:]
    out_ref[...] = x + gate * (y * lax.rsqrt(ms + EPS) * gpost_ref[...])


def _merge_call(xs, oa, ob, oc, mod4, g_pre, g_post, wgm, b_mg, wbr, wout, layer, n_lat_tiles,
                with_ctx, tm):
    nb, t_all, _ = xs.shape
    n_tiles = t_all // tm if with_ctx else n_lat_tiles

    def mod_idx(b, t):
        return (layer, jnp.where(t < n_lat_tiles, b, nb), 0, 0)

    tok = lambda b, t: (b, t, 0)
    lay2 = lambda b, t: (layer, 0, 0)
    return pl.pallas_call(
        _merge_kernel,
        out_shape=jax.ShapeDtypeStruct((nb, n_tiles * tm, D), F32),
        grid=(nb, n_tiles),
        in_specs=[
            pl.BlockSpec((None, tm, D), tok),
            pl.BlockSpec((None, tm, QW), tok),
            pl.BlockSpec((None, tm, QW), tok),
            pl.BlockSpec((None, tm, QW), tok),
            pl.BlockSpec((None, None, 1, 3 * D), mod_idx),
            pl.BlockSpec((None, 1, D), lay2),
            pl.BlockSpec((None, 1, D), lay2),
            pl.BlockSpec((None, D, N_G + N_M), lay2),
            pl.BlockSpec((None, 1, N_M), lay2),
            pl.BlockSpec((None, 3, QW, D), lambda b, t: (layer, 0, 0, 0)),
            pl.BlockSpec((None, D, D), lay2),
        ],
        out_specs=pl.BlockSpec((None, tm, D), tok),
        compiler_params=_cparams(2),
        name="merge",
    )(xs, oa, ob, oc, mod4, g_pre, g_post, wgm, b_mg, wbr, wout)


def _rope_tables(s_len, c_len):
    rows = s_len // GRID_W
    row = jnp.repeat(jnp.arange(rows), GRID_W).astype(F32)
    col = jnp.tile(jnp.arange(GRID_W), rows).astype(F32)
    freqs = ROPE_THETA ** (-jnp.arange(ROPE_PAIRS, dtype=F32) / ROPE_PAIRS)
    ang_r = row[:, None] * freqs
    ang_c = col[:, None] * freqs
    ang = jnp.concatenate([ang_r, ang_r, ang_c, ang_c], axis=-1)
    cos = jnp.concatenate([jnp.cos(ang), jnp.ones((c_len, HD), F32)], axis=0)
    sin = jnp.concatenate([jnp.sin(ang), jnp.zeros((c_len, HD), F32)], axis=0)
    first = (np.arange(HD) % 32) < 16
    sin_a = jnp.where(first, -sin, 0.0)
    sin_b = jnp.where(first, 0.0, sin)
    tile2 = lambda a: jnp.concatenate([a, a], axis=-1)
    cost = cos.T
    sint = (sin_a + sin_b).T
    return cost, sint, tile2(cos), tile2(sin_a), tile2(sin_b)


def _cols(w, names):
    return jnp.concatenate([w[..., _IN[n][0]:_IN[n][1]] for n in names], axis=-1)


def kernel(x, c, ctx, c_ctx, w_ada, b_ada, g_pre, g_post, w_in, q_norm, k_norm, lam_q1, lam_k1, lam_q2,
           lam_k2, subln, sink, w_br_a, w_br_b, w_br_c, w_mg, b_mg, w_out):
    nb, s_len, _ = x.shape
    c_len = ctx.shape[1]
    depth = w_in.shape[0]
    tm = 256
    tq_a, tq_b, tq_c = 128, 256, 256
    n_lat_tiles = s_len // tm

    wp = _cols(w_in, ("qa", "qb", "qc", "ka", "kb", "kc", "va", "vb", "vc")).astype(BF16)
    wgm = jnp.concatenate([_cols(w_in, ("ga", "gb", "gc")), w_mg], axis=-1).astype(BF16)
    wbr = jnp.stack([w_br_a, w_br_b, w_br_c], axis=1).astype(BF16)
    wout = w_out.astype(BF16)

    tabs = _rope_tables(s_len, c_len)
    gq_t = jnp.broadcast_to(q_norm[:, :, None], (depth, HD, tm))
    gk_n = jnp.concatenate([k_norm, k_norm], axis=-1)[:, None, :]
    lamv = jnp.stack([lam_q1, lam_k1, lam_q2, lam_k2], axis=1)
    subln_t = jnp.broadcast_to(subln[:, :, None], (depth, 2 * HD, tq_b))
    g_pre3 = g_pre[:, None, :]
    g_post3 = g_post[:, None, :]
    b_mg3 = b_mg[:, None, :]

    rows = ((nb + 1 + 7) // 8) * 8
    sc_in = jnp.concatenate([c, c_ctx[None, :], jnp.zeros((rows - nb - 1, D), F32)], axis=0)
    mod = _ada_call(sc_in, w_ada, b_ada)
    mod4 = mod[:, :, None, :]

    xs = jnp.concatenate([x, ctx], axis=1)
    for layer in range(depth):
        last = layer == depth - 1
        lam_init = 0.8 - 0.6 * math.exp(-0.3 * layer)
        qt_all, k_all, vt_all = _proj_call(xs, mod4, g_pre3, wp, tabs, gq_t, gk_n, layer, n_lat_tiles, tm)
        oa, ob, oc = _attn_calls(qt_all, k_all, vt_all, sink, lamv, subln_t, layer, not last, s_len,
                                 lam_init, tq_a, tq_b, tq_c)
        xs = _merge_call(xs, oa, ob, oc, mod4, g_pre3, g_post3, wgm, b_mg3, wbr, wout, layer,
                         n_lat_tiles, not last, tm)
    return xs
```

```python
import functools
import math

import jax
import jax.numpy as jnp
import numpy as np
from jax import lax
from jax.experimental import pallas as pl
from jax.experimental.pallas import tpu as pltpu

F32 = jnp.float32
BF16 = jnp.bfloat16

D = 1024
HD = 64
GRID_W = 64
WINDOW = 128
ROPE_THETA = 10000.0
ROPE_PAIRS = HD // 4
EPS = 1e-6
SUBLN_EPS = 1e-5
ATTN_SCALE = HD ** -0.5
LOG2E = math.log2(math.e)

QW = 512
N_Q = 3 * QW
N_K = 128 + 512 + 128
N_V = 128 + 512 + 128
N_P = N_Q + N_K + N_V
N_G = 3 * QW
N_M = 3 * D

LANES = 128
VMEM_LIMIT = 56 * 1024 * 1024

_IN = dict(qa=(0, 512), ka=(512, 640), va=(640, 768), ga=(768, 1280),
           qb=(1280, 1792), kb=(1792, 2304), vb=(2304, 2816), gb=(2816, 3328),
           qc=(3328, 3840), kc=(3840, 3968), vc=(3968, 4096), gc=(4096, 4608))


def _sigmoid(v):
    return 1.0 / (1.0 + jnp.exp(-v))


def _cparams(n_axes):
    return pltpu.CompilerParams(dimension_semantics=("arbitrary",) * n_axes,
                                vmem_limit_bytes=VMEM_LIMIT)


def _ada_kernel(sc_ref, w_ref, b_ref, o_ref):
    v = sc_ref[...]
    s = (v * _sigmoid(v)).astype(BF16)
    o_ref[...] = jnp.dot(s, w_ref[...].astype(BF16), preferred_element_type=F32) + b_ref[...]


def _ada_call(sc_in, w_ada, b_ada):
    depth = w_ada.shape[0]
    rows = sc_in.shape[0]
    nblk = 3
    return pl.pallas_call(
        _ada_kernel,
        out_shape=jax.ShapeDtypeStruct((depth, rows, 3 * D), F32),
        grid=(depth, nblk),
        in_specs=[
            pl.BlockSpec((rows, D), lambda l, n: (0, 0)),
            pl.BlockSpec((None, D, D), lambda l, n: (l, 0, n)),
            pl.BlockSpec((None, 1, D), lambda l, n: (l, 0, n)),
        ],
        out_specs=pl.BlockSpec((None, rows, D), lambda l, n: (l, 0, n)),
        compiler_params=_cparams(2),
        name="adaln",
    )(sc_in, w_ada, b_ada.reshape(depth, 1, 3 * D))


def _modulated_norm(x, mod, gpre):
    shift = mod[:, :D]
    scale = mod[:, D:2 * D]
    ms = jnp.mean(x * x, axis=-1, keepdims=True)
    return x * lax.rsqrt(ms + EPS) * gpre * (1.0 + scale) + shift


def _proj_kernel(x_ref, mod_ref, gpre_ref, w_ref, cost_ref, sint_ref, cosn_ref, sina_ref, sinb_ref,
                 gq_ref, gk_ref, qt_ref, k_ref, vt_ref):
    tm = x_ref.shape[0]
    h = _modulated_norm(x_ref[...], mod_ref[...], gpre_ref[...])
    y = jnp.dot(h.astype(BF16), w_ref[...], preferred_element_type=F32)

    q3 = y[:, :N_Q].T.reshape(N_Q // HD, HD, tm)
    qa = q3[:8]
    ss = jnp.sum(qa * qa, axis=1, keepdims=True)
    qa = qa * lax.rsqrt(ss * (1.0 / HD) + EPS) * gq_ref[...][None]
    q3 = jnp.concatenate([qa, q3[8:]], axis=0)
    rot = jnp.concatenate([q3[:, 16:32], q3[:, 0:16], q3[:, 48:64], q3[:, 32:48]], axis=1)
    q3 = (q3 * cost_ref[...][None] + rot * sint_ref[...][None]) * (ATTN_SCALE * LOG2E)
    qt_ref[...] = q3.reshape(N_Q, tm).astype(BF16)

    ka = y[:, N_Q:N_Q + LANES]
    lane = lax.broadcasted_iota(jnp.int32, (1, LANES), 1)
    lo = lane < HD
    sq = ka * ka
    s_lo = jnp.sum(jnp.where(lo, sq, 0.0), axis=-1, keepdims=True)
    s_hi = jnp.sum(jnp.where(lo, 0.0, sq), axis=-1, keepdims=True)
    r = jnp.where(lo, lax.rsqrt(s_lo * (1.0 / HD) + EPS), lax.rsqrt(s_hi * (1.0 / HD) + EPS))
    ka = ka * r * gk_ref[...]
    cosn = cosn_ref[...]
    sina = sina_ref[...]
    sinb = sinb_ref[...]
    for i in range(N_K // LANES):
        t = ka if i == 0 else y[:, N_Q + i * LANES:N_Q + (i + 1) * LANES]
        t = t * cosn + pltpu.roll(t, LANES - 16, 1) * sina + pltpu.roll(t, 16, 1) * sinb
        k_ref[:, i * LANES:(i + 1) * LANES] = t.astype(BF16)

    vt_ref[...] = y[:, N_Q + N_K:].T.astype(BF16)


def _proj_call(xs, mod4, g_pre, wp, tabs, gq_t, gk_n, layer, n_lat_tiles, tm):
    nb, t_all, _ = xs.shape
    cost, sint, cosn, sina, sinb = tabs
    n_ctx_row = mod4.shape[1] - 1
    del n_ctx_row

    def mod_idx(b, t):
        return (layer, jnp.where(t < n_lat_tiles, b, nb), 0, 0)

    return pl.pallas_call(
        _proj_kernel,
        out_shape=(jax.ShapeDtypeStruct((nb, N_Q, t_all), BF16),
                   jax.ShapeDtypeStruct((nb, t_all, N_K), BF16),
                   jax.ShapeDtypeStruct((nb, N_V, t_all), BF16)),
        grid=(nb, t_all // tm),
        in_specs=[
            pl.BlockSpec((None, tm, D), lambda b, t: (b, t, 0)),
            pl.BlockSpec((None, None, 1, 3 * D), mod_idx),
            pl.BlockSpec((None, 1, D), lambda b, t: (layer, 0, 0)),
            pl.BlockSpec((None, D, N_P), lambda b, t: (layer, 0, 0)),
            pl.BlockSpec((HD, tm), lambda b, t: (0, t)),
            pl.BlockSpec((HD, tm), lambda b, t: (0, t)),
            pl.BlockSpec((tm, LANES), lambda b, t: (t, 0)),
            pl.BlockSpec((tm, LANES), lambda b, t: (t, 0)),
            pl.BlockSpec((tm, LANES), lambda b, t: (t, 0)),
            pl.BlockSpec((None, HD, tm), lambda b, t: (layer, 0, 0)),
            pl.BlockSpec((None, 1, LANES), lambda b, t: (layer, 0, 0)),
        ],
        out_specs=(pl.BlockSpec((None, N_Q, tm), lambda b, t: (b, 0, t)),
                   pl.BlockSpec((None, tm, N_K), lambda b, t: (b, t, 0)),
                   pl.BlockSpec((None, N_V, tm), lambda b, t: (b, 0, t))),
        compiler_params=_cparams(2),
        name="proj",
    )(xs, mod4, g_pre, wp, cost, sint, cosn, sina, sinb, gq_t, gk_n)


KCHUNK = 256
ONES_ROWS = 16
STAGE_LAG = 2


def _flash(chunks, rhs, dv, extra=None):
    m = extra
    acc = None
    for k, vt, bias in chunks:
        s = jnp.dot(k, rhs, preferred_element_type=F32)
        if bias is not None:
            s = s + bias
        cm = jnp.max(s, axis=0, keepdims=True)
        m_new = cm if m is None else jnp.maximum(m, cm)
        p = jnp.exp2(s - m_new).astype(BF16)
        vt_aug = jnp.concatenate([vt, jnp.ones((ONES_ROWS, vt.shape[1]), BF16)], axis=0)
        pv = jnp.dot(vt_aug, p, preferred_element_type=F32)
        acc = pv if acc is None else acc * jnp.exp2(m - m_new) + pv
        m = m_new
    l = acc[dv:dv + 1]
    if extra is not None:
        l = l + jnp.exp2(extra - m)
    return acc[:dv] * (1.0 / l)


def _key_chunks(k_ref, vt_ref, lo, n_keys):
    return [(k_ref[lo + c * KCHUNK:lo + (c + 1) * KCHUNK, :],
             vt_ref[:, lo + c * KCHUNK:lo + (c + 1) * KCHUNK], None) for c in range(n_keys // KCHUNK)]


def _gqa_rhs(qt, j, tq):
    z = jnp.zeros((HD, tq), qt.dtype)
    first = j == 0
    cols = []
    for g in range(4):
        qg = qt[g * HD:(g + 1) * HD, :]
        cols.append(jnp.concatenate([jnp.where(first, qg, z), jnp.where(first, z, qg)], axis=0))
    return jnp.concatenate(cols, axis=1)


def _store_heads(o_ref, ot, tq, nheads):
    o = jnp.concatenate([ot[:, g * tq:(g + 1) * tq] for g in range(nheads)], axis=0)
    o_ref[...] = o.T.astype(o_ref.dtype)


def _sink_row(sink_ref, base, tq):
    blk = lax.broadcasted_iota(jnp.int32, (1, 4 * tq), 1) // tq
    row = jnp.zeros((1, 4 * tq), F32)
    for g in range(4):
        row = jnp.where(blk == g, sink_ref[base + g], row)
    return row * LOG2E


def _tile(t, size):
    if isinstance(t, int):
        return pl.ds(t * size, size)
    return pl.ds(pl.multiple_of(t * size, size), size)


def _two_stage(k_ref, vt_ref, rhs_next, s_next, s_cur, m_cur, dv):
    cm = None
    acc = None
    parts = []
    for c in range(k_ref.shape[0] // KCHUNK):
        rows = slice(c * KCHUNK, (c + 1) * KCHUNK)
        if rhs_next is not None:
            s = jnp.dot(k_ref[rows, :], rhs_next, preferred_element_type=F32)
            s_next[rows, :] = s
            parts.append(jnp.max(s, axis=0, keepdims=True))
            cm = parts[-1] if cm is None else jnp.maximum(cm, parts[-1])
        if s_cur is not None:
            m_c = m_cur
            if rhs_next is not None and c >= STAGE_LAG:
                m_c = jnp.maximum(m_cur, jnp.minimum(parts[c - STAGE_LAG], m_cur))
            p = jnp.exp2(s_cur[rows, :] - m_c).astype(BF16)
            vt_aug = jnp.concatenate([vt_ref[:, rows], jnp.ones((ONES_ROWS, KCHUNK), BF16)], axis=0)
            pv = jnp.dot(vt_aug, p, preferred_element_type=F32)
            acc = pv if acc is None else acc + pv
    ot = None if acc is None else acc[:dv] * (1.0 / acc[dv:dv + 1])
    return cm, ot


def _dense_pipeline(make_rhs, finish, k_ref, vt_ref, s0, s1, n_lat, dv):
    m0, _ = _two_stage(k_ref, vt_ref, make_rhs(0), s0, None, None, dv)

    def body(i, m_even):
        t = 2 * i
        m_odd, ot = _two_stage(k_ref, vt_ref, make_rhs(t + 1), s1, s0, m_even, dv)
        finish(t, ot)
        m_next, ot = _two_stage(k_ref, vt_ref, make_rhs(jnp.minimum(t + 2, n_lat - 1)), s0, s1, m_odd, dv)
        finish(t + 1, ot)
        return m_next

    lax.fori_loop(0, n_lat // 2, body, m0)


def _attn_a_kernel(qt_ref, k_ref, vt_ref, o_ref, s0, s1, *, tq, n_lat, n_ctx, s_len):
    j = pl.program_id(1)
    t_all = k_ref.shape[0]

    def make_rhs(t):
        return _gqa_rhs(qt_ref[:, _tile(t, tq)], j, tq)

    def finish(t, ot):
        _store_heads(o_ref.at[_tile(t, tq), :], ot, tq, 4)

    _dense_pipeline(make_rhs, finish, k_ref, vt_ref, s0, s1, n_lat, HD)
    for t in range(n_lat, n_lat + n_ctx):
        ot = _flash(_key_chunks(k_ref, vt_ref, s_len, t_all - s_len), make_rhs(t), HD)
        finish(t, ot)


def _attn_c_kernel(sink_ref, bias_ref, qt_ref, k_ref, vt_ref, o_ref, *, tq, n_lat, s_len):
    j = pl.program_id(1)
    qi = pl.program_id(2)
    rhs = _gqa_rhs(qt_ref[...], j, tq)
    t_all = k_ref.shape[0]
    wk = tq + 2 * WINDOW
    srow = _sink_row(sink_ref, j * 4, tq)
    ctx_chunks = _key_chunks(k_ref, vt_ref, s_len, t_all - s_len)

    @pl.when(qi < n_lat)
    def _():
        q0 = qi * tq
        start = pl.multiple_of(jnp.clip(q0 - WINDOW, 0, s_len - wk), LANES)
        variant = (q0 - start) // WINDOW
        chunks = list(ctx_chunks)
        for c in range(wk // KCHUNK):
            lo = pl.multiple_of(start + c * KCHUNK, LANES)
            b = bias_ref[variant, c * KCHUNK:(c + 1) * KCHUNK, :]
            chunks.append((k_ref[pl.ds(lo, KCHUNK), :], vt_ref[:, pl.ds(lo, KCHUNK)],
                           jnp.concatenate([b] * 4, axis=1)))
        _store_heads(o_ref, _flash(chunks, rhs, HD, extra=srow), tq, 4)

    @pl.when(qi >= n_lat)
    def _():
        _store_heads(o_ref, _flash(ctx_chunks, rhs, HD, extra=srow), tq, 4)


def _attn_b_kernel(lamv_ref, subln_ref, qt_ref, k_ref, vt_ref, o_ref, s0, s1, *, tq, n_lat, n_ctx, s_len,
                   lam_init):
    lv = lamv_ref[...]
    lam = (jnp.exp(jnp.sum(lv[0:1] * lv[1:2], axis=-1, keepdims=True))
           - jnp.exp(jnp.sum(lv[2:3] * lv[3:4], axis=-1, keepdims=True)) + lam_init)
    t_all = k_ref.shape[0]

    def make_rhs(t):
        qt = qt_ref[:, _tile(t, tq)]
        z = jnp.zeros((HD, tq), qt.dtype)
        return jnp.concatenate([jnp.concatenate([qt[:HD], z], axis=0),
                                jnp.concatenate([z, qt[HD:]], axis=0)], axis=1)

    def finish(t, o2):
        o = o2[:, :tq] - lam * o2[:, tq:]
        ms = jnp.mean(o * o, axis=0, keepdims=True)
        o = o * lax.rsqrt(ms + SUBLN_EPS) * subln_ref[...] * (1.0 - lam_init)
        o_ref[_tile(t, tq), :] = o.T.astype(o_ref.dtype)

    _dense_pipeline(make_rhs, finish, k_ref, vt_ref, s0, s1, n_lat, 2 * HD)
    for t in range(n_lat, n_lat + n_ctx):
        finish(t, _flash(_key_chunks(k_ref, vt_ref, s_len, t_all - s_len), make_rhs(t), 2 * HD))


def _window_bias(tq):
    wk = tq + 2 * WINDOW
    r = np.arange(wk)[:, None]
    c = np.arange(tq)[None, :]
    out = np.stack([np.where(np.abs(c - r + v * WINDOW) <= WINDOW, 0.0, -np.inf) for v in range(3)])
    return jnp.asarray(out, F32)


def _attn_calls(qt_all, k_all, vt_all, sink, lamv, subln_t, layer, with_ctx, s_len, lam_init,
                tq_a, tq_b, tq_c):
    nb, _, t_all = qt_all.shape
    c_len = t_all - s_len
    o_rows = t_all if with_ctx else s_len
    o_shape = jax.ShapeDtypeStruct((nb, o_rows, QW), BF16)

    def steps(tq):
        n_lat = s_len // tq
        return n_lat, n_lat + (c_len // tq if with_ctx else 0)

    score_scratch = lambda n: [pltpu.VMEM((t_all, n), F32), pltpu.VMEM((t_all, n), F32)]
    n_lat, n_all = steps(tq_a)
    oa = pl.pallas_call(
        functools.partial(_attn_a_kernel, tq=tq_a, n_lat=n_lat, n_ctx=n_all - n_lat, s_len=s_len),
        out_shape=o_shape,
        grid=(nb, 2),
        in_specs=[
            pl.BlockSpec((None, 4 * HD, t_all), lambda b, j: (b, j, 0)),
            pl.BlockSpec((None, t_all, LANES), lambda b, j: (b, 0, 0)),
            pl.BlockSpec((None, HD, t_all), lambda b, j: (b, j, 0)),
        ],
        out_specs=pl.BlockSpec((None, o_rows, 4 * HD), lambda b, j: (b, 0, j)),
        scratch_shapes=score_scratch(4 * tq_a),
        compiler_params=_cparams(2),
        name="attn_a",
    )(qt_all, k_all, vt_all)

    n_lat, n_all = steps(tq_b)
    ob = pl.pallas_call(
        functools.partial(_attn_b_kernel, tq=tq_b, n_lat=n_lat, n_ctx=n_all - n_lat, s_len=s_len,
                          lam_init=lam_init),
        out_shape=o_shape,
        grid=(nb, 4),
        in_specs=[
            pl.BlockSpec((None, 4, HD), lambda b, h: (layer, 0, 0)),
            pl.BlockSpec((None, 2 * HD, tq_b), lambda b, h: (layer, 0, 0)),
            pl.BlockSpec((None, 2 * HD, t_all), lambda b, h: (b, 4 + h, 0)),
            pl.BlockSpec((None, t_all, LANES), lambda b, h: (b, 0, 1 + h)),
            pl.BlockSpec((None, 2 * HD, t_all), lambda b, h: (b, 1 + h, 0)),
        ],
        out_specs=pl.BlockSpec((None, o_rows, 2 * HD), lambda b, h: (b, 0, h)),
        scratch_shapes=score_scratch(2 * tq_b),
        compiler_params=_cparams(2),
        name="attn_b",
    )(lamv, subln_t, qt_all, k_all, vt_all)

    n_lat, n_all = steps(tq_c)
    oc = pl.pallas_call(
        functools.partial(_attn_c_kernel, tq=tq_c, n_lat=n_lat, s_len=s_len),
        out_shape=o_shape,
        grid=(nb, 2, n_all),
        in_specs=[
            pl.BlockSpec(memory_space=pltpu.SMEM),
            pl.BlockSpec((3, tq_c + 2 * WINDOW, tq_c), lambda b, j, q: (0, 0, 0)),
            pl.BlockSpec((None, 4 * HD, tq_c), lambda b, j, q: (b, 4 + j, q)),
            pl.BlockSpec((None, t_all, LANES), lambda b, j, q: (b, 0, 5)),
            pl.BlockSpec((None, HD, t_all), lambda b, j, q: (b, 10 + j, 0)),
        ],
        out_specs=pl.BlockSpec((None, tq_c, 4 * HD), lambda b, j, q: (b, q, j)),
        compiler_params=_cparams(3),
        name="attn_c",
    )(sink[layer], _window_bias(tq_c), qt_all, k_all, vt_all)
    return oa, ob, oc


def _merge_kernel(x_ref, oa_ref, ob_ref, oc_ref, mod_ref, gpre_ref, gpost_ref, wgm_ref, bmg_ref,
                  wbr_ref, wout_ref, out_ref):
    x = x_ref[...]
    mod = mod_ref[...]
    h = _modulated_norm(x, mod, gpre_ref[...])
    gm = jnp.dot(h.astype(BF16), wgm_ref[...], preferred_element_type=F32)
    z = None
    for i, o_ref in enumerate((oa_ref, ob_ref, oc_ref)):
        g = gm[:, i * QW:(i + 1) * QW]
        u = (o_ref[...].astype(F32) * (g * _sigmoid(g))).astype(BF16)
        p = jnp.dot(u, wbr_ref[i], preferred_element_type=F32)
        mg = _sigmoid(gm[:, N_G + i * D:N_G + (i + 1) * D] + bmg_ref[:, i * D:(i + 1) * D])
        z = mg * p if z is None else z + mg * p
    y = jnp.dot(z.astype(BF16), wout_ref[...], preferred_element_type=F32)
    ms = jnp.mean(y * y, axis=-1, keepdims=True)
    gate = mod[:, 2 * D:]
    out_ref[...] = x + gate * (y * lax.rsqrt(ms + EPS) * gpost_ref[...])


def _merge_call(xs, oa, ob, oc, mod4, g_pre, g_post, wgm, b_mg, wbr, wout, layer, n_lat_tiles,
                with_ctx, tm):
    nb, t_all, _ = xs.shape
    n_tiles = t_all // tm if with_ctx else n_lat_tiles

    def mod_idx(b, t):
        return (layer, jnp.where(t < n_lat_tiles, b, nb), 0, 0)

    tok = lambda b, t: (b, t, 0)
    lay2 = lambda b, t: (layer, 0, 0)
    return pl.pallas_call(
        _merge_kernel,
        out_shape=jax.ShapeDtypeStruct((nb, n_tiles * tm, D), F32),
        grid=(nb, n_tiles),
        in_specs=[
            pl.BlockSpec((None, tm, D), tok),
            pl.BlockSpec((None, tm, QW), tok),
            pl.BlockSpec((None, tm, QW), tok),
            pl.BlockSpec((None, tm, QW), tok),
            pl.BlockSpec((None, None, 1, 3 * D), mod_idx),
            pl.BlockSpec((None, 1, D), lay2),
            pl.BlockSpec((None, 1, D), lay2),
            pl.BlockSpec((None, D, N_G + N_M), lay2),
            pl.BlockSpec((None, 1, N_M), lay2),
            pl.BlockSpec((None, 3, QW, D), lambda b, t: (layer, 0, 0, 0)),
            pl.BlockSpec((None, D, D), lay2),
        ],
        out_specs=pl.BlockSpec((None, tm, D), tok),
        compiler_params=_cparams(2),
        name="merge",
    )(xs, oa, ob, oc, mod4, g_pre, g_post, wgm, b_mg, wbr, wout)


def _rope_tables(s_len, c_len):
    rows = s_len // GRID_W
    row = jnp.repeat(jnp.arange(rows), GRID_W).astype(F32)
    col = jnp.tile(jnp.arange(GRID_W), rows).astype(F32)
    freqs = ROPE_THETA ** (-jnp.arange(ROPE_PAIRS, dtype=F32) / ROPE_PAIRS)
    ang_r = row[:, None] * freqs
    ang_c = col[:, None] * freqs
    ang = jnp.concatenate([ang_r, ang_r, ang_c, ang_c], axis=-1)
    cos = jnp.concatenate([jnp.cos(ang), jnp.ones((c_len, HD), F32)], axis=0)
    sin = jnp.concatenate([jnp.sin(ang), jnp.zeros((c_len, HD), F32)], axis=0)
    first = (np.arange(HD) % 32) < 16
    sin_a = jnp.where(first, -sin, 0.0)
    sin_b = jnp.where(first, 0.0, sin)
    tile2 = lambda a: jnp.concatenate([a, a], axis=-1)
    cost = cos.T
    sint = (sin_a + sin_b).T
    return cost, sint, tile2(cos), tile2(sin_a), tile2(sin_b)


def _cols(w, names):
    return jnp.concatenate([w[..., _IN[n][0]:_IN[n][1]] for n in names], axis=-1)


def kernel(x, c, ctx, c_ctx, w_ada, b_ada, g_pre, g_post, w_in, q_norm, k_norm, lam_q1, lam_k1, lam_q2,
           lam_k2, subln, sink, w_br_a, w_br_b, w_br_c, w_mg, b_mg, w_out):
    nb, s_len, _ = x.shape
    c_len = ctx.shape[1]
    depth = w_in.shape[0]
    tm = 256
    tq_a, tq_b, tq_c = 128, 256, 256
    n_lat_tiles = s_len // tm

    wp = _cols(w_in, ("qa", "qb", "qc", "ka", "kb", "kc", "va", "vb", "vc")).astype(BF16)
    wgm = jnp.concatenate([_cols(w_in, ("ga", "gb", "gc")), w_mg], axis=-1).astype(BF16)
    wbr = jnp.stack([w_br_a, w_br_b, w_br_c], axis=1).astype(BF16)
    wout = w_out.astype(BF16)

    tabs = _rope_tables(s_len, c_len)
    gq_t = jnp.broadcast_to(q_norm[:, :, None], (depth, HD, tm))
    gk_n = jnp.concatenate([k_norm, k_norm], axis=-1)[:, None, :]
    lamv = jnp.stack([lam_q1, lam_k1, lam_q2, lam_k2], axis=1)
    subln_t = jnp.broadcast_to(subln[:, :, None], (depth, 2 * HD, tq_b))
    g_pre3 = g_pre[:, None, :]
    g_post3 = g_post[:, None, :]
    b_mg3 = b_mg[:, None, :]

    rows = ((nb + 1 + 7) // 8) * 8
    sc_in = jnp.concatenate([c, c_ctx[None, :], jnp.zeros((rows - nb - 1, D), F32)], axis=0)
    mod = _ada_call(sc_in, w_ada, b_ada)
    mod4 = mod[:, :, None, :]

    xs = jnp.concatenate([x, ctx], axis=1)
    for layer in range(depth):
        last = layer == depth - 1
        lam_init = 0.8 - 0.6 * math.exp(-0.3 * layer)
        qt_all, k_all, vt_all = _proj_call(xs, mod4, g_pre3, wp, tabs, gq_t, gk_n, layer, n_lat_tiles, tm)
        oa, ob, oc = _attn_calls(qt_all, k_all, vt_all, sink, lamv, subln_t, layer, not last, s_len,
                                 lam_init, tq_a, tq_b, tq_c)
        xs = _merge_call(xs, oa, ob, oc, mod4, g_pre3, g_post3, wgm, b_mg3, wbr, wout, layer,
                         n_lat_tiles, not last, tm)
    return xs
```

```python
import functools
import math

import jax
import jax.numpy as jnp
import numpy as np
from jax import lax
from jax.experimental import pallas as pl
from jax.experimental.pallas import tpu as pltpu

F32 = jnp.float32
BF16 = jnp.bfloat16

D = 1024
HD = 64
GRID_W = 64
WINDOW = 128
ROPE_THETA = 10000.0
ROPE_PAIRS = HD // 4
EPS = 1e-6
SUBLN_EPS = 1e-5
ATTN_SCALE = HD ** -0.5
LOG2E = math.log2(math.e)

QW = 512
N_Q = 3 * QW
N_K = 128 + 512 + 128
N_V = 128 + 512 + 128
N_P = N_Q + N_K + N_V
N_G = 3 * QW
N_M = 3 * D

LANES = 128
VMEM_LIMIT = 56 * 1024 * 1024

_IN = dict(qa=(0, 512), ka=(512, 640), va=(640, 768), ga=(768, 1280),
           qb=(1280, 1792), kb=(1792, 2304), vb=(2304, 2816), gb=(2816, 3328),
           qc=(3328, 3840), kc=(3840, 3968), vc=(3968, 4096), gc=(4096, 4608))


def _sigmoid(v):
    return 1.0 / (1.0 + jnp.exp(-v))


def _cparams(n_axes):
    return pltpu.CompilerParams(dimension_semantics=("arbitrary",) * n_axes,
                                vmem_limit_bytes=VMEM_LIMIT)


def _ada_kernel(sc_ref, w_ref, b_ref, o_ref):
    v = sc_ref[...]
    s = (v * _sigmoid(v)).astype(BF16)
    o_ref[...] = jnp.dot(s, w_ref[...].astype(BF16), preferred_element_type=F32) + b_ref[...]


def _ada_call(sc_in, w_ada, b_ada):
    depth = w_ada.shape[0]
    rows = sc_in.shape[0]
    nblk = 3
    return pl.pallas_call(
        _ada_kernel,
        out_shape=jax.ShapeDtypeStruct((depth, rows, 3 * D), F32),
        grid=(depth, nblk),
        in_specs=[
            pl.BlockSpec((rows, D), lambda l, n: (0, 0)),
            pl.BlockSpec((None, D, D), lambda l, n: (l, 0, n)),
            pl.BlockSpec((None, 1, D), lambda l, n: (l, 0, n)),
        ],
        out_specs=pl.BlockSpec((None, rows, D), lambda l, n: (l, 0, n)),
        compiler_params=_cparams(2),
        name="adaln",
    )(sc_in, w_ada, b_ada.reshape(depth, 1, 3 * D))


def _modulated_norm(x, mod, gpre):
    shift = mod[:, :D]
    scale = mod[:, D:2 * D]
    ms = jnp.mean(x * x, axis=-1, keepdims=True)
    return x * lax.rsqrt(ms + EPS) * gpre * (1.0 + scale) + shift


def _proj_kernel(x_ref, mod_ref, gpre_ref, w_ref, cost_ref, sint_ref, cosn_ref, sina_ref, sinb_ref,
                 gq_ref, gk_ref, qt_ref, k_ref, vt_ref):
    tm = x_ref.shape[0]
    h = _modulated_norm(x_ref[...], mod_ref[...], gpre_ref[...])
    y = jnp.dot(h.astype(BF16), w_ref[...], preferred_element_type=F32)

    q3 = y[:, :N_Q].T.reshape(N_Q // HD, HD, tm)
    qa = q3[:8]
    ss = jnp.sum(qa * qa, axis=1, keepdims=True)
    qa = qa * lax.rsqrt(ss * (1.0 / HD) + EPS) * gq_ref[...][None]
    q3 = jnp.concatenate([qa, q3[8:]], axis=0)
    rot = jnp.concatenate([q3[:, 16:32], q3[:, 0:16], q3[:, 48:64], q3[:, 32:48]], axis=1)
    q3 = (q3 * cost_ref[...][None] + rot * sint_ref[...][None]) * (ATTN_SCALE * LOG2E)
    qt_ref[...] = q3.reshape(N_Q, tm).astype(BF16)

    ka = y[:, N_Q:N_Q + LANES]
    lane = lax.broadcasted_iota(jnp.int32, (1, LANES), 1)
    lo = lane < HD
    sq = ka * ka
    s_lo = jnp.sum(jnp.where(lo, sq, 0.0), axis=-1, keepdims=True)
    s_hi = jnp.sum(jnp.where(lo, 0.0, sq), axis=-1, keepdims=True)
    r = jnp.where(lo, lax.rsqrt(s_lo * (1.0 / HD) + EPS), lax.rsqrt(s_hi * (1.0 / HD) + EPS))
    ka = ka * r * gk_ref[...]
    cosn = cosn_ref[...]
    sina = sina_ref[...]
    sinb = sinb_ref[...]
    for i in range(N_K // LANES):
        t = ka if i == 0 else y[:, N_Q + i * LANES:N_Q + (i + 1) * LANES]
        t = t * cosn + pltpu.roll(t, LANES - 16, 1) * sina + pltpu.roll(t, 16, 1) * sinb
        k_ref[:, i * LANES:(i + 1) * LANES] = t.astype(BF16)

    vt_ref[...] = y[:, N_Q + N_K:].T.astype(BF16)


def _proj_call(xs, mod4, g_pre, wp, tabs, gq_t, gk_n, layer, n_lat_tiles, tm):
    nb, t_all, _ = xs.shape
    cost, sint, cosn, sina, sinb = tabs
    n_ctx_row = mod4.shape[1] - 1
    del n_ctx_row

    def mod_idx(b, t):
        return (layer, jnp.where(t < n_lat_tiles, b, nb), 0, 0)

    return pl.pallas_call(
        _proj_kernel,
        out_shape=(jax.ShapeDtypeStruct((nb, N_Q, t_all), BF16),
                   jax.ShapeDtypeStruct((nb, t_all, N_K), BF16),
                   jax.ShapeDtypeStruct((nb, N_V, t_all), BF16)),
        grid=(nb, t_all // tm),
        in_specs=[
            pl.BlockSpec((None, tm, D), lambda b, t: (b, t, 0)),
            pl.BlockSpec((None, None, 1, 3 * D), mod_idx),
            pl.BlockSpec((None, 1, D), lambda b, t: (layer, 0, 0)),
            pl.BlockSpec((None, D, N_P), lambda b, t: (layer, 0, 0)),
            pl.BlockSpec((HD, tm), lambda b, t: (0, t)),
            pl.BlockSpec((HD, tm), lambda b, t: (0, t)),
            pl.BlockSpec((tm, LANES), lambda b, t: (t, 0)),
            pl.BlockSpec((tm, LANES), lambda b, t: (t, 0)),
            pl.BlockSpec((tm, LANES), lambda b, t: (t, 0)),
            pl.BlockSpec((None, HD, tm), lambda b, t: (layer, 0, 0)),
            pl.BlockSpec((None, 1, LANES), lambda b, t: (layer, 0, 0)),
        ],
        out_specs=(pl.BlockSpec((None, N_Q, tm), lambda b, t: (b, 0, t)),
                   pl.BlockSpec((None, tm, N_K), lambda b, t: (b, t, 0)),
                   pl.BlockSpec((None, N_V, tm), lambda b, t: (b, 0, t))),
        compiler_params=_cparams(2),
        name="proj",
    )(xs, mod4, g_pre, wp, cost, sint, cosn, sina, sinb, gq_t, gk_n)


KCHUNK = 256
ONES_ROWS = 16
STAGE_LAG = 2


def _flash(chunks, rhs, dv, extra=None):
    m = extra
    acc = None
    for k, vt, bias in chunks:
        s = jnp.dot(k, rhs, preferred_element_type=F32)
        if bias is not None:
            s = s + bias
        cm = jnp.max(s, axis=0, keepdims=True)
        m_new = cm if m is None else jnp.maximum(m, cm)
        p = jnp.exp2(s - m_new).astype(BF16)
        vt_aug = jnp.concatenate([vt, jnp.ones((ONES_ROWS, vt.shape[1]), BF16)], axis=0)
        pv = jnp.dot(vt_aug, p, preferred_element_type=F32)
        acc = pv if acc is None else acc * jnp.exp2(m - m_new) + pv
        m = m_new
    l = acc[dv:dv + 1]
    if extra is not None:
        l = l + jnp.exp2(extra - m)
    return acc[:dv] * (1.0 / l)


def _key_chunks(k_ref, vt_ref, lo, n_keys):
    return [(k_ref[lo + c * KCHUNK:lo + (c + 1) * KCHUNK, :],
             vt_ref[:, lo + c * KCHUNK:lo + (c + 1) * KCHUNK], None) for c in range(n_keys // KCHUNK)]


def _gqa_rhs(qt, j, tq):
    z = jnp.zeros((HD, tq), qt.dtype)
    first = j == 0
    cols = []
    for g in range(4):
        qg = qt[g * HD:(g + 1) * HD, :]
        cols.append(jnp.concatenate([jnp.where(first, qg, z), jnp.where(first, z, qg)], axis=0))
    return jnp.concatenate(cols, axis=1)


def _store_heads(o_ref, ot, tq, nheads):
    o = jnp.concatenate([ot[:, g * tq:(g + 1) * tq] for g in range(nheads)], axis=0)
    o_ref[...] = o.T.astype(o_ref.dtype)


def _sink_row(sink_ref, base, tq):
    blk = lax.broadcasted_iota(jnp.int32, (1, 4 * tq), 1) // tq
    row = jnp.zeros((1, 4 * tq), F32)
    for g in range(4):
        row = jnp.where(blk == g, sink_ref[base + g], row)
    return row * LOG2E


def _tile(t, size):
    if isinstance(t, int):
        return pl.ds(t * size, size)
    return pl.ds(pl.multiple_of(t * size, size), size)


def _two_stage(score_chunks, value_chunks, rhs_next, s_next, s_cur, m_cur, dv, extra=None):
    cm = None
    acc = None
    parts = []
    for c in range(len(score_chunks or value_chunks)):
        rows = slice(c * KCHUNK, (c + 1) * KCHUNK)
        if score_chunks is not None:
            k, bias = score_chunks[c]()
            s = jnp.dot(k, rhs_next, preferred_element_type=F32)
            if bias is not None:
                s = s + bias
            s_next[rows, :] = s
            parts.append(jnp.max(s, axis=0, keepdims=True))
            cm = parts[-1] if cm is None else jnp.maximum(cm, parts[-1])
        if value_chunks is not None:
            m_c = m_cur
            if score_chunks is not None and c >= STAGE_LAG:
                m_c = jnp.maximum(m_cur, jnp.minimum(parts[c - STAGE_LAG], m_cur))
            p = jnp.exp2(s_cur[rows, :] - m_c).astype(BF16)
            vt_aug = jnp.concatenate([value_chunks[c](), jnp.ones((ONES_ROWS, KCHUNK), BF16)], axis=0)
            pv = jnp.dot(vt_aug, p, preferred_element_type=F32)
            acc = pv if acc is None else acc + pv
    if cm is not None and extra is not None:
        cm = jnp.maximum(cm, extra)
    ot = None
    if acc is not None:
        l = acc[dv:dv + 1]
        if extra is not None:
            l = l + jnp.exp2(extra - m_cur)
        ot = acc[:dv] * (1.0 / l)
    return cm, ot


def _pipeline(make_rhs, score_chunks, value_chunks, finish, s0, s1, n_lat, dv, extra=None):
    m0, _ = _two_stage(score_chunks(0), None, make_rhs(0), s0, None, None, dv, extra)

    def body(i, m_even):
        t = 2 * i
        m_odd, ot = _two_stage(score_chunks(t + 1), value_chunks(t), make_rhs(t + 1), s1, s0, m_even, dv, extra)
        finish(t, ot)
        t2 = jnp.minimum(t + 2, n_lat - 1)
        m_next, ot = _two_stage(score_chunks(t2), value_chunks(t + 1), make_rhs(t2), s0, s1, m_odd, dv, extra)
        finish(t + 1, ot)
        return m_next

    lax.fori_loop(0, n_lat // 2, body, m0)


def _dense_chunks(k_ref, vt_ref):
    n = k_ref.shape[0] // KCHUNK
    sc = [functools.partial(lambda c: (k_ref[c * KCHUNK:(c + 1) * KCHUNK, :], None), c) for c in range(n)]
    vc = [functools.partial(lambda c: vt_ref[:, c * KCHUNK:(c + 1) * KCHUNK], c) for c in range(n)]
    return (lambda t: sc), (lambda t: vc)


def _attn_a_kernel(qt_ref, k_ref, vt_ref, o_ref, s0, s1, *, tq, n_lat, n_ctx, s_len):
    j = pl.program_id(1)
    t_all = k_ref.shape[0]

    def make_rhs(t):
        return _gqa_rhs(qt_ref[:, _tile(t, tq)], j, tq)

    def finish(t, ot):
        _store_heads(o_ref.at[_tile(t, tq), :], ot, tq, 4)

    sc, vc = _dense_chunks(k_ref, vt_ref)
    _pipeline(make_rhs, sc, vc, finish, s0, s1, n_lat, HD)
    for t in range(n_lat, n_lat + n_ctx):
        finish(t, _flash(_key_chunks(k_ref, vt_ref, s_len, t_all - s_len), make_rhs(t), HD))


def _attn_c_kernel(sink_ref, bias_ref, qt_ref, k_ref, vt_ref, o_ref, s0, s1, *, tq, n_lat, n_ctx, s_len):
    j = pl.program_id(1)
    t_all = k_ref.shape[0]
    wk = tq + 2 * WINDOW
    srow = _sink_row(sink_ref, j * 4, tq)

    def make_rhs(t):
        return _gqa_rhs(qt_ref[:, _tile(t, tq)], j, tq)

    def finish(t, ot):
        _store_heads(o_ref.at[_tile(t, tq), :], ot, tq, 4)

    def window(t):
        q0 = t * tq
        start = jnp.clip(q0 - WINDOW, 0, s_len - wk)
        return start, (q0 - start) // WINDOW

    def local(start, c):
        return pl.ds(pl.multiple_of(start + c * KCHUNK, LANES), KCHUNK)

    def score_chunks(t):
        start, variant = window(t)

        def loc(c):
            b = bias_ref[variant, c * KCHUNK:(c + 1) * KCHUNK, :]
            return k_ref[local(start, c), :], jnp.concatenate([b] * 4, axis=1)

        return ([lambda: (k_ref[s_len:s_len + KCHUNK, :], None)]
                + [functools.partial(loc, c) for c in range(wk // KCHUNK)])

    def value_chunks(t):
        start, _ = window(t)
        return ([lambda: vt_ref[:, s_len:s_len + KCHUNK]]
                + [functools.partial(lambda c: vt_ref[:, local(start, c)], c) for c in range(wk // KCHUNK)])

    _pipeline(make_rhs, score_chunks, value_chunks, finish, s0, s1, n_lat, HD, extra=srow)
    for t in range(n_lat, n_lat + n_ctx):
        finish(t, _flash(_key_chunks(k_ref, vt_ref, s_len, t_all - s_len), make_rhs(t), HD, extra=srow))


def _attn_b_kernel(lamv_ref, subln_ref, qt_ref, k_ref, vt_ref, o_ref, s0, s1, *, tq, n_lat, n_ctx, s_len,
                   lam_init):
    lv = lamv_ref[...]
    lam = (jnp.exp(jnp.sum(lv[0:1] * lv[1:2], axis=-1, keepdims=True))
           - jnp.exp(jnp.sum(lv[2:3] * lv[3:4], axis=-1, keepdims=True)) + lam_init)
    t_all = k_ref.shape[0]

    def make_rhs(t):
        qt = qt_ref[:, _tile(t, tq)]
        z = jnp.zeros((HD, tq), qt.dtype)
        return jnp.concatenate([jnp.concatenate([qt[:HD], z], axis=0),
                                jnp.concatenate([z, qt[HD:]], axis=0)], axis=1)

    def finish(t, o2):
        o = o2[:, :tq] - lam * o2[:, tq:]
        ms = jnp.mean(o * o, axis=0, keepdims=True)
        o = o * lax.rsqrt(ms + SUBLN_EPS) * subln_ref[...] * (1.0 - lam_init)
        o_ref[_tile(t, tq), :] = o.T.astype(o_ref.dtype)

    sc, vc = _dense_chunks(k_ref, vt_ref)
    _pipeline(make_rhs, sc, vc, finish, s0, s1, n_lat, 2 * HD)
    for t in range(n_lat, n_lat + n_ctx):
        finish(t, _flash(_key_chunks(k_ref, vt_ref, s_len, t_all - s_len), make_rhs(t), 2 * HD))


def _window_bias(tq):
    wk = tq + 2 * WINDOW
    r = np.arange(wk)[:, None]
    c = np.arange(tq)[None, :]
    out = np.stack([np.where(np.abs(c - r + v * WINDOW) <= WINDOW, 0.0, -np.inf) for v in range(3)])
    return jnp.asarray(out, F32)


def _attn_calls(qt_all, k_all, vt_all, sink, lamv, subln_t, layer, with_ctx, s_len, lam_init,
                tq_a, tq_b, tq_c):
    nb, _, t_all = qt_all.shape
    c_len = t_all - s_len
    o_rows = t_all if with_ctx else s_len
    o_shape = jax.ShapeDtypeStruct((nb, o_rows, QW), BF16)

    def steps(tq):
        n_lat = s_len // tq
        return n_lat, n_lat + (c_len // tq if with_ctx else 0)

    score_scratch = lambda n: [pltpu.VMEM((t_all, n), F32), pltpu.VMEM((t_all, n), F32)]
    n_lat, n_all = steps(tq_a)
    oa = pl.pallas_call(
        functools.partial(_attn_a_kernel, tq=tq_a, n_lat=n_lat, n_ctx=n_all - n_lat, s_len=s_len),
        out_shape=o_shape,
        grid=(nb, 2),
        in_specs=[
            pl.BlockSpec((None, 4 * HD, t_all), lambda b, j: (b, j, 0)),
            pl.BlockSpec((None, t_all, LANES), lambda b, j: (b, 0, 0)),
            pl.BlockSpec((None, HD, t_all), lambda b, j: (b, j, 0)),
        ],
        out_specs=pl.BlockSpec((None, o_rows, 4 * HD), lambda b, j: (b, 0, j)),
        scratch_shapes=score_scratch(4 * tq_a),
        compiler_params=_cparams(2),
        name="attn_a",
    )(qt_all, k_all, vt_all)

    n_lat, n_all = steps(tq_b)
    ob = pl.pallas_call(
        functools.partial(_attn_b_kernel, tq=tq_b, n_lat=n_lat, n_ctx=n_all - n_lat, s_len=s_len,
                          lam_init=lam_init),
        out_shape=o_shape,
        grid=(nb, 4),
        in_specs=[
            pl.BlockSpec((None, 4, HD), lambda b, h: (layer, 0, 0)),
            pl.BlockSpec((None, 2 * HD, tq_b), lambda b, h: (layer, 0, 0)),
            pl.BlockSpec((None, 2 * HD, t_all), lambda b, h: (b, 4 + h, 0)),
            pl.BlockSpec((None, t_all, LANES), lambda b, h: (b, 0, 1 + h)),
            pl.BlockSpec((None, 2 * HD, t_all), lambda b, h: (b, 1 + h, 0)),
        ],
        out_specs=pl.BlockSpec((None, o_rows, 2 * HD), lambda b, h: (b, 0, h)),
        scratch_shapes=score_scratch(2 * tq_b),
        compiler_params=_cparams(2),
        name="attn_b",
    )(lamv, subln_t, qt_all, k_all, vt_all)

    n_lat, n_all = steps(tq_c)
    wk_c = tq_c + 2 * WINDOW
    oc = pl.pallas_call(
        functools.partial(_attn_c_kernel, tq=tq_c, n_lat=n_lat, n_ctx=n_all - n_lat, s_len=s_len),
        out_shape=o_shape,
        grid=(nb, 2),
        in_specs=[
            pl.BlockSpec(memory_space=pltpu.SMEM),
            pl.BlockSpec((3, wk_c, tq_c), lambda b, j: (0, 0, 0)),
            pl.BlockSpec((None, 4 * HD, t_all), lambda b, j: (b, 4 + j, 0)),
            pl.BlockSpec((None, t_all, LANES), lambda b, j: (b, 0, 5)),
            pl.BlockSpec((None, HD, t_all), lambda b, j: (b, 10 + j, 0)),
        ],
        out_specs=pl.BlockSpec((None, o_rows, 4 * HD), lambda b, j: (b, 0, j)),
        scratch_shapes=[pltpu.VMEM((c_len + wk_c, 4 * tq_c), F32), pltpu.VMEM((c_len + wk_c, 4 * tq_c), F32)],
        compiler_params=_cparams(2),
        name="attn_c",
    )(sink[layer], _window_bias(tq_c), qt_all, k_all, vt_all)
    return oa, ob, oc


def _merge_kernel(x_ref, oa_ref, ob_ref, oc_ref, mod_ref, gpre_ref, gpost_ref, wgm_ref, bmg_ref,
                  wbr_ref, wout_ref, out_ref):
    x = x_ref[...]
    mod = mod_ref[...]
    h = _modulated_norm(x, mod, gpre_ref[...])
    gm = jnp.dot(h.astype(BF16), wgm_ref[...], preferred_element_type=F32)
    z = None
    for i, o_ref in enumerate((oa_ref, ob_ref, oc_ref)):
        g = gm[:, i * QW:(i + 1) * QW]
        u = (o_ref[...].astype(F32) * (g * _sigmoid(g))).astype(BF16)
        p = jnp.dot(u, wbr_ref[i], preferred_element_type=F32)
        mg = _sigmoid(gm[:, N_G + i * D:N_G + (i + 1) * D] + bmg_ref[:, i * D:(i + 1) * D])
        z = mg * p if z is None else z + mg * p
    y = jnp.dot(z.astype(BF16), wout_ref[...], preferred_element_type=F32)
    ms = jnp.mean(y * y, axis=-1, keepdims=True)
    gate = mod[:, 2 * D:]
    out_ref[...] = x + gate * (y * lax.rsqrt(ms + EPS) * gpost_ref[...])


def _merge_call(xs, oa, ob, oc, mod4, g_pre, g_post, wgm, b_mg, wbr, wout, layer, n_lat_tiles,
                with_ctx, tm):
    nb, t_all, _ = xs.shape
    n_tiles = t_all // tm if with_ctx else n_lat_tiles

    def mod_idx(b, t):
        return (layer, jnp.where(t < n_lat_tiles, b, nb), 0, 0)

    tok = lambda b, t: (b, t, 0)
    lay2 = lambda b, t: (layer, 0, 0)
    return pl.pallas_call(
        _merge_kernel,
        out_shape=jax.ShapeDtypeStruct((nb, n_tiles * tm, D), F32),
        grid=(nb, n_tiles),
        in_specs=[
            pl.BlockSpec((None, tm, D), tok),
            pl.BlockSpec((None, tm, QW), tok),
            pl.BlockSpec((None, tm, QW), tok),
            pl.BlockSpec((None, tm, QW), tok),
            pl.BlockSpec((None, None, 1, 3 * D), mod_idx),
            pl.BlockSpec((None, 1, D), lay2),
            pl.BlockSpec((None, 1, D), lay2),
            pl.BlockSpec((None, D, N_G + N_M), lay2),
            pl.BlockSpec((None, 1, N_M), lay2),
            pl.BlockSpec((None, 3, QW, D), lambda b, t: (layer, 0, 0, 0)),
            pl.BlockSpec((None, D, D), lay2),
        ],
        out_specs=pl.BlockSpec((None, tm, D), tok),
        compiler_params=_cparams(2),
        name="merge",
    )(xs, oa, ob, oc, mod4, g_pre, g_post, wgm, b_mg, wbr, wout)


def _rope_tables(s_len, c_len):
    rows = s_len // GRID_W
    row = jnp.repeat(jnp.arange(rows), GRID_W).astype(F32)
    col = jnp.tile(jnp.arange(GRID_W), rows).astype(F32)
    freqs = ROPE_THETA ** (-jnp.arange(ROPE_PAIRS, dtype=F32) / ROPE_PAIRS)
    ang_r = row[:, None] * freqs
    ang_c = col[:, None] * freqs
    ang = jnp.concatenate([ang_r, ang_r, ang_c, ang_c], axis=-1)
    cos = jnp.concatenate([jnp.cos(ang), jnp.ones((c_len, HD), F32)], axis=0)
    sin = jnp.concatenate([jnp.sin(ang), jnp.zeros((c_len, HD), F32)], axis=0)
    first = (np.arange(HD) % 32) < 16
    sin_a = jnp.where(first, -sin, 0.0)
    sin_b = jnp.where(first, 0.0, sin)
    tile2 = lambda a: jnp.concatenate([a, a], axis=-1)
    cost = cos.T
    sint = (sin_a + sin_b).T
    return cost, sint, tile2(cos), tile2(sin_a), tile2(sin_b)


def _cols(w, names):
    return jnp.concatenate([w[..., _IN[n][0]:_IN[n][1]] for n in names], axis=-1)


def kernel(x, c, ctx, c_ctx, w_ada, b_ada, g_pre, g_post, w_in, q_norm, k_norm, lam_q1, lam_k1, lam_q2,
           lam_k2, subln, sink, w_br_a, w_br_b, w_br_c, w_mg, b_mg, w_out):
    nb, s_len, _ = x.shape
    c_len = ctx.shape[1]
    depth = w_in.shape[0]
    tm = 256
    tq_a, tq_b, tq_c = 128, 256, 256
    n_lat_tiles = s_len // tm

    wp = _cols(w_in, ("qa", "qb", "qc", "ka", "kb", "kc", "va", "vb", "vc")).astype(BF16)
    wgm = jnp.concatenate([_cols(w_in, ("ga", "gb", "gc")), w_mg], axis=-1).astype(BF16)
    wbr = jnp.stack([w_br_a, w_br_b, w_br_c], axis=1).astype(BF16)
    wout = w_out.astype(BF16)

    tabs = _rope_tables(s_len, c_len)
    gq_t = jnp.broadcast_to(q_norm[:, :, None], (depth, HD, tm))
    gk_n = jnp.concatenate([k_norm, k_norm], axis=-1)[:, None, :]
    lamv = jnp.stack([lam_q1, lam_k1, lam_q2, lam_k2], axis=1)
    subln_t = jnp.broadcast_to(subln[:, :, None], (depth, 2 * HD, tq_b))
    g_pre3 = g_pre[:, None, :]
    g_post3 = g_post[:, None, :]
    b_mg3 = b_mg[:, None, :]

    rows = ((nb + 1 + 7) // 8) * 8
    sc_in = jnp.concatenate([c, c_ctx[None, :], jnp.zeros((rows - nb - 1, D), F32)], axis=0)
    mod = _ada_call(sc_in, w_ada, b_ada)
    mod4 = mod[:, :, None, :]

    xs = jnp.concatenate([x, ctx], axis=1)
    for layer in range(depth):
        last = layer == depth - 1
        lam_init = 0.8 - 0.6 * math.exp(-0.3 * layer)
        qt_all, k_all, vt_all = _proj_call(xs, mod4, g_pre3, wp, tabs, gq_t, gk_n, layer, n_lat_tiles, tm)
        oa, ob, oc = _attn_calls(qt_all, k_all, vt_all, sink, lamv, subln_t, layer, not last, s_len,
                                 lam_init, tq_a, tq_b, tq_c)
        xs = _merge_call(xs, oa, ob, oc, mod4, g_pre3, g_post3, wgm, b_mg3, wbr, wout, layer,
                         n_lat_tiles, not last, tm)
    return xs
```

```python
import functools
import math

import jax
import jax.numpy as jnp
import numpy as np
from jax import lax
from jax.experimental import pallas as pl
from jax.experimental.pallas import tpu as pltpu

F32 = jnp.float32
BF16 = jnp.bfloat16

D = 1024
HD = 64
GRID_W = 64
WINDOW = 128
ROPE_THETA = 10000.0
ROPE_PAIRS = HD // 4
EPS = 1e-6
SUBLN_EPS = 1e-5
ATTN_SCALE = HD ** -0.5
LOG2E = math.log2(math.e)

QW = 512
N_Q = 3 * QW
N_K = 512 + 128 + 128
N_V = 512 + 128 + 128
KA_TILE = 4
N_P = N_Q + N_K + N_V
N_G = 3 * QW
N_M = 3 * D

LANES = 128
VMEM_LIMIT = 56 * 1024 * 1024

_IN = dict(qa=(0, 512), ka=(512, 640), va=(640, 768), ga=(768, 1280),
           qb=(1280, 1792), kb=(1792, 2304), vb=(2304, 2816), gb=(2816, 3328),
           qc=(3328, 3840), kc=(3840, 3968), vc=(3968, 4096), gc=(4096, 4608))


def _sigmoid(v):
    return 1.0 / (1.0 + jnp.exp(-v))


def _cparams(n_axes):
    return pltpu.CompilerParams(dimension_semantics=("arbitrary",) * n_axes,
                                vmem_limit_bytes=VMEM_LIMIT)


def _ada_kernel(sc_ref, w_ref, b_ref, o_ref):
    v = sc_ref[...]
    s = (v * _sigmoid(v)).astype(BF16)
    o_ref[...] = jnp.dot(s, w_ref[...].astype(BF16), preferred_element_type=F32) + b_ref[...]


def _ada_call(sc_in, w_ada, b_ada):
    depth = w_ada.shape[0]
    rows = sc_in.shape[0]
    nblk = 3
    return pl.pallas_call(
        _ada_kernel,
        out_shape=jax.ShapeDtypeStruct((depth, rows, 3 * D), F32),
        grid=(depth, nblk),
        in_specs=[
            pl.BlockSpec((rows, D), lambda l, n: (0, 0)),
            pl.BlockSpec((None, D, D), lambda l, n: (l, 0, n)),
            pl.BlockSpec((None, 1, D), lambda l, n: (l, 0, n)),
        ],
        out_specs=pl.BlockSpec((None, rows, D), lambda l, n: (l, 0, n)),
        compiler_params=_cparams(2),
        name="adaln",
    )(sc_in, w_ada, b_ada.reshape(depth, 1, 3 * D))


def _modulated_norm(x, mod, gpre):
    shift = mod[:, :D]
    scale = mod[:, D:2 * D]
    ms = jnp.mean(x * x, axis=-1, keepdims=True)
    return x * lax.rsqrt(ms + EPS) * gpre * (1.0 + scale) + shift


def _proj_kernel(x_ref, mod_ref, gpre_ref, w_ref, cost_ref, sint_ref, cosn_ref, sina_ref, sinb_ref,
                 gq_ref, gk_ref, qt_ref, k_ref, vt_ref):
    tm = x_ref.shape[0]
    h = _modulated_norm(x_ref[...], mod_ref[...], gpre_ref[...])
    y = jnp.dot(h.astype(BF16), w_ref[...], preferred_element_type=F32)

    q3 = y[:, :N_Q].T.reshape(N_Q // HD, HD, tm)
    qa = q3[:8]
    ss = jnp.sum(qa * qa, axis=1, keepdims=True)
    qa = qa * lax.rsqrt(ss * (1.0 / HD) + EPS) * gq_ref[...][None]
    q3 = jnp.concatenate([qa, q3[8:]], axis=0)
    rot = jnp.concatenate([q3[:, 16:32], q3[:, 0:16], q3[:, 48:64], q3[:, 32:48]], axis=1)
    q3 = (q3 * cost_ref[...][None] + rot * sint_ref[...][None]) * (ATTN_SCALE * LOG2E)
    qt_ref[...] = q3.reshape(N_Q, tm).astype(BF16)

    ka = y[:, N_Q + KA_TILE * LANES:N_Q + (KA_TILE + 1) * LANES]
    lane = lax.broadcasted_iota(jnp.int32, (1, LANES), 1)
    lo = lane < HD
    sq = ka * ka
    s_lo = jnp.sum(jnp.where(lo, sq, 0.0), axis=-1, keepdims=True)
    s_hi = jnp.sum(jnp.where(lo, 0.0, sq), axis=-1, keepdims=True)
    r = jnp.where(lo, lax.rsqrt(s_lo * (1.0 / HD) + EPS), lax.rsqrt(s_hi * (1.0 / HD) + EPS))
    ka = ka * r * gk_ref[...]
    cosn = cosn_ref[...]
    sina = sina_ref[...]
    sinb = sinb_ref[...]
    for i in range(N_K // LANES):
        t = ka if i == KA_TILE else y[:, N_Q + i * LANES:N_Q + (i + 1) * LANES]
        t = t * cosn + pltpu.roll(t, LANES - 16, 1) * sina + pltpu.roll(t, 16, 1) * sinb
        k_ref[:, i * LANES:(i + 1) * LANES] = t.astype(BF16)

    vt_ref[...] = y[:, N_Q + N_K:].T.astype(BF16)


def _proj_call(xs, mod4, g_pre, wp, tabs, gq_t, gk_n, layer, n_lat_tiles, tm):
    nb, t_all, _ = xs.shape
    cost, sint, cosn, sina, sinb = tabs
    n_ctx_row = mod4.shape[1] - 1
    del n_ctx_row

    def mod_idx(b, t):
        return (layer, jnp.where(t < n_lat_tiles, b, nb), 0, 0)

    return pl.pallas_call(
        _proj_kernel,
        out_shape=(jax.ShapeDtypeStruct((nb, N_Q, t_all), BF16),
                   jax.ShapeDtypeStruct((nb, t_all, N_K), BF16),
                   jax.ShapeDtypeStruct((nb, N_V, t_all), BF16)),
        grid=(nb, t_all // tm),
        in_specs=[
            pl.BlockSpec((None, tm, D), lambda b, t: (b, t, 0)),
            pl.BlockSpec((None, None, 1, 3 * D), mod_idx),
            pl.BlockSpec((None, 1, D), lambda b, t: (layer, 0, 0)),
            pl.BlockSpec((None, D, N_P), lambda b, t: (layer, 0, 0)),
            pl.BlockSpec((HD, tm), lambda b, t: (0, t)),
            pl.BlockSpec((HD, tm), lambda b, t: (0, t)),
            pl.BlockSpec((tm, LANES), lambda b, t: (t, 0)),
            pl.BlockSpec((tm, LANES), lambda b, t: (t, 0)),
            pl.BlockSpec((tm, LANES), lambda b, t: (t, 0)),
            pl.BlockSpec((None, HD, tm), lambda b, t: (layer, 0, 0)),
            pl.BlockSpec((None, 1, LANES), lambda b, t: (layer, 0, 0)),
        ],
        out_specs=(pl.BlockSpec((None, N_Q, tm), lambda b, t: (b, 0, t)),
                   pl.BlockSpec((None, tm, N_K), lambda b, t: (b, t, 0)),
                   pl.BlockSpec((None, N_V, tm), lambda b, t: (b, 0, t))),
        compiler_params=_cparams(2),
        name="proj",
    )(xs, mod4, g_pre, wp, cost, sint, cosn, sina, sinb, gq_t, gk_n)


KCHUNK = 256
ONES_ROWS = 16


def _flash(chunks, rhs, dv, extra=None):
    m = extra
    acc = None
    for k, vt, bias in chunks:
        s = jnp.dot(k, rhs, preferred_element_type=F32)
        if bias is not None:
            s = s + bias
        cm = jnp.max(s, axis=0, keepdims=True)
        m_new = cm if m is None else jnp.maximum(m, cm)
        p = jnp.exp2(s - m_new).astype(BF16)
        vt_aug = jnp.concatenate([vt, jnp.ones((ONES_ROWS, vt.shape[1]), BF16)], axis=0)
        pv = jnp.dot(vt_aug, p, preferred_element_type=F32)
        acc = pv if acc is None else acc * jnp.exp2(m - m_new) + pv
        m = m_new
    l = acc[dv:dv + 1]
    if extra is not None:
        l = l + jnp.exp2(extra - m)
    return acc[:dv] * (1.0 / l)


def _key_chunks(k_ref, vt_ref, lo, n_keys, k_lanes=slice(None), v_rows=slice(None)):
    return [(k_ref[lo + c * KCHUNK:lo + (c + 1) * KCHUNK, k_lanes],
             vt_ref[v_rows, lo + c * KCHUNK:lo + (c + 1) * KCHUNK], None) for c in range(n_keys // KCHUNK)]


def _gqa_rhs(qt, j, tq):
    z = jnp.zeros((HD, tq), qt.dtype)
    first = j == 0
    cols = []
    for g in range(4):
        qg = qt[g * HD:(g + 1) * HD, :]
        cols.append(jnp.concatenate([jnp.where(first, qg, z), jnp.where(first, z, qg)], axis=0))
    return jnp.concatenate(cols, axis=1)


def _store_heads(o_ref, ot, tq, nheads):
    o = jnp.concatenate([ot[:, g * tq:(g + 1) * tq] for g in range(nheads)], axis=0)
    o_ref[...] = o.T.astype(o_ref.dtype)


def _sink_row(sink_ref, base, tq):
    blk = lax.broadcasted_iota(jnp.int32, (1, 4 * tq), 1) // tq
    row = jnp.zeros((1, 4 * tq), F32)
    for g in range(4):
        row = jnp.where(blk == g, sink_ref[base + g], row)
    return row * LOG2E


def _tile(t, size):
    if isinstance(t, int):
        return pl.ds(t * size, size)
    return pl.ds(pl.multiple_of(t * size, size), size)


def _two_stage(score_chunks, value_chunks, rhs_next, s_next, s_cur, m_cur, extra_next=None, extra_cur=None,
               lag=2):
    cm = None
    acc = None
    parts = []
    for c in range(len(score_chunks or value_chunks)):
        rows = slice(c * KCHUNK, (c + 1) * KCHUNK)
        if score_chunks is not None:
            k, bias = score_chunks[c]()
            s = jnp.dot(k, rhs_next, preferred_element_type=F32)
            if bias is not None:
                s = s + bias
            s_next[rows, :] = s
            parts.append(jnp.max(s, axis=0, keepdims=True))
            cm = parts[-1] if cm is None else jnp.maximum(cm, parts[-1])
        if value_chunks is not None:
            m_c = m_cur
            if score_chunks is not None and c >= lag:
                m_c = jnp.maximum(m_cur, jnp.minimum(parts[c - lag], m_cur))
            p = jnp.exp2(s_cur[rows, :] - m_c).astype(BF16)
            vt_aug = jnp.concatenate([value_chunks[c](), jnp.ones((ONES_ROWS, KCHUNK), BF16)], axis=0)
            pv = jnp.dot(vt_aug, p, preferred_element_type=F32)
            acc = pv if acc is None else acc + pv
    if cm is not None and extra_next is not None:
        cm = jnp.maximum(cm, extra_next)
    ot = None
    if acc is not None:
        dv = acc.shape[0] - ONES_ROWS
        l = acc[dv:dv + 1]
        if extra_cur is not None:
            l = l + jnp.exp2(extra_cur - m_cur)
        ot = acc[:dv] * (1.0 / l)
    return cm, ot


def _pipeline(make_rhs, score_chunks, value_chunks, finish, s0, s1, n_units, extra=None, lag=2):
    ex = (lambda u: None) if extra is None else extra
    m0, _ = _two_stage(score_chunks(0), None, make_rhs(0), s0, None, None, ex(0), None)

    def body(i, m_even):
        u = 2 * i
        m_odd, ot = _two_stage(score_chunks(u + 1), value_chunks(u), make_rhs(u + 1), s1, s0, m_even,
                               ex(u + 1), ex(u), lag)
        finish(u, ot)
        u2 = jnp.minimum(u + 2, n_units - 1)
        m_next, ot = _two_stage(score_chunks(u2), value_chunks(u + 1), make_rhs(u2), s0, s1, m_odd,
                                ex(u2), ex(u + 1), lag)
        finish(u + 1, ot)
        return m_next

    lax.fori_loop(0, n_units // 2, body, m0)


def _split_unit(u, n_tiles):
    if isinstance(u, int):
        return u // n_tiles, u % n_tiles
    return lax.div(u, n_tiles), lax.rem(u, n_tiles)


def _rows(g, size):
    return _tile(g, size)


def _attn_a_kernel(qt_ref, k_ref, vt_ref, o_ref, s0, s1, *, tq, n_lat, n_ctx, s_len):
    t_all = k_ref.shape[0]
    n_chunks = t_all // KCHUNK

    def rhs_of(j, t):
        return _gqa_rhs(qt_ref[_rows(j, 4 * HD), _tile(t, tq)], j, tq)

    def store(j, t, ot):
        _store_heads(o_ref.at[_tile(t, tq), _rows(j, 4 * HD)], ot, tq, 4)

    def score_chunks(u):
        return [functools.partial(lambda c: (k_ref[c * KCHUNK:(c + 1) * KCHUNK, :], None), c)
                for c in range(n_chunks)]

    def value_chunks(u):
        j, _ = _split_unit(u, n_lat)
        return [functools.partial(lambda c: vt_ref[_rows(j, HD), c * KCHUNK:(c + 1) * KCHUNK], c)
                for c in range(n_chunks)]

    _pipeline(lambda u: rhs_of(*_split_unit(u, n_lat)), score_chunks, value_chunks,
              lambda u, ot: store(*_split_unit(u, n_lat), ot), s0, s1, 2 * n_lat)
    for j in range(2):
        for t in range(n_lat, n_lat + n_ctx):
            chunks = _key_chunks(k_ref, vt_ref, s_len, t_all - s_len, v_rows=_rows(j, HD))
            store(j, t, _flash(chunks, rhs_of(j, t), HD))


def _attn_c_kernel(sink_ref, bias_ref, qt_ref, k_ref, vt_ref, o_ref, s0, s1, *, tq, n_lat, n_ctx, s_len):
    t_all = k_ref.shape[0]
    wk = tq + 2 * WINDOW

    def rhs_of(j, t):
        return _gqa_rhs(qt_ref[_rows(j, 4 * HD), _tile(t, tq)], j, tq)

    def store(j, t, ot):
        _store_heads(o_ref.at[_tile(t, tq), _rows(j, 4 * HD)], ot, tq, 4)

    def sink_of(u):
        j, _ = _split_unit(u, n_lat)
        return _sink_row(sink_ref, j * 4, tq)

    def window(t):
        q0 = t * tq
        start = jnp.clip(q0 - WINDOW, 0, s_len - wk)
        return start, (q0 - start) // WINDOW

    def local(start, c):
        return pl.ds(pl.multiple_of(start + c * KCHUNK, LANES), KCHUNK)

    def score_chunks(u):
        _, t = _split_unit(u, n_lat)
        start, variant = window(t)

        def loc(c):
            b = bias_ref[variant, c * KCHUNK:(c + 1) * KCHUNK, :]
            return k_ref[local(start, c), :], jnp.concatenate([b] * 4, axis=1)

        return ([lambda: (k_ref[s_len:s_len + KCHUNK, :], None)]
                + [functools.partial(loc, c) for c in range(wk // KCHUNK)])

    def value_chunks(u):
        j, t = _split_unit(u, n_lat)
        start, _ = window(t)
        return ([lambda: vt_ref[_rows(j, HD), s_len:s_len + KCHUNK]]
                + [functools.partial(lambda c: vt_ref[_rows(j, HD), local(start, c)], c)
                   for c in range(wk // KCHUNK)])

    _pipeline(lambda u: rhs_of(*_split_unit(u, n_lat)), score_chunks, value_chunks,
              lambda u, ot: store(*_split_unit(u, n_lat), ot), s0, s1, 2 * n_lat, extra=sink_of, lag=1)
    for j in range(2):
        for t in range(n_lat, n_lat + n_ctx):
            chunks = _key_chunks(k_ref, vt_ref, s_len, t_all - s_len, v_rows=_rows(j, HD))
            store(j, t, _flash(chunks, rhs_of(j, t), HD, extra=_sink_row(sink_ref, j * 4, tq)))


def _attn_b_kernel(lamv_ref, subln_ref, qt_ref, k_ref, vt_ref, o_ref, s0, s1, *, tq, n_lat, n_ctx, s_len,
                   lam_init):
    lv = lamv_ref[...]
    lam = (jnp.exp(jnp.sum(lv[0:1] * lv[1:2], axis=-1, keepdims=True))
           - jnp.exp(jnp.sum(lv[2:3] * lv[3:4], axis=-1, keepdims=True)) + lam_init)
    t_all = k_ref.shape[0]
    n_chunks = t_all // KCHUNK
    hw = 2 * HD

    def rhs_of(h, t):
        qt = qt_ref[_rows(h, hw), _tile(t, tq)]
        z = jnp.zeros((HD, tq), qt.dtype)
        return jnp.concatenate([jnp.concatenate([qt[:HD], z], axis=0),
                                jnp.concatenate([z, qt[HD:]], axis=0)], axis=1)

    def store(h, t, o2):
        o = o2[:, :tq] - lam * o2[:, tq:]
        ms = jnp.mean(o * o, axis=0, keepdims=True)
        o = o * lax.rsqrt(ms + SUBLN_EPS) * subln_ref[...] * (1.0 - lam_init)
        o_ref[_tile(t, tq), _rows(h, hw)] = o.T.astype(o_ref.dtype)

    def score_chunks(u):
        h, _ = _split_unit(u, n_lat)
        return [functools.partial(lambda c: (k_ref[c * KCHUNK:(c + 1) * KCHUNK, _rows(h, hw)], None), c)
                for c in range(n_chunks)]

    def value_chunks(u):
        h, _ = _split_unit(u, n_lat)
        return [functools.partial(lambda c: vt_ref[_rows(h, hw), c * KCHUNK:(c + 1) * KCHUNK], c)
                for c in range(n_chunks)]

    _pipeline(lambda u: rhs_of(*_split_unit(u, n_lat)), score_chunks, value_chunks,
              lambda u, ot: store(*_split_unit(u, n_lat), ot), s0, s1, 4 * n_lat)
    for h in range(4):
        for t in range(n_lat, n_lat + n_ctx):
            chunks = _key_chunks(k_ref, vt_ref, s_len, t_all - s_len, k_lanes=_rows(h, hw), v_rows=_rows(h, hw))
            store(h, t, _flash(chunks, rhs_of(h, t), hw))


def _window_bias(tq):
    wk = tq + 2 * WINDOW
    r = np.arange(wk)[:, None]
    c = np.arange(tq)[None, :]
    out = np.stack([np.where(np.abs(c - r + v * WINDOW) <= WINDOW, 0.0, -np.inf) for v in range(3)])
    return jnp.asarray(out, F32)


def _attn_calls(qt_all, k_all, vt_all, sink, lamv, subln_t, layer, with_ctx, s_len, lam_init,
                tq_a, tq_b, tq_c):
    nb, _, t_all = qt_all.shape
    c_len = t_all - s_len
    o_rows = t_all if with_ctx else s_len
    o_shape = jax.ShapeDtypeStruct((nb, o_rows, QW), BF16)

    def steps(tq):
        n_lat = s_len // tq
        return n_lat, n_lat + (c_len // tq if with_ctx else 0)

    whole = lambda rows, cols, r, c: pl.BlockSpec((None, rows, cols), lambda b: (b, r, c))
    o_spec = pl.BlockSpec((None, o_rows, QW), lambda b: (b, 0, 0))
    score_scratch = lambda rows, n: [pltpu.VMEM((rows, n), F32), pltpu.VMEM((rows, n), F32)]

    n_lat, n_all = steps(tq_a)
    oa = pl.pallas_call(
        functools.partial(_attn_a_kernel, tq=tq_a, n_lat=n_lat, n_ctx=n_all - n_lat, s_len=s_len),
        out_shape=o_shape,
        grid=(nb,),
        in_specs=[whole(QW, t_all, 0, 0), whole(t_all, LANES, 0, KA_TILE), whole(LANES, t_all, KA_TILE, 0)],
        out_specs=o_spec,
        scratch_shapes=score_scratch(t_all, 4 * tq_a),
        compiler_params=_cparams(1),
        name="attn_a",
    )(qt_all, k_all, vt_all)

    n_lat, n_all = steps(tq_b)
    ob = pl.pallas_call(
        functools.partial(_attn_b_kernel, tq=tq_b, n_lat=n_lat, n_ctx=n_all - n_lat, s_len=s_len,
                          lam_init=lam_init),
        out_shape=o_shape,
        grid=(nb,),
        in_specs=[
            pl.BlockSpec((None, 4, HD), lambda b: (layer, 0, 0)),
            pl.BlockSpec((None, 2 * HD, tq_b), lambda b: (layer, 0, 0)),
            whole(QW, t_all, 1, 0), whole(t_all, QW, 0, 0), whole(QW, t_all, 0, 0),
        ],
        out_specs=o_spec,
        scratch_shapes=score_scratch(t_all, 2 * tq_b),
        compiler_params=_cparams(1),
        name="attn_b",
    )(lamv, subln_t, qt_all, k_all, vt_all)

    n_lat, n_all = steps(tq_c)
    wk_c = tq_c + 2 * WINDOW
    oc = pl.pallas_call(
        functools.partial(_attn_c_kernel, tq=tq_c, n_lat=n_lat, n_ctx=n_all - n_lat, s_len=s_len),
        out_shape=o_shape,
        grid=(nb,),
        in_specs=[
            pl.BlockSpec(memory_space=pltpu.SMEM),
            pl.BlockSpec((3, wk_c, tq_c), lambda b: (0, 0, 0)),
            whole(QW, t_all, 2, 0), whole(t_all, LANES, 0, KA_TILE + 1), whole(LANES, t_all, KA_TILE + 1, 0),
        ],
        out_specs=o_spec,
        scratch_shapes=score_scratch(c_len + wk_c, 4 * tq_c),
        compiler_params=_cparams(1),
        name="attn_c",
    )(sink[layer], _window_bias(tq_c), qt_all, k_all, vt_all)
    return oa, ob, oc


def _merge_kernel(x_ref, oa_ref, ob_ref, oc_ref, mod_ref, gpre_ref, gpost_ref, wgm_ref, bmg_ref,
                  wbr_ref, wout_ref, out_ref):
    x = x_ref[...]
    mod = mod_ref[...]
    h = _modulated_norm(x, mod, gpre_ref[...])
    gm = jnp.dot(h.astype(BF16), wgm_ref[...], preferred_element_type=F32)
    z = None
    for i, o_ref in enumerate((oa_ref, ob_ref, oc_ref)):
        g = gm[:, i * QW:(i + 1) * QW]
        u = (o_ref[...].astype(F32) * (g * _sigmoid(g))).astype(BF16)
        p = jnp.dot(u, wbr_ref[i], preferred_element_type=F32)
        mg = _sigmoid(gm[:, N_G + i * D:N_G + (i + 1) * D] + bmg_ref[:, i * D:(i + 1) * D])
        z = mg * p if z is None else z + mg * p
    y = jnp.dot(z.astype(BF16), wout_ref[...], preferred_element_type=F32)
    ms = jnp.mean(y * y, axis=-1, keepdims=True)
    gate = mod[:, 2 * D:]
    out_ref[...] = x + gate * (y * lax.rsqrt(ms + EPS) * gpost_ref[...])


def _merge_call(xs, oa, ob, oc, mod4, g_pre, g_post, wgm, b_mg, wbr, wout, layer, n_lat_tiles,
                with_ctx, tm):
    nb, t_all, _ = xs.shape
    n_tiles = t_all // tm if with_ctx else n_lat_tiles

    def mod_idx(b, t):
        return (layer, jnp.where(t < n_lat_tiles, b, nb), 0, 0)

    tok = lambda b, t: (b, t, 0)
    lay2 = lambda b, t: (layer, 0, 0)
    return pl.pallas_call(
        _merge_kernel,
        out_shape=jax.ShapeDtypeStruct((nb, n_tiles * tm, D), F32),
        grid=(nb, n_tiles),
        in_specs=[
            pl.BlockSpec((None, tm, D), tok),
            pl.BlockSpec((None, tm, QW), tok),
            pl.BlockSpec((None, tm, QW), tok),
            pl.BlockSpec((None, tm, QW), tok),
            pl.BlockSpec((None, None, 1, 3 * D), mod_idx),
            pl.BlockSpec((None, 1, D), lay2),
            pl.BlockSpec((None, 1, D), lay2),
            pl.BlockSpec((None, D, N_G + N_M), lay2),
            pl.BlockSpec((None, 1, N_M), lay2),
            pl.BlockSpec((None, 3, QW, D), lambda b, t: (layer, 0, 0, 0)),
            pl.BlockSpec((None, D, D), lay2),
        ],
        out_specs=pl.BlockSpec((None, tm, D), tok),
        compiler_params=_cparams(2),
        name="merge",
    )(xs, oa, ob, oc, mod4, g_pre, g_post, wgm, b_mg, wbr, wout)


def _rope_tables(s_len, c_len):
    rows = s_len // GRID_W
    row = jnp.repeat(jnp.arange(rows), GRID_W).astype(F32)
    col = jnp.tile(jnp.arange(GRID_W), rows).astype(F32)
    freqs = ROPE_THETA ** (-jnp.arange(ROPE_PAIRS, dtype=F32) / ROPE_PAIRS)
    ang_r = row[:, None] * freqs
    ang_c = col[:, None] * freqs
    ang = jnp.concatenate([ang_r, ang_r, ang_c, ang_c], axis=-1)
    cos = jnp.concatenate([jnp.cos(ang), jnp.ones((c_len, HD), F32)], axis=0)
    sin = jnp.concatenate([jnp.sin(ang), jnp.zeros((c_len, HD), F32)], axis=0)
    first = (np.arange(HD) % 32) < 16
    sin_a = jnp.where(first, -sin, 0.0)
    sin_b = jnp.where(first, 0.0, sin)
    tile2 = lambda a: jnp.concatenate([a, a], axis=-1)
    cost = cos.T
    sint = (sin_a + sin_b).T
    return cost, sint, tile2(cos), tile2(sin_a), tile2(sin_b)


def _cols(w, names):
    return jnp.concatenate([w[..., _IN[n][0]:_IN[n][1]] for n in names], axis=-1)


def kernel(x, c, ctx, c_ctx, w_ada, b_ada, g_pre, g_post, w_in, q_norm, k_norm, lam_q1, lam_k1, lam_q2,
           lam_k2, subln, sink, w_br_a, w_br_b, w_br_c, w_mg, b_mg, w_out):
    nb, s_len, _ = x.shape
    c_len = ctx.shape[1]
    depth = w_in.shape[0]
    tm = 256
    tq_a, tq_b, tq_c = 128, 256, 256
    n_lat_tiles = s_len // tm

    wp = _cols(w_in, ("qa", "qb", "qc", "kb", "ka", "kc", "vb", "va", "vc")).astype(BF16)
    wgm = jnp.concatenate([_cols(w_in, ("ga", "gb", "gc")), w_mg], axis=-1).astype(BF16)
    wbr = jnp.stack([w_br_a, w_br_b, w_br_c], axis=1).astype(BF16)
    wout = w_out.astype(BF16)

    tabs = _rope_tables(s_len, c_len)
    gq_t = jnp.broadcast_to(q_norm[:, :, None], (depth, HD, tm))
    gk_n = jnp.concatenate([k_norm, k_norm], axis=-1)[:, None, :]
    lamv = jnp.stack([lam_q1, lam_k1, lam_q2, lam_k2], axis=1)
    subln_t = jnp.broadcast_to(subln[:, :, None], (depth, 2 * HD, tq_b))
    g_pre3 = g_pre[:, None, :]
    g_post3 = g_post[:, None, :]
    b_mg3 = b_mg[:, None, :]

    rows = ((nb + 1 + 7) // 8) * 8
    sc_in = jnp.concatenate([c, c_ctx[None, :], jnp.zeros((rows - nb - 1, D), F32)], axis=0)
    mod = _ada_call(sc_in, w_ada, b_ada)
    mod4 = mod[:, :, None, :]

    xs = jnp.concatenate([x, ctx], axis=1)
    for layer in range(depth):
        last = layer == depth - 1
        lam_init = 0.8 - 0.6 * math.exp(-0.3 * layer)
        qt_all, k_all, vt_all = _proj_call(xs, mod4, g_pre3, wp, tabs, gq_t, gk_n, layer, n_lat_tiles, tm)
        oa, ob, oc = _attn_calls(qt_all, k_all, vt_all, sink, lamv, subln_t, layer, not last, s_len,
                                 lam_init, tq_a, tq_b, tq_c)
        xs = _merge_call(xs, oa, ob, oc, mod4, g_pre3, g_post3, wgm, b_mg3, wbr, wout, layer,
                         n_lat_tiles, not last, tm)
    return xs
```

```python
import functools
import math

import jax
import jax.numpy as jnp
import numpy as np
from jax import lax
from jax.experimental import pallas as pl
from jax.experimental.pallas import tpu as pltpu

F32 = jnp.float32
BF16 = jnp.bfloat16

D = 1024
HD = 64
GRID_W = 64
WINDOW = 128
ROPE_THETA = 10000.0
ROPE_PAIRS = HD // 4
EPS = 1e-6
SUBLN_EPS = 1e-5
ATTN_SCALE = HD ** -0.5
LOG2E = math.log2(math.e)

QW = 512
N_Q = 3 * QW
N_K = 512 + 128 + 128
N_V = 512 + 128 + 128
KA_TILE = 4
N_P = N_Q + N_K + N_V
N_G = 3 * QW
N_M = 3 * D

LANES = 128
VMEM_LIMIT = 56 * 1024 * 1024

_IN = dict(qa=(0, 512), ka=(512, 640), va=(640, 768), ga=(768, 1280),
           qb=(1280, 1792), kb=(1792, 2304), vb=(2304, 2816), gb=(2816, 3328),
           qc=(3328, 3840), kc=(3840, 3968), vc=(3968, 4096), gc=(4096, 4608))


def _sigmoid(v):
    return 1.0 / (1.0 + jnp.exp(-v))


def _cparams(n_axes):
    return pltpu.CompilerParams(dimension_semantics=("arbitrary",) * n_axes,
                                vmem_limit_bytes=VMEM_LIMIT)


def _ada_kernel(sc_ref, w_ref, b_ref, o_ref):
    v = sc_ref[...]
    s = (v * _sigmoid(v)).astype(BF16)
    o_ref[...] = jnp.dot(s, w_ref[...].astype(BF16), preferred_element_type=F32) + b_ref[...]


def _ada_call(sc_in, w_ada, b_ada):
    depth = w_ada.shape[0]
    rows = sc_in.shape[0]
    nblk = 3
    return pl.pallas_call(
        _ada_kernel,
        out_shape=jax.ShapeDtypeStruct((depth, rows, 3 * D), F32),
        grid=(depth, nblk),
        in_specs=[
            pl.BlockSpec((rows, D), lambda l, n: (0, 0)),
            pl.BlockSpec((None, D, D), lambda l, n: (l, 0, n)),
            pl.BlockSpec((None, 1, D), lambda l, n: (l, 0, n)),
        ],
        out_specs=pl.BlockSpec((None, rows, D), lambda l, n: (l, 0, n)),
        compiler_params=_cparams(2),
        name="adaln",
    )(sc_in, w_ada, b_ada.reshape(depth, 1, 3 * D))


def _modulated_norm(x, mod, gpre):
    shift = mod[:, :D]
    scale = mod[:, D:2 * D]
    ms = jnp.mean(x * x, axis=-1, keepdims=True)
    return x * lax.rsqrt(ms + EPS) * gpre * (1.0 + scale) + shift


def _proj_kernel(x_ref, modb_ref, modc_ref, gpre_ref, w_ref, cost_ref, sint_ref, cosn_ref, sina_ref,
                 sinb_ref, gq_ref, gk_ref, qt_ref, k_ref, vt_ref, *, sub, s_len):
    tm = x_ref.shape[0]
    row0 = pl.program_id(1) * tm

    def project(i):
        rows = slice(i * sub, (i + 1) * sub)
        mod = jnp.where(row0 + i * sub >= s_len, modc_ref[...], modb_ref[...])
        h = _modulated_norm(x_ref[rows, :], mod, gpre_ref[...])
        return jnp.dot(h.astype(BF16), w_ref[...], preferred_element_type=F32)

    def finish(i, y):
        rows = slice(i * sub, (i + 1) * sub)
        q3 = y[:, :N_Q].T.reshape(N_Q // HD, HD, sub)
        qa = q3[:8]
        ss = jnp.sum(qa * qa, axis=1, keepdims=True)
        qa = qa * lax.rsqrt(ss * (1.0 / HD) + EPS) * gq_ref[...][None]
        q3 = jnp.concatenate([qa, q3[8:]], axis=0)
        rot = jnp.concatenate([q3[:, 16:32], q3[:, 0:16], q3[:, 48:64], q3[:, 32:48]], axis=1)
        q3 = (q3 * cost_ref[:, rows][None] + rot * sint_ref[:, rows][None]) * (ATTN_SCALE * LOG2E)
        qt_ref[:, rows] = q3.reshape(N_Q, sub).astype(BF16)

        ka = y[:, N_Q + KA_TILE * LANES:N_Q + (KA_TILE + 1) * LANES]
        lane = lax.broadcasted_iota(jnp.int32, (1, LANES), 1)
        lo = lane < HD
        sq = ka * ka
        s_lo = jnp.sum(jnp.where(lo, sq, 0.0), axis=-1, keepdims=True)
        s_hi = jnp.sum(jnp.where(lo, 0.0, sq), axis=-1, keepdims=True)
        r = jnp.where(lo, lax.rsqrt(s_lo * (1.0 / HD) + EPS), lax.rsqrt(s_hi * (1.0 / HD) + EPS))
        ka = ka * r * gk_ref[...]
        cosn = cosn_ref[rows, :]
        sina = sina_ref[rows, :]
        sinb = sinb_ref[rows, :]
        for j in range(N_K // LANES):
            t = ka if j == KA_TILE else y[:, N_Q + j * LANES:N_Q + (j + 1) * LANES]
            t = t * cosn + pltpu.roll(t, LANES - 16, 1) * sina + pltpu.roll(t, 16, 1) * sinb
            k_ref[rows, j * LANES:(j + 1) * LANES] = t.astype(BF16)

        vt_ref[:, rows] = y[:, N_Q + N_K:].T.astype(BF16)

    y_prev = project(0)
    for i in range(1, tm // sub):
        y = project(i)
        finish(i - 1, y_prev)
        y_prev = y
    finish(tm // sub - 1, y_prev)


def _proj_call(xs, mod4, g_pre, wp, tabs, gq_t, gk_n, layer, s_len, tm, sub):
    nb, t_all, _ = xs.shape
    cost, sint, cosn, sina, sinb = tabs
    return pl.pallas_call(
        functools.partial(_proj_kernel, sub=sub, s_len=s_len),
        out_shape=(jax.ShapeDtypeStruct((nb, N_Q, t_all), BF16),
                   jax.ShapeDtypeStruct((nb, t_all, N_K), BF16),
                   jax.ShapeDtypeStruct((nb, N_V, t_all), BF16)),
        grid=(nb, t_all // tm),
        in_specs=[
            pl.BlockSpec((None, tm, D), lambda b, t: (b, t, 0)),
            pl.BlockSpec((None, None, 1, 3 * D), lambda b, t: (layer, b, 0, 0)),
            pl.BlockSpec((None, None, 1, 3 * D), lambda b, t: (layer, nb, 0, 0)),
            pl.BlockSpec((None, 1, D), lambda b, t: (layer, 0, 0)),
            pl.BlockSpec((None, D, N_P), lambda b, t: (layer, 0, 0)),
            pl.BlockSpec((HD, tm), lambda b, t: (0, t)),
            pl.BlockSpec((HD, tm), lambda b, t: (0, t)),
            pl.BlockSpec((tm, LANES), lambda b, t: (t, 0)),
            pl.BlockSpec((tm, LANES), lambda b, t: (t, 0)),
            pl.BlockSpec((tm, LANES), lambda b, t: (t, 0)),
            pl.BlockSpec((None, HD, sub), lambda b, t: (layer, 0, 0)),
            pl.BlockSpec((None, 1, LANES), lambda b, t: (layer, 0, 0)),
        ],
        out_specs=(pl.BlockSpec((None, N_Q, tm), lambda b, t: (b, 0, t)),
                   pl.BlockSpec((None, tm, N_K), lambda b, t: (b, t, 0)),
                   pl.BlockSpec((None, N_V, tm), lambda b, t: (b, 0, t))),
        compiler_params=_cparams(2),
        name="proj",
    )(xs, mod4, mod4, g_pre, wp, cost, sint, cosn, sina, sinb, gq_t, gk_n)


KCHUNK = 256
ONES_ROWS = 16


def _flash(chunks, rhs, dv, extra=None):
    m = extra
    acc = None
    for k, vt, bias in chunks:
        s = jnp.dot(k, rhs, preferred_element_type=F32)
        if bias is not None:
            s = s + bias
        cm = jnp.max(s, axis=0, keepdims=True)
        m_new = cm if m is None else jnp.maximum(m, cm)
        p = jnp.exp2(s - m_new).astype(BF16)
        vt_aug = jnp.concatenate([vt, jnp.ones((ONES_ROWS, vt.shape[1]), BF16)], axis=0)
        pv = jnp.dot(vt_aug, p, preferred_element_type=F32)
        acc = pv if acc is None else acc * jnp.exp2(m - m_new) + pv
        m = m_new
    l = acc[dv:dv + 1]
    if extra is not None:
        l = l + jnp.exp2(extra - m)
    return acc[:dv] * (1.0 / l)


def _key_chunks(k_ref, vt_ref, lo, n_keys, k_lanes=slice(None), v_rows=slice(None)):
    return [(k_ref[lo + c * KCHUNK:lo + (c + 1) * KCHUNK, k_lanes],
             vt_ref[v_rows, lo + c * KCHUNK:lo + (c + 1) * KCHUNK], None) for c in range(n_keys // KCHUNK)]


def _gqa_rhs(qt, j, tq):
    z = jnp.zeros((HD, tq), qt.dtype)
    first = j == 0
    cols = []
    for g in range(4):
        qg = qt[g * HD:(g + 1) * HD, :]
        cols.append(jnp.concatenate([jnp.where(first, qg, z), jnp.where(first, z, qg)], axis=0))
    return jnp.concatenate(cols, axis=1)


def _store_heads(o_ref, ot, tq, nheads):
    o = jnp.concatenate([ot[:, g * tq:(g + 1) * tq] for g in range(nheads)], axis=0)
    o_ref[...] = o.T.astype(o_ref.dtype)


def _sink_row(sink_ref, base, tq):
    blk = lax.broadcasted_iota(jnp.int32, (1, 4 * tq), 1) // tq
    row = jnp.zeros((1, 4 * tq), F32)
    for g in range(4):
        row = jnp.where(blk == g, sink_ref[base + g], row)
    return row * LOG2E


def _tile(t, size):
    if isinstance(t, int):
        return pl.ds(t * size, size)
    return pl.ds(pl.multiple_of(t * size, size), size)


def _two_stage(score_chunks, value_chunks, rhs_next, s_next, s_cur, m_cur, extra_next=None, extra_cur=None,
               lag=2):
    cm = None
    acc = None
    parts = []
    for c in range(len(score_chunks or value_chunks)):
        rows = slice(c * KCHUNK, (c + 1) * KCHUNK)
        if score_chunks is not None:
            k, bias = score_chunks[c]()
            s = jnp.dot(k, rhs_next, preferred_element_type=F32)
            if bias is not None:
                s = s + bias
            s_next[rows, :] = s
            parts.append(jnp.max(s, axis=0, keepdims=True))
            cm = parts[-1] if cm is None else jnp.maximum(cm, parts[-1])
        if value_chunks is not None:
            m_c = m_cur
            if score_chunks is not None and c >= lag:
                m_c = jnp.maximum(m_cur, jnp.minimum(parts[c - lag], m_cur))
            p = jnp.exp2(s_cur[rows, :] - m_c).astype(BF16)
            vt_aug = jnp.concatenate([value_chunks[c](), jnp.ones((ONES_ROWS, KCHUNK), BF16)], axis=0)
            pv = jnp.dot(vt_aug, p, preferred_element_type=F32)
            acc = pv if acc is None else acc + pv
    if cm is not None and extra_next is not None:
        cm = jnp.maximum(cm, extra_next)
    ot = None
    if acc is not None:
        dv = acc.shape[0] - ONES_ROWS
        l = acc[dv:dv + 1]
        if extra_cur is not None:
            l = l + jnp.exp2(extra_cur - m_cur)
        ot = acc[:dv] * (1.0 / l)
    return cm, ot


def _pipeline(make_rhs, score_chunks, value_chunks, finish, s0, s1, n_units, extra=None, lag=2):
    ex = (lambda u: None) if extra is None else extra
    m0, _ = _two_stage(score_chunks(0), None, make_rhs(0), s0, None, None, ex(0), None)

    def body(i, m_even):
        u = 2 * i
        m_odd, ot = _two_stage(score_chunks(u + 1), value_chunks(u), make_rhs(u + 1), s1, s0, m_even,
                               ex(u + 1), ex(u), lag)
        finish(u, ot)
        u2 = jnp.minimum(u + 2, n_units - 1)
        m_next, ot = _two_stage(score_chunks(u2), value_chunks(u + 1), make_rhs(u2), s0, s1, m_odd,
                                ex(u2), ex(u + 1), lag)
        finish(u + 1, ot)
        return m_next

    lax.fori_loop(0, n_units // 2, body, m0)


def _split_unit(u, n_tiles):
    if isinstance(u, int):
        return u // n_tiles, u % n_tiles
    return lax.div(u, n_tiles), lax.rem(u, n_tiles)


def _rows(g, size):
    return _tile(g, size)


def _attn_a_kernel(qt_ref, k_ref, vt_ref, o_ref, s0, s1, *, tq, n_lat, n_ctx, s_len):
    t_all = k_ref.shape[0]
    n_chunks = t_all // KCHUNK

    def rhs_of(j, t):
        return _gqa_rhs(qt_ref[_rows(j, 4 * HD), _tile(t, tq)], j, tq)

    def store(j, t, ot):
        _store_heads(o_ref.at[_tile(t, tq), _rows(j, 4 * HD)], ot, tq, 4)

    def score_chunks(u):
        return [functools.partial(lambda c: (k_ref[c * KCHUNK:(c + 1) * KCHUNK, :], None), c)
                for c in range(n_chunks)]

    def value_chunks(u):
        j, _ = _split_unit(u, n_lat)
        return [functools.partial(lambda c: vt_ref[_rows(j, HD), c * KCHUNK:(c + 1) * KCHUNK], c)
                for c in range(n_chunks)]

    _pipeline(lambda u: rhs_of(*_split_unit(u, n_lat)), score_chunks, value_chunks,
              lambda u, ot: store(*_split_unit(u, n_lat), ot), s0, s1, 2 * n_lat)
    for j in range(2):
        for t in range(n_lat, n_lat + n_ctx):
            chunks = _key_chunks(k_ref, vt_ref, s_len, t_all - s_len, v_rows=_rows(j, HD))
            store(j, t, _flash(chunks, rhs_of(j, t), HD))


def _attn_c_kernel(sink_ref, bias_ref, qt_ref, k_ref, vt_ref, o_ref, s0, s1, *, tq, n_lat, n_ctx, s_len):
    t_all = k_ref.shape[0]
    wk = tq + 2 * WINDOW

    def rhs_of(j, t):
        return _gqa_rhs(qt_ref[_rows(j, 4 * HD), _tile(t, tq)], j, tq)

    def store(j, t, ot):
        _store_heads(o_ref.at[_tile(t, tq), _rows(j, 4 * HD)], ot, tq, 4)

    def sink_of(u):
        j, _ = _split_unit(u, n_lat)
        return _sink_row(sink_ref, j * 4, tq)

    def window(t):
        q0 = t * tq
        start = jnp.clip(q0 - WINDOW, 0, s_len - wk)
        return start, (q0 - start) // WINDOW

    def local(start, c):
        return pl.ds(pl.multiple_of(start + c * KCHUNK, LANES), KCHUNK)

    def score_chunks(u):
        _, t = _split_unit(u, n_lat)
        start, variant = window(t)

        def loc(c):
            b = bias_ref[variant, c * KCHUNK:(c + 1) * KCHUNK, :]
            return k_ref[local(start, c), :], jnp.concatenate([b] * 4, axis=1)

        return ([lambda: (k_ref[s_len:s_len + KCHUNK, :], None)]
                + [functools.partial(loc, c) for c in range(wk // KCHUNK)])

    def value_chunks(u):
        j, t = _split_unit(u, n_lat)
        start, _ = window(t)
        return ([lambda: vt_ref[_rows(j, HD), s_len:s_len + KCHUNK]]
                + [functools.partial(lambda c: vt_ref[_rows(j, HD), local(start, c)], c)
                   for c in range(wk // KCHUNK)])

    _pipeline(lambda u: rhs_of(*_split_unit(u, n_lat)), score_chunks, value_chunks,
              lambda u, ot: store(*_split_unit(u, n_lat), ot), s0, s1, 2 * n_lat, extra=sink_of, lag=1)
    for j in range(2):
        for t in range(n_lat, n_lat + n_ctx):
            chunks = _key_chunks(k_ref, vt_ref, s_len, t_all - s_len, v_rows=_rows(j, HD))
            store(j, t, _flash(chunks, rhs_of(j, t), HD, extra=_sink_row(sink_ref, j * 4, tq)))


def _attn_b_kernel(lamv_ref, subln_ref, qt_ref, k_ref, vt_ref, o_ref, s0, s1, *, tq, n_lat, n_ctx, s_len,
                   lam_init):
    lv = lamv_ref[...]
    lam = (jnp.exp(jnp.sum(lv[0:1] * lv[1:2], axis=-1, keepdims=True))
           - jnp.exp(jnp.sum(lv[2:3] * lv[3:4], axis=-1, keepdims=True)) + lam_init)
    t_all = k_ref.shape[0]
    n_chunks = t_all // KCHUNK
    hw = 2 * HD

    def rhs_of(h, t):
        qt = qt_ref[_rows(h, hw), _tile(t, tq)]
        z = jnp.zeros((HD, tq), qt.dtype)
        return jnp.concatenate([jnp.concatenate([qt[:HD], z], axis=0),
                                jnp.concatenate([z, qt[HD:]], axis=0)], axis=1)

    def store(h, t, o2):
        o = o2[:, :tq] - lam * o2[:, tq:]
        ms = jnp.mean(o * o, axis=0, keepdims=True)
        o = o * lax.rsqrt(ms + SUBLN_EPS) * subln_ref[...] * (1.0 - lam_init)
        o_ref[_tile(t, tq), _rows(h, hw)] = o.T.astype(o_ref.dtype)

    def score_chunks(u):
        h, _ = _split_unit(u, n_lat)
        return [functools.partial(lambda c: (k_ref[c * KCHUNK:(c + 1) * KCHUNK, _rows(h, hw)], None), c)
                for c in range(n_chunks)]

    def value_chunks(u):
        h, _ = _split_unit(u, n_lat)
        return [functools.partial(lambda c: vt_ref[_rows(h, hw), c * KCHUNK:(c + 1) * KCHUNK], c)
                for c in range(n_chunks)]

    _pipeline(lambda u: rhs_of(*_split_unit(u, n_lat)), score_chunks, value_chunks,
              lambda u, ot: store(*_split_unit(u, n_lat), ot), s0, s1, 4 * n_lat)
    for h in range(4):
        for t in range(n_lat, n_lat + n_ctx):
            chunks = _key_chunks(k_ref, vt_ref, s_len, t_all - s_len, k_lanes=_rows(h, hw), v_rows=_rows(h, hw))
            store(h, t, _flash(chunks, rhs_of(h, t), hw))


def _window_bias(tq):
    wk = tq + 2 * WINDOW
    r = np.arange(wk)[:, None]
    c = np.arange(tq)[None, :]
    out = np.stack([np.where(np.abs(c - r + v * WINDOW) <= WINDOW, 0.0, -np.inf) for v in range(3)])
    return jnp.asarray(out, F32)


def _attn_calls(qt_all, k_all, vt_all, sink, lamv, subln_t, layer, with_ctx, s_len, lam_init,
                tq_a, tq_b, tq_c):
    nb, _, t_all = qt_all.shape
    c_len = t_all - s_len
    o_rows = t_all if with_ctx else s_len
    o_shape = jax.ShapeDtypeStruct((nb, o_rows, QW), BF16)

    def steps(tq):
        n_lat = s_len // tq
        return n_lat, n_lat + (c_len // tq if with_ctx else 0)

    whole = lambda rows, cols, r, c: pl.BlockSpec((None, rows, cols), lambda b: (b, r, c))
    o_spec = pl.BlockSpec((None, o_rows, QW), lambda b: (b, 0, 0))
    score_scratch = lambda rows, n: [pltpu.VMEM((rows, n), F32), pltpu.VMEM((rows, n), F32)]

    n_lat, n_all = steps(tq_a)
    oa = pl.pallas_call(
        functools.partial(_attn_a_kernel, tq=tq_a, n_lat=n_lat, n_ctx=n_all - n_lat, s_len=s_len),
        out_shape=o_shape,
        grid=(nb,),
        in_specs=[whole(QW, t_all, 0, 0), whole(t_all, LANES, 0, KA_TILE), whole(LANES, t_all, KA_TILE, 0)],
        out_specs=o_spec,
        scratch_shapes=score_scratch(t_all, 4 * tq_a),
        compiler_params=_cparams(1),
        name="attn_a",
    )(qt_all, k_all, vt_all)

    n_lat, n_all = steps(tq_b)
    ob = pl.pallas_call(
        functools.partial(_attn_b_kernel, tq=tq_b, n_lat=n_lat, n_ctx=n_all - n_lat, s_len=s_len,
                          lam_init=lam_init),
        out_shape=o_shape,
        grid=(nb,),
        in_specs=[
            pl.BlockSpec((None, 4, HD), lambda b: (layer, 0, 0)),
            pl.BlockSpec((None, 2 * HD, tq_b), lambda b: (layer, 0, 0)),
            whole(QW, t_all, 1, 0), whole(t_all, QW, 0, 0), whole(QW, t_all, 0, 0),
        ],
        out_specs=o_spec,
        scratch_shapes=score_scratch(t_all, 2 * tq_b),
        compiler_params=_cparams(1),
        name="attn_b",
    )(lamv, subln_t, qt_all, k_all, vt_all)

    n_lat, n_all = steps(tq_c)
    wk_c = tq_c + 2 * WINDOW
    oc = pl.pallas_call(
        functools.partial(_attn_c_kernel, tq=tq_c, n_lat=n_lat, n_ctx=n_all - n_lat, s_len=s_len),
        out_shape=o_shape,
        grid=(nb,),
        in_specs=[
            pl.BlockSpec(memory_space=pltpu.SMEM),
            pl.BlockSpec((3, wk_c, tq_c), lambda b: (0, 0, 0)),
            whole(QW, t_all, 2, 0), whole(t_all, LANES, 0, KA_TILE + 1), whole(LANES, t_all, KA_TILE + 1, 0),
        ],
        out_specs=o_spec,
        scratch_shapes=score_scratch(c_len + wk_c, 4 * tq_c),
        compiler_params=_cparams(1),
        name="attn_c",
    )(sink[layer], _window_bias(tq_c), qt_all, k_all, vt_all)
    return oa, ob, oc


def _merge_kernel(x_ref, oa_ref, ob_ref, oc_ref, modb_ref, modc_ref, gpre_ref, gpost_ref, wgm_ref, bmg_ref,
                  wbr_ref, wout_ref, out_ref, *, sub, s_len):
    tm = x_ref.shape[0]
    row0 = pl.program_id(1) * tm

    def gates(i):
        rows = slice(i * sub, (i + 1) * sub)
        mod = jnp.where(row0 + i * sub >= s_len, modc_ref[...], modb_ref[...])
        h = _modulated_norm(x_ref[rows, :], mod, gpre_ref[...])
        gm = jnp.dot(h.astype(BF16), wgm_ref[...], preferred_element_type=F32)
        return mod, gm

    def finish(i, mod, gm):
        rows = slice(i * sub, (i + 1) * sub)
        z = None
        for j, o_ref in enumerate((oa_ref, ob_ref, oc_ref)):
            g = gm[:, j * QW:(j + 1) * QW]
            u = (o_ref[rows, :].astype(F32) * (g * _sigmoid(g))).astype(BF16)
            p = jnp.dot(u, wbr_ref[j], preferred_element_type=F32)
            mg = _sigmoid(gm[:, N_G + j * D:N_G + (j + 1) * D] + bmg_ref[:, j * D:(j + 1) * D])
            z = mg * p if z is None else z + mg * p
        y = jnp.dot(z.astype(BF16), wout_ref[...], preferred_element_type=F32)
        ms = jnp.mean(y * y, axis=-1, keepdims=True)
        gate = mod[:, 2 * D:]
        out_ref[rows, :] = x_ref[rows, :] + gate * (y * lax.rsqrt(ms + EPS) * gpost_ref[...])

    prev = gates(0)
    for i in range(1, tm // sub):
        cur = gates(i)
        finish(i - 1, *prev)
        prev = cur
    finish(tm // sub - 1, *prev)


def _merge_call(xs, oa, ob, oc, mod4, g_pre, g_post, wgm, b_mg, wbr, wout, layer, s_len, rows_out, tm, sub):
    nb = xs.shape[0]
    tok = lambda b, t: (b, t, 0)
    lay2 = lambda b, t: (layer, 0, 0)
    const = dict(pipeline_mode=pl.Buffered(1))
    return pl.pallas_call(
        functools.partial(_merge_kernel, sub=sub, s_len=s_len),
        out_shape=jax.ShapeDtypeStruct((nb, rows_out, D), F32),
        grid=(nb, rows_out // tm),
        in_specs=[
            pl.BlockSpec((None, tm, D), tok),
            pl.BlockSpec((None, tm, QW), tok),
            pl.BlockSpec((None, tm, QW), tok),
            pl.BlockSpec((None, tm, QW), tok),
            pl.BlockSpec((None, None, 1, 3 * D), lambda b, t: (layer, b, 0, 0)),
            pl.BlockSpec((None, None, 1, 3 * D), lambda b, t: (layer, nb, 0, 0)),
            pl.BlockSpec((None, 1, D), lay2),
            pl.BlockSpec((None, 1, D), lay2),
            pl.BlockSpec((None, D, N_G + N_M), lay2, **const),
            pl.BlockSpec((None, 1, N_M), lay2),
            pl.BlockSpec((None, 3, QW, D), lambda b, t: (layer, 0, 0, 0), **const),
            pl.BlockSpec((None, D, D), lay2, **const),
        ],
        out_specs=pl.BlockSpec((None, tm, D), tok),
        compiler_params=_cparams(2),
        name="merge",
    )(xs, oa, ob, oc, mod4, mod4, g_pre, g_post, wgm, b_mg, wbr, wout)


def _rope_tables(s_len, c_len):
    rows = s_len // GRID_W
    row = jnp.repeat(jnp.arange(rows), GRID_W).astype(F32)
    col = jnp.tile(jnp.arange(GRID_W), rows).astype(F32)
    freqs = ROPE_THETA ** (-jnp.arange(ROPE_PAIRS, dtype=F32) / ROPE_PAIRS)
    ang_r = row[:, None] * freqs
    ang_c = col[:, None] * freqs
    ang = jnp.concatenate([ang_r, ang_r, ang_c, ang_c], axis=-1)
    cos = jnp.concatenate([jnp.cos(ang), jnp.ones((c_len, HD), F32)], axis=0)
    sin = jnp.concatenate([jnp.sin(ang), jnp.zeros((c_len, HD), F32)], axis=0)
    first = (np.arange(HD) % 32) < 16
    sin_a = jnp.where(first, -sin, 0.0)
    sin_b = jnp.where(first, 0.0, sin)
    tile2 = lambda a: jnp.concatenate([a, a], axis=-1)
    cost = cos.T
    sint = (sin_a + sin_b).T
    return cost, sint, tile2(cos), tile2(sin_a), tile2(sin_b)


def _cols(w, names):
    return jnp.concatenate([w[..., _IN[n][0]:_IN[n][1]] for n in names], axis=-1)


def kernel(x, c, ctx, c_ctx, w_ada, b_ada, g_pre, g_post, w_in, q_norm, k_norm, lam_q1, lam_k1, lam_q2,
           lam_k2, subln, sink, w_br_a, w_br_b, w_br_c, w_mg, b_mg, w_out):
    nb, s_len, _ = x.shape
    c_len = ctx.shape[1]
    depth = w_in.shape[0]
    sub = 256
    tm_all = 3 * sub
    tm_lat = 2 * sub
    tq_a, tq_b, tq_c = 128, 256, 256
    assert (s_len + c_len) % tm_all == 0 and s_len % tm_lat == 0 and s_len % sub == 0

    wp = _cols(w_in, ("qa", "qb", "qc", "kb", "ka", "kc", "vb", "va", "vc")).astype(BF16)
    wgm = jnp.concatenate([_cols(w_in, ("ga", "gb", "gc")), w_mg], axis=-1).astype(BF16)
    wbr = jnp.stack([w_br_a, w_br_b, w_br_c], axis=1).astype(BF16)
    wout = w_out.astype(BF16)

    tabs = _rope_tables(s_len, c_len)
    gq_t = jnp.broadcast_to(q_norm[:, :, None], (depth, HD, sub))
    gk_n = jnp.concatenate([k_norm, k_norm], axis=-1)[:, None, :]
    lamv = jnp.stack([lam_q1, lam_k1, lam_q2, lam_k2], axis=1)
    subln_t = jnp.broadcast_to(subln[:, :, None], (depth, 2 * HD, tq_b))
    g_pre3 = g_pre[:, None, :]
    g_post3 = g_post[:, None, :]
    b_mg3 = b_mg[:, None, :]

    rows = ((nb + 1 + 7) // 8) * 8
    sc_in = jnp.concatenate([c, c_ctx[None, :], jnp.zeros((rows - nb - 1, D), F32)], axis=0)
    mod = _ada_call(sc_in, w_ada, b_ada)
    mod4 = mod[:, :, None, :]

    xs = jnp.concatenate([x, ctx], axis=1)
    for layer in range(depth):
        last = layer == depth - 1
        lam_init = 0.8 - 0.6 * math.exp(-0.3 * layer)
        qt_all, k_all, vt_all = _proj_call(xs, mod4, g_pre3, wp, tabs, gq_t, gk_n, layer, s_len, tm_all, sub)
        oa, ob, oc = _attn_calls(qt_all, k_all, vt_all, sink, lamv, subln_t, layer, not last, s_len,
                                 lam_init, tq_a, tq_b, tq_c)
        rows_out, tm = (s_len, tm_lat) if last else (s_len + c_len, tm_all)
        xs = _merge_call(xs, oa, ob, oc, mod4, g_pre3, g_post3, wgm, b_mg3, wbr, wout, layer, s_len,
                         rows_out, tm, sub)
    return xs
```

```python
import functools
import math

import jax
import jax.numpy as jnp
import numpy as np
from jax import lax
from jax.experimental import pallas as pl
from jax.experimental.pallas import tpu as pltpu

F32 = jnp.float32
BF16 = jnp.bfloat16

D = 1024
HD = 64
GRID_W = 64
WINDOW = 128
ROPE_THETA = 10000.0
ROPE_PAIRS = HD // 4
EPS = 1e-6
SUBLN_EPS = 1e-5
ATTN_SCALE = HD ** -0.5
LOG2E = math.log2(math.e)

QW = 512
N_Q = 3 * QW
N_K = 512 + 128 + 128
N_V = 512 + 128 + 128
KA_TILE = 4
N_P = N_Q + N_K + N_V
N_G = 3 * QW
N_M = 3 * D

LANES = 128
VMEM_LIMIT = 56 * 1024 * 1024

_IN = dict(qa=(0, 512), ka=(512, 640), va=(640, 768), ga=(768, 1280),
           qb=(1280, 1792), kb=(1792, 2304), vb=(2304, 2816), gb=(2816, 3328),
           qc=(3328, 3840), kc=(3840, 3968), vc=(3968, 4096), gc=(4096, 4608))


def _sigmoid(v):
    return 1.0 / (1.0 + jnp.exp(-v))


def _cparams(n_axes):
    return pltpu.CompilerParams(dimension_semantics=("arbitrary",) * n_axes,
                                vmem_limit_bytes=VMEM_LIMIT)


def _ada_kernel(sc_ref, w_ref, b_ref, o_ref):
    v = sc_ref[...]
    s = (v * _sigmoid(v)).astype(BF16)
    o_ref[...] = jnp.dot(s, w_ref[...].astype(BF16), preferred_element_type=F32) + b_ref[...]


def _ada_call(sc_in, w_ada, b_ada):
    depth = w_ada.shape[0]
    rows = sc_in.shape[0]
    nblk = 3
    return pl.pallas_call(
        _ada_kernel,
        out_shape=jax.ShapeDtypeStruct((depth, rows, 3 * D), F32),
        grid=(depth, nblk),
        in_specs=[
            pl.BlockSpec((rows, D), lambda l, n: (0, 0)),
            pl.BlockSpec((None, D, D), lambda l, n: (l, 0, n)),
            pl.BlockSpec((None, 1, D), lambda l, n: (l, 0, n)),
        ],
        out_specs=pl.BlockSpec((None, rows, D), lambda l, n: (l, 0, n)),
        compiler_params=_cparams(2),
        name="adaln",
    )(sc_in, w_ada, b_ada.reshape(depth, 1, 3 * D))


def _modulated_norm(x, mod, gpre):
    shift = mod[:, :D]
    scale = mod[:, D:2 * D]
    ms = jnp.mean(x * x, axis=-1, keepdims=True)
    return x * lax.rsqrt(ms + EPS) * gpre * (1.0 + scale) + shift


def _sub_tile_input(x_refs, ctx_ref, i, ctx_step):
    x = x_refs[i][...]
    if ctx_ref is None or i != len(x_refs) - 1:
        return x, False
    is_ctx = pl.program_id(1) == ctx_step
    return jnp.where(is_ctx, ctx_ref[...], x), is_ctx


def _proj_kernel(*refs, sub, n_sub, ctx_step):
    x_refs, ctx_ref = refs[:n_sub], refs[n_sub]
    (modb_ref, modc_ref, gpre_ref, w_ref, cost_ref, sint_ref, cosn_ref, sina_ref, sinb_ref, gq_ref, gk_ref,
     qt_ref, k_ref, vt_ref) = refs[n_sub + 1:]

    def project(i):
        x, is_ctx = _sub_tile_input(x_refs, ctx_ref, i, ctx_step)
        mod = jnp.where(is_ctx, modc_ref[...], modb_ref[...])
        h = _modulated_norm(x, mod, gpre_ref[...])
        return jnp.dot(h.astype(BF16), w_ref[...], preferred_element_type=F32)

    def finish(i, y):
        rows = slice(i * sub, (i + 1) * sub)
        q3 = y[:, :N_Q].T.reshape(N_Q // HD, HD, sub)
        qa = q3[:8]
        ss = jnp.sum(qa * qa, axis=1, keepdims=True)
        qa = qa * lax.rsqrt(ss * (1.0 / HD) + EPS) * gq_ref[...][None]
        q3 = jnp.concatenate([qa, q3[8:]], axis=0)
        rot = jnp.concatenate([q3[:, 16:32], q3[:, 0:16], q3[:, 48:64], q3[:, 32:48]], axis=1)
        q3 = (q3 * cost_ref[:, rows][None] + rot * sint_ref[:, rows][None]) * (ATTN_SCALE * LOG2E)
        qt_ref[:, rows] = q3.reshape(N_Q, sub).astype(BF16)

        ka = y[:, N_Q + KA_TILE * LANES:N_Q + (KA_TILE + 1) * LANES]
        lane = lax.broadcasted_iota(jnp.int32, (1, LANES), 1)
        lo = lane < HD
        sq = ka * ka
        s_lo = jnp.sum(jnp.where(lo, sq, 0.0), axis=-1, keepdims=True)
        s_hi = jnp.sum(jnp.where(lo, 0.0, sq), axis=-1, keepdims=True)
        r = jnp.where(lo, lax.rsqrt(s_lo * (1.0 / HD) + EPS), lax.rsqrt(s_hi * (1.0 / HD) + EPS))
        ka = ka * r * gk_ref[...]
        cosn = cosn_ref[rows, :]
        sina = sina_ref[rows, :]
        sinb = sinb_ref[rows, :]
        for j in range(N_K // LANES):
            t = ka if j == KA_TILE else y[:, N_Q + j * LANES:N_Q + (j + 1) * LANES]
            t = t * cosn + pltpu.roll(t, LANES - 16, 1) * sina + pltpu.roll(t, 16, 1) * sinb
            k_ref[rows, j * LANES:(j + 1) * LANES] = t.astype(BF16)

        vt_ref[:, rows] = y[:, N_Q + N_K:].T.astype(BF16)

    y_prev = project(0)
    for i in range(1, n_sub):
        y = project(i)
        finish(i - 1, y_prev)
        y_prev = y
    finish(n_sub - 1, y_prev)


def _token_specs(x_src, ctx_src, n_sub, sub, s_len):
    last = s_len // sub - 1
    specs = [pl.BlockSpec((None, sub, D), functools.partial(lambda i, b, t: (b, jnp.minimum(n_sub * t + i, last), 0), i))
             for i in range(n_sub)]
    operands = [x_src] * n_sub
    if ctx_src is not None:
        arr, blk = ctx_src
        specs.append(pl.BlockSpec((None, sub, D), lambda b, t: (b, blk, 0)))
        operands.append(arr)
    return specs, operands


def _proj_call(x_src, ctx_src, mod4, g_pre, wp, tabs, gq_t, gk_n, layer, s_len, t_all, tm, sub):
    nb = x_src.shape[0]
    cost, sint, cosn, sina, sinb = tabs
    n_sub = tm // sub
    tok_specs, tok_args = _token_specs(x_src, ctx_src, n_sub, sub, s_len)
    return pl.pallas_call(
        functools.partial(_proj_kernel, sub=sub, n_sub=n_sub, ctx_step=t_all // tm - 1),
        out_shape=(jax.ShapeDtypeStruct((nb, N_Q, t_all), BF16),
                   jax.ShapeDtypeStruct((nb, t_all, N_K), BF16),
                   jax.ShapeDtypeStruct((nb, N_V, t_all), BF16)),
        grid=(nb, t_all // tm),
        in_specs=tok_specs + [
            pl.BlockSpec((None, None, 1, 3 * D), lambda b, t: (layer, b, 0, 0)),
            pl.BlockSpec((None, None, 1, 3 * D), lambda b, t: (layer, nb, 0, 0)),
            pl.BlockSpec((None, 1, D), lambda b, t: (layer, 0, 0)),
            pl.BlockSpec((None, D, N_P), lambda b, t: (layer, 0, 0)),
            pl.BlockSpec((HD, tm), lambda b, t: (0, t)),
            pl.BlockSpec((HD, tm), lambda b, t: (0, t)),
            pl.BlockSpec((tm, LANES), lambda b, t: (t, 0)),
            pl.BlockSpec((tm, LANES), lambda b, t: (t, 0)),
            pl.BlockSpec((tm, LANES), lambda b, t: (t, 0)),
            pl.BlockSpec((None, HD, sub), lambda b, t: (layer, 0, 0)),
            pl.BlockSpec((None, 1, LANES), lambda b, t: (layer, 0, 0)),
        ],
        out_specs=(pl.BlockSpec((None, N_Q, tm), lambda b, t: (b, 0, t)),
                   pl.BlockSpec((None, tm, N_K), lambda b, t: (b, t, 0)),
                   pl.BlockSpec((None, N_V, tm), lambda b, t: (b, 0, t))),
        compiler_params=_cparams(2),
        name="proj",
    )(*tok_args, mod4, mod4, g_pre, wp, cost, sint, cosn, sina, sinb, gq_t, gk_n)


KCHUNK = 256
ONES_ROWS = 16


def _flash(chunks, rhs, dv, extra=None):
    m = extra
    acc = None
    for k, vt, bias in chunks:
        s = jnp.dot(k, rhs, preferred_element_type=F32)
        if bias is not None:
            s = s + bias
        cm = jnp.max(s, axis=0, keepdims=True)
        m_new = cm if m is None else jnp.maximum(m, cm)
        p = jnp.exp2(s - m_new).astype(BF16)
        vt_aug = jnp.concatenate([vt, jnp.ones((ONES_ROWS, vt.shape[1]), BF16)], axis=0)
        pv = jnp.dot(vt_aug, p, preferred_element_type=F32)
        acc = pv if acc is None else acc * jnp.exp2(m - m_new) + pv
        m = m_new
    l = acc[dv:dv + 1]
    if extra is not None:
        l = l + jnp.exp2(extra - m)
    return acc[:dv] * (1.0 / l)


def _key_chunks(k_ref, vt_ref, lo, n_keys, k_lanes=slice(None), v_rows=slice(None)):
    return [(k_ref[lo + c * KCHUNK:lo + (c + 1) * KCHUNK, k_lanes],
             vt_ref[v_rows, lo + c * KCHUNK:lo + (c + 1) * KCHUNK], None) for c in range(n_keys // KCHUNK)]


def _gqa_rhs(qt, j, tq):
    z = jnp.zeros((HD, tq), qt.dtype)
    first = j == 0
    cols = []
    for g in range(4):
        qg = qt[g * HD:(g + 1) * HD, :]
        cols.append(jnp.concatenate([jnp.where(first, qg, z), jnp.where(first, z, qg)], axis=0))
    return jnp.concatenate(cols, axis=1)


def _store_heads(o_ref, ot, tq, nheads):
    o = jnp.concatenate([ot[:, g * tq:(g + 1) * tq] for g in range(nheads)], axis=0)
    o_ref[...] = o.T.astype(o_ref.dtype)


def _sink_row(sink_ref, base, tq):
    blk = lax.broadcasted_iota(jnp.int32, (1, 4 * tq), 1) // tq
    row = jnp.zeros((1, 4 * tq), F32)
    for g in range(4):
        row = jnp.where(blk == g, sink_ref[base + g], row)
    return row * LOG2E


def _tile(t, size):
    if isinstance(t, int):
        return pl.ds(t * size, size)
    return pl.ds(pl.multiple_of(t * size, size), size)


def _two_stage(score_chunks, value_chunks, rhs_next, s_next, s_cur, m_cur, extra_next=None, extra_cur=None,
               lag=2):
    cm = None
    acc = None
    parts = []
    row = 0
    for c in range(len(score_chunks or value_chunks)):
        if score_chunks is not None:
            k, bias = score_chunks[c]()
            rows = slice(row, row + k.shape[0])
            s = jnp.dot(k, rhs_next, preferred_element_type=F32)
            if bias is not None:
                s = s + bias
            s_next[rows, :] = s
            parts.append(jnp.max(s, axis=0, keepdims=True))
            cm = parts[-1] if cm is None else jnp.maximum(cm, parts[-1])
        if value_chunks is not None:
            m_c = m_cur
            if score_chunks is not None and c >= lag:
                m_c = jnp.maximum(m_cur, jnp.minimum(parts[c - lag], m_cur))
            vt = value_chunks[c]()
            rows = slice(row, row + vt.shape[1])
            p = jnp.exp2(s_cur[rows, :] - m_c).astype(BF16)
            vt_aug = jnp.concatenate([vt, jnp.ones((ONES_ROWS, vt.shape[1]), BF16)], axis=0)
            pv = jnp.dot(vt_aug, p, preferred_element_type=F32)
            acc = pv if acc is None else acc + pv
        row = rows.stop
    if cm is not None and extra_next is not None:
        cm = jnp.maximum(cm, extra_next)
    ot = None
    if acc is not None:
        dv = acc.shape[0] - ONES_ROWS
        l = acc[dv:dv + 1]
        if extra_cur is not None:
            l = l + jnp.exp2(extra_cur - m_cur)
        ot = acc[:dv] * (1.0 / l)
    return cm, ot


def _pipeline(make_rhs, score_chunks, value_chunks, finish, s0, s1, n_units, extra=None, lag=2):
    ex = (lambda u: None) if extra is None else extra
    m0, _ = _two_stage(score_chunks(0), None, make_rhs(0), s0, None, None, ex(0), None)

    def body(i, m_even):
        u = 2 * i
        m_odd, ot = _two_stage(score_chunks(u + 1), value_chunks(u), make_rhs(u + 1), s1, s0, m_even,
                               ex(u + 1), ex(u), lag)
        finish(u, ot)
        u2 = jnp.minimum(u + 2, n_units - 1)
        m_next, ot = _two_stage(score_chunks(u2), value_chunks(u + 1), make_rhs(u2), s0, s1, m_odd,
                                ex(u2), ex(u + 1), lag)
        finish(u + 1, ot)
        return m_next

    lax.fori_loop(0, n_units // 2, body, m0)


def _split_unit(u, n_tiles):
    if isinstance(u, int):
        return u // n_tiles, u % n_tiles
    return lax.div(u, n_tiles), lax.rem(u, n_tiles)


def _rows(g, size):
    return _tile(g, size)


def _attn_a_kernel(qt_ref, k_ref, vt_ref, o_ref, s0, s1, *, tq, n_lat, n_ctx, s_len):
    t_all = k_ref.shape[0]
    n_chunks = t_all // KCHUNK

    def rhs_of(j, t):
        return _gqa_rhs(qt_ref[_rows(j, 4 * HD), _tile(t, tq)], j, tq)

    def store(j, t, ot):
        _store_heads(o_ref.at[_tile(t, tq), _rows(j, 4 * HD)], ot, tq, 4)

    def score_chunks(u):
        return [functools.partial(lambda c: (k_ref[c * KCHUNK:(c + 1) * KCHUNK, :], None), c)
                for c in range(n_chunks)]

    def value_chunks(u):
        j, _ = _split_unit(u, n_lat)
        return [functools.partial(lambda c: vt_ref[_rows(j, HD), c * KCHUNK:(c + 1) * KCHUNK], c)
                for c in range(n_chunks)]

    _pipeline(lambda u: rhs_of(*_split_unit(u, n_lat)), score_chunks, value_chunks,
              lambda u, ot: store(*_split_unit(u, n_lat), ot), s0, s1, 2 * n_lat)
    for j in range(2):
        for t in range(n_lat, n_lat + n_ctx):
            chunks = _key_chunks(k_ref, vt_ref, s_len, t_all - s_len, v_rows=_rows(j, HD))
            store(j, t, _flash(chunks, rhs_of(j, t), HD))


def _attn_c_kernel(sink_ref, bias_ref, qt_ref, k_ref, vt_ref, o_ref, s0, s1, *, tq, n_lat, n_ctx, s_len):
    t_all = k_ref.shape[0]
    wk = tq + 2 * WINDOW

    def rhs_of(j, t):
        return _gqa_rhs(qt_ref[_rows(j, 4 * HD), _tile(t, tq)], j, tq)

    def store(j, t, ot):
        _store_heads(o_ref.at[_tile(t, tq), _rows(j, 4 * HD)], ot, tq, 4)

    def sink_of(u):
        j, _ = _split_unit(u, n_lat)
        return _sink_row(sink_ref, j * 4, tq)

    def window(t):
        q0 = t * tq
        start = jnp.clip(q0 - WINDOW, 0, s_len - wk)
        return start, (q0 - start) // WINDOW

    def local(start, c):
        return pl.ds(pl.multiple_of(start + c * KCHUNK, LANES), KCHUNK)

    def score_chunks(u):
        _, t = _split_unit(u, n_lat)
        start, variant = window(t)

        def loc(c):
            b = bias_ref[variant, c * KCHUNK:(c + 1) * KCHUNK, :]
            return k_ref[local(start, c), :], jnp.concatenate([b] * 4, axis=1)

        return ([lambda: (k_ref[s_len:s_len + KCHUNK, :], None)]
                + [functools.partial(loc, c) for c in range(wk // KCHUNK)])

    def value_chunks(u):
        j, t = _split_unit(u, n_lat)
        start, _ = window(t)
        return ([lambda: vt_ref[_rows(j, HD), s_len:s_len + KCHUNK]]
                + [functools.partial(lambda c: vt_ref[_rows(j, HD), local(start, c)], c)
                   for c in range(wk // KCHUNK)])

    _pipeline(lambda u: rhs_of(*_split_unit(u, n_lat)), score_chunks, value_chunks,
              lambda u, ot: store(*_split_unit(u, n_lat), ot), s0, s1, 2 * n_lat, extra=sink_of, lag=1)
    for j in range(2):
        for t in range(n_lat, n_lat + n_ctx):
            chunks = _key_chunks(k_ref, vt_ref, s_len, t_all - s_len, v_rows=_rows(j, HD))
            store(j, t, _flash(chunks, rhs_of(j, t), HD, extra=_sink_row(sink_ref, j * 4, tq)))


def _attn_b_kernel(lamv_ref, subln_ref, qt_ref, k_ref, vt_ref, o_ref, s0, s1, *, tq, n_lat, n_ctx, s_len,
                   lam_init):
    lv = lamv_ref[...]
    lam = (jnp.exp(jnp.sum(lv[0:1] * lv[1:2], axis=-1, keepdims=True))
           - jnp.exp(jnp.sum(lv[2:3] * lv[3:4], axis=-1, keepdims=True)) + lam_init)
    t_all = k_ref.shape[0]
    n_chunks = t_all // KCHUNK
    hw = 2 * HD

    def rhs_of(h, t):
        qt = qt_ref[_rows(h, hw), _tile(t, tq)]
        z = jnp.zeros((HD, tq), qt.dtype)
        return jnp.concatenate([jnp.concatenate([qt[:HD], z], axis=0),
                                jnp.concatenate([z, qt[HD:]], axis=0)], axis=1)

    def store(h, t, o2):
        o = o2[:, :tq] - lam * o2[:, tq:]
        ms = jnp.mean(o * o, axis=0, keepdims=True)
        o = o * lax.rsqrt(ms + SUBLN_EPS) * subln_ref[...] * (1.0 - lam_init)
        o_ref[_tile(t, tq), _rows(h, hw)] = o.T.astype(o_ref.dtype)

    def score_chunks(u):
        h, _ = _split_unit(u, n_lat)
        return [functools.partial(lambda c: (k_ref[c * KCHUNK:(c + 1) * KCHUNK, _rows(h, hw)], None), c)
                for c in range(n_chunks)]

    def value_chunks(u):
        h, _ = _split_unit(u, n_lat)
        return [functools.partial(lambda c: vt_ref[_rows(h, hw), c * KCHUNK:(c + 1) * KCHUNK], c)
                for c in range(n_chunks)]

    _pipeline(lambda u: rhs_of(*_split_unit(u, n_lat)), score_chunks, value_chunks,
              lambda u, ot: store(*_split_unit(u, n_lat), ot), s0, s1, 4 * n_lat)
    for h in range(4):
        for t in range(n_lat, n_lat + n_ctx):
            chunks = _key_chunks(k_ref, vt_ref, s_len, t_all - s_len, k_lanes=_rows(h, hw), v_rows=_rows(h, hw))
            store(h, t, _flash(chunks, rhs_of(h, t), hw))


def _window_bias(tq):
    wk = tq + 2 * WINDOW
    r = np.arange(wk)[:, None]
    c = np.arange(tq)[None, :]
    out = np.stack([np.where(np.abs(c - r + v * WINDOW) <= WINDOW, 0.0, -np.inf) for v in range(3)])
    return jnp.asarray(out, F32)


def _attn_calls(qt_all, k_all, vt_all, sink, lamv, subln_t, layer, with_ctx, s_len, lam_init,
                tq_a, tq_b, tq_c):
    nb, _, t_all = qt_all.shape
    c_len = t_all - s_len
    o_rows = t_all if with_ctx else s_len
    o_shape = jax.ShapeDtypeStruct((nb, o_rows, QW), BF16)

    def steps(tq):
        n_lat = s_len // tq
        return n_lat, n_lat + (c_len // tq if with_ctx else 0)

    whole = lambda rows, cols, r, c: pl.BlockSpec((None, rows, cols), lambda b: (b, r, c))
    o_spec = pl.BlockSpec((None, o_rows, QW), lambda b: (b, 0, 0))
    score_scratch = lambda rows, n: [pltpu.VMEM((rows, n), F32), pltpu.VMEM((rows, n), F32)]

    n_lat, n_all = steps(tq_a)
    oa = pl.pallas_call(
        functools.partial(_attn_a_kernel, tq=tq_a, n_lat=n_lat, n_ctx=n_all - n_lat, s_len=s_len),
        out_shape=o_shape,
        grid=(nb,),
        in_specs=[whole(QW, t_all, 0, 0), whole(t_all, LANES, 0, KA_TILE), whole(LANES, t_all, KA_TILE, 0)],
        out_specs=o_spec,
        scratch_shapes=score_scratch(t_all, 4 * tq_a),
        compiler_params=_cparams(1),
        name="attn_a",
    )(qt_all, k_all, vt_all)

    n_lat, n_all = steps(tq_b)
    ob = pl.pallas_call(
        functools.partial(_attn_b_kernel, tq=tq_b, n_lat=n_lat, n_ctx=n_all - n_lat, s_len=s_len,
                          lam_init=lam_init),
        out_shape=o_shape,
        grid=(nb,),
        in_specs=[
            pl.BlockSpec((None, 4, HD), lambda b: (layer, 0, 0)),
            pl.BlockSpec((None, 2 * HD, tq_b), lambda b: (layer, 0, 0)),
            whole(QW, t_all, 1, 0), whole(t_all, QW, 0, 0), whole(QW, t_all, 0, 0),
        ],
        out_specs=o_spec,
        scratch_shapes=score_scratch(t_all, 2 * tq_b),
        compiler_params=_cparams(1),
        name="attn_b",
    )(lamv, subln_t, qt_all, k_all, vt_all)

    n_lat, n_all = steps(tq_c)
    wk_c = tq_c + 2 * WINDOW
    oc = pl.pallas_call(
        functools.partial(_attn_c_kernel, tq=tq_c, n_lat=n_lat, n_ctx=n_all - n_lat, s_len=s_len),
        out_shape=o_shape,
        grid=(nb,),
        in_specs=[
            pl.BlockSpec(memory_space=pltpu.SMEM),
            pl.BlockSpec((3, wk_c, tq_c), lambda b: (0, 0, 0)),
            whole(QW, t_all, 2, 0), whole(t_all, LANES, 0, KA_TILE + 1), whole(LANES, t_all, KA_TILE + 1, 0),
        ],
        out_specs=o_spec,
        scratch_shapes=score_scratch(c_len + wk_c, 4 * tq_c),
        compiler_params=_cparams(1),
        name="attn_c",
    )(sink[layer], _window_bias(tq_c), qt_all, k_all, vt_all)
    return oa, ob, oc


def _merge_kernel(*refs, sub, n_sub, ctx_step):
    x_refs = refs[:n_sub]
    rest = refs[n_sub:]
    ctx_ref = None
    if ctx_step is not None:
        ctx_ref, rest = rest[0], rest[1:]
    (oa_ref, ob_ref, oc_ref, modb_ref, modc_ref, gpre_ref, gpost_ref, wgm_ref, bmg_ref, wbr_ref, wout_ref,
     out_ref) = rest

    def gates(i):
        x, is_ctx = _sub_tile_input(x_refs, ctx_ref, i, ctx_step)
        mod = jnp.where(is_ctx, modc_ref[...], modb_ref[...])
        h = _modulated_norm(x, mod, gpre_ref[...])
        gm = jnp.dot(h.astype(BF16), wgm_ref[...], preferred_element_type=F32)
        return x, mod, gm

    def finish(i, x, mod, gm):
        rows = slice(i * sub, (i + 1) * sub)
        z = None
        for j, o_ref in enumerate((oa_ref, ob_ref, oc_ref)):
            g = gm[:, j * QW:(j + 1) * QW]
            u = (o_ref[rows, :].astype(F32) * (g * _sigmoid(g))).astype(BF16)
            p = jnp.dot(u, wbr_ref[j], preferred_element_type=F32)
            mg = _sigmoid(gm[:, N_G + j * D:N_G + (j + 1) * D] + bmg_ref[:, j * D:(j + 1) * D])
            z = mg * p if z is None else z + mg * p
        y = jnp.dot(z.astype(BF16), wout_ref[...], preferred_element_type=F32)
        ms = jnp.mean(y * y, axis=-1, keepdims=True)
        gate = mod[:, 2 * D:]
        out_ref[rows, :] = x + gate * (y * lax.rsqrt(ms + EPS) * gpost_ref[...])

    prev = gates(0)
    for i in range(1, n_sub):
        cur = gates(i)
        finish(i - 1, *prev)
        prev = cur
    finish(n_sub - 1, *prev)


def _merge_call(x_src, ctx_src, oa, ob, oc, mod4, g_pre, g_post, wgm, b_mg, wbr, wout, layer, s_len, rows_out,
                tm, sub):
    nb = x_src.shape[0]
    n_sub = tm // sub
    tok_specs, tok_args = _token_specs(x_src, ctx_src, n_sub, sub, s_len)
    ctx_step = None if ctx_src is None else rows_out // tm - 1
    tok = lambda b, t: (b, t, 0)
    lay2 = lambda b, t: (layer, 0, 0)
    const = dict(pipeline_mode=pl.Buffered(1))
    return pl.pallas_call(
        functools.partial(_merge_kernel, sub=sub, n_sub=n_sub, ctx_step=ctx_step),
        out_shape=jax.ShapeDtypeStruct((nb, rows_out, D), F32),
        grid=(nb, rows_out // tm),
        in_specs=tok_specs + [
            pl.BlockSpec((None, tm, QW), tok),
            pl.BlockSpec((None, tm, QW), tok),
            pl.BlockSpec((None, tm, QW), tok),
            pl.BlockSpec((None, None, 1, 3 * D), lambda b, t: (layer, b, 0, 0)),
            pl.BlockSpec((None, None, 1, 3 * D), lambda b, t: (layer, nb, 0, 0)),
            pl.BlockSpec((None, 1, D), lay2),
            pl.BlockSpec((None, 1, D), lay2),
            pl.BlockSpec((None, D, N_G + N_M), lay2, **const),
            pl.BlockSpec((None, 1, N_M), lay2),
            pl.BlockSpec((None, 3, QW, D), lambda b, t: (layer, 0, 0, 0), **const),
            pl.BlockSpec((None, D, D), lay2, **const),
        ],
        out_specs=pl.BlockSpec((None, tm, D), tok),
        compiler_params=_cparams(2),
        name="merge",
    )(*tok_args, oa, ob, oc, mod4, mod4, g_pre, g_post, wgm, b_mg, wbr, wout)


def _rope_tables(s_len, c_len):
    rows = s_len // GRID_W
    row = np.repeat(np.arange(rows), GRID_W).astype(np.float32)
    col = np.tile(np.arange(GRID_W), rows).astype(np.float32)
    freqs = (np.float32(ROPE_THETA) ** (-np.arange(ROPE_PAIRS, dtype=np.float32) / ROPE_PAIRS)).astype(np.float32)
    ang_r = row[:, None] * freqs
    ang_c = col[:, None] * freqs
    ang = np.concatenate([ang_r, ang_r, ang_c, ang_c], axis=-1)
    cos = np.concatenate([np.cos(ang), np.ones((c_len, HD), np.float32)], axis=0).astype(np.float32)
    sin = np.concatenate([np.sin(ang), np.zeros((c_len, HD), np.float32)], axis=0).astype(np.float32)
    first = (np.arange(HD) % 32) < 16
    sin_a = np.where(first, -sin, np.float32(0.0))
    sin_b = np.where(first, np.float32(0.0), sin)
    tile2 = lambda a: np.concatenate([a, a], axis=-1)
    cost = np.ascontiguousarray(cos.T)
    sint = np.ascontiguousarray((sin_a + sin_b).T)
    return tuple(jnp.asarray(a, F32) for a in (cost, sint, tile2(cos), tile2(sin_a), tile2(sin_b)))


def _cols(w, names):
    return jnp.concatenate([w[..., _IN[n][0]:_IN[n][1]] for n in names], axis=-1)


def kernel(x, c, ctx, c_ctx, w_ada, b_ada, g_pre, g_post, w_in, q_norm, k_norm, lam_q1, lam_k1, lam_q2,
           lam_k2, subln, sink, w_br_a, w_br_b, w_br_c, w_mg, b_mg, w_out):
    nb, s_len, _ = x.shape
    c_len = ctx.shape[1]
    depth = w_in.shape[0]
    sub = 256
    tm_all = 3 * sub
    tm_lat = 2 * sub
    tq_a, tq_b, tq_c = 128, 256, 256
    assert (s_len + c_len) % tm_all == 0 and s_len % tm_lat == 0 and s_len % sub == 0

    wp = _cols(w_in, ("qa", "qb", "qc", "kb", "ka", "kc", "vb", "va", "vc")).astype(BF16)
    wgm = jnp.concatenate([_cols(w_in, ("ga", "gb", "gc")), w_mg], axis=-1).astype(BF16)
    wbr = jnp.stack([w_br_a, w_br_b, w_br_c], axis=1).astype(BF16)
    wout = w_out.astype(BF16)

    tabs = _rope_tables(s_len, c_len)
    gq_t = jnp.broadcast_to(q_norm[:, :, None], (depth, HD, sub))
    gk_n = jnp.concatenate([k_norm, k_norm], axis=-1)[:, None, :]
    lamv = jnp.stack([lam_q1, lam_k1, lam_q2, lam_k2], axis=1)
    subln_t = jnp.broadcast_to(subln[:, :, None], (depth, 2 * HD, tq_b))
    g_pre3 = g_pre[:, None, :]
    g_post3 = g_post[:, None, :]
    b_mg3 = b_mg[:, None, :]

    rows = ((nb + 1 + 7) // 8) * 8
    sc_in = jnp.concatenate([c, c_ctx[None, :], jnp.zeros((rows - nb - 1, D), F32)], axis=0)
    mod = _ada_call(sc_in, w_ada, b_ada)
    mod4 = mod[:, :, None, :]

    t_all = s_len + c_len
    x_src, ctx_src = x, (ctx, 0)
    for layer in range(depth):
        last = layer == depth - 1
        lam_init = 0.8 - 0.6 * math.exp(-0.3 * layer)
        qt_all, k_all, vt_all = _proj_call(x_src, ctx_src, mod4, g_pre3, wp, tabs, gq_t, gk_n, layer, s_len,
                                           t_all, tm_all, sub)
        oa, ob, oc = _attn_calls(qt_all, k_all, vt_all, sink, lamv, subln_t, layer, not last, s_len,
                                 lam_init, tq_a, tq_b, tq_c)
        rows_out, tm = (s_len, tm_lat) if last else (t_all, tm_all)
        xs = _merge_call(x_src, None if last else ctx_src, oa, ob, oc, mod4, g_pre3, g_post3, wgm, b_mg3, wbr,
                         wout, layer, s_len, rows_out, tm, sub)
        x_src, ctx_src = xs, (xs, s_len // sub)
    return xs
```

```python
import functools
import math

import jax
import jax.numpy as jnp
import numpy as np
from jax import lax
from jax.experimental import pallas as pl
from jax.experimental.pallas import tpu as pltpu

F32 = jnp.float32
BF16 = jnp.bfloat16

D = 1024
HD = 64
GRID_W = 64
WINDOW = 128
ROPE_THETA = 10000.0
ROPE_PAIRS = HD // 4
EPS = 1e-6
SUBLN_EPS = 1e-5
ATTN_SCALE = HD ** -0.5
LOG2E = math.log2(math.e)

QW = 512
N_Q = 3 * QW
N_K = 512 + 128 + 128
N_V = 512 + 128 + 128
KA_TILE = 4
N_P = N_Q + N_K + N_V
N_G = 3 * QW
N_M = 3 * D

LANES = 128
VMEM_LIMIT = 56 * 1024 * 1024

_IN = dict(qa=(0, 512), ka=(512, 640), va=(640, 768), ga=(768, 1280),
           qb=(1280, 1792), kb=(1792, 2304), vb=(2304, 2816), gb=(2816, 3328),
           qc=(3328, 3840), kc=(3840, 3968), vc=(3968, 4096), gc=(4096, 4608))


def _sigmoid(v):
    return 1.0 / (1.0 + jnp.exp(-v))


def _cparams(n_axes):
    return pltpu.CompilerParams(dimension_semantics=("arbitrary",) * n_axes,
                                vmem_limit_bytes=VMEM_LIMIT)


def _ada_kernel(sc_ref, w_ref, b_ref, o_ref):
    v = sc_ref[...]
    s = (v * _sigmoid(v)).astype(BF16)
    o_ref[...] = jnp.dot(s, w_ref[...].astype(BF16), preferred_element_type=F32) + b_ref[...]


def _ada_call(sc_in, w_ada, b_ada):
    depth = w_ada.shape[0]
    rows = sc_in.shape[0]
    nblk = 3
    return pl.pallas_call(
        _ada_kernel,
        out_shape=jax.ShapeDtypeStruct((depth, rows, 3 * D), F32),
        grid=(depth, nblk),
        in_specs=[
            pl.BlockSpec((rows, D), lambda l, n: (0, 0)),
            pl.BlockSpec((None, D, D), lambda l, n: (l, 0, n)),
            pl.BlockSpec((None, 1, D), lambda l, n: (l, 0, n)),
        ],
        out_specs=pl.BlockSpec((None, rows, D), lambda l, n: (l, 0, n)),
        compiler_params=_cparams(2),
        name="adaln",
    )(sc_in, w_ada, b_ada.reshape(depth, 1, 3 * D))


def _modulated_norm(x, mod, gpre):
    shift = mod[:, :D]
    scale = mod[:, D:2 * D]
    ms = jnp.mean(x * x, axis=-1, keepdims=True)
    return x * lax.rsqrt(ms + EPS) * gpre * (1.0 + scale) + shift


def _sub_tile_input(x_refs, ctx_ref, i, ctx_step):
    x = x_refs[i][...]
    if ctx_ref is None or i != len(x_refs) - 1:
        return x, False
    is_ctx = pl.program_id(1) == ctx_step
    return jnp.where(is_ctx, ctx_ref[...], x), is_ctx


def _proj_kernel(*refs, sub, n_sub, ctx_step):
    x_refs, ctx_ref = refs[:n_sub], refs[n_sub]
    (modb_ref, modc_ref, gpre_ref, w_ref, cost_ref, sint_ref, cosn_ref, sina_ref, sinb_ref, gq_ref, gk_ref,
     qt_ref, k_ref, vt_ref) = refs[n_sub + 1:]

    def project(i):
        x, is_ctx = _sub_tile_input(x_refs, ctx_ref, i, ctx_step)
        mod = jnp.where(is_ctx, modc_ref[...], modb_ref[...])
        h = _modulated_norm(x, mod, gpre_ref[...])
        return jnp.dot(h.astype(BF16), w_ref[...], preferred_element_type=F32)

    def finish(i, y):
        rows = slice(i * sub, (i + 1) * sub)
        q3 = y[:, :N_Q].T.reshape(N_Q // HD, HD, sub)
        qa = q3[:8]
        ss = jnp.sum(qa * qa, axis=1, keepdims=True)
        qa = qa * lax.rsqrt(ss * (1.0 / HD) + EPS) * gq_ref[...][None]
        q3 = jnp.concatenate([qa, q3[8:]], axis=0)
        rot = jnp.concatenate([q3[:, 16:32], q3[:, 0:16], q3[:, 48:64], q3[:, 32:48]], axis=1)
        q3 = (q3 * cost_ref[:, rows][None] + rot * sint_ref[:, rows][None]) * (ATTN_SCALE * LOG2E)
        qt_ref[:, rows] = q3.reshape(N_Q, sub).astype(BF16)

        ka = y[:, N_Q + KA_TILE * LANES:N_Q + (KA_TILE + 1) * LANES]
        lane = lax.broadcasted_iota(jnp.int32, (1, LANES), 1)
        lo = lane < HD
        sq = ka * ka
        s_lo = jnp.sum(jnp.where(lo, sq, 0.0), axis=-1, keepdims=True)
        s_hi = jnp.sum(jnp.where(lo, 0.0, sq), axis=-1, keepdims=True)
        r = jnp.where(lo, lax.rsqrt(s_lo * (1.0 / HD) + EPS), lax.rsqrt(s_hi * (1.0 / HD) + EPS))
        ka = ka * r * gk_ref[...]
        cosn = cosn_ref[rows, :]
        sina = sina_ref[rows, :]
        sinb = sinb_ref[rows, :]
        for j in range(N_K // LANES):
            t = ka if j == KA_TILE else y[:, N_Q + j * LANES:N_Q + (j + 1) * LANES]
            t = t * cosn + pltpu.roll(t, LANES - 16, 1) * sina + pltpu.roll(t, 16, 1) * sinb
            k_ref[rows, j * LANES:(j + 1) * LANES] = t.astype(BF16)

        vt_ref[:, rows] = y[:, N_Q + N_K:].T.astype(BF16)

    y_prev = project(0)
    for i in range(1, n_sub):
        y = project(i)
        finish(i - 1, y_prev)
        y_prev = y
    finish(n_sub - 1, y_prev)


def _token_specs(x_src, ctx_src, n_sub, sub, s_len):
    last = s_len // sub - 1
    specs = [pl.BlockSpec((None, sub, D), functools.partial(lambda i, b, t: (b, jnp.minimum(n_sub * t + i, last), 0), i))
             for i in range(n_sub)]
    operands = [x_src] * n_sub
    if ctx_src is not None:
        arr, blk = ctx_src
        specs.append(pl.BlockSpec((None, sub, D), lambda b, t: (b, blk, 0)))
        operands.append(arr)
    return specs, operands


def _proj_call(x_src, ctx_src, mod4, g_pre, wp, tabs, gq_t, gk_n, layer, s_len, t_all, tm, sub):
    nb = x_src.shape[0]
    cost, sint, cosn, sina, sinb = tabs
    n_sub = tm // sub
    tok_specs, tok_args = _token_specs(x_src, ctx_src, n_sub, sub, s_len)
    return pl.pallas_call(
        functools.partial(_proj_kernel, sub=sub, n_sub=n_sub, ctx_step=t_all // tm - 1),
        out_shape=(jax.ShapeDtypeStruct((nb, N_Q, t_all), BF16),
                   jax.ShapeDtypeStruct((nb, t_all, N_K), BF16),
                   jax.ShapeDtypeStruct((nb, N_V, t_all), BF16)),
        grid=(nb, t_all // tm),
        in_specs=tok_specs + [
            pl.BlockSpec((None, None, 1, 3 * D), lambda b, t: (layer, b, 0, 0)),
            pl.BlockSpec((None, None, 1, 3 * D), lambda b, t: (layer, nb, 0, 0)),
            pl.BlockSpec((None, 1, D), lambda b, t: (layer, 0, 0)),
            pl.BlockSpec((None, D, N_P), lambda b, t: (layer, 0, 0)),
            pl.BlockSpec((HD, tm), lambda b, t: (0, t)),
            pl.BlockSpec((HD, tm), lambda b, t: (0, t)),
            pl.BlockSpec((tm, LANES), lambda b, t: (t, 0)),
            pl.BlockSpec((tm, LANES), lambda b, t: (t, 0)),
            pl.BlockSpec((tm, LANES), lambda b, t: (t, 0)),
            pl.BlockSpec((None, HD, sub), lambda b, t: (layer, 0, 0)),
            pl.BlockSpec((None, 1, LANES), lambda b, t: (layer, 0, 0)),
        ],
        out_specs=(pl.BlockSpec((None, N_Q, tm), lambda b, t: (b, 0, t)),
                   pl.BlockSpec((None, tm, N_K), lambda b, t: (b, t, 0)),
                   pl.BlockSpec((None, N_V, tm), lambda b, t: (b, 0, t))),
        compiler_params=_cparams(2),
        name="proj",
    )(*tok_args, mod4, mod4, g_pre, wp, cost, sint, cosn, sina, sinb, gq_t, gk_n)


KCHUNK = 256
ONES_ROWS = 16


def _flash(chunks, rhs, dv, extra=None):
    m = extra
    acc = None
    for k, vt, bias in chunks:
        s = jnp.dot(k, rhs, preferred_element_type=F32)
        if bias is not None:
            s = s + bias
        cm = jnp.max(s, axis=0, keepdims=True)
        m_new = cm if m is None else jnp.maximum(m, cm)
        p = jnp.exp2(s - m_new).astype(BF16)
        vt_aug = jnp.concatenate([vt, jnp.ones((ONES_ROWS, vt.shape[1]), BF16)], axis=0)
        pv = jnp.dot(vt_aug, p, preferred_element_type=F32)
        acc = pv if acc is None else acc * jnp.exp2(m - m_new) + pv
        m = m_new
    l = acc[dv:dv + 1]
    if extra is not None:
        l = l + jnp.exp2(extra - m)
    return acc[:dv] * (1.0 / l)


def _key_chunks(k_ref, vt_ref, lo, n_keys, k_lanes=slice(None), v_rows=slice(None)):
    return [(k_ref[lo + c * KCHUNK:lo + (c + 1) * KCHUNK, k_lanes],
             vt_ref[v_rows, lo + c * KCHUNK:lo + (c + 1) * KCHUNK], None) for c in range(n_keys // KCHUNK)]


def _gqa_rhs(qt, j, tq):
    z = jnp.zeros((HD, tq), qt.dtype)
    first = j == 0
    cols = []
    for g in range(4):
        qg = qt[g * HD:(g + 1) * HD, :]
        cols.append(jnp.concatenate([jnp.where(first, qg, z), jnp.where(first, z, qg)], axis=0))
    return jnp.concatenate(cols, axis=1)


def _store_heads(o_ref, ot, tq, nheads):
    o = jnp.concatenate([ot[:, g * tq:(g + 1) * tq] for g in range(nheads)], axis=0)
    o_ref[...] = o.T.astype(o_ref.dtype)


def _sink_row(sink_ref, base, tq):
    blk = lax.broadcasted_iota(jnp.int32, (1, 4 * tq), 1) // tq
    row = jnp.zeros((1, 4 * tq), F32)
    for g in range(4):
        row = jnp.where(blk == g, sink_ref[base + g], row)
    return row * LOG2E


def _tile(t, size):
    if isinstance(t, int):
        return pl.ds(t * size, size)
    return pl.ds(pl.multiple_of(t * size, size), size)


def _two_stage(score_chunks, value_chunks, rhs_next, s_next, s_cur, m_cur, extra_next, extra_cur, lag=2):
    n_groups = len(rhs_next if rhs_next is not None else m_cur)
    cm = [None] * n_groups
    acc = [None] * n_groups
    parts = []
    row = 0
    for c in range(len(score_chunks or value_chunks)):
        if score_chunks is not None:
            k, bias, g = score_chunks[c]()
            rows = slice(row, row + k.shape[0])
            s = jnp.dot(k, rhs_next[g], preferred_element_type=F32)
            if bias is not None:
                s = s + bias
            s_next[rows, :] = s
            parts.append(jnp.max(s, axis=0, keepdims=True))
            cm[g] = parts[-1] if cm[g] is None else jnp.maximum(cm[g], parts[-1])
        if value_chunks is not None:
            vt, g = value_chunks[c]()
            m_c = m_cur[g]
            if score_chunks is not None and c >= lag:
                m_c = jnp.maximum(m_c, jnp.minimum(parts[c - lag], m_c))
            rows = slice(row, row + vt.shape[1])
            p = jnp.exp2(s_cur[rows, :] - m_c).astype(BF16)
            vt_aug = jnp.concatenate([vt, jnp.ones((ONES_ROWS, vt.shape[1]), BF16)], axis=0)
            pv = jnp.dot(vt_aug, p, preferred_element_type=F32)
            acc[g] = pv if acc[g] is None else acc[g] + pv
        row = rows.stop
    ots = None
    if score_chunks is not None:
        cm = [m if e is None else jnp.maximum(m, e) for m, e in zip(cm, extra_next)]
    if value_chunks is not None:
        ots = []
        for g in range(n_groups):
            dv = acc[g].shape[0] - ONES_ROWS
            l = acc[g][dv:dv + 1]
            if extra_cur[g] is not None:
                l = l + jnp.exp2(extra_cur[g] - m_cur[g])
            ots.append(acc[g][:dv] * (1.0 / l))
    return tuple(cm), ots


def _pipeline(make_rhs, score_chunks, value_chunks, finish, s0, s1, n_units, extra=None, lag=2):
    rhs0 = make_rhs(0)
    ex = (lambda u: [None] * len(rhs0)) if extra is None else extra
    m0, _ = _two_stage(score_chunks(0), None, rhs0, s0, None, None, ex(0), None)

    def body(i, m_even):
        u = 2 * i
        m_odd, ot = _two_stage(score_chunks(u + 1), value_chunks(u), make_rhs(u + 1), s1, s0, m_even,
                               ex(u + 1), ex(u), lag)
        finish(u, ot)
        u2 = jnp.minimum(u + 2, n_units - 1)
        m_next, ot = _two_stage(score_chunks(u2), value_chunks(u + 1), make_rhs(u2), s0, s1, m_odd,
                                ex(u2), ex(u + 1), lag)
        finish(u + 1, ot)
        return m_next

    lax.fori_loop(0, n_units // 2, body, m0)


def _split_unit(u, n_tiles):
    if isinstance(u, int):
        return u // n_tiles, u % n_tiles
    return lax.div(u, n_tiles), lax.rem(u, n_tiles)


def _rows(g, size):
    return _tile(g, size)


def _attn_ac_kernel(sink_ref, bias_ref, qa_ref, qc_ref, ka_ref, kc_ref, vta_ref, vtc_ref, oa_ref, oc_ref,
                    s0, s1, *, tq, n_lat, n_ctx, s_len):
    t_all = ka_ref.shape[0]
    wk = tq + 2 * WINDOW
    spans = [(lo, min(lo + KCHUNK, wk)) for lo in range(0, wk, KCHUNK)]

    def rhs_of(q_ref, j, t):
        return _gqa_rhs(q_ref[_rows(j, 4 * HD), _tile(t, tq)], j, tq)

    def store(o_ref, j, t, ot):
        _store_heads(o_ref.at[_tile(t, tq), _rows(j, 4 * HD)], ot, tq, 4)

    def window(t):
        q0 = t * tq
        start = jnp.clip(q0 - WINDOW, 0, s_len - wk)
        return start, (q0 - start) // WINDOW

    def local(start, lo, hi):
        return pl.ds(pl.multiple_of(start + lo, LANES), hi - lo)

    def score_chunks(u):
        _, t = _split_unit(u, n_lat)
        start, variant = window(t)

        def loc(lo, hi):
            b = bias_ref[variant, lo:hi, :]
            return kc_ref[local(start, lo, hi), :], jnp.concatenate([b] * 4, axis=1), 1

        return ([functools.partial(lambda c: (ka_ref[c * KCHUNK:(c + 1) * KCHUNK, :], None, 0), c)
                 for c in range(t_all // KCHUNK)]
                + [lambda: (kc_ref[s_len:s_len + KCHUNK, :], None, 1)]
                + [functools.partial(loc, lo, hi) for lo, hi in spans])

    def value_chunks(u):
        j, t = _split_unit(u, n_lat)
        start, _ = window(t)
        return ([functools.partial(lambda c: (vta_ref[_rows(j, HD), c * KCHUNK:(c + 1) * KCHUNK], 0), c)
                 for c in range(t_all // KCHUNK)]
                + [lambda: (vtc_ref[_rows(j, HD), s_len:s_len + KCHUNK], 1)]
                + [functools.partial(lambda lo, hi: (vtc_ref[_rows(j, HD), local(start, lo, hi)], 1), lo, hi)
                   for lo, hi in spans])

    def make_rhs(u):
        j, t = _split_unit(u, n_lat)
        return [rhs_of(qa_ref, j, t), rhs_of(qc_ref, j, t)]

    def extra(u):
        j, _ = _split_unit(u, n_lat)
        return [None, _sink_row(sink_ref, j * 4, tq)]

    def finish(u, ots):
        j, t = _split_unit(u, n_lat)
        store(oa_ref, j, t, ots[0])
        store(oc_ref, j, t, ots[1])

    _pipeline(make_rhs, score_chunks, value_chunks, finish, s0, s1, 2 * n_lat, extra=extra)
    ctx_tiles = range(n_lat, n_lat + n_ctx)
    n = 4 * tq
    for j in range(2 if n_ctx else 0):
        for q_ref, k_ref, vt_ref, o_ref, sink in ((qa_ref, ka_ref, vta_ref, oa_ref, None),
                                                  (qc_ref, kc_ref, vtc_ref, oc_ref, _sink_row(sink_ref, j * 4, tq))):
            rhs = jnp.concatenate([rhs_of(q_ref, j, t) for t in ctx_tiles], axis=1)
            extra = None if sink is None else jnp.concatenate([sink] * n_ctx, axis=1)
            chunks = _key_chunks(k_ref, vt_ref, s_len, t_all - s_len, v_rows=_rows(j, HD))
            ot = _flash(chunks, rhs, HD, extra=extra)
            for i, t in enumerate(ctx_tiles):
                store(o_ref, j, t, ot[:, i * n:(i + 1) * n])


def _attn_b_kernel(lamv_ref, subln_ref, qt_ref, k_ref, vt_ref, o_ref, s0, s1, *, tq, n_lat, n_ctx, s_len,
                   lam_init):
    lv = lamv_ref[...]
    lam = (jnp.exp(jnp.sum(lv[0:1] * lv[1:2], axis=-1, keepdims=True))
           - jnp.exp(jnp.sum(lv[2:3] * lv[3:4], axis=-1, keepdims=True)) + lam_init)
    t_all = k_ref.shape[0]
    n_chunks = t_all // KCHUNK
    hw = 2 * HD

    def rhs_of(h, t):
        qt = qt_ref[_rows(h, hw), _tile(t, tq)]
        z = jnp.zeros((HD, tq), qt.dtype)
        return jnp.concatenate([jnp.concatenate([qt[:HD], z], axis=0),
                                jnp.concatenate([z, qt[HD:]], axis=0)], axis=1)

    def store(h, t, o2):
        o = o2[:, :tq] - lam * o2[:, tq:]
        ms = jnp.mean(o * o, axis=0, keepdims=True)
        o = o * lax.rsqrt(ms + SUBLN_EPS) * subln_ref[...] * (1.0 - lam_init)
        o_ref[_tile(t, tq), _rows(h, hw)] = o.T.astype(o_ref.dtype)

    def score_chunks(u):
        h, _ = _split_unit(u, n_lat)
        return [functools.partial(lambda c: (k_ref[c * KCHUNK:(c + 1) * KCHUNK, _rows(h, hw)], None, 0), c)
                for c in range(n_chunks)]

    def value_chunks(u):
        h, _ = _split_unit(u, n_lat)
        return [functools.partial(lambda c: (vt_ref[_rows(h, hw), c * KCHUNK:(c + 1) * KCHUNK], 0), c)
                for c in range(n_chunks)]

    _pipeline(lambda u: [rhs_of(*_split_unit(u, n_lat))], score_chunks, value_chunks,
              lambda u, ots: store(*_split_unit(u, n_lat), ots[0]), s0, s1, 4 * n_lat)
    for h in range(4):
        for t in range(n_lat, n_lat + n_ctx):
            chunks = _key_chunks(k_ref, vt_ref, s_len, t_all - s_len, k_lanes=_rows(h, hw), v_rows=_rows(h, hw))
            store(h, t, _flash(chunks, rhs_of(h, t), hw))


def _window_bias(tq):
    wk = tq + 2 * WINDOW
    r = np.arange(wk)[:, None]
    c = np.arange(tq)[None, :]
    out = np.stack([np.where(np.abs(c - r + v * WINDOW) <= WINDOW, 0.0, -np.inf) for v in range(3)])
    return jnp.asarray(out, F32)


def _attn_calls(qt_all, k_all, vt_all, sink, lamv, subln_t, layer, with_ctx, s_len, lam_init,
                tq_a, tq_b):
    nb, _, t_all = qt_all.shape
    c_len = t_all - s_len
    o_rows = t_all if with_ctx else s_len
    o_shape = jax.ShapeDtypeStruct((nb, o_rows, QW), BF16)

    def steps(tq):
        n_lat = s_len // tq
        return n_lat, n_lat + (c_len // tq if with_ctx else 0)

    whole = lambda rows, cols, r, c: pl.BlockSpec((None, rows, cols), lambda b: (b, r, c))
    o_spec = pl.BlockSpec((None, o_rows, QW), lambda b: (b, 0, 0))
    score_scratch = lambda rows, n: [pltpu.VMEM((rows, n), F32), pltpu.VMEM((rows, n), F32)]

    n_lat, n_all = steps(tq_a)
    wk = tq_a + 2 * WINDOW
    oa, oc = pl.pallas_call(
        functools.partial(_attn_ac_kernel, tq=tq_a, n_lat=n_lat, n_ctx=n_all - n_lat, s_len=s_len),
        out_shape=(o_shape, o_shape),
        grid=(nb,),
        in_specs=[
            pl.BlockSpec(memory_space=pltpu.SMEM),
            pl.BlockSpec((3, wk, tq_a), lambda b: (0, 0, 0)),
            whole(QW, t_all, 0, 0), whole(QW, t_all, 2, 0),
            whole(t_all, LANES, 0, KA_TILE), whole(t_all, LANES, 0, KA_TILE + 1),
            whole(LANES, t_all, KA_TILE, 0), whole(LANES, t_all, KA_TILE + 1, 0),
        ],
        out_specs=(o_spec, o_spec),
        scratch_shapes=score_scratch(t_all + c_len + wk, 4 * tq_a),
        compiler_params=_cparams(1),
        name="attn_ac",
    )(sink[layer], _window_bias(tq_a), qt_all, qt_all, k_all, k_all, vt_all, vt_all)

    n_lat, n_all = steps(tq_b)
    ob = pl.pallas_call(
        functools.partial(_attn_b_kernel, tq=tq_b, n_lat=n_lat, n_ctx=n_all - n_lat, s_len=s_len,
                          lam_init=lam_init),
        out_shape=o_shape,
        grid=(nb,),
        in_specs=[
            pl.BlockSpec((None, 4, HD), lambda b: (layer, 0, 0)),
            pl.BlockSpec((None, 2 * HD, tq_b), lambda b: (layer, 0, 0)),
            whole(QW, t_all, 1, 0), whole(t_all, QW, 0, 0), whole(QW, t_all, 0, 0),
        ],
        out_specs=o_spec,
        scratch_shapes=score_scratch(t_all, 2 * tq_b),
        compiler_params=_cparams(1),
        name="attn_b",
    )(lamv, subln_t, qt_all, k_all, vt_all)

    return oa, ob, oc


def _merge_kernel(*refs, sub, n_sub, ctx_step):
    x_refs = refs[:n_sub]
    rest = refs[n_sub:]
    ctx_ref = None
    if ctx_step is not None:
        ctx_ref, rest = rest[0], rest[1:]
    (oa_ref, ob_ref, oc_ref, modb_ref, modc_ref, gpre_ref, gpost_ref, wgm_ref, bmg_ref, wbr_ref, wout_ref,
     out_ref) = rest

    def gates(i):
        x, is_ctx = _sub_tile_input(x_refs, ctx_ref, i, ctx_step)
        mod = jnp.where(is_ctx, modc_ref[...], modb_ref[...])
        h = _modulated_norm(x, mod, gpre_ref[...])
        gm = jnp.dot(h.astype(BF16), wgm_ref[...], preferred_element_type=F32)
        return x, mod, gm

    def finish(i, x, mod, gm):
        rows = slice(i * sub, (i + 1) * sub)
        z = None
        for j, o_ref in enumerate((oa_ref, ob_ref, oc_ref)):
            g = gm[:, j * QW:(j + 1) * QW]
            u = (o_ref[rows, :].astype(F32) * (g * _sigmoid(g))).astype(BF16)
            p = jnp.dot(u, wbr_ref[j], preferred_element_type=F32)
            mg = _sigmoid(gm[:, N_G + j * D:N_G + (j + 1) * D] + bmg_ref[:, j * D:(j + 1) * D])
            z = mg * p if z is None else z + mg * p
        y = jnp.dot(z.astype(BF16), wout_ref[...], preferred_element_type=F32)
        ms = jnp.mean(y * y, axis=-1, keepdims=True)
        gate = mod[:, 2 * D:]
        out_ref[rows, :] = x + gate * (y * lax.rsqrt(ms + EPS) * gpost_ref[...])

    prev = gates(0)
    for i in range(1, n_sub):
        cur = gates(i)
        finish(i - 1, *prev)
        prev = cur
    finish(n_sub - 1, *prev)


def _merge_call(x_src, ctx_src, oa, ob, oc, mod4, g_pre, g_post, wgm, b_mg, wbr, wout, layer, s_len, rows_out,
                tm, sub):
    nb = x_src.shape[0]
    n_sub = tm // sub
    tok_specs, tok_args = _token_specs(x_src, ctx_src, n_sub, sub, s_len)
    ctx_step = None if ctx_src is None else rows_out // tm - 1
    tok = lambda b, t: (b, t, 0)
    lay2 = lambda b, t: (layer, 0, 0)
    const = dict(pipeline_mode=pl.Buffered(1))
    return pl.pallas_call(
        functools.partial(_merge_kernel, sub=sub, n_sub=n_sub, ctx_step=ctx_step),
        out_shape=jax.ShapeDtypeStruct((nb, rows_out, D), F32),
        grid=(nb, rows_out // tm),
        in_specs=tok_specs + [
            pl.BlockSpec((None, tm, QW), tok),
            pl.BlockSpec((None, tm, QW), tok),
            pl.BlockSpec((None, tm, QW), tok),
            pl.BlockSpec((None, None, 1, 3 * D), lambda b, t: (layer, b, 0, 0)),
            pl.BlockSpec((None, None, 1, 3 * D), lambda b, t: (layer, nb, 0, 0)),
            pl.BlockSpec((None, 1, D), lay2),
            pl.BlockSpec((None, 1, D), lay2),
            pl.BlockSpec((None, D, N_G + N_M), lay2, **const),
            pl.BlockSpec((None, 1, N_M), lay2),
            pl.BlockSpec((None, 3, QW, D), lambda b, t: (layer, 0, 0, 0), **const),
            pl.BlockSpec((None, D, D), lay2, **const),
        ],
        out_specs=pl.BlockSpec((None, tm, D), tok),
        compiler_params=_cparams(2),
        name="merge",
    )(*tok_args, oa, ob, oc, mod4, mod4, g_pre, g_post, wgm, b_mg, wbr, wout)


def _rope_tables(s_len, c_len):
    rows = s_len // GRID_W
    row = np.repeat(np.arange(rows), GRID_W).astype(np.float32)
    col = np.tile(np.arange(GRID_W), rows).astype(np.float32)
    freqs = (np.float32(ROPE_THETA) ** (-np.arange(ROPE_PAIRS, dtype=np.float32) / ROPE_PAIRS)).astype(np.float32)
    ang_r = row[:, None] * freqs
    ang_c = col[:, None] * freqs
    ang = np.concatenate([ang_r, ang_r, ang_c, ang_c], axis=-1)
    cos = np.concatenate([np.cos(ang), np.ones((c_len, HD), np.float32)], axis=0).astype(np.float32)
    sin = np.concatenate([np.sin(ang), np.zeros((c_len, HD), np.float32)], axis=0).astype(np.float32)
    first = (np.arange(HD) % 32) < 16
    sin_a = np.where(first, -sin, np.float32(0.0))
    sin_b = np.where(first, np.float32(0.0), sin)
    tile2 = lambda a: np.concatenate([a, a], axis=-1)
    cost = np.ascontiguousarray(cos.T)
    sint = np.ascontiguousarray((sin_a + sin_b).T)
    return tuple(jnp.asarray(a, F32) for a in (cost, sint, tile2(cos), tile2(sin_a), tile2(sin_b)))


def _cols(w, names):
    return jnp.concatenate([w[..., _IN[n][0]:_IN[n][1]] for n in names], axis=-1)


def kernel(x, c, ctx, c_ctx, w_ada, b_ada, g_pre, g_post, w_in, q_norm, k_norm, lam_q1, lam_k1, lam_q2,
           lam_k2, subln, sink, w_br_a, w_br_b, w_br_c, w_mg, b_mg, w_out):
    nb, s_len, _ = x.shape
    c_len = ctx.shape[1]
    depth = w_in.shape[0]
    sub = 256
    tm_all = 3 * sub
    tm_lat = 2 * sub
    tq_a, tq_b = 128, 256
    assert (s_len + c_len) % tm_all == 0 and s_len % tm_lat == 0 and s_len % sub == 0

    wp = _cols(w_in, ("qa", "qb", "qc", "kb", "ka", "kc", "vb", "va", "vc")).astype(BF16)
    wgm = jnp.concatenate([_cols(w_in, ("ga", "gb", "gc")), w_mg], axis=-1).astype(BF16)
    wbr = jnp.stack([w_br_a, w_br_b, w_br_c], axis=1).astype(BF16)
    wout = w_out.astype(BF16)

    tabs = _rope_tables(s_len, c_len)
    gq_t = jnp.broadcast_to(q_norm[:, :, None], (depth, HD, sub))
    gk_n = jnp.concatenate([k_norm, k_norm], axis=-1)[:, None, :]
    lamv = jnp.stack([lam_q1, lam_k1, lam_q2, lam_k2], axis=1)
    subln_t = jnp.broadcast_to(subln[:, :, None], (depth, 2 * HD, tq_b))
    g_pre3 = g_pre[:, None, :]
    g_post3 = g_post[:, None, :]
    b_mg3 = b_mg[:, None, :]

    rows = ((nb + 1 + 7) // 8) * 8
    sc_in = jnp.concatenate([c, c_ctx[None, :], jnp.zeros((rows - nb - 1, D), F32)], axis=0)
    mod = _ada_call(sc_in, w_ada, b_ada)
    mod4 = mod[:, :, None, :]

    t_all = s_len + c_len
    x_src, ctx_src = x, (ctx, 0)
    for layer in range(depth):
        last = layer == depth - 1
        lam_init = 0.8 - 0.6 * math.exp(-0.3 * layer)
        qt_all, k_all, vt_all = _proj_call(x_src, ctx_src, mod4, g_pre3, wp, tabs, gq_t, gk_n, layer, s_len,
                                           t_all, tm_all, sub)
        oa, ob, oc = _attn_calls(qt_all, k_all, vt_all, sink, lamv, subln_t, layer, not last, s_len,
                                 lam_init, tq_a, tq_b)
        rows_out, tm = (s_len, tm_lat) if last else (t_all, tm_all)
        xs = _merge_call(x_src, None if last else ctx_src, oa, ob, oc, mod4, g_pre3, g_post3, wgm, b_mg3, wbr,
                         wout, layer, s_len, rows_out, tm, sub)
        x_src, ctx_src = xs, (xs, s_len // sub)
    return xs
```

```python
import functools
import math

import jax
import jax.numpy as jnp
import numpy as np
from jax import lax
from jax.experimental import pallas as pl
from jax.experimental.pallas import tpu as pltpu

F32 = jnp.float32
BF16 = jnp.bfloat16

D = 1024
HD = 64
GRID_W = 64
WINDOW = 128
ROPE_THETA = 10000.0
ROPE_PAIRS = HD // 4
EPS = 1e-6
SUBLN_EPS = 1e-5
ATTN_SCALE = HD ** -0.5
LOG2E = math.log2(math.e)

QW = 512
N_Q = 3 * QW
N_K = 512 + 128 + 128
N_V = 512 + 128 + 128
KA_TILE = 4
N_P = N_Q + N_K + N_V
N_G = 3 * QW
N_M = 3 * D

LANES = 128
VMEM_LIMIT = 56 * 1024 * 1024

_IN = dict(qa=(0, 512), ka=(512, 640), va=(640, 768), ga=(768, 1280),
           qb=(1280, 1792), kb=(1792, 2304), vb=(2304, 2816), gb=(2816, 3328),
           qc=(3328, 3840), kc=(3840, 3968), vc=(3968, 4096), gc=(4096, 4608))


def _sigmoid(v):
    return 1.0 / (1.0 + jnp.exp(-v))


def _cparams(n_axes):
    return pltpu.CompilerParams(dimension_semantics=("arbitrary",) * n_axes,
                                vmem_limit_bytes=VMEM_LIMIT)


def _ada_kernel(sc_ref, w_ref, b_ref, o_ref):
    v = sc_ref[...]
    s = (v * _sigmoid(v)).astype(BF16)
    o_ref[...] = jnp.dot(s, w_ref[...].astype(BF16), preferred_element_type=F32) + b_ref[...]


def _ada_call(sc_in, w_ada, b_ada):
    depth = w_ada.shape[0]
    rows = sc_in.shape[0]
    nblk = 3
    return pl.pallas_call(
        _ada_kernel,
        out_shape=jax.ShapeDtypeStruct((depth, rows, 3 * D), F32),
        grid=(depth, nblk),
        in_specs=[
            pl.BlockSpec((rows, D), lambda l, n: (0, 0)),
            pl.BlockSpec((None, D, D), lambda l, n: (l, 0, n)),
            pl.BlockSpec((None, 1, D), lambda l, n: (l, 0, n)),
        ],
        out_specs=pl.BlockSpec((None, rows, D), lambda l, n: (l, 0, n)),
        compiler_params=_cparams(2),
        name="adaln",
    )(sc_in, w_ada, b_ada.reshape(depth, 1, 3 * D))


def _modulated_norm(x, mod, gpre):
    shift = mod[:, :D]
    scale = mod[:, D:2 * D]
    ms = jnp.mean(x * x, axis=-1, keepdims=True)
    return x * lax.rsqrt(ms + EPS) * gpre * (1.0 + scale) + shift


def _sub_tile_input(x_refs, ctx_ref, i, ctx_step):
    x = x_refs[i][...]
    if ctx_ref is None or i != len(x_refs) - 1:
        return x, False
    is_ctx = pl.program_id(1) == ctx_step
    return jnp.where(is_ctx, ctx_ref[...], x), is_ctx


def _proj_kernel(*refs, sub, n_sub, ctx_step):
    x_refs, ctx_ref = refs[:n_sub], refs[n_sub]
    (modb_ref, modc_ref, gpre_ref, w_ref, cost_ref, sint_ref, cosn_ref, sina_ref, sinb_ref, gq_ref, gk_ref,
     qt_ref, k_ref, vt_ref) = refs[n_sub + 1:]

    def project(i):
        x, is_ctx = _sub_tile_input(x_refs, ctx_ref, i, ctx_step)
        mod = jnp.where(is_ctx, modc_ref[...], modb_ref[...])
        h = _modulated_norm(x, mod, gpre_ref[...])
        return jnp.dot(h.astype(BF16), w_ref[...], preferred_element_type=F32)

    def finish(i, y):
        rows = slice(i * sub, (i + 1) * sub)
        q3 = y[:, :N_Q].T.reshape(N_Q // HD, HD, sub)
        qa = q3[:8]
        ss = jnp.sum(qa * qa, axis=1, keepdims=True)
        qa = qa * lax.rsqrt(ss * (1.0 / HD) + EPS) * gq_ref[...][None]
        q3 = jnp.concatenate([qa, q3[8:]], axis=0)
        rot = jnp.concatenate([q3[:, 16:32], q3[:, 0:16], q3[:, 48:64], q3[:, 32:48]], axis=1)
        q3 = (q3 * cost_ref[:, rows][None] + rot * sint_ref[:, rows][None]) * (ATTN_SCALE * LOG2E)
        qt_ref[:, rows] = q3.reshape(N_Q, sub).astype(BF16)

        ka = y[:, N_Q + KA_TILE * LANES:N_Q + (KA_TILE + 1) * LANES]
        lane = lax.broadcasted_iota(jnp.int32, (1, LANES), 1)
        lo = lane < HD
        sq = ka * ka
        s_lo = jnp.sum(jnp.where(lo, sq, 0.0), axis=-1, keepdims=True)
        s_hi = jnp.sum(jnp.where(lo, 0.0, sq), axis=-1, keepdims=True)
        r = jnp.where(lo, lax.rsqrt(s_lo * (1.0 / HD) + EPS), lax.rsqrt(s_hi * (1.0 / HD) + EPS))
        ka = ka * r * gk_ref[...]
        cosn = cosn_ref[rows, :]
        sina = sina_ref[rows, :]
        sinb = sinb_ref[rows, :]
        for j in range(N_K // LANES):
            t = ka if j == KA_TILE else y[:, N_Q + j * LANES:N_Q + (j + 1) * LANES]
            t = t * cosn + pltpu.roll(t, LANES - 16, 1) * sina + pltpu.roll(t, 16, 1) * sinb
            k_ref[rows, j * LANES:(j + 1) * LANES] = t.astype(BF16)

        vt_ref[:, rows] = y[:, N_Q + N_K:].T.astype(BF16)

    y_prev = project(0)
    for i in range(1, n_sub):
        y = project(i)
        finish(i - 1, y_prev)
        y_prev = y
    finish(n_sub - 1, y_prev)


def _token_specs(x_src, ctx_src, n_sub, sub, s_len):
    last = s_len // sub - 1
    specs = [pl.BlockSpec((None, sub, D), functools.partial(lambda i, b, t: (b, jnp.minimum(n_sub * t + i, last), 0), i))
             for i in range(n_sub)]
    operands = [x_src] * n_sub
    if ctx_src is not None:
        arr, blk = ctx_src
        specs.append(pl.BlockSpec((None, sub, D), lambda b, t: (b, blk, 0)))
        operands.append(arr)
    return specs, operands


def _proj_call(x_src, ctx_src, mod4, g_pre, wp, tabs, gq_t, gk_n, layer, s_len, t_all, tm, sub):
    nb = x_src.shape[0]
    cost, sint, cosn, sina, sinb = tabs
    n_sub = tm // sub
    tok_specs, tok_args = _token_specs(x_src, ctx_src, n_sub, sub, s_len)
    return pl.pallas_call(
        functools.partial(_proj_kernel, sub=sub, n_sub=n_sub, ctx_step=t_all // tm - 1),
        out_shape=(jax.ShapeDtypeStruct((nb, N_Q, t_all), BF16),
                   jax.ShapeDtypeStruct((nb, t_all, N_K), BF16),
                   jax.ShapeDtypeStruct((nb, N_V, t_all), BF16)),
        grid=(nb, t_all // tm),
        in_specs=tok_specs + [
            pl.BlockSpec((None, None, 1, 3 * D), lambda b, t: (layer, b, 0, 0)),
            pl.BlockSpec((None, None, 1, 3 * D), lambda b, t: (layer, nb, 0, 0)),
            pl.BlockSpec((None, 1, D), lambda b, t: (layer, 0, 0)),
            pl.BlockSpec((None, D, N_P), lambda b, t: (layer, 0, 0)),
            pl.BlockSpec((HD, tm), lambda b, t: (0, t)),
            pl.BlockSpec((HD, tm), lambda b, t: (0, t)),
            pl.BlockSpec((tm, LANES), lambda b, t: (t, 0)),
            pl.BlockSpec((tm, LANES), lambda b, t: (t, 0)),
            pl.BlockSpec((tm, LANES), lambda b, t: (t, 0)),
            pl.BlockSpec((None, HD, sub), lambda b, t: (layer, 0, 0)),
            pl.BlockSpec((None, 1, LANES), lambda b, t: (layer, 0, 0)),
        ],
        out_specs=(pl.BlockSpec((None, N_Q, tm), lambda b, t: (b, 0, t)),
                   pl.BlockSpec((None, tm, N_K), lambda b, t: (b, t, 0)),
                   pl.BlockSpec((None, N_V, tm), lambda b, t: (b, 0, t))),
        compiler_params=_cparams(2),
        name="proj",
    )(*tok_args, mod4, mod4, g_pre, wp, cost, sint, cosn, sina, sinb, gq_t, gk_n)


KCHUNK = 256
ONES_ROWS = 16


def _flash(chunks, rhs, dv, extra=None):
    m = extra
    acc = None
    for k, vt, bias in chunks:
        s = jnp.dot(k, rhs, preferred_element_type=F32)
        if bias is not None:
            s = s + bias
        cm = jnp.max(s, axis=0, keepdims=True)
        m_new = cm if m is None else jnp.maximum(m, cm)
        p = jnp.exp2(s - m_new).astype(BF16)
        vt_aug = jnp.concatenate([vt, jnp.ones((ONES_ROWS, vt.shape[1]), BF16)], axis=0)
        pv = jnp.dot(vt_aug, p, preferred_element_type=F32)
        acc = pv if acc is None else acc * jnp.exp2(m - m_new) + pv
        m = m_new
    l = acc[dv:dv + 1]
    if extra is not None:
        l = l + jnp.exp2(extra - m)
    return acc[:dv] * (1.0 / l)


def _key_chunks(k_ref, vt_ref, lo, n_keys, k_lanes=slice(None), v_rows=slice(None)):
    return [(k_ref[lo + c * KCHUNK:lo + (c + 1) * KCHUNK, k_lanes],
             vt_ref[v_rows, lo + c * KCHUNK:lo + (c + 1) * KCHUNK], None) for c in range(n_keys // KCHUNK)]


def _gqa_rhs(qt, j, tq):
    z = jnp.zeros((HD, tq), qt.dtype)
    first = j == 0
    cols = []
    for g in range(4):
        qg = qt[g * HD:(g + 1) * HD, :]
        cols.append(jnp.concatenate([jnp.where(first, qg, z), jnp.where(first, z, qg)], axis=0))
    return jnp.concatenate(cols, axis=1)


def _store_heads(o_ref, ot, tq, nheads):
    o = jnp.concatenate([ot[:, g * tq:(g + 1) * tq] for g in range(nheads)], axis=0)
    o_ref[...] = o.T.astype(o_ref.dtype)


def _sink_row(sink_ref, base, tq):
    blk = lax.broadcasted_iota(jnp.int32, (1, 4 * tq), 1) // tq
    row = jnp.zeros((1, 4 * tq), F32)
    for g in range(4):
        row = jnp.where(blk == g, sink_ref[base + g], row)
    return row * LOG2E


def _tile(t, size):
    if isinstance(t, int):
        return pl.ds(t * size, size)
    return pl.ds(pl.multiple_of(t * size, size), size)


def _two_stage(score_chunks, value_chunks, rhs_next, s_next, s_cur, m_cur, extra_next, extra_cur, lag=2):
    n_groups = len(rhs_next if rhs_next is not None else m_cur)
    cm = [None] * n_groups
    acc = [None] * n_groups
    parts = []
    row = 0
    for c in range(len(score_chunks or value_chunks)):
        if score_chunks is not None:
            k, bias, g = score_chunks[c]()
            rows = slice(row, row + k.shape[0])
            s = jnp.dot(k, rhs_next[g], preferred_element_type=F32)
            if bias is not None:
                s = s + bias
            s_next[rows, :] = s
            parts.append(jnp.max(s, axis=0, keepdims=True))
            cm[g] = parts[-1] if cm[g] is None else jnp.maximum(cm[g], parts[-1])
        if value_chunks is not None:
            vt, g = value_chunks[c]()
            m_c = m_cur[g]
            if score_chunks is not None and c >= lag:
                m_c = jnp.maximum(m_c, jnp.minimum(parts[c - lag], m_c))
            rows = slice(row, row + vt.shape[1])
            p = jnp.exp2(s_cur[rows, :] - m_c).astype(BF16)
            vt_aug = jnp.concatenate([vt, jnp.ones((ONES_ROWS, vt.shape[1]), BF16)], axis=0)
            pv = jnp.dot(vt_aug, p, preferred_element_type=F32)
            acc[g] = pv if acc[g] is None else acc[g] + pv
        row = rows.stop
    ots = None
    if score_chunks is not None:
        cm = [m if e is None else jnp.maximum(m, e) for m, e in zip(cm, extra_next)]
    if value_chunks is not None:
        ots = []
        for g in range(n_groups):
            dv = acc[g].shape[0] - ONES_ROWS
            l = acc[g][dv:dv + 1]
            if extra_cur[g] is not None:
                l = l + jnp.exp2(extra_cur[g] - m_cur[g])
            ots.append(acc[g][:dv] * (1.0 / l))
    return tuple(cm), ots


def _pipeline(make_rhs, score_chunks, value_chunks, finish, s0, s1, n_units, extra=None, lag=2):
    rhs0 = make_rhs(0)
    ex = (lambda u: [None] * len(rhs0)) if extra is None else extra
    m0, _ = _two_stage(score_chunks(0), None, rhs0, s0, None, None, ex(0), None)

    def body(i, m_even):
        u = 2 * i
        m_odd, ot = _two_stage(score_chunks(u + 1), value_chunks(u), make_rhs(u + 1), s1, s0, m_even,
                               ex(u + 1), ex(u), lag)
        finish(u, ot)
        u2 = jnp.minimum(u + 2, n_units - 1)
        m_next, ot = _two_stage(score_chunks(u2), value_chunks(u + 1), make_rhs(u2), s0, s1, m_odd,
                                ex(u2), ex(u + 1), lag)
        finish(u + 1, ot)
        return m_next

    lax.fori_loop(0, n_units // 2, body, m0)


def _split_unit(u, n_tiles):
    if isinstance(u, int):
        return u // n_tiles, u % n_tiles
    return lax.div(u, n_tiles), lax.rem(u, n_tiles)


def _rows(g, size):
    return _tile(g, size)


def _attn_ac_kernel(sink_ref, bias_ref, qa_ref, qc_ref, ka_ref, kc_ref, vta_ref, vtc_ref, oa_ref, oc_ref,
                    s0, s1, *, tq, n_lat, n_ctx, s_len):
    t_all = ka_ref.shape[0]
    wk = tq + 2 * WINDOW
    spans = [(lo, min(lo + KCHUNK, wk)) for lo in range(0, wk, KCHUNK)]

    def rhs_of(q_ref, j, t):
        return _gqa_rhs(q_ref[_rows(j, 4 * HD), _tile(t, tq)], j, tq)

    def store(o_ref, j, t, ot):
        _store_heads(o_ref.at[_tile(t, tq), _rows(j, 4 * HD)], ot, tq, 4)

    def window(t):
        q0 = t * tq
        start = jnp.clip(q0 - WINDOW, 0, s_len - wk)
        return start, (q0 - start) // WINDOW

    def local(start, lo, hi):
        return pl.ds(pl.multiple_of(start + lo, LANES), hi - lo)

    def score_chunks(u):
        _, t = _split_unit(u, n_lat)
        start, variant = window(t)

        def loc(lo, hi):
            b = bias_ref[variant, lo:hi, :]
            return kc_ref[local(start, lo, hi), :], jnp.concatenate([b] * 4, axis=1), 1

        return ([functools.partial(lambda c: (ka_ref[c * KCHUNK:(c + 1) * KCHUNK, :], None, 0), c)
                 for c in range(t_all // KCHUNK)]
                + [lambda: (kc_ref[s_len:s_len + KCHUNK, :], None, 1)]
                + [functools.partial(loc, lo, hi) for lo, hi in spans])

    def value_chunks(u):
        j, t = _split_unit(u, n_lat)
        start, _ = window(t)
        return ([functools.partial(lambda c: (vta_ref[_rows(j, HD), c * KCHUNK:(c + 1) * KCHUNK], 0), c)
                 for c in range(t_all // KCHUNK)]
                + [lambda: (vtc_ref[_rows(j, HD), s_len:s_len + KCHUNK], 1)]
                + [functools.partial(lambda lo, hi: (vtc_ref[_rows(j, HD), local(start, lo, hi)], 1), lo, hi)
                   for lo, hi in spans])

    def make_rhs(u):
        j, t = _split_unit(u, n_lat)
        return [rhs_of(qa_ref, j, t), rhs_of(qc_ref, j, t)]

    def extra(u):
        j, _ = _split_unit(u, n_lat)
        return [None, _sink_row(sink_ref, j * 4, tq)]

    def finish(u, ots):
        j, t = _split_unit(u, n_lat)
        store(oa_ref, j, t, ots[0])
        store(oc_ref, j, t, ots[1])

    _pipeline(make_rhs, score_chunks, value_chunks, finish, s0, s1, 2 * n_lat, extra=extra, lag=4)
    ctx_tiles = range(n_lat, n_lat + n_ctx)
    n = 4 * tq
    for j in range(2 if n_ctx else 0):
        for q_ref, k_ref, vt_ref, o_ref, sink in ((qa_ref, ka_ref, vta_ref, oa_ref, None),
                                                  (qc_ref, kc_ref, vtc_ref, oc_ref, _sink_row(sink_ref, j * 4, tq))):
            rhs = jnp.concatenate([rhs_of(q_ref, j, t) for t in ctx_tiles], axis=1)
            extra = None if sink is None else jnp.concatenate([sink] * n_ctx, axis=1)
            chunks = _key_chunks(k_ref, vt_ref, s_len, t_all - s_len, v_rows=_rows(j, HD))
            ot = _flash(chunks, rhs, HD, extra=extra)
            for i, t in enumerate(ctx_tiles):
                store(o_ref, j, t, ot[:, i * n:(i + 1) * n])


def _attn_b_kernel(lamv_ref, subln_ref, qt_ref, k_ref, vt_ref, o_ref, s0, s1, *, tq, n_lat, n_ctx, s_len,
                   lam_init):
    lv = lamv_ref[...]
    lam = (jnp.exp(jnp.sum(lv[0:1] * lv[1:2], axis=-1, keepdims=True))
           - jnp.exp(jnp.sum(lv[2:3] * lv[3:4], axis=-1, keepdims=True)) + lam_init)
    t_all = k_ref.shape[0]
    n_chunks = t_all // KCHUNK
    hw = 2 * HD

    def rhs_of(h, t):
        qt = qt_ref[_rows(h, hw), _tile(t, tq)]
        z = jnp.zeros((HD, tq), qt.dtype)
        return jnp.concatenate([jnp.concatenate([qt[:HD], z], axis=0),
                                jnp.concatenate([z, qt[HD:]], axis=0)], axis=1)

    def store(h, t, o2):
        o = o2[:, :tq] - lam * o2[:, tq:]
        ms = jnp.mean(o * o, axis=0, keepdims=True)
        o = o * lax.rsqrt(ms + SUBLN_EPS) * subln_ref[...] * (1.0 - lam_init)
        o_ref[_tile(t, tq), _rows(h, hw)] = o.T.astype(o_ref.dtype)

    def score_chunks(u):
        h, _ = _split_unit(u, n_lat)
        return [functools.partial(lambda c: (k_ref[c * KCHUNK:(c + 1) * KCHUNK, _rows(h, hw)], None, 0), c)
                for c in range(n_chunks)]

    def value_chunks(u):
        h, _ = _split_unit(u, n_lat)
        return [functools.partial(lambda c: (vt_ref[_rows(h, hw), c * KCHUNK:(c + 1) * KCHUNK], 0), c)
                for c in range(n_chunks)]

    _pipeline(lambda u: [rhs_of(*_split_unit(u, n_lat))], score_chunks, value_chunks,
              lambda u, ots: store(*_split_unit(u, n_lat), ots[0]), s0, s1, 4 * n_lat)
    for h in range(4):
        for t in range(n_lat, n_lat + n_ctx):
            chunks = _key_chunks(k_ref, vt_ref, s_len, t_all - s_len, k_lanes=_rows(h, hw), v_rows=_rows(h, hw))
            store(h, t, _flash(chunks, rhs_of(h, t), hw))


def _window_bias(tq):
    wk = tq + 2 * WINDOW
    r = np.arange(wk)[:, None]
    c = np.arange(tq)[None, :]
    out = np.stack([np.where(np.abs(c - r + v * WINDOW) <= WINDOW, 0.0, -np.inf) for v in range(3)])
    return jnp.asarray(out, F32)


def _attn_calls(qt_all, k_all, vt_all, sink, lamv, subln_t, layer, with_ctx, s_len, lam_init,
                tq_a, tq_b):
    nb, _, t_all = qt_all.shape
    c_len = t_all - s_len
    o_rows = t_all if with_ctx else s_len
    o_shape = jax.ShapeDtypeStruct((nb, o_rows, QW), BF16)

    def steps(tq):
        n_lat = s_len // tq
        return n_lat, n_lat + (c_len // tq if with_ctx else 0)

    whole = lambda rows, cols, r, c: pl.BlockSpec((None, rows, cols), lambda b: (b, r, c))
    o_spec = pl.BlockSpec((None, o_rows, QW), lambda b: (b, 0, 0))
    score_scratch = lambda rows, n: [pltpu.VMEM((rows, n), F32), pltpu.VMEM((rows, n), F32)]

    n_lat, n_all = steps(tq_a)
    wk = tq_a + 2 * WINDOW
    oa, oc = pl.pallas_call(
        functools.partial(_attn_ac_kernel, tq=tq_a, n_lat=n_lat, n_ctx=n_all - n_lat, s_len=s_len),
        out_shape=(o_shape, o_shape),
        grid=(nb,),
        in_specs=[
            pl.BlockSpec(memory_space=pltpu.SMEM),
            pl.BlockSpec((3, wk, tq_a), lambda b: (0, 0, 0)),
            whole(QW, t_all, 0, 0), whole(QW, t_all, 2, 0),
            whole(t_all, LANES, 0, KA_TILE), whole(t_all, LANES, 0, KA_TILE + 1),
            whole(LANES, t_all, KA_TILE, 0), whole(LANES, t_all, KA_TILE + 1, 0),
        ],
        out_specs=(o_spec, o_spec),
        scratch_shapes=score_scratch(t_all + c_len + wk, 4 * tq_a),
        compiler_params=_cparams(1),
        name="attn_ac",
    )(sink[layer], _window_bias(tq_a), qt_all, qt_all, k_all, k_all, vt_all, vt_all)

    n_lat, n_all = steps(tq_b)
    ob = pl.pallas_call(
        functools.partial(_attn_b_kernel, tq=tq_b, n_lat=n_lat, n_ctx=n_all - n_lat, s_len=s_len,
                          lam_init=lam_init),
        out_shape=o_shape,
        grid=(nb,),
        in_specs=[
            pl.BlockSpec((None, 4, HD), lambda b: (layer, 0, 0)),
            pl.BlockSpec((None, 2 * HD, tq_b), lambda b: (layer, 0, 0)),
            whole(QW, t_all, 1, 0), whole(t_all, QW, 0, 0), whole(QW, t_all, 0, 0),
        ],
        out_specs=o_spec,
        scratch_shapes=score_scratch(t_all, 2 * tq_b),
        compiler_params=_cparams(1),
        name="attn_b",
    )(lamv, subln_t, qt_all, k_all, vt_all)

    return oa, ob, oc


def _merge_kernel(*refs, sub, n_sub, ctx_step):
    x_refs = refs[:n_sub]
    rest = refs[n_sub:]
    ctx_ref = None
    if ctx_step is not None:
        ctx_ref, rest = rest[0], rest[1:]
    (oa_ref, ob_ref, oc_ref, modb_ref, modc_ref, gpre_ref, gpost_ref, wgm_ref, bmg_ref, wbr_ref, wout_ref,
     out_ref) = rest

    def gates(i):
        x, is_ctx = _sub_tile_input(x_refs, ctx_ref, i, ctx_step)
        mod = jnp.where(is_ctx, modc_ref[...], modb_ref[...])
        h = _modulated_norm(x, mod, gpre_ref[...])
        gm = jnp.dot(h.astype(BF16), wgm_ref[...], preferred_element_type=F32)
        return x, mod, gm

    def finish(i, x, mod, gm):
        rows = slice(i * sub, (i + 1) * sub)
        z = None
        for j, o_ref in enumerate((oa_ref, ob_ref, oc_ref)):
            g = gm[:, j * QW:(j + 1) * QW]
            u = (o_ref[rows, :].astype(F32) * (g * _sigmoid(g))).astype(BF16)
            p = jnp.dot(u, wbr_ref[j], preferred_element_type=F32)
            mg = _sigmoid(gm[:, N_G + j * D:N_G + (j + 1) * D] + bmg_ref[:, j * D:(j + 1) * D])
            z = mg * p if z is None else z + mg * p
        y = jnp.dot(z.astype(BF16), wout_ref[...], preferred_element_type=F32)
        ms = jnp.mean(y * y, axis=-1, keepdims=True)
        gate = mod[:, 2 * D:]
        out_ref[rows, :] = x + gate * (y * lax.rsqrt(ms + EPS) * gpost_ref[...])

    prev = gates(0)
    for i in range(1, n_sub):
        cur = gates(i)
        finish(i - 1, *prev)
        prev = cur
    finish(n_sub - 1, *prev)


def _merge_call(x_src, ctx_src, oa, ob, oc, mod4, g_pre, g_post, wgm, b_mg, wbr, wout, layer, s_len, rows_out,
                tm, sub):
    nb = x_src.shape[0]
    n_sub = tm // sub
    tok_specs, tok_args = _token_specs(x_src, ctx_src, n_sub, sub, s_len)
    ctx_step = None if ctx_src is None else rows_out // tm - 1
    tok = lambda b, t: (b, t, 0)
    lay2 = lambda b, t: (layer, 0, 0)
    const = dict(pipeline_mode=pl.Buffered(1))
    return pl.pallas_call(
        functools.partial(_merge_kernel, sub=sub, n_sub=n_sub, ctx_step=ctx_step),
        out_shape=jax.ShapeDtypeStruct((nb, rows_out, D), F32),
        grid=(nb, rows_out // tm),
        in_specs=tok_specs + [
            pl.BlockSpec((None, tm, QW), tok),
            pl.BlockSpec((None, tm, QW), tok),
            pl.BlockSpec((None, tm, QW), tok),
            pl.BlockSpec((None, None, 1, 3 * D), lambda b, t: (layer, b, 0, 0)),
            pl.BlockSpec((None, None, 1, 3 * D), lambda b, t: (layer, nb, 0, 0)),
            pl.BlockSpec((None, 1, D), lay2),
            pl.BlockSpec((None, 1, D), lay2),
            pl.BlockSpec((None, D, N_G + N_M), lay2, **const),
            pl.BlockSpec((None, 1, N_M), lay2),
            pl.BlockSpec((None, 3, QW, D), lambda b, t: (layer, 0, 0, 0), **const),
            pl.BlockSpec((None, D, D), lay2, **const),
        ],
        out_specs=pl.BlockSpec((None, tm, D), tok),
        compiler_params=_cparams(2),
        name="merge",
    )(*tok_args, oa, ob, oc, mod4, mod4, g_pre, g_post, wgm, b_mg, wbr, wout)


def _rope_tables(s_len, c_len):
    rows = s_len // GRID_W
    row = np.repeat(np.arange(rows), GRID_W).astype(np.float32)
    col = np.tile(np.arange(GRID_W), rows).astype(np.float32)
    freqs = (np.float32(ROPE_THETA) ** (-np.arange(ROPE_PAIRS, dtype=np.float32) / ROPE_PAIRS)).astype(np.float32)
    ang_r = row[:, None] * freqs
    ang_c = col[:, None] * freqs
    ang = np.concatenate([ang_r, ang_r, ang_c, ang_c], axis=-1)
    cos = np.concatenate([np.cos(ang), np.ones((c_len, HD), np.float32)], axis=0).astype(np.float32)
    sin = np.concatenate([np.sin(ang), np.zeros((c_len, HD), np.float32)], axis=0).astype(np.float32)
    first = (np.arange(HD) % 32) < 16
    sin_a = np.where(first, -sin, np.float32(0.0))
    sin_b = np.where(first, np.float32(0.0), sin)
    tile2 = lambda a: np.concatenate([a, a], axis=-1)
    cost = np.ascontiguousarray(cos.T)
    sint = np.ascontiguousarray((sin_a + sin_b).T)
    return tuple(jnp.asarray(a, F32) for a in (cost, sint, tile2(cos), tile2(sin_a), tile2(sin_b)))


def _cols(w, names):
    return jnp.concatenate([w[..., _IN[n][0]:_IN[n][1]] for n in names], axis=-1)


def kernel(x, c, ctx, c_ctx, w_ada, b_ada, g_pre, g_post, w_in, q_norm, k_norm, lam_q1, lam_k1, lam_q2,
           lam_k2, subln, sink, w_br_a, w_br_b, w_br_c, w_mg, b_mg, w_out):
    nb, s_len, _ = x.shape
    c_len = ctx.shape[1]
    depth = w_in.shape[0]
    sub = 256
    tm_all = 3 * sub
    tm_lat = 4 * sub
    tq_a, tq_b = 128, 256
    assert (s_len + c_len) % tm_all == 0 and s_len % tm_lat == 0 and s_len % sub == 0

    wp = _cols(w_in, ("qa", "qb", "qc", "kb", "ka", "kc", "vb", "va", "vc")).astype(BF16)
    wgm = jnp.concatenate([_cols(w_in, ("ga", "gb", "gc")), w_mg], axis=-1).astype(BF16)
    wbr = jnp.stack([w_br_a, w_br_b, w_br_c], axis=1).astype(BF16)
    wout = w_out.astype(BF16)

    tabs = _rope_tables(s_len, c_len)
    gq_t = jnp.broadcast_to(q_norm[:, :, None], (depth, HD, sub))
    gk_n = jnp.concatenate([k_norm, k_norm], axis=-1)[:, None, :]
    lamv = jnp.stack([lam_q1, lam_k1, lam_q2, lam_k2], axis=1)
    subln_t = jnp.broadcast_to(subln[:, :, None], (depth, 2 * HD, tq_b))
    g_pre3 = g_pre[:, None, :]
    g_post3 = g_post[:, None, :]
    b_mg3 = b_mg[:, None, :]

    rows = ((nb + 1 + 7) // 8) * 8
    sc_in = jnp.concatenate([c, c_ctx[None, :], jnp.zeros((rows - nb - 1, D), F32)], axis=0)
    mod = _ada_call(sc_in, w_ada, b_ada)
    mod4 = mod[:, :, None, :]

    t_all = s_len + c_len
    x_src, ctx_src = x, (ctx, 0)
    for layer in range(depth):
        last = layer == depth - 1
        lam_init = 0.8 - 0.6 * math.exp(-0.3 * layer)
        qt_all, k_all, vt_all = _proj_call(x_src, ctx_src, mod4, g_pre3, wp, tabs, gq_t, gk_n, layer, s_len,
                                           t_all, tm_all, sub)
        oa, ob, oc = _attn_calls(qt_all, k_all, vt_all, sink, lamv, subln_t, layer, not last, s_len,
                                 lam_init, tq_a, tq_b)
        rows_out, tm = (s_len, tm_lat) if last else (t_all, tm_all)
        xs = _merge_call(x_src, None if last else ctx_src, oa, ob, oc, mod4, g_pre3, g_post3, wgm, b_mg3, wbr,
                         wout, layer, s_len, rows_out, tm, sub)
        x_src, ctx_src = xs, (xs, s_len // sub)
    return xs
```

```python
import functools
import math

import jax
import jax.numpy as jnp
import numpy as np
from jax import lax
from jax.experimental import pallas as pl
from jax.experimental.pallas import tpu as pltpu

F32 = jnp.float32
BF16 = jnp.bfloat16

D = 1024
HD = 64
GRID_W = 64
WINDOW = 128
ROPE_THETA = 10000.0
ROPE_PAIRS = HD // 4
EPS = 1e-6
SUBLN_EPS = 1e-5
ATTN_SCALE = HD ** -0.5
LOG2E = math.log2(math.e)

QW = 512
N_Q = 3 * QW
N_K = 512 + 128 + 128
N_V = 512 + 128 + 128
KA_TILE = 4
N_P = N_Q + N_K + N_V
N_G = 3 * QW
N_M = 3 * D

LANES = 128
VMEM_LIMIT = 56 * 1024 * 1024

_IN = dict(qa=(0, 512), ka=(512, 640), va=(640, 768), ga=(768, 1280),
           qb=(1280, 1792), kb=(1792, 2304), vb=(2304, 2816), gb=(2816, 3328),
           qc=(3328, 3840), kc=(3840, 3968), vc=(3968, 4096), gc=(4096, 4608))


def _sigmoid(v):
    return 1.0 / (1.0 + jnp.exp(-v))


def _cparams(n_axes):
    return pltpu.CompilerParams(dimension_semantics=("arbitrary",) * n_axes,
                                vmem_limit_bytes=VMEM_LIMIT)


def _ada_kernel(sc_ref, w_ref, b_ref, o_ref):
    v = sc_ref[...]
    s = (v * _sigmoid(v)).astype(BF16)
    o_ref[...] = jnp.dot(s, w_ref[...].astype(BF16), preferred_element_type=F32) + b_ref[...]


def _ada_call(sc_in, w_ada, b_ada):
    depth = w_ada.shape[0]
    rows = sc_in.shape[0]
    nblk = 3
    return pl.pallas_call(
        _ada_kernel,
        out_shape=jax.ShapeDtypeStruct((depth, rows, 3 * D), F32),
        grid=(depth, nblk),
        in_specs=[
            pl.BlockSpec((rows, D), lambda l, n: (0, 0)),
            pl.BlockSpec((None, D, D), lambda l, n: (l, 0, n)),
            pl.BlockSpec((None, 1, D), lambda l, n: (l, 0, n)),
        ],
        out_specs=pl.BlockSpec((None, rows, D), lambda l, n: (l, 0, n)),
        compiler_params=_cparams(2),
        name="adaln",
    )(sc_in, w_ada, b_ada.reshape(depth, 1, 3 * D))


def _modulated_norm(x, mod, gpre):
    shift = mod[:, :D]
    scale = mod[:, D:2 * D]
    ms = jnp.mean(x * x, axis=-1, keepdims=True)
    return x * lax.rsqrt(ms + EPS) * gpre * (1.0 + scale) + shift


def _sub_tile_input(x_refs, ctx_ref, i, ctx_step):
    x = x_refs[i][...]
    if ctx_ref is None or i != len(x_refs) - 1:
        return x, False
    is_ctx = pl.program_id(1) == ctx_step
    return jnp.where(is_ctx, ctx_ref[...], x), is_ctx


def _proj_kernel(*refs, sub, n_sub, ctx_step):
    x_refs, ctx_ref = refs[:n_sub], refs[n_sub]
    (modb_ref, modc_ref, gpre_ref, w_ref, cost_ref, sint_ref, cosn_ref, sina_ref, sinb_ref, gq_ref, gk_ref,
     qt_ref, k_ref, vt_ref) = refs[n_sub + 1:]

    def project(i):
        x, is_ctx = _sub_tile_input(x_refs, ctx_ref, i, ctx_step)
        mod = jnp.where(is_ctx, modc_ref[...], modb_ref[...])
        h = _modulated_norm(x, mod, gpre_ref[...])
        return jnp.dot(h.astype(BF16), w_ref[...], preferred_element_type=F32)

    def finish(i, y):
        rows = slice(i * sub, (i + 1) * sub)
        q3 = y[:, :N_Q].T.reshape(N_Q // HD, HD, sub)
        qa = q3[:8]
        ss = jnp.sum(qa * qa, axis=1, keepdims=True)
        qa = qa * lax.rsqrt(ss * (1.0 / HD) + EPS) * gq_ref[...][None]
        q3 = jnp.concatenate([qa, q3[8:]], axis=0)
        rot = jnp.concatenate([q3[:, 16:32], q3[:, 0:16], q3[:, 48:64], q3[:, 32:48]], axis=1)
        q3 = (q3 * cost_ref[:, rows][None] + rot * sint_ref[:, rows][None]) * (ATTN_SCALE * LOG2E)
        qt_ref[:, rows] = q3.reshape(N_Q, sub).astype(BF16)

        ka = y[:, N_Q + KA_TILE * LANES:N_Q + (KA_TILE + 1) * LANES]
        lane = lax.broadcasted_iota(jnp.int32, (1, LANES), 1)
        lo = lane < HD
        sq = ka * ka
        s_lo = jnp.sum(jnp.where(lo, sq, 0.0), axis=-1, keepdims=True)
        s_hi = jnp.sum(jnp.where(lo, 0.0, sq), axis=-1, keepdims=True)
        r = jnp.where(lo, lax.rsqrt(s_lo * (1.0 / HD) + EPS), lax.rsqrt(s_hi * (1.0 / HD) + EPS))
        ka = ka * r * gk_ref[...]
        cosn = cosn_ref[rows, :]
        sina = sina_ref[rows, :]
        sinb = sinb_ref[rows, :]
        for j in range(N_K // LANES):
            t = ka if j == KA_TILE else y[:, N_Q + j * LANES:N_Q + (j + 1) * LANES]
            t = t * cosn + pltpu.roll(t, LANES - 16, 1) * sina + pltpu.roll(t, 16, 1) * sinb
            k_ref[rows, j * LANES:(j + 1) * LANES] = t.astype(BF16)

        vt_ref[:, rows] = y[:, N_Q + N_K:].T.astype(BF16)

    y_prev = project(0)
    for i in range(1, n_sub):
        y = project(i)
        finish(i - 1, y_prev)
        y_prev = y
    finish(n_sub - 1, y_prev)


def _token_specs(x_src, ctx_src, n_sub, sub, s_len):
    last = s_len // sub - 1
    specs = [pl.BlockSpec((None, sub, D), functools.partial(lambda i, b, t: (b, jnp.minimum(n_sub * t + i, last), 0), i))
             for i in range(n_sub)]
    operands = [x_src] * n_sub
    if ctx_src is not None:
        arr, blk = ctx_src
        specs.append(pl.BlockSpec((None, sub, D), lambda b, t: (b, blk, 0)))
        operands.append(arr)
    return specs, operands


def _proj_call(x_src, ctx_src, mod4, g_pre, wp, tabs, gq_t, gk_n, layer, s_len, t_all, tm, sub):
    nb = x_src.shape[0]
    cost, sint, cosn, sina, sinb = tabs
    n_sub = tm // sub
    tok_specs, tok_args = _token_specs(x_src, ctx_src, n_sub, sub, s_len)
    return pl.pallas_call(
        functools.partial(_proj_kernel, sub=sub, n_sub=n_sub, ctx_step=t_all // tm - 1),
        out_shape=(jax.ShapeDtypeStruct((nb, N_Q, t_all), BF16),
                   jax.ShapeDtypeStruct((nb, t_all, N_K), BF16),
                   jax.ShapeDtypeStruct((nb, N_V, t_all), BF16)),
        grid=(nb, t_all // tm),
        in_specs=tok_specs + [
            pl.BlockSpec((None, None, 1, 3 * D), lambda b, t: (layer, b, 0, 0)),
            pl.BlockSpec((None, None, 1, 3 * D), lambda b, t: (layer, nb, 0, 0)),
            pl.BlockSpec((None, 1, D), lambda b, t: (layer, 0, 0)),
            pl.BlockSpec((None, D, N_P), lambda b, t: (layer, 0, 0)),
            pl.BlockSpec((HD, tm), lambda b, t: (0, t)),
            pl.BlockSpec((HD, tm), lambda b, t: (0, t)),
            pl.BlockSpec((tm, LANES), lambda b, t: (t, 0)),
            pl.BlockSpec((tm, LANES), lambda b, t: (t, 0)),
            pl.BlockSpec((tm, LANES), lambda b, t: (t, 0)),
            pl.BlockSpec((None, HD, sub), lambda b, t: (layer, 0, 0)),
            pl.BlockSpec((None, 1, LANES), lambda b, t: (layer, 0, 0)),
        ],
        out_specs=(pl.BlockSpec((None, N_Q, tm), lambda b, t: (b, 0, t)),
                   pl.BlockSpec((None, tm, N_K), lambda b, t: (b, t, 0)),
                   pl.BlockSpec((None, N_V, tm), lambda b, t: (b, 0, t))),
        compiler_params=_cparams(2),
        name="proj",
    )(*tok_args, mod4, mod4, g_pre, wp, cost, sint, cosn, sina, sinb, gq_t, gk_n)


KCHUNK = 256
ONES_ROWS = 16


def _flash(chunks, rhs, dv, extra=None):
    m = extra
    acc = None
    for k, vt, bias in chunks:
        s = jnp.dot(k, rhs, preferred_element_type=F32)
        if bias is not None:
            s = s + bias
        cm = jnp.max(s, axis=0, keepdims=True)
        m_new = cm if m is None else jnp.maximum(m, cm)
        p = jnp.exp2(s - m_new).astype(BF16)
        vt_aug = jnp.concatenate([vt, jnp.ones((ONES_ROWS, vt.shape[1]), BF16)], axis=0)
        pv = jnp.dot(vt_aug, p, preferred_element_type=F32)
        acc = pv if acc is None else acc * jnp.exp2(m - m_new) + pv
        m = m_new
    l = acc[dv:dv + 1]
    if extra is not None:
        l = l + jnp.exp2(extra - m)
    return acc[:dv] * (1.0 / l)


def _key_chunks(k_ref, vt_ref, lo, n_keys, k_lanes=slice(None), v_rows=slice(None)):
    return [(k_ref[lo + c * KCHUNK:lo + (c + 1) * KCHUNK, k_lanes],
             vt_ref[v_rows, lo + c * KCHUNK:lo + (c + 1) * KCHUNK], None) for c in range(n_keys // KCHUNK)]


def _gqa_rhs(qt, j, tq):
    z = jnp.zeros((HD, tq), qt.dtype)
    first = j == 0
    cols = []
    for g in range(4):
        qg = qt[g * HD:(g + 1) * HD, :]
        cols.append(jnp.concatenate([jnp.where(first, qg, z), jnp.where(first, z, qg)], axis=0))
    return jnp.concatenate(cols, axis=1)


def _store_heads(o_ref, ot, tq, nheads):
    o = jnp.concatenate([ot[:, g * tq:(g + 1) * tq] for g in range(nheads)], axis=0)
    o_ref[...] = o.T.astype(o_ref.dtype)


def _sink_row(sink_ref, base, tq):
    blk = lax.broadcasted_iota(jnp.int32, (1, 4 * tq), 1) // tq
    row = jnp.zeros((1, 4 * tq), F32)
    for g in range(4):
        row = jnp.where(blk == g, sink_ref[base + g], row)
    return row * LOG2E


def _tile(t, size):
    if isinstance(t, int):
        return pl.ds(t * size, size)
    return pl.ds(pl.multiple_of(t * size, size), size)


def _two_stage(score_chunks, value_chunks, rhs_next, s_next, s_cur, m_cur, extra_next, extra_cur, lag=2,
               zero=None):
    n_groups = len(rhs_next if rhs_next is not None else m_cur)
    cm = [None] * n_groups
    acc = [None] * n_groups
    parts = []
    row = 0
    for c in range(len(score_chunks or value_chunks)):
        if score_chunks is not None:
            k, bias, g = score_chunks[c]()
            rows = slice(row, row + k.shape[0])
            s = jnp.dot(k, rhs_next[g], preferred_element_type=F32)
            if bias is not None:
                s = s + bias
            s_next[rows, :] = s
            parts.append(jnp.max(s, axis=0, keepdims=True))
            cm[g] = parts[-1] if cm[g] is None else jnp.maximum(cm[g], parts[-1])
        if value_chunks is not None:
            vt, g = value_chunks[c]()
            m_c = m_cur[g]
            if score_chunks is not None and c >= lag:
                m_c = jnp.maximum(m_c, jnp.minimum(parts[c - lag], m_c))
            rows = slice(row, row + vt.shape[1])
            src_rows = rows if zero is None else pl.ds(pl.multiple_of(row + zero, LANES), vt.shape[1])
            p = jnp.exp2(s_cur[src_rows, :] - m_c).astype(BF16)
            vt_aug = jnp.concatenate([vt, jnp.ones((ONES_ROWS, vt.shape[1]), BF16)], axis=0)
            pv = jnp.dot(vt_aug, p, preferred_element_type=F32)
            acc[g] = pv if acc[g] is None else acc[g] + pv
        row = rows.stop
    ots = None
    if score_chunks is not None:
        cm = [m if e is None else jnp.maximum(m, e) for m, e in zip(cm, extra_next)]
    if value_chunks is not None:
        ots = []
        for g in range(n_groups):
            dv = acc[g].shape[0] - ONES_ROWS
            l = acc[g][dv:dv + 1]
            if extra_cur[g] is not None:
                l = l + jnp.exp2(extra_cur[g] - m_cur[g])
            ots.append(acc[g][:dv] * (1.0 / l))
    return tuple(cm), ots


def _pipeline(make_rhs, score_chunks, value_chunks, finish, s0, s1, n_units, extra=None, lag=2, zero=None):
    rhs0 = make_rhs(0)
    ex = (lambda u: [None] * len(rhs0)) if extra is None else extra
    m0, _ = _two_stage(score_chunks(0), None, rhs0, s0, None, None, ex(0), None)

    def body(i, m_even):
        u = 2 * i
        m_odd, ot = _two_stage(score_chunks(u + 1), value_chunks(u), make_rhs(u + 1), s1, s0, m_even,
                               ex(u + 1), ex(u), lag, zero)
        finish(u, ot)
        u2 = jnp.minimum(u + 2, n_units - 1)
        m_next, ot = _two_stage(score_chunks(u2), value_chunks(u + 1), make_rhs(u2), s0, s1, m_odd,
                                ex(u2), ex(u + 1), lag, zero)
        finish(u + 1, ot)
        return m_next

    lax.fori_loop(0, n_units // 2, body, m0)


def _split_unit(u, n_tiles):
    if isinstance(u, int):
        return u // n_tiles, u % n_tiles
    return lax.div(u, n_tiles), lax.rem(u, n_tiles)


def _rows(g, size):
    return _tile(g, size)


def _attn_ac_kernel(zero_ref, sink_ref, bias_ref, qa_ref, qc_ref, ka_ref, kc_ref, vta_ref, vtc_ref, oa_ref, oc_ref,
                    s0, s1, *, tq, n_lat, n_ctx, s_len):
    t_all = ka_ref.shape[0]
    wk = tq + 2 * WINDOW
    spans = [(lo, min(lo + KCHUNK, wk)) for lo in range(0, wk, KCHUNK)]

    def rhs_of(q_ref, j, t):
        return _gqa_rhs(q_ref[_rows(j, 4 * HD), _tile(t, tq)], j, tq)

    def store(o_ref, j, t, ot):
        _store_heads(o_ref.at[_tile(t, tq), _rows(j, 4 * HD)], ot, tq, 4)

    def window(t):
        q0 = t * tq
        start = jnp.clip(q0 - WINDOW, 0, s_len - wk)
        return start, (q0 - start) // WINDOW

    def local(start, lo, hi):
        return pl.ds(pl.multiple_of(start + lo, LANES), hi - lo)

    def score_chunks(u):
        _, t = _split_unit(u, n_lat)
        start, variant = window(t)

        def loc(lo, hi):
            b = bias_ref[variant, lo:hi, :]
            return kc_ref[local(start, lo, hi), :], jnp.concatenate([b] * 4, axis=1), 1

        return ([functools.partial(lambda c: (ka_ref[c * KCHUNK:(c + 1) * KCHUNK, :], None, 0), c)
                 for c in range(t_all // KCHUNK)]
                + [lambda: (kc_ref[s_len:s_len + KCHUNK, :], None, 1)]
                + [functools.partial(loc, lo, hi) for lo, hi in spans])

    def value_chunks(u):
        j, t = _split_unit(u, n_lat)
        start, _ = window(t)
        return ([functools.partial(lambda c: (vta_ref[_rows(j, HD), c * KCHUNK:(c + 1) * KCHUNK], 0), c)
                 for c in range(t_all // KCHUNK)]
                + [lambda: (vtc_ref[_rows(j, HD), s_len:s_len + KCHUNK], 1)]
                + [functools.partial(lambda lo, hi: (vtc_ref[_rows(j, HD), local(start, lo, hi)], 1), lo, hi)
                   for lo, hi in spans])

    def make_rhs(u):
        j, t = _split_unit(u, n_lat)
        return [rhs_of(qa_ref, j, t), rhs_of(qc_ref, j, t)]

    def extra(u):
        j, _ = _split_unit(u, n_lat)
        return [None, _sink_row(sink_ref, j * 4, tq)]

    def finish(u, ots):
        j, t = _split_unit(u, n_lat)
        store(oa_ref, j, t, ots[0])
        store(oc_ref, j, t, ots[1])

    _pipeline(make_rhs, score_chunks, value_chunks, finish, s0, s1, 2 * n_lat, extra=extra, lag=4, zero=zero_ref[0])
    ctx_tiles = range(n_lat, n_lat + n_ctx)
    n = 4 * tq
    for j in range(2 if n_ctx else 0):
        for q_ref, k_ref, vt_ref, o_ref, sink in ((qa_ref, ka_ref, vta_ref, oa_ref, None),
                                                  (qc_ref, kc_ref, vtc_ref, oc_ref, _sink_row(sink_ref, j * 4, tq))):
            rhs = jnp.concatenate([rhs_of(q_ref, j, t) for t in ctx_tiles], axis=1)
            extra = None if sink is None else jnp.concatenate([sink] * n_ctx, axis=1)
            chunks = _key_chunks(k_ref, vt_ref, s_len, t_all - s_len, v_rows=_rows(j, HD))
            ot = _flash(chunks, rhs, HD, extra=extra)
            for i, t in enumerate(ctx_tiles):
                store(o_ref, j, t, ot[:, i * n:(i + 1) * n])


def _attn_b_kernel(zero_ref, lamv_ref, subln_ref, qt_ref, k_ref, vt_ref, o_ref, s0, s1, *, tq, n_lat, n_ctx, s_len,
                   lam_init):
    lv = lamv_ref[...]
    lam = (jnp.exp(jnp.sum(lv[0:1] * lv[1:2], axis=-1, keepdims=True))
           - jnp.exp(jnp.sum(lv[2:3] * lv[3:4], axis=-1, keepdims=True)) + lam_init)
    t_all = k_ref.shape[0]
    n_chunks = t_all // KCHUNK
    hw = 2 * HD

    def rhs_of(h, t):
        qt = qt_ref[_rows(h, hw), _tile(t, tq)]
        z = jnp.zeros((HD, tq), qt.dtype)
        return jnp.concatenate([jnp.concatenate([qt[:HD], z], axis=0),
                                jnp.concatenate([z, qt[HD:]], axis=0)], axis=1)

    def store(h, t, o2):
        o = o2[:, :tq] - lam * o2[:, tq:]
        ms = jnp.mean(o * o, axis=0, keepdims=True)
        o = o * lax.rsqrt(ms + SUBLN_EPS) * subln_ref[...] * (1.0 - lam_init)
        o_ref[_tile(t, tq), _rows(h, hw)] = o.T.astype(o_ref.dtype)

    def score_chunks(u):
        h, _ = _split_unit(u, n_lat)
        return [functools.partial(lambda c: (k_ref[c * KCHUNK:(c + 1) * KCHUNK, _rows(h, hw)], None, 0), c)
                for c in range(n_chunks)]

    def value_chunks(u):
        h, _ = _split_unit(u, n_lat)
        return [functools.partial(lambda c: (vt_ref[_rows(h, hw), c * KCHUNK:(c + 1) * KCHUNK], 0), c)
                for c in range(n_chunks)]

    _pipeline(lambda u: [rhs_of(*_split_unit(u, n_lat))], score_chunks, value_chunks,
              lambda u, ots: store(*_split_unit(u, n_lat), ots[0]), s0, s1, 4 * n_lat, zero=zero_ref[0])
    for h in range(4):
        for t in range(n_lat, n_lat + n_ctx):
            chunks = _key_chunks(k_ref, vt_ref, s_len, t_all - s_len, k_lanes=_rows(h, hw), v_rows=_rows(h, hw))
            store(h, t, _flash(chunks, rhs_of(h, t), hw))


def _window_bias(tq):
    wk = tq + 2 * WINDOW
    r = np.arange(wk)[:, None]
    c = np.arange(tq)[None, :]
    out = np.stack([np.where(np.abs(c - r + v * WINDOW) <= WINDOW, 0.0, -np.inf) for v in range(3)])
    return jnp.asarray(out, F32)


def _attn_calls(qt_all, k_all, vt_all, sink, lamv, subln_t, layer, with_ctx, s_len, lam_init,
                tq_a, tq_b):
    nb, _, t_all = qt_all.shape
    c_len = t_all - s_len
    o_rows = t_all if with_ctx else s_len
    o_shape = jax.ShapeDtypeStruct((nb, o_rows, QW), BF16)

    def steps(tq):
        n_lat = s_len // tq
        return n_lat, n_lat + (c_len // tq if with_ctx else 0)

    whole = lambda rows, cols, r, c: pl.BlockSpec((None, rows, cols), lambda b: (b, r, c))
    o_spec = pl.BlockSpec((None, o_rows, QW), lambda b: (b, 0, 0))
    score_scratch = lambda rows, n: [pltpu.VMEM((rows, n), F32), pltpu.VMEM((rows, n), F32)]
    zero = jnp.zeros((1,), jnp.int32)

    n_lat, n_all = steps(tq_a)
    wk = tq_a + 2 * WINDOW
    oa, oc = pl.pallas_call(
        functools.partial(_attn_ac_kernel, tq=tq_a, n_lat=n_lat, n_ctx=n_all - n_lat, s_len=s_len),
        out_shape=(o_shape, o_shape),
        grid=(nb,),
        in_specs=[
            pl.BlockSpec(memory_space=pltpu.SMEM),
            pl.BlockSpec(memory_space=pltpu.SMEM),
            pl.BlockSpec((3, wk, tq_a), lambda b: (0, 0, 0)),
            whole(QW, t_all, 0, 0), whole(QW, t_all, 2, 0),
            whole(t_all, LANES, 0, KA_TILE), whole(t_all, LANES, 0, KA_TILE + 1),
            whole(LANES, t_all, KA_TILE, 0), whole(LANES, t_all, KA_TILE + 1, 0),
        ],
        out_specs=(o_spec, o_spec),
        scratch_shapes=score_scratch(t_all + c_len + wk, 4 * tq_a),
        compiler_params=_cparams(1),
        name="attn_ac",
    )(zero, sink[layer], _window_bias(tq_a), qt_all, qt_all, k_all, k_all, vt_all, vt_all)

    n_lat, n_all = steps(tq_b)
    ob = pl.pallas_call(
        functools.partial(_attn_b_kernel, tq=tq_b, n_lat=n_lat, n_ctx=n_all - n_lat, s_len=s_len,
                          lam_init=lam_init),
        out_shape=o_shape,
        grid=(nb,),
        in_specs=[
            pl.BlockSpec(memory_space=pltpu.SMEM),
            pl.BlockSpec((None, 4, HD), lambda b: (layer, 0, 0)),
            pl.BlockSpec((None, 2 * HD, tq_b), lambda b: (layer, 0, 0)),
            whole(QW, t_all, 1, 0), whole(t_all, QW, 0, 0), whole(QW, t_all, 0, 0),
        ],
        out_specs=o_spec,
        scratch_shapes=score_scratch(t_all, 2 * tq_b),
        compiler_params=_cparams(1),
        name="attn_b",
    )(zero, lamv, subln_t, qt_all, k_all, vt_all)

    return oa, ob, oc


def _merge_kernel(*refs, sub, n_sub, ctx_step):
    x_refs = refs[:n_sub]
    rest = refs[n_sub:]
    ctx_ref = None
    if ctx_step is not None:
        ctx_ref, rest = rest[0], rest[1:]
    (oa_ref, ob_ref, oc_ref, modb_ref, modc_ref, gpre_ref, gpost_ref, wgm_ref, bmg_ref, wbr_ref, wout_ref,
     out_ref) = rest

    def gates(i):
        x, is_ctx = _sub_tile_input(x_refs, ctx_ref, i, ctx_step)
        mod = jnp.where(is_ctx, modc_ref[...], modb_ref[...])
        h = _modulated_norm(x, mod, gpre_ref[...])
        gm = jnp.dot(h.astype(BF16), wgm_ref[...], preferred_element_type=F32)
        return x, mod, gm

    def finish(i, x, mod, gm):
        rows = slice(i * sub, (i + 1) * sub)
        z = None
        for j, o_ref in enumerate((oa_ref, ob_ref, oc_ref)):
            g = gm[:, j * QW:(j + 1) * QW]
            u = (o_ref[rows, :].astype(F32) * (g * _sigmoid(g))).astype(BF16)
            p = jnp.dot(u, wbr_ref[j], preferred_element_type=F32)
            mg = _sigmoid(gm[:, N_G + j * D:N_G + (j + 1) * D] + bmg_ref[:, j * D:(j + 1) * D])
            z = mg * p if z is None else z + mg * p
        y = jnp.dot(z.astype(BF16), wout_ref[...], preferred_element_type=F32)
        ms = jnp.mean(y * y, axis=-1, keepdims=True)
        gate = mod[:, 2 * D:]
        out_ref[rows, :] = x + gate * (y * lax.rsqrt(ms + EPS) * gpost_ref[...])

    prev = gates(0)
    for i in range(1, n_sub):
        cur = gates(i)
        finish(i - 1, *prev)
        prev = cur
    finish(n_sub - 1, *prev)


def _merge_call(x_src, ctx_src, oa, ob, oc, mod4, g_pre, g_post, wgm, b_mg, wbr, wout, layer, s_len, rows_out,
                tm, sub):
    nb = x_src.shape[0]
    n_sub = tm // sub
    tok_specs, tok_args = _token_specs(x_src, ctx_src, n_sub, sub, s_len)
    ctx_step = None if ctx_src is None else rows_out // tm - 1
    tok = lambda b, t: (b, t, 0)
    lay2 = lambda b, t: (layer, 0, 0)
    const = dict(pipeline_mode=pl.Buffered(1))
    return pl.pallas_call(
        functools.partial(_merge_kernel, sub=sub, n_sub=n_sub, ctx_step=ctx_step),
        out_shape=jax.ShapeDtypeStruct((nb, rows_out, D), F32),
        grid=(nb, rows_out // tm),
        in_specs=tok_specs + [
            pl.BlockSpec((None, tm, QW), tok),
            pl.BlockSpec((None, tm, QW), tok),
            pl.BlockSpec((None, tm, QW), tok),
            pl.BlockSpec((None, None, 1, 3 * D), lambda b, t: (layer, b, 0, 0)),
            pl.BlockSpec((None, None, 1, 3 * D), lambda b, t: (layer, nb, 0, 0)),
            pl.BlockSpec((None, 1, D), lay2),
            pl.BlockSpec((None, 1, D), lay2),
            pl.BlockSpec((None, D, N_G + N_M), lay2, **const),
            pl.BlockSpec((None, 1, N_M), lay2),
            pl.BlockSpec((None, 3, QW, D), lambda b, t: (layer, 0, 0, 0), **const),
            pl.BlockSpec((None, D, D), lay2, **const),
        ],
        out_specs=pl.BlockSpec((None, tm, D), tok),
        compiler_params=_cparams(2),
        name="merge",
    )(*tok_args, oa, ob, oc, mod4, mod4, g_pre, g_post, wgm, b_mg, wbr, wout)


def _rope_tables(s_len, c_len):
    rows = s_len // GRID_W
    row = np.repeat(np.arange(rows), GRID_W).astype(np.float32)
    col = np.tile(np.arange(GRID_W), rows).astype(np.float32)
    freqs = (np.float32(ROPE_THETA) ** (-np.arange(ROPE_PAIRS, dtype=np.float32) / ROPE_PAIRS)).astype(np.float32)
    ang_r = row[:, None] * freqs
    ang_c = col[:, None] * freqs
    ang = np.concatenate([ang_r, ang_r, ang_c, ang_c], axis=-1)
    cos = np.concatenate([np.cos(ang), np.ones((c_len, HD), np.float32)], axis=0).astype(np.float32)
    sin = np.concatenate([np.sin(ang), np.zeros((c_len, HD), np.float32)], axis=0).astype(np.float32)
    first = (np.arange(HD) % 32) < 16
    sin_a = np.where(first, -sin, np.float32(0.0))
    sin_b = np.where(first, np.float32(0.0), sin)
    tile2 = lambda a: np.concatenate([a, a], axis=-1)
    cost = np.ascontiguousarray(cos.T)
    sint = np.ascontiguousarray((sin_a + sin_b).T)
    return tuple(jnp.asarray(a, F32) for a in (cost, sint, tile2(cos), tile2(sin_a), tile2(sin_b)))


def _cols(w, names):
    return jnp.concatenate([w[..., _IN[n][0]:_IN[n][1]] for n in names], axis=-1)


def kernel(x, c, ctx, c_ctx, w_ada, b_ada, g_pre, g_post, w_in, q_norm, k_norm, lam_q1, lam_k1, lam_q2,
           lam_k2, subln, sink, w_br_a, w_br_b, w_br_c, w_mg, b_mg, w_out):
    nb, s_len, _ = x.shape
    c_len = ctx.shape[1]
    depth = w_in.shape[0]
    sub = 256
    tm_all = 3 * sub
    tm_lat = 4 * sub
    tq_a, tq_b = 128, 256
    assert (s_len + c_len) % tm_all == 0 and s_len % tm_lat == 0 and s_len % sub == 0

    wp = _cols(w_in, ("qa", "qb", "qc", "kb", "ka", "kc", "vb", "va", "vc")).astype(BF16)
    wgm = jnp.concatenate([_cols(w_in, ("ga", "gb", "gc")), w_mg], axis=-1).astype(BF16)
    wbr = jnp.stack([w_br_a, w_br_b, w_br_c], axis=1).astype(BF16)
    wout = w_out.astype(BF16)

    tabs = _rope_tables(s_len, c_len)
    gq_t = jnp.broadcast_to(q_norm[:, :, None], (depth, HD, sub))
    gk_n = jnp.concatenate([k_norm, k_norm], axis=-1)[:, None, :]
    lamv = jnp.stack([lam_q1, lam_k1, lam_q2, lam_k2], axis=1)
    subln_t = jnp.broadcast_to(subln[:, :, None], (depth, 2 * HD, tq_b))
    g_pre3 = g_pre[:, None, :]
    g_post3 = g_post[:, None, :]
    b_mg3 = b_mg[:, None, :]

    rows = ((nb + 1 + 7) // 8) * 8
    sc_in = jnp.concatenate([c, c_ctx[None, :], jnp.zeros((rows - nb - 1, D), F32)], axis=0)
    mod = _ada_call(sc_in, w_ada, b_ada)
    mod4 = mod[:, :, None, :]

    t_all = s_len + c_len
    x_src, ctx_src = x, (ctx, 0)
    for layer in range(depth):
        last = layer == depth - 1
        lam_init = 0.8 - 0.6 * math.exp(-0.3 * layer)
        qt_all, k_all, vt_all = _proj_call(x_src, ctx_src, mod4, g_pre3, wp, tabs, gq_t, gk_n, layer, s_len,
                                           t_all, tm_all, sub)
        oa, ob, oc = _attn_calls(qt_all, k_all, vt_all, sink, lamv, subln_t, layer, not last, s_len,
                                 lam_init, tq_a, tq_b)
        rows_out, tm = (s_len, tm_lat) if last else (t_all, tm_all)
        xs = _merge_call(x_src, None if last else ctx_src, oa, ob, oc, mod4, g_pre3, g_post3, wgm, b_mg3, wbr,
                         wout, layer, s_len, rows_out, tm, sub)
        x_src, ctx_src = xs, (xs, s_len // sub)
    return xs
```

```python
import functools
import math

import jax
import jax.numpy as jnp
import numpy as np
from jax import lax
from jax.experimental import pallas as pl
from jax.experimental.pallas import tpu as pltpu

F32 = jnp.float32
BF16 = jnp.bfloat16

D = 1024
HD = 64
GRID_W = 64
WINDOW = 128
ROPE_THETA = 10000.0
ROPE_PAIRS = HD // 4
EPS = 1e-6
SUBLN_EPS = 1e-5
ATTN_SCALE = HD ** -0.5
LOG2E = math.log2(math.e)

QW = 512
N_Q = 3 * QW
N_K = 512 + 128 + 128
N_V = 512 + 128 + 128
KA_TILE = 4
N_P = N_Q + N_K + N_V
N_G = 3 * QW
N_M = 3 * D

LANES = 128
VMEM_LIMIT = 56 * 1024 * 1024

_IN = dict(qa=(0, 512), ka=(512, 640), va=(640, 768), ga=(768, 1280),
           qb=(1280, 1792), kb=(1792, 2304), vb=(2304, 2816), gb=(2816, 3328),
           qc=(3328, 3840), kc=(3840, 3968), vc=(3968, 4096), gc=(4096, 4608))


def _sigmoid(v):
    return 1.0 / (1.0 + jnp.exp(-v))


def _cparams(n_axes):
    return pltpu.CompilerParams(dimension_semantics=("arbitrary",) * n_axes,
                                vmem_limit_bytes=VMEM_LIMIT)


def _ada_kernel(sc_ref, w_ref, b_ref, o_ref):
    v = sc_ref[...]
    s = (v * _sigmoid(v)).astype(BF16)
    o_ref[...] = jnp.dot(s, w_ref[...].astype(BF16), preferred_element_type=F32) + b_ref[...]


def _ada_call(sc_in, w_ada, b_ada):
    depth = w_ada.shape[0]
    rows = sc_in.shape[0]
    nblk = 3
    return pl.pallas_call(
        _ada_kernel,
        out_shape=jax.ShapeDtypeStruct((depth, rows, 3 * D), F32),
        grid=(depth, nblk),
        in_specs=[
            pl.BlockSpec((rows, D), lambda l, n: (0, 0)),
            pl.BlockSpec((None, D, D), lambda l, n: (l, 0, n)),
            pl.BlockSpec((None, 1, D), lambda l, n: (l, 0, n)),
        ],
        out_specs=pl.BlockSpec((None, rows, D), lambda l, n: (l, 0, n)),
        compiler_params=_cparams(2),
        name="adaln",
    )(sc_in, w_ada, b_ada.reshape(depth, 1, 3 * D))


def _modulated_norm(x, mod, gpre):
    shift = mod[:, :D]
    scale = mod[:, D:2 * D]
    ms = jnp.mean(x * x, axis=-1, keepdims=True)
    return x * lax.rsqrt(ms + EPS) * gpre * (1.0 + scale) + shift


def _sub_tile_input(x_refs, ctx_ref, i, ctx_step):
    x = x_refs[i][...]
    if ctx_ref is None or i != len(x_refs) - 1:
        return x, False
    is_ctx = pl.program_id(1) == ctx_step
    return jnp.where(is_ctx, ctx_ref[...], x), is_ctx


def _proj_kernel(*refs, sub, n_sub, ctx_step):
    x_refs, ctx_ref = refs[:n_sub], refs[n_sub]
    (modb_ref, modc_ref, gpre_ref, w_ref, cost_ref, sint_ref, cosn_ref, sina_ref, sinb_ref, gq_ref, gk_ref,
     qt_ref, k_ref, vt_ref) = refs[n_sub + 1:]

    def project(i):
        x, is_ctx = _sub_tile_input(x_refs, ctx_ref, i, ctx_step)
        mod = jnp.where(is_ctx, modc_ref[...], modb_ref[...])
        h = _modulated_norm(x, mod, gpre_ref[...])
        return jnp.dot(h.astype(BF16), w_ref[...], preferred_element_type=F32)

    def finish(i, y):
        rows = slice(i * sub, (i + 1) * sub)
        q3 = y[:, :N_Q].T.reshape(N_Q // HD, HD, sub)
        qa = q3[:8]
        ss = jnp.sum(qa * qa, axis=1, keepdims=True)
        qa = qa * lax.rsqrt(ss * (1.0 / HD) + EPS) * gq_ref[...][None]
        q3 = jnp.concatenate([qa, q3[8:]], axis=0)
        rot = jnp.concatenate([q3[:, 16:32], q3[:, 0:16], q3[:, 48:64], q3[:, 32:48]], axis=1)
        q3 = (q3 * cost_ref[:, rows][None] + rot * sint_ref[:, rows][None]) * (ATTN_SCALE * LOG2E)
        qt_ref[:, rows] = q3.reshape(N_Q, sub).astype(BF16)

        ka = y[:, N_Q + KA_TILE * LANES:N_Q + (KA_TILE + 1) * LANES]
        lane = lax.broadcasted_iota(jnp.int32, (1, LANES), 1)
        lo = lane < HD
        sq = ka * ka
        s_lo = jnp.sum(jnp.where(lo, sq, 0.0), axis=-1, keepdims=True)
        s_hi = jnp.sum(jnp.where(lo, 0.0, sq), axis=-1, keepdims=True)
        r = jnp.where(lo, lax.rsqrt(s_lo * (1.0 / HD) + EPS), lax.rsqrt(s_hi * (1.0 / HD) + EPS))
        ka = ka * r * gk_ref[...]
        cosn = cosn_ref[rows, :]
        sina = sina_ref[rows, :]
        sinb = sinb_ref[rows, :]
        for j in range(N_K // LANES):
            t = ka if j == KA_TILE else y[:, N_Q + j * LANES:N_Q + (j + 1) * LANES]
            t = t * cosn + pltpu.roll(t, LANES - 16, 1) * sina + pltpu.roll(t, 16, 1) * sinb
            k_ref[rows, j * LANES:(j + 1) * LANES] = t.astype(BF16)

        vt_ref[:, rows] = y[:, N_Q + N_K:].T.astype(BF16)

    y_prev = project(0)
    for i in range(1, n_sub):
        y = project(i)
        finish(i - 1, y_prev)
        y_prev = y
    finish(n_sub - 1, y_prev)


def _token_specs(x_src, ctx_src, n_sub, sub, s_len):
    last = s_len // sub - 1
    specs = [pl.BlockSpec((None, sub, D), functools.partial(lambda i, b, t: (b, jnp.minimum(n_sub * t + i, last), 0), i))
             for i in range(n_sub)]
    operands = [x_src] * n_sub
    if ctx_src is not None:
        arr, blk = ctx_src
        specs.append(pl.BlockSpec((None, sub, D), lambda b, t: (b, blk, 0)))
        operands.append(arr)
    return specs, operands


def _proj_call(x_src, ctx_src, mod4, g_pre, wp, tabs, gq_t, gk_n, layer, s_len, t_all, tm, sub):
    nb = x_src.shape[0]
    cost, sint, cosn, sina, sinb = tabs
    n_sub = tm // sub
    tok_specs, tok_args = _token_specs(x_src, ctx_src, n_sub, sub, s_len)
    return pl.pallas_call(
        functools.partial(_proj_kernel, sub=sub, n_sub=n_sub, ctx_step=t_all // tm - 1),
        out_shape=(jax.ShapeDtypeStruct((nb, N_Q, t_all), BF16),
                   jax.ShapeDtypeStruct((nb, t_all, N_K), BF16),
                   jax.ShapeDtypeStruct((nb, N_V, t_all), BF16)),
        grid=(nb, t_all // tm),
        in_specs=tok_specs + [
            pl.BlockSpec((None, None, 1, 3 * D), lambda b, t: (layer, b, 0, 0)),
            pl.BlockSpec((None, None, 1, 3 * D), lambda b, t: (layer, nb, 0, 0)),
            pl.BlockSpec((None, 1, D), lambda b, t: (layer, 0, 0)),
            pl.BlockSpec((None, D, N_P), lambda b, t: (layer, 0, 0)),
            pl.BlockSpec((HD, tm), lambda b, t: (0, t)),
            pl.BlockSpec((HD, tm), lambda b, t: (0, t)),
            pl.BlockSpec((tm, LANES), lambda b, t: (t, 0)),
            pl.BlockSpec((tm, LANES), lambda b, t: (t, 0)),
            pl.BlockSpec((tm, LANES), lambda b, t: (t, 0)),
            pl.BlockSpec((None, HD, sub), lambda b, t: (layer, 0, 0)),
            pl.BlockSpec((None, 1, LANES), lambda b, t: (layer, 0, 0)),
        ],
        out_specs=(pl.BlockSpec((None, N_Q, tm), lambda b, t: (b, 0, t)),
                   pl.BlockSpec((None, tm, N_K), lambda b, t: (b, t, 0)),
                   pl.BlockSpec((None, N_V, tm), lambda b, t: (b, 0, t))),
        compiler_params=_cparams(2),
        name="proj",
    )(*tok_args, mod4, mod4, g_pre, wp, cost, sint, cosn, sina, sinb, gq_t, gk_n)


KCHUNK = 256
ONES_ROWS = 16


def _small_softmax_pv(problems):
    ss = [jnp.dot(k, rhs, preferred_element_type=F32) for k, _, rhs, _ in problems]
    ms = [jnp.max(s, axis=0, keepdims=True) for s in ss]
    ms = [m if pr[3] is None else jnp.maximum(m, pr[3]) for m, pr in zip(ms, problems)]
    ps = [jnp.exp2(s - m).astype(BF16) for s, m in zip(ss, ms)]
    outs = []
    for p, m, (_, vt, _, extra) in zip(ps, ms, problems):
        dv = vt.shape[0]
        vt_aug = jnp.concatenate([vt, jnp.ones((ONES_ROWS, vt.shape[1]), BF16)], axis=0)
        pv = jnp.dot(vt_aug, p, preferred_element_type=F32)
        l = pv[dv:dv + 1]
        if extra is not None:
            l = l + jnp.exp2(extra - m)
        outs.append(pv[:dv] * (1.0 / l))
    return outs


def _gqa_rhs(qt, j, tq):
    z = jnp.zeros((HD, tq), qt.dtype)
    first = j == 0
    cols = []
    for g in range(4):
        qg = qt[g * HD:(g + 1) * HD, :]
        cols.append(jnp.concatenate([jnp.where(first, qg, z), jnp.where(first, z, qg)], axis=0))
    return jnp.concatenate(cols, axis=1)


def _store_heads(o_ref, ot, tq, nheads):
    o = jnp.concatenate([ot[:, g * tq:(g + 1) * tq] for g in range(nheads)], axis=0)
    o_ref[...] = o.T.astype(o_ref.dtype)


def _sink_row(sink_ref, base, tq):
    blk = lax.broadcasted_iota(jnp.int32, (1, 4 * tq), 1) // tq
    row = jnp.zeros((1, 4 * tq), F32)
    for g in range(4):
        row = jnp.where(blk == g, sink_ref[base + g], row)
    return row * LOG2E


def _tile(t, size):
    if isinstance(t, int):
        return pl.ds(t * size, size)
    return pl.ds(pl.multiple_of(t * size, size), size)


def _two_stage(score_chunks, value_chunks, rhs_next, s_next, s_cur, m_cur, extra_next, extra_cur, lag=2,
               zero=None):
    n_groups = len(rhs_next if rhs_next is not None else m_cur)
    cm = [None] * n_groups
    acc = [None] * n_groups
    parts = []
    row = 0
    for c in range(len(score_chunks or value_chunks)):
        if score_chunks is not None:
            k, bias, g = score_chunks[c]()
            rows = slice(row, row + k.shape[0])
            s = jnp.dot(k, rhs_next[g], preferred_element_type=F32)
            if bias is not None:
                s = s + bias
            s_next[rows, :] = s
            parts.append(jnp.max(s, axis=0, keepdims=True))
            cm[g] = parts[-1] if cm[g] is None else jnp.maximum(cm[g], parts[-1])
        if value_chunks is not None:
            vt, g = value_chunks[c]()
            m_c = m_cur[g]
            if score_chunks is not None and c >= lag:
                m_c = jnp.maximum(m_c, jnp.minimum(parts[c - lag], m_c))
            rows = slice(row, row + vt.shape[1])
            src_rows = rows if zero is None else pl.ds(pl.multiple_of(row + zero, LANES), vt.shape[1])
            p = jnp.exp2(s_cur[src_rows, :] - m_c).astype(BF16)
            vt_aug = jnp.concatenate([vt, jnp.ones((ONES_ROWS, vt.shape[1]), BF16)], axis=0)
            pv = jnp.dot(vt_aug, p, preferred_element_type=F32)
            acc[g] = pv if acc[g] is None else acc[g] + pv
        row = rows.stop
    ots = None
    if score_chunks is not None:
        cm = [m if e is None else jnp.maximum(m, e) for m, e in zip(cm, extra_next)]
    if value_chunks is not None:
        ots = []
        for g in range(n_groups):
            dv = acc[g].shape[0] - ONES_ROWS
            l = acc[g][dv:dv + 1]
            if extra_cur[g] is not None:
                l = l + jnp.exp2(extra_cur[g] - m_cur[g])
            ots.append(acc[g][:dv] * (1.0 / l))
    return tuple(cm), ots


def _pipeline(make_rhs, score_chunks, value_chunks, finish, s0, s1, n_units, extra=None, lag=2, zero=None):
    rhs0 = make_rhs(0)
    ex = (lambda u: [None] * len(rhs0)) if extra is None else extra
    m0, _ = _two_stage(score_chunks(0), None, rhs0, s0, None, None, ex(0), None)

    def body(i, m_even):
        u = 2 * i
        m_odd, ot = _two_stage(score_chunks(u + 1), value_chunks(u), make_rhs(u + 1), s1, s0, m_even,
                               ex(u + 1), ex(u), lag, zero)
        finish(u, ot)
        u2 = jnp.minimum(u + 2, n_units - 1)
        m_next, ot = _two_stage(score_chunks(u2), value_chunks(u + 1), make_rhs(u2), s0, s1, m_odd,
                                ex(u2), ex(u + 1), lag, zero)
        finish(u + 1, ot)
        return m_next

    lax.fori_loop(0, n_units // 2, body, m0)


def _split_unit(u, n_tiles):
    if isinstance(u, int):
        return u // n_tiles, u % n_tiles
    return lax.div(u, n_tiles), lax.rem(u, n_tiles)


def _rows(g, size):
    return _tile(g, size)


def _attn_ac_kernel(zero_ref, sink_ref, bias_ref, qa_ref, qc_ref, ka_ref, kc_ref, vta_ref, vtc_ref, oa_ref, oc_ref,
                    s0, s1, *, tq, n_lat, n_ctx, s_len):
    t_all = ka_ref.shape[0]
    wk = tq + 2 * WINDOW
    spans = [(lo, min(lo + KCHUNK, wk)) for lo in range(0, wk, KCHUNK)]

    def rhs_of(q_ref, j, t):
        return _gqa_rhs(q_ref[_rows(j, 4 * HD), _tile(t, tq)], j, tq)

    def store(o_ref, j, t, ot):
        _store_heads(o_ref.at[_tile(t, tq), _rows(j, 4 * HD)], ot, tq, 4)

    def window(t):
        q0 = t * tq
        start = jnp.clip(q0 - WINDOW, 0, s_len - wk)
        return start, (q0 - start) // WINDOW

    def local(start, lo, hi):
        return pl.ds(pl.multiple_of(start + lo, LANES), hi - lo)

    def score_chunks(u):
        _, t = _split_unit(u, n_lat)
        start, variant = window(t)

        def loc(lo, hi):
            b = bias_ref[variant, lo:hi, :]
            return kc_ref[local(start, lo, hi), :], jnp.concatenate([b] * 4, axis=1), 1

        return ([functools.partial(lambda c: (ka_ref[c * KCHUNK:(c + 1) * KCHUNK, :], None, 0), c)
                 for c in range(t_all // KCHUNK)]
                + [lambda: (kc_ref[s_len:s_len + KCHUNK, :], None, 1)]
                + [functools.partial(loc, lo, hi) for lo, hi in spans])

    def value_chunks(u):
        j, t = _split_unit(u, n_lat)
        start, _ = window(t)
        return ([functools.partial(lambda c: (vta_ref[_rows(j, HD), c * KCHUNK:(c + 1) * KCHUNK], 0), c)
                 for c in range(t_all // KCHUNK)]
                + [lambda: (vtc_ref[_rows(j, HD), s_len:s_len + KCHUNK], 1)]
                + [functools.partial(lambda lo, hi: (vtc_ref[_rows(j, HD), local(start, lo, hi)], 1), lo, hi)
                   for lo, hi in spans])

    def make_rhs(u):
        j, t = _split_unit(u, n_lat)
        return [rhs_of(qa_ref, j, t), rhs_of(qc_ref, j, t)]

    def extra(u):
        j, _ = _split_unit(u, n_lat)
        return [None, _sink_row(sink_ref, j * 4, tq)]

    def finish(u, ots):
        j, t = _split_unit(u, n_lat)
        store(oa_ref, j, t, ots[0])
        store(oc_ref, j, t, ots[1])

    _pipeline(make_rhs, score_chunks, value_chunks, finish, s0, s1, 2 * n_lat, extra=extra, lag=6, zero=zero_ref[0])
    ctx_tiles = range(n_lat, n_lat + n_ctx)
    n = 4 * tq
    problems, targets = [], []
    for j in range(2 if n_ctx else 0):
        for q_ref, k_ref, vt_ref, o_ref, sink in ((qa_ref, ka_ref, vta_ref, oa_ref, None),
                                                  (qc_ref, kc_ref, vtc_ref, oc_ref, _sink_row(sink_ref, j * 4, tq))):
            rhs = jnp.concatenate([rhs_of(q_ref, j, t) for t in ctx_tiles], axis=1)
            extra = None if sink is None else jnp.concatenate([sink] * n_ctx, axis=1)
            problems.append((k_ref[s_len:, :], vt_ref[_rows(j, HD), s_len:], rhs, extra))
            targets.append((o_ref, j))
    for (o_ref, j), ot in zip(targets, _small_softmax_pv(problems)):
        for i, t in enumerate(ctx_tiles):
            store(o_ref, j, t, ot[:, i * n:(i + 1) * n])


def _attn_b_kernel(zero_ref, lamv_ref, subln_ref, qt_ref, k_ref, vt_ref, o_ref, s0, s1, *, tq, n_lat, n_ctx, s_len,
                   lam_init):
    lv = lamv_ref[...]
    lam = (jnp.exp(jnp.sum(lv[0:1] * lv[1:2], axis=-1, keepdims=True))
           - jnp.exp(jnp.sum(lv[2:3] * lv[3:4], axis=-1, keepdims=True)) + lam_init)
    t_all = k_ref.shape[0]
    n_chunks = t_all // KCHUNK
    hw = 2 * HD

    def rhs_of(h, t):
        qt = qt_ref[_rows(h, hw), _tile(t, tq)]
        z = jnp.zeros((HD, tq), qt.dtype)
        return jnp.concatenate([jnp.concatenate([qt[:HD], z], axis=0),
                                jnp.concatenate([z, qt[HD:]], axis=0)], axis=1)

    def store(h, t, o2):
        o = o2[:, :tq] - lam * o2[:, tq:]
        ms = jnp.mean(o * o, axis=0, keepdims=True)
        o = o * lax.rsqrt(ms + SUBLN_EPS) * subln_ref[...] * (1.0 - lam_init)
        o_ref[_tile(t, tq), _rows(h, hw)] = o.T.astype(o_ref.dtype)

    def score_chunks(u):
        h, _ = _split_unit(u, n_lat)
        return [functools.partial(lambda c: (k_ref[c * KCHUNK:(c + 1) * KCHUNK, _rows(h, hw)], None, 0), c)
                for c in range(n_chunks)]

    def value_chunks(u):
        h, _ = _split_unit(u, n_lat)
        return [functools.partial(lambda c: (vt_ref[_rows(h, hw), c * KCHUNK:(c + 1) * KCHUNK], 0), c)
                for c in range(n_chunks)]

    _pipeline(lambda u: [rhs_of(*_split_unit(u, n_lat))], score_chunks, value_chunks,
              lambda u, ots: store(*_split_unit(u, n_lat), ots[0]), s0, s1, 4 * n_lat, lag=3, zero=zero_ref[0])
    units = [(h, t) for h in range(4) for t in range(n_lat, n_lat + n_ctx)]
    problems = [(k_ref[s_len:, _rows(h, hw)], vt_ref[_rows(h, hw), s_len:], rhs_of(h, t), None) for h, t in units]
    for (h, t), o2 in zip(units, _small_softmax_pv(problems)):
        store(h, t, o2)


def _window_bias(tq):
    wk = tq + 2 * WINDOW
    r = np.arange(wk)[:, None]
    c = np.arange(tq)[None, :]
    out = np.stack([np.where(np.abs(c - r + v * WINDOW) <= WINDOW, 0.0, -np.inf) for v in range(3)])
    return jnp.asarray(out, F32)


def _attn_calls(qt_all, k_all, vt_all, sink, lamv, subln_t, layer, with_ctx, s_len, lam_init,
                tq_a, tq_b):
    nb, _, t_all = qt_all.shape
    c_len = t_all - s_len
    o_rows = t_all if with_ctx else s_len
    o_shape = jax.ShapeDtypeStruct((nb, o_rows, QW), BF16)

    def steps(tq):
        n_lat = s_len // tq
        return n_lat, n_lat + (c_len // tq if with_ctx else 0)

    whole = lambda rows, cols, r, c: pl.BlockSpec((None, rows, cols), lambda b: (b, r, c))
    o_spec = pl.BlockSpec((None, o_rows, QW), lambda b: (b, 0, 0))
    score_scratch = lambda rows, n: [pltpu.VMEM((rows, n), F32), pltpu.VMEM((rows, n), F32)]
    zero = jnp.zeros((1,), jnp.int32)

    n_lat, n_all = steps(tq_a)
    wk = tq_a + 2 * WINDOW
    oa, oc = pl.pallas_call(
        functools.partial(_attn_ac_kernel, tq=tq_a, n_lat=n_lat, n_ctx=n_all - n_lat, s_len=s_len),
        out_shape=(o_shape, o_shape),
        grid=(nb,),
        in_specs=[
            pl.BlockSpec(memory_space=pltpu.SMEM),
            pl.BlockSpec(memory_space=pltpu.SMEM),
            pl.BlockSpec((3, wk, tq_a), lambda b: (0, 0, 0)),
            whole(QW, t_all, 0, 0), whole(QW, t_all, 2, 0),
            whole(t_all, LANES, 0, KA_TILE), whole(t_all, LANES, 0, KA_TILE + 1),
            whole(LANES, t_all, KA_TILE, 0), whole(LANES, t_all, KA_TILE + 1, 0),
        ],
        out_specs=(o_spec, o_spec),
        scratch_shapes=score_scratch(t_all + c_len + wk, 4 * tq_a),
        compiler_params=_cparams(1),
        name="attn_ac",
    )(zero, sink[layer], _window_bias(tq_a), qt_all, qt_all, k_all, k_all, vt_all, vt_all)

    n_lat, n_all = steps(tq_b)
    ob = pl.pallas_call(
        functools.partial(_attn_b_kernel, tq=tq_b, n_lat=n_lat, n_ctx=n_all - n_lat, s_len=s_len,
                          lam_init=lam_init),
        out_shape=o_shape,
        grid=(nb,),
        in_specs=[
            pl.BlockSpec(memory_space=pltpu.SMEM),
            pl.BlockSpec((None, 4, HD), lambda b: (layer, 0, 0)),
            pl.BlockSpec((None, 2 * HD, tq_b), lambda b: (layer, 0, 0)),
            whole(QW, t_all, 1, 0), whole(t_all, QW, 0, 0), whole(QW, t_all, 0, 0),
        ],
        out_specs=o_spec,
        scratch_shapes=score_scratch(t_all, 2 * tq_b),
        compiler_params=_cparams(1),
        name="attn_b",
    )(zero, lamv, subln_t, qt_all, k_all, vt_all)

    return oa, ob, oc


def _merge_kernel(*refs, sub, n_sub, ctx_step):
    x_refs = refs[:n_sub]
    rest = refs[n_sub:]
    ctx_ref = None
    if ctx_step is not None:
        ctx_ref, rest = rest[0], rest[1:]
    (oa_ref, ob_ref, oc_ref, modb_ref, modc_ref, gpre_ref, gpost_ref, wgm_ref, bmg_ref, wbr_ref, wout_ref,
     out_ref) = rest

    def gates(i):
        x, is_ctx = _sub_tile_input(x_refs, ctx_ref, i, ctx_step)
        mod = jnp.where(is_ctx, modc_ref[...], modb_ref[...])
        h = _modulated_norm(x, mod, gpre_ref[...])
        gm = jnp.dot(h.astype(BF16), wgm_ref[...], preferred_element_type=F32)
        return x, mod, gm

    def finish(i, x, mod, gm):
        rows = slice(i * sub, (i + 1) * sub)
        z = None
        for j, o_ref in enumerate((oa_ref, ob_ref, oc_ref)):
            g = gm[:, j * QW:(j + 1) * QW]
            u = (o_ref[rows, :].astype(F32) * (g * _sigmoid(g))).astype(BF16)
            p = jnp.dot(u, wbr_ref[j], preferred_element_type=F32)
            mg = _sigmoid(gm[:, N_G + j * D:N_G + (j + 1) * D] + bmg_ref[:, j * D:(j + 1) * D])
            z = mg * p if z is None else z + mg * p
        y = jnp.dot(z.astype(BF16), wout_ref[...], preferred_element_type=F32)
        ms = jnp.mean(y * y, axis=-1, keepdims=True)
        gate = mod[:, 2 * D:]
        out_ref[rows, :] = x + gate * (y * lax.rsqrt(ms + EPS) * gpost_ref[...])

    prev = gates(0)
    for i in range(1, n_sub):
        cur = gates(i)
        finish(i - 1, *prev)
        prev = cur
    finish(n_sub - 1, *prev)


def _merge_call(x_src, ctx_src, oa, ob, oc, mod4, g_pre, g_post, wgm, b_mg, wbr, wout, layer, s_len, rows_out,
                tm, sub):
    nb = x_src.shape[0]
    n_sub = tm // sub
    tok_specs, tok_args = _token_specs(x_src, ctx_src, n_sub, sub, s_len)
    ctx_step = None if ctx_src is None else rows_out // tm - 1
    tok = lambda b, t: (b, t, 0)
    lay2 = lambda b, t: (layer, 0, 0)
    const = dict(pipeline_mode=pl.Buffered(1))
    return pl.pallas_call(
        functools.partial(_merge_kernel, sub=sub, n_sub=n_sub, ctx_step=ctx_step),
        out_shape=jax.ShapeDtypeStruct((nb, rows_out, D), F32),
        grid=(nb, rows_out // tm),
        in_specs=tok_specs + [
            pl.BlockSpec((None, tm, QW), tok),
            pl.BlockSpec((None, tm, QW), tok),
            pl.BlockSpec((None, tm, QW), tok),
            pl.BlockSpec((None, None, 1, 3 * D), lambda b, t: (layer, b, 0, 0)),
            pl.BlockSpec((None, None, 1, 3 * D), lambda b, t: (layer, nb, 0, 0)),
            pl.BlockSpec((None, 1, D), lay2),
            pl.BlockSpec((None, 1, D), lay2),
            pl.BlockSpec((None, D, N_G + N_M), lay2, **const),
            pl.BlockSpec((None, 1, N_M), lay2),
            pl.BlockSpec((None, 3, QW, D), lambda b, t: (layer, 0, 0, 0), **const),
            pl.BlockSpec((None, D, D), lay2, **const),
        ],
        out_specs=pl.BlockSpec((None, tm, D), tok),
        compiler_params=_cparams(2),
        name="merge",
    )(*tok_args, oa, ob, oc, mod4, mod4, g_pre, g_post, wgm, b_mg, wbr, wout)


def _rope_tables(s_len, c_len):
    rows = s_len // GRID_W
    row = np.repeat(np.arange(rows), GRID_W).astype(np.float32)
    col = np.tile(np.arange(GRID_W), rows).astype(np.float32)
    freqs = (np.float32(ROPE_THETA) ** (-np.arange(ROPE_PAIRS, dtype=np.float32) / ROPE_PAIRS)).astype(np.float32)
    ang_r = row[:, None] * freqs
    ang_c = col[:, None] * freqs
    ang = np.concatenate([ang_r, ang_r, ang_c, ang_c], axis=-1)
    cos = np.concatenate([np.cos(ang), np.ones((c_len, HD), np.float32)], axis=0).astype(np.float32)
    sin = np.concatenate([np.sin(ang), np.zeros((c_len, HD), np.float32)], axis=0).astype(np.float32)
    first = (np.arange(HD) % 32) < 16
    sin_a = np.where(first, -sin, np.float32(0.0))
    sin_b = np.where(first, np.float32(0.0), sin)
    tile2 = lambda a: np.concatenate([a, a], axis=-1)
    cost = np.ascontiguousarray(cos.T)
    sint = np.ascontiguousarray((sin_a + sin_b).T)
    return tuple(jnp.asarray(a, F32) for a in (cost, sint, tile2(cos), tile2(sin_a), tile2(sin_b)))


def _cols(w, names):
    return jnp.concatenate([w[..., _IN[n][0]:_IN[n][1]] for n in names], axis=-1)


def kernel(x, c, ctx, c_ctx, w_ada, b_ada, g_pre, g_post, w_in, q_norm, k_norm, lam_q1, lam_k1, lam_q2,
           lam_k2, subln, sink, w_br_a, w_br_b, w_br_c, w_mg, b_mg, w_out):
    nb, s_len, _ = x.shape
    c_len = ctx.shape[1]
    depth = w_in.shape[0]
    sub = 256
    tm_all = 3 * sub
    tm_lat = 4 * sub
    tq_a, tq_b = 128, 256
    assert (s_len + c_len) % tm_all == 0 and s_len % tm_lat == 0 and s_len % sub == 0
    assert c_len == KCHUNK == sub

    wp = _cols(w_in, ("qa", "qb", "qc", "kb", "ka", "kc", "vb", "va", "vc")).astype(BF16)
    wgm = jnp.concatenate([_cols(w_in, ("ga", "gb", "gc")), w_mg], axis=-1).astype(BF16)
    wbr = jnp.stack([w_br_a, w_br_b, w_br_c], axis=1).astype(BF16)
    wout = w_out.astype(BF16)

    tabs = _rope_tables(s_len, c_len)
    gq_t = jnp.broadcast_to(q_norm[:, :, None], (depth, HD, sub))
    gk_n = jnp.concatenate([k_norm, k_norm], axis=-1)[:, None, :]
    lamv = jnp.stack([lam_q1, lam_k1, lam_q2, lam_k2], axis=1)
    subln_t = jnp.broadcast_to(subln[:, :, None], (depth, 2 * HD, tq_b))
    g_pre3 = g_pre[:, None, :]
    g_post3 = g_post[:, None, :]
    b_mg3 = b_mg[:, None, :]

    rows = ((nb + 1 + 7) // 8) * 8
    sc_in = jnp.concatenate([c, c_ctx[None, :], jnp.zeros((rows - nb - 1, D), F32)], axis=0)
    mod = _ada_call(sc_in, w_ada, b_ada)
    mod4 = mod[:, :, None, :]

    t_all = s_len + c_len
    x_src, ctx_src = x, (ctx, 0)
    for layer in range(depth):
        last = layer == depth - 1
        lam_init = 0.8 - 0.6 * math.exp(-0.3 * layer)
        qt_all, k_all, vt_all = _proj_call(x_src, ctx_src, mod4, g_pre3, wp, tabs, gq_t, gk_n, layer, s_len,
                                           t_all, tm_all, sub)
        oa, ob, oc = _attn_calls(qt_all, k_all, vt_all, sink, lamv, subln_t, layer, not last, s_len,
                                 lam_init, tq_a, tq_b)
        rows_out, tm = (s_len, tm_lat) if last else (t_all, tm_all)
        xs = _merge_call(x_src, None if last else ctx_src, oa, ob, oc, mod4, g_pre3, g_post3, wgm, b_mg3, wbr,
                         wout, layer, s_len, rows_out, tm, sub)
        x_src, ctx_src = xs, (xs, s_len // sub)
    return xs
```

```python
import functools
import math

import jax
import jax.numpy as jnp
import numpy as np
from jax import lax
from jax.experimental import pallas as pl
from jax.experimental.pallas import tpu as pltpu

F32 = jnp.float32
BF16 = jnp.bfloat16

D = 1024
HD = 64
GRID_W = 64
WINDOW = 128
ROPE_THETA = 10000.0
ROPE_PAIRS = HD // 4
EPS = 1e-6
SUBLN_EPS = 1e-5
ATTN_SCALE = HD ** -0.5
LOG2E = math.log2(math.e)

QW = 512
N_Q = 3 * QW
N_K = 512 + 128 + 128
N_V = 512 + 128 + 128
KA_TILE = 4
N_P = N_Q + N_K + N_V
N_G = 3 * QW
N_M = 3 * D

LANES = 128
VMEM_LIMIT = 56 * 1024 * 1024

_IN = dict(qa=(0, 512), ka=(512, 640), va=(640, 768), ga=(768, 1280),
           qb=(1280, 1792), kb=(1792, 2304), vb=(2304, 2816), gb=(2816, 3328),
           qc=(3328, 3840), kc=(3840, 3968), vc=(3968, 4096), gc=(4096, 4608))


def _sigmoid(v):
    return 1.0 / (1.0 + jnp.exp(-v))


def _cparams(n_axes):
    return pltpu.CompilerParams(dimension_semantics=("arbitrary",) * n_axes,
                                vmem_limit_bytes=VMEM_LIMIT)


def _ada_kernel(sc_ref, w_ref, b_ref, o_ref):
    v = sc_ref[...]
    s = (v * _sigmoid(v)).astype(BF16)
    o_ref[...] = jnp.dot(s, w_ref[...].astype(BF16), preferred_element_type=F32) + b_ref[...]


def _ada_call(sc_in, w_ada, b_ada):
    depth = w_ada.shape[0]
    rows = sc_in.shape[0]
    nblk = 3
    return pl.pallas_call(
        _ada_kernel,
        out_shape=jax.ShapeDtypeStruct((depth, rows, 3 * D), F32),
        grid=(depth, nblk),
        in_specs=[
            pl.BlockSpec((rows, D), lambda l, n: (0, 0)),
            pl.BlockSpec((None, D, D), lambda l, n: (l, 0, n)),
            pl.BlockSpec((None, 1, D), lambda l, n: (l, 0, n)),
        ],
        out_specs=pl.BlockSpec((None, rows, D), lambda l, n: (l, 0, n)),
        compiler_params=_cparams(2),
        name="adaln",
    )(sc_in, w_ada, b_ada.reshape(depth, 1, 3 * D))


def _modulated_norm(x, mod, gpre):
    shift = mod[:, :D]
    scale = mod[:, D:2 * D]
    ms = jnp.mean(x * x, axis=-1, keepdims=True)
    return x * lax.rsqrt(ms + EPS) * gpre * (1.0 + scale) + shift


def _sub_tile_input(x_refs, ctx_ref, i, ctx_step):
    x = x_refs[i][...]
    if ctx_ref is None or i != len(x_refs) - 1:
        return x, False
    is_ctx = pl.program_id(1) == ctx_step
    return jnp.where(is_ctx, ctx_ref[...], x), is_ctx


def _proj_kernel(*refs, sub, n_sub, ctx_step):
    x_refs, ctx_ref = refs[:n_sub], refs[n_sub]
    (modb_ref, modc_ref, gpre_ref, w_ref, cost_ref, sint_ref, cosn_ref, sina_ref, sinb_ref, gq_ref, gk_ref,
     qt_ref, k_ref, vt_ref) = refs[n_sub + 1:]

    def project(i):
        x, is_ctx = _sub_tile_input(x_refs, ctx_ref, i, ctx_step)
        mod = jnp.where(is_ctx, modc_ref[...], modb_ref[...])
        h = _modulated_norm(x, mod, gpre_ref[...])
        return jnp.dot(h.astype(BF16), w_ref[...], preferred_element_type=F32)

    def finish(i, y):
        rows = slice(i * sub, (i + 1) * sub)
        q3 = y[:, :N_Q].T.reshape(N_Q // HD, HD, sub)
        qa = q3[:8]
        ss = jnp.sum(qa * qa, axis=1, keepdims=True)
        qa = qa * lax.rsqrt(ss * (1.0 / HD) + EPS) * gq_ref[...][None]
        q3 = jnp.concatenate([qa, q3[8:]], axis=0)
        rot = jnp.concatenate([q3[:, 16:32], q3[:, 0:16], q3[:, 48:64], q3[:, 32:48]], axis=1)
        q3 = (q3 * cost_ref[:, rows][None] + rot * sint_ref[:, rows][None]) * (ATTN_SCALE * LOG2E)
        qt_ref[:, rows] = q3.reshape(N_Q, sub).astype(BF16)

        ka = y[:, N_Q + KA_TILE * LANES:N_Q + (KA_TILE + 1) * LANES]
        lane = lax.broadcasted_iota(jnp.int32, (1, LANES), 1)
        lo = lane < HD
        sq = ka * ka
        s_lo = jnp.sum(jnp.where(lo, sq, 0.0), axis=-1, keepdims=True)
        s_hi = jnp.sum(jnp.where(lo, 0.0, sq), axis=-1, keepdims=True)
        r = jnp.where(lo, lax.rsqrt(s_lo * (1.0 / HD) + EPS), lax.rsqrt(s_hi * (1.0 / HD) + EPS))
        ka = ka * r * gk_ref[...]
        cosn = cosn_ref[rows, :]
        sina = sina_ref[rows, :]
        sinb = sinb_ref[rows, :]
        for j in range(N_K // LANES):
            t = ka if j == KA_TILE else y[:, N_Q + j * LANES:N_Q + (j + 1) * LANES]
            t = t * cosn + pltpu.roll(t, LANES - 16, 1) * sina + pltpu.roll(t, 16, 1) * sinb
            k_ref[rows, j * LANES:(j + 1) * LANES] = t.astype(BF16)

        vt_ref[:, rows] = y[:, N_Q + N_K:].T.astype(BF16)

    y_prev = project(0)
    for i in range(1, n_sub):
        y = project(i)
        finish(i - 1, y_prev)
        y_prev = y
    finish(n_sub - 1, y_prev)


def _token_specs(x_src, ctx_src, n_sub, sub, s_len):
    last = s_len // sub - 1
    specs = [pl.BlockSpec((None, sub, D), functools.partial(lambda i, b, t: (b, jnp.minimum(n_sub * t + i, last), 0), i))
             for i in range(n_sub)]
    operands = [x_src] * n_sub
    if ctx_src is not None:
        arr, blk = ctx_src
        specs.append(pl.BlockSpec((None, sub, D), lambda b, t: (b, blk, 0)))
        operands.append(arr)
    return specs, operands


def _proj_call(x_src, ctx_src, mod4, g_pre, wp, tabs, gq_t, gk_n, layer, s_len, t_all, tm, sub):
    nb = x_src.shape[0]
    cost, sint, cosn, sina, sinb = tabs
    n_sub = tm // sub
    tok_specs, tok_args = _token_specs(x_src, ctx_src, n_sub, sub, s_len)
    return pl.pallas_call(
        functools.partial(_proj_kernel, sub=sub, n_sub=n_sub, ctx_step=t_all // tm - 1),
        out_shape=(jax.ShapeDtypeStruct((nb, N_Q, t_all), BF16),
                   jax.ShapeDtypeStruct((nb, t_all, N_K), BF16),
                   jax.ShapeDtypeStruct((nb, N_V, t_all), BF16)),
        grid=(nb, t_all // tm),
        in_specs=tok_specs + [
            pl.BlockSpec((None, None, 1, 3 * D), lambda b, t: (layer, b, 0, 0)),
            pl.BlockSpec((None, None, 1, 3 * D), lambda b, t: (layer, nb, 0, 0)),
            pl.BlockSpec((None, 1, D), lambda b, t: (layer, 0, 0)),
            pl.BlockSpec((None, D, N_P), lambda b, t: (layer, 0, 0)),
            pl.BlockSpec((HD, tm), lambda b, t: (0, t)),
            pl.BlockSpec((HD, tm), lambda b, t: (0, t)),
            pl.BlockSpec((tm, LANES), lambda b, t: (t, 0)),
            pl.BlockSpec((tm, LANES), lambda b, t: (t, 0)),
            pl.BlockSpec((tm, LANES), lambda b, t: (t, 0)),
            pl.BlockSpec((None, HD, sub), lambda b, t: (layer, 0, 0)),
            pl.BlockSpec((None, 1, LANES), lambda b, t: (layer, 0, 0)),
        ],
        out_specs=(pl.BlockSpec((None, N_Q, tm), lambda b, t: (b, 0, t)),
                   pl.BlockSpec((None, tm, N_K), lambda b, t: (b, t, 0)),
                   pl.BlockSpec((None, N_V, tm), lambda b, t: (b, 0, t))),
        compiler_params=_cparams(2),
        name="proj",
    )(*tok_args, mod4, mod4, g_pre, wp, cost, sint, cosn, sina, sinb, gq_t, gk_n)


KCHUNK = 256
ONES_ROWS = 16


def _small_softmax_pv(problems):
    ss = [jnp.dot(k, rhs, preferred_element_type=F32) for k, _, rhs, _ in problems]
    ms = [jnp.max(s, axis=0, keepdims=True) for s in ss]
    ms = [m if pr[3] is None else jnp.maximum(m, pr[3]) for m, pr in zip(ms, problems)]
    ps = [jnp.exp2(s - m).astype(BF16) for s, m in zip(ss, ms)]
    outs = []
    for p, m, (_, vt, _, extra) in zip(ps, ms, problems):
        dv = vt.shape[0]
        vt_aug = jnp.concatenate([vt, jnp.ones((ONES_ROWS, vt.shape[1]), BF16)], axis=0)
        pv = jnp.dot(vt_aug, p, preferred_element_type=F32)
        l = pv[dv:dv + 1]
        if extra is not None:
            l = l + jnp.exp2(extra - m)
        outs.append(pv[:dv] * (1.0 / l))
    return outs


def _gqa_rhs(qt, j, tq):
    z = jnp.zeros((HD, tq), qt.dtype)
    first = j == 0
    cols = []
    for g in range(4):
        qg = qt[g * HD:(g + 1) * HD, :]
        cols.append(jnp.concatenate([jnp.where(first, qg, z), jnp.where(first, z, qg)], axis=0))
    return jnp.concatenate(cols, axis=1)


def _store_heads(o_ref, ot, tq, nheads):
    o = jnp.concatenate([ot[:, g * tq:(g + 1) * tq] for g in range(nheads)], axis=0)
    o_ref[...] = o.T.astype(o_ref.dtype)


def _sink_row(sink_ref, base, tq):
    blk = lax.broadcasted_iota(jnp.int32, (1, 4 * tq), 1) // tq
    row = jnp.zeros((1, 4 * tq), F32)
    for g in range(4):
        row = jnp.where(blk == g, sink_ref[base + g], row)
    return row * LOG2E


def _tile(t, size):
    if isinstance(t, int):
        return pl.ds(t * size, size)
    return pl.ds(pl.multiple_of(t * size, size), size)


def _two_stage(score_chunks, value_chunks, rhs_next, s_next, s_cur, m_cur, extra_next, extra_cur, lag=2,
               zero=None):
    n_groups = len(rhs_next if rhs_next is not None else m_cur)
    cm = [None] * n_groups
    acc = [None] * n_groups
    parts = []
    row = 0
    for c in range(len(score_chunks or value_chunks)):
        if score_chunks is not None:
            k, bias, g = score_chunks[c]()
            rows = slice(row, row + k.shape[0])
            s = jnp.dot(k, rhs_next[g], preferred_element_type=F32)
            if bias is not None:
                s = s + bias
            s_next[rows, :] = s
            parts.append(jnp.max(s, axis=0, keepdims=True))
            cm[g] = parts[-1] if cm[g] is None else jnp.maximum(cm[g], parts[-1])
        if value_chunks is not None:
            vt, g = value_chunks[c]()
            m_c = m_cur[g]
            if score_chunks is not None and c >= lag:
                m_c = jnp.maximum(m_c, jnp.minimum(parts[c - lag], m_c))
            rows = slice(row, row + vt.shape[1])
            src_rows = rows if zero is None else pl.ds(pl.multiple_of(row + zero, LANES), vt.shape[1])
            p = jnp.exp2(s_cur[src_rows, :] - m_c).astype(BF16)
            vt_aug = jnp.concatenate([vt, jnp.ones((ONES_ROWS, vt.shape[1]), BF16)], axis=0)
            pv = jnp.dot(vt_aug, p, preferred_element_type=F32)
            acc[g] = pv if acc[g] is None else acc[g] + pv
        row = rows.stop
    ots = None
    if score_chunks is not None:
        cm = [m if e is None else jnp.maximum(m, e) for m, e in zip(cm, extra_next)]
    if value_chunks is not None:
        ots = []
        for g in range(n_groups):
            dv = acc[g].shape[0] - ONES_ROWS
            l = acc[g][dv:dv + 1]
            if extra_cur[g] is not None:
                l = l + jnp.exp2(extra_cur[g] - m_cur[g])
            ots.append(acc[g][:dv] * (1.0 / l))
    return tuple(cm), ots


def _pipeline(make_rhs, score_chunks, value_chunks, finish, out_shapes, s0, s1, n_units, extra=None, lag=2,
              zero=None):
    rhs0 = make_rhs(0)
    ex = (lambda u: [None] * len(rhs0)) if extra is None else extra
    m0, _ = _two_stage(score_chunks(0), None, rhs0, s0, None, None, ex(0), None)

    def body(i, carry):
        m_even, ot_prev = carry
        u = 2 * i
        finish(jnp.maximum(u - 1, 0), ot_prev)
        m_odd, ot = _two_stage(score_chunks(u + 1), value_chunks(u), make_rhs(u + 1), s1, s0, m_even,
                               ex(u + 1), ex(u), lag, zero)
        finish(u, ot)
        u2 = jnp.minimum(u + 2, n_units - 1)
        m_next, ot = _two_stage(score_chunks(u2), value_chunks(u + 1), make_rhs(u2), s0, s1, m_odd,
                                ex(u2), ex(u + 1), lag, zero)
        return m_next, ot

    _, ot_last = lax.fori_loop(0, n_units // 2, body, (m0, [jnp.zeros(s, F32) for s in out_shapes]))
    finish(n_units - 1, ot_last)


def _split_unit(u, n_tiles):
    if isinstance(u, int):
        return u // n_tiles, u % n_tiles
    return lax.div(u, n_tiles), lax.rem(u, n_tiles)


def _rows(g, size):
    return _tile(g, size)


def _attn_ac_kernel(zero_ref, sink_ref, bias_ref, qa_ref, qc_ref, ka_ref, kc_ref, vta_ref, vtc_ref, oa_ref, oc_ref,
                    s0, s1, *, tq, n_lat, n_ctx, s_len):
    t_all = ka_ref.shape[0]
    wk = tq + 2 * WINDOW
    spans = [(lo, min(lo + KCHUNK, wk)) for lo in range(0, wk, KCHUNK)]

    def rhs_of(q_ref, j, t):
        return _gqa_rhs(q_ref[_rows(j, 4 * HD), _tile(t, tq)], j, tq)

    def store(o_ref, j, t, ot):
        _store_heads(o_ref.at[_tile(t, tq), _rows(j, 4 * HD)], ot, tq, 4)

    def window(t):
        q0 = t * tq
        start = jnp.clip(q0 - WINDOW, 0, s_len - wk)
        return start, (q0 - start) // WINDOW

    def local(start, lo, hi):
        return pl.ds(pl.multiple_of(start + lo, LANES), hi - lo)

    def score_chunks(u):
        _, t = _split_unit(u, n_lat)
        start, variant = window(t)

        def loc(lo, hi):
            b = bias_ref[variant, lo:hi, :]
            return kc_ref[local(start, lo, hi), :], jnp.concatenate([b] * 4, axis=1), 1

        return ([functools.partial(lambda c: (ka_ref[c * KCHUNK:(c + 1) * KCHUNK, :], None, 0), c)
                 for c in range(t_all // KCHUNK)]
                + [lambda: (kc_ref[s_len:s_len + KCHUNK, :], None, 1)]
                + [functools.partial(loc, lo, hi) for lo, hi in spans])

    def value_chunks(u):
        j, t = _split_unit(u, n_lat)
        start, _ = window(t)
        return ([functools.partial(lambda c: (vta_ref[_rows(j, HD), c * KCHUNK:(c + 1) * KCHUNK], 0), c)
                 for c in range(t_all // KCHUNK)]
                + [lambda: (vtc_ref[_rows(j, HD), s_len:s_len + KCHUNK], 1)]
                + [functools.partial(lambda lo, hi: (vtc_ref[_rows(j, HD), local(start, lo, hi)], 1), lo, hi)
                   for lo, hi in spans])

    def make_rhs(u):
        j, t = _split_unit(u, n_lat)
        return [rhs_of(qa_ref, j, t), rhs_of(qc_ref, j, t)]

    def extra(u):
        j, _ = _split_unit(u, n_lat)
        return [None, _sink_row(sink_ref, j * 4, tq)]

    def finish(u, ots):
        j, t = _split_unit(u, n_lat)
        store(oa_ref, j, t, ots[0])
        store(oc_ref, j, t, ots[1])

    _pipeline(make_rhs, score_chunks, value_chunks, finish, [(HD, 4 * tq)] * 2, s0, s1, 2 * n_lat, extra=extra,
              lag=6, zero=zero_ref[0])
    ctx_tiles = range(n_lat, n_lat + n_ctx)
    n = 4 * tq
    problems, targets = [], []
    for j in range(2 if n_ctx else 0):
        for q_ref, k_ref, vt_ref, o_ref, sink in ((qa_ref, ka_ref, vta_ref, oa_ref, None),
                                                  (qc_ref, kc_ref, vtc_ref, oc_ref, _sink_row(sink_ref, j * 4, tq))):
            rhs = jnp.concatenate([rhs_of(q_ref, j, t) for t in ctx_tiles], axis=1)
            extra = None if sink is None else jnp.concatenate([sink] * n_ctx, axis=1)
            problems.append((k_ref[s_len:, :], vt_ref[_rows(j, HD), s_len:], rhs, extra))
            targets.append((o_ref, j))
    for (o_ref, j), ot in zip(targets, _small_softmax_pv(problems)):
        for i, t in enumerate(ctx_tiles):
            store(o_ref, j, t, ot[:, i * n:(i + 1) * n])


def _attn_b_kernel(zero_ref, lamv_ref, subln_ref, qt_ref, k_ref, vt_ref, o_ref, s0, s1, *, tq, n_lat, n_ctx, s_len,
                   lam_init):
    lv = lamv_ref[...]
    lam = (jnp.exp(jnp.sum(lv[0:1] * lv[1:2], axis=-1, keepdims=True))
           - jnp.exp(jnp.sum(lv[2:3] * lv[3:4], axis=-1, keepdims=True)) + lam_init)
    t_all = k_ref.shape[0]
    n_chunks = t_all // KCHUNK
    hw = 2 * HD

    def rhs_of(h, t):
        qt = qt_ref[_rows(h, hw), _tile(t, tq)]
        z = jnp.zeros((HD, tq), qt.dtype)
        return jnp.concatenate([jnp.concatenate([qt[:HD], z], axis=0),
                                jnp.concatenate([z, qt[HD:]], axis=0)], axis=1)

    def store(h, t, o2):
        o = o2[:, :tq] - lam * o2[:, tq:]
        ms = jnp.mean(o * o, axis=0, keepdims=True)
        o = o * lax.rsqrt(ms + SUBLN_EPS) * subln_ref[...] * (1.0 - lam_init)
        o_ref[_tile(t, tq), _rows(h, hw)] = o.T.astype(o_ref.dtype)

    def score_chunks(u):
        h, _ = _split_unit(u, n_lat)
        return [functools.partial(lambda c: (k_ref[c * KCHUNK:(c + 1) * KCHUNK, _rows(h, hw)], None, 0), c)
                for c in range(n_chunks)]

    def value_chunks(u):
        h, _ = _split_unit(u, n_lat)
        return [functools.partial(lambda c: (vt_ref[_rows(h, hw), c * KCHUNK:(c + 1) * KCHUNK], 0), c)
                for c in range(n_chunks)]

    _pipeline(lambda u: [rhs_of(*_split_unit(u, n_lat))], score_chunks, value_chunks,
              lambda u, ots: store(*_split_unit(u, n_lat), ots[0]), [(hw, 2 * tq)], s0, s1, 4 * n_lat, lag=3,
              zero=zero_ref[0])
    units = [(h, t) for h in range(4) for t in range(n_lat, n_lat + n_ctx)]
    problems = [(k_ref[s_len:, _rows(h, hw)], vt_ref[_rows(h, hw), s_len:], rhs_of(h, t), None) for h, t in units]
    for (h, t), o2 in zip(units, _small_softmax_pv(problems)):
        store(h, t, o2)


def _window_bias(tq):
    wk = tq + 2 * WINDOW
    r = np.arange(wk)[:, None]
    c = np.arange(tq)[None, :]
    out = np.stack([np.where(np.abs(c - r + v * WINDOW) <= WINDOW, 0.0, -np.inf) for v in range(3)])
    return jnp.asarray(out, F32)


def _attn_calls(qt_all, k_all, vt_all, sink, lamv, subln_t, layer, with_ctx, s_len, lam_init,
                tq_a, tq_b):
    nb, _, t_all = qt_all.shape
    c_len = t_all - s_len
    o_rows = t_all if with_ctx else s_len
    o_shape = jax.ShapeDtypeStruct((nb, o_rows, QW), BF16)

    def steps(tq):
        n_lat = s_len // tq
        return n_lat, n_lat + (c_len // tq if with_ctx else 0)

    whole = lambda rows, cols, r, c: pl.BlockSpec((None, rows, cols), lambda b: (b, r, c))
    o_spec = pl.BlockSpec((None, o_rows, QW), lambda b: (b, 0, 0))
    score_scratch = lambda rows, n: [pltpu.VMEM((rows, n), F32), pltpu.VMEM((rows, n), F32)]
    zero = jnp.zeros((1,), jnp.int32)

    n_lat, n_all = steps(tq_a)
    wk = tq_a + 2 * WINDOW
    oa, oc = pl.pallas_call(
        functools.partial(_attn_ac_kernel, tq=tq_a, n_lat=n_lat, n_ctx=n_all - n_lat, s_len=s_len),
        out_shape=(o_shape, o_shape),
        grid=(nb,),
        in_specs=[
            pl.BlockSpec(memory_space=pltpu.SMEM),
            pl.BlockSpec(memory_space=pltpu.SMEM),
            pl.BlockSpec((3, wk, tq_a), lambda b: (0, 0, 0)),
            whole(QW, t_all, 0, 0), whole(QW, t_all, 2, 0),
            whole(t_all, LANES, 0, KA_TILE), whole(t_all, LANES, 0, KA_TILE + 1),
            whole(LANES, t_all, KA_TILE, 0), whole(LANES, t_all, KA_TILE + 1, 0),
        ],
        out_specs=(o_spec, o_spec),
        scratch_shapes=score_scratch(t_all + c_len + wk, 4 * tq_a),
        compiler_params=_cparams(1),
        name="attn_ac",
    )(zero, sink[layer], _window_bias(tq_a), qt_all, qt_all, k_all, k_all, vt_all, vt_all)

    n_lat, n_all = steps(tq_b)
    ob = pl.pallas_call(
        functools.partial(_attn_b_kernel, tq=tq_b, n_lat=n_lat, n_ctx=n_all - n_lat, s_len=s_len,
                          lam_init=lam_init),
        out_shape=o_shape,
        grid=(nb,),
        in_specs=[
            pl.BlockSpec(memory_space=pltpu.SMEM),
            pl.BlockSpec((None, 4, HD), lambda b: (layer, 0, 0)),
            pl.BlockSpec((None, 2 * HD, tq_b), lambda b: (layer, 0, 0)),
            whole(QW, t_all, 1, 0), whole(t_all, QW, 0, 0), whole(QW, t_all, 0, 0),
        ],
        out_specs=o_spec,
        scratch_shapes=score_scratch(t_all, 2 * tq_b),
        compiler_params=_cparams(1),
        name="attn_b",
    )(zero, lamv, subln_t, qt_all, k_all, vt_all)

    return oa, ob, oc


def _merge_kernel(*refs, sub, n_sub, ctx_step):
    x_refs = refs[:n_sub]
    rest = refs[n_sub:]
    ctx_ref = None
    if ctx_step is not None:
        ctx_ref, rest = rest[0], rest[1:]
    (oa_ref, ob_ref, oc_ref, modb_ref, modc_ref, gpre_ref, gpost_ref, wgm_ref, bmg_ref, wbr_ref, wout_ref,
     out_ref) = rest

    def gates(i):
        x, is_ctx = _sub_tile_input(x_refs, ctx_ref, i, ctx_step)
        mod = jnp.where(is_ctx, modc_ref[...], modb_ref[...])
        h = _modulated_norm(x, mod, gpre_ref[...])
        gm = jnp.dot(h.astype(BF16), wgm_ref[...], preferred_element_type=F32)
        return x, mod, gm

    def finish(i, x, mod, gm):
        rows = slice(i * sub, (i + 1) * sub)
        z = None
        for j, o_ref in enumerate((oa_ref, ob_ref, oc_ref)):
            g = gm[:, j * QW:(j + 1) * QW]
            u = (o_ref[rows, :].astype(F32) * (g * _sigmoid(g))).astype(BF16)
            p = jnp.dot(u, wbr_ref[j], preferred_element_type=F32)
            mg = _sigmoid(gm[:, N_G + j * D:N_G + (j + 1) * D] + bmg_ref[:, j * D:(j + 1) * D])
            z = mg * p if z is None else z + mg * p
        y = jnp.dot(z.astype(BF16), wout_ref[...], preferred_element_type=F32)
        ms = jnp.mean(y * y, axis=-1, keepdims=True)
        gate = mod[:, 2 * D:]
        out_ref[rows, :] = x + gate * (y * lax.rsqrt(ms + EPS) * gpost_ref[...])

    prev = gates(0)
    for i in range(1, n_sub):
        cur = gates(i)
        finish(i - 1, *prev)
        prev = cur
    finish(n_sub - 1, *prev)


def _merge_call(x_src, ctx_src, oa, ob, oc, mod4, g_pre, g_post, wgm, b_mg, wbr, wout, layer, s_len, rows_out,
                tm, sub):
    nb = x_src.shape[0]
    n_sub = tm // sub
    tok_specs, tok_args = _token_specs(x_src, ctx_src, n_sub, sub, s_len)
    ctx_step = None if ctx_src is None else rows_out // tm - 1
    tok = lambda b, t: (b, t, 0)
    lay2 = lambda b, t: (layer, 0, 0)
    const = dict(pipeline_mode=pl.Buffered(1))
    return pl.pallas_call(
        functools.partial(_merge_kernel, sub=sub, n_sub=n_sub, ctx_step=ctx_step),
        out_shape=jax.ShapeDtypeStruct((nb, rows_out, D), F32),
        grid=(nb, rows_out // tm),
        in_specs=tok_specs + [
            pl.BlockSpec((None, tm, QW), tok),
            pl.BlockSpec((None, tm, QW), tok),
            pl.BlockSpec((None, tm, QW), tok),
            pl.BlockSpec((None, None, 1, 3 * D), lambda b, t: (layer, b, 0, 0)),
            pl.BlockSpec((None, None, 1, 3 * D), lambda b, t: (layer, nb, 0, 0)),
            pl.BlockSpec((None, 1, D), lay2),
            pl.BlockSpec((None, 1, D), lay2),
            pl.BlockSpec((None, D, N_G + N_M), lay2, **const),
            pl.BlockSpec((None, 1, N_M), lay2),
            pl.BlockSpec((None, 3, QW, D), lambda b, t: (layer, 0, 0, 0), **const),
            pl.BlockSpec((None, D, D), lay2, **const),
        ],
        out_specs=pl.BlockSpec((None, tm, D), tok),
        compiler_params=_cparams(2),
        name="merge",
    )(*tok_args, oa, ob, oc, mod4, mod4, g_pre, g_post, wgm, b_mg, wbr, wout)


def _rope_tables(s_len, c_len):
    rows = s_len // GRID_W
    row = np.repeat(np.arange(rows), GRID_W).astype(np.float32)
    col = np.tile(np.arange(GRID_W), rows).astype(np.float32)
    freqs = (np.float32(ROPE_THETA) ** (-np.arange(ROPE_PAIRS, dtype=np.float32) / ROPE_PAIRS)).astype(np.float32)
    ang_r = row[:, None] * freqs
    ang_c = col[:, None] * freqs
    ang = np.concatenate([ang_r, ang_r, ang_c, ang_c], axis=-1)
    cos = np.concatenate([np.cos(ang), np.ones((c_len, HD), np.float32)], axis=0).astype(np.float32)
    sin = np.concatenate([np.sin(ang), np.zeros((c_len, HD), np.float32)], axis=0).astype(np.float32)
    first = (np.arange(HD) % 32) < 16
    sin_a = np.where(first, -sin, np.float32(0.0))
    sin_b = np.where(first, np.float32(0.0), sin)
    tile2 = lambda a: np.concatenate([a, a], axis=-1)
    cost = np.ascontiguousarray(cos.T)
    sint = np.ascontiguousarray((sin_a + sin_b).T)
    return tuple(jnp.asarray(a, F32) for a in (cost, sint, tile2(cos), tile2(sin_a), tile2(sin_b)))


def _cols(w, names):
    return jnp.concatenate([w[..., _IN[n][0]:_IN[n][1]] for n in names], axis=-1)


def kernel(x, c, ctx, c_ctx, w_ada, b_ada, g_pre, g_post, w_in, q_norm, k_norm, lam_q1, lam_k1, lam_q2,
           lam_k2, subln, sink, w_br_a, w_br_b, w_br_c, w_mg, b_mg, w_out):
    nb, s_len, _ = x.shape
    c_len = ctx.shape[1]
    depth = w_in.shape[0]
    sub = 256
    tm_all = 3 * sub
    tm_lat = 4 * sub
    tq_a, tq_b = 128, 256
    assert (s_len + c_len) % tm_all == 0 and s_len % tm_lat == 0 and s_len % sub == 0
    assert c_len == KCHUNK == sub

    wp = _cols(w_in, ("qa", "qb", "qc", "kb", "ka", "kc", "vb", "va", "vc")).astype(BF16)
    wgm = jnp.concatenate([_cols(w_in, ("ga", "gb", "gc")), w_mg], axis=-1).astype(BF16)
    wbr = jnp.stack([w_br_a, w_br_b, w_br_c], axis=1).astype(BF16)
    wout = w_out.astype(BF16)

    tabs = _rope_tables(s_len, c_len)
    gq_t = jnp.broadcast_to(q_norm[:, :, None], (depth, HD, sub))
    gk_n = jnp.concatenate([k_norm, k_norm], axis=-1)[:, None, :]
    lamv = jnp.stack([lam_q1, lam_k1, lam_q2, lam_k2], axis=1)
    subln_t = jnp.broadcast_to(subln[:, :, None], (depth, 2 * HD, tq_b))
    g_pre3 = g_pre[:, None, :]
    g_post3 = g_post[:, None, :]
    b_mg3 = b_mg[:, None, :]

    rows = ((nb + 1 + 7) // 8) * 8
    sc_in = jnp.concatenate([c, c_ctx[None, :], jnp.zeros((rows - nb - 1, D), F32)], axis=0)
    mod = _ada_call(sc_in, w_ada, b_ada)
    mod4 = mod[:, :, None, :]

    t_all = s_len + c_len
    x_src, ctx_src = x, (ctx, 0)
    for layer in range(depth):
        last = layer == depth - 1
        lam_init = 0.8 - 0.6 * math.exp(-0.3 * layer)
        qt_all, k_all, vt_all = _proj_call(x_src, ctx_src, mod4, g_pre3, wp, tabs, gq_t, gk_n, layer, s_len,
                                           t_all, tm_all, sub)
        oa, ob, oc = _attn_calls(qt_all, k_all, vt_all, sink, lamv, subln_t, layer, not last, s_len,
                                 lam_init, tq_a, tq_b)
        rows_out, tm = (s_len, tm_lat) if last else (t_all, tm_all)
        xs = _merge_call(x_src, None if last else ctx_src, oa, ob, oc, mod4, g_pre3, g_post3, wgm, b_mg3, wbr,
                         wout, layer, s_len, rows_out, tm, sub)
        x_src, ctx_src = xs, (xs, s_len // sub)
    return xs
```

```python
import functools
import math

import jax
import jax.numpy as jnp
import numpy as np
from jax import lax
from jax.experimental import pallas as pl
from jax.experimental.pallas import tpu as pltpu

F32 = jnp.float32
BF16 = jnp.bfloat16

D = 1024
HD = 64
GRID_W = 64
WINDOW = 128
ROPE_THETA = 10000.0
ROPE_PAIRS = HD // 4
EPS = 1e-6
SUBLN_EPS = 1e-5
ATTN_SCALE = HD ** -0.5
LOG2E = math.log2(math.e)

QW = 512
N_Q = 3 * QW
N_K = 512 + 128 + 128
N_V = 512 + 128 + 128
KA_TILE = 4
N_P = N_Q + N_K + N_V
N_G = 3 * QW
N_M = 3 * D

LANES = 128
VMEM_LIMIT = 56 * 1024 * 1024

_IN = dict(qa=(0, 512), ka=(512, 640), va=(640, 768), ga=(768, 1280),
           qb=(1280, 1792), kb=(1792, 2304), vb=(2304, 2816), gb=(2816, 3328),
           qc=(3328, 3840), kc=(3840, 3968), vc=(3968, 4096), gc=(4096, 4608))


def _sigmoid(v):
    return 1.0 / (1.0 + jnp.exp(-v))


def _cparams(n_axes):
    return pltpu.CompilerParams(dimension_semantics=("arbitrary",) * n_axes,
                                vmem_limit_bytes=VMEM_LIMIT)


def _ada_kernel(sc_ref, w_ref, b_ref, o_ref):
    v = sc_ref[...]
    s = (v * _sigmoid(v)).astype(BF16)
    o_ref[...] = jnp.dot(s, w_ref[...].astype(BF16), preferred_element_type=F32) + b_ref[...]


def _ada_call(sc_in, w_ada, b_ada):
    depth = w_ada.shape[0]
    rows = sc_in.shape[0]
    nblk = 3
    return pl.pallas_call(
        _ada_kernel,
        out_shape=jax.ShapeDtypeStruct((depth, rows, 3 * D), F32),
        grid=(depth, nblk),
        in_specs=[
            pl.BlockSpec((rows, D), lambda l, n: (0, 0)),
            pl.BlockSpec((None, D, D), lambda l, n: (l, 0, n)),
            pl.BlockSpec((None, 1, D), lambda l, n: (l, 0, n)),
        ],
        out_specs=pl.BlockSpec((None, rows, D), lambda l, n: (l, 0, n)),
        compiler_params=_cparams(2),
        name="adaln",
    )(sc_in, w_ada, b_ada.reshape(depth, 1, 3 * D))


def _modulated_norm(x, mod, gpre):
    shift = mod[:, :D]
    scale = mod[:, D:2 * D]
    ms = jnp.mean(x * x, axis=-1, keepdims=True)
    return x * lax.rsqrt(ms + EPS) * gpre * (1.0 + scale) + shift


def _sub_tile_input(x_refs, ctx_ref, i, ctx_step):
    x = x_refs[i][...]
    if ctx_ref is None or i != len(x_refs) - 1:
        return x, False
    is_ctx = pl.program_id(1) == ctx_step
    return jnp.where(is_ctx, ctx_ref[...], x), is_ctx


def _proj_kernel(*refs, sub, n_sub, ctx_step):
    x_refs, ctx_ref = refs[:n_sub], refs[n_sub]
    (modb_ref, modc_ref, gpre_ref, w_ref, cost_ref, sint_ref, cosn_ref, sina_ref, sinb_ref, gq_ref, gk_ref,
     qt_ref, k_ref, vt_ref) = refs[n_sub + 1:]

    def project(i):
        x, is_ctx = _sub_tile_input(x_refs, ctx_ref, i, ctx_step)
        mod = jnp.where(is_ctx, modc_ref[...], modb_ref[...])
        h = _modulated_norm(x, mod, gpre_ref[...])
        return jnp.dot(h.astype(BF16), w_ref[...], preferred_element_type=F32)

    def finish(i, y):
        rows = slice(i * sub, (i + 1) * sub)
        q3 = y[:, :N_Q].T.reshape(N_Q // HD, HD, sub)
        qa = q3[:8]
        ss = jnp.sum(qa * qa, axis=1, keepdims=True)
        qa = qa * lax.rsqrt(ss * (1.0 / HD) + EPS) * gq_ref[...][None]
        q3 = jnp.concatenate([qa, q3[8:]], axis=0)
        rot = jnp.concatenate([q3[:, 16:32], q3[:, 0:16], q3[:, 48:64], q3[:, 32:48]], axis=1)
        q3 = (q3 * cost_ref[:, rows][None] + rot * sint_ref[:, rows][None]) * (ATTN_SCALE * LOG2E)
        qt_ref[:, rows] = q3.reshape(N_Q, sub).astype(BF16)

        ka = y[:, N_Q + KA_TILE * LANES:N_Q + (KA_TILE + 1) * LANES]
        lane = lax.broadcasted_iota(jnp.int32, (1, LANES), 1)
        lo = lane < HD
        sq = ka * ka
        s_lo = jnp.sum(jnp.where(lo, sq, 0.0), axis=-1, keepdims=True)
        s_hi = jnp.sum(jnp.where(lo, 0.0, sq), axis=-1, keepdims=True)
        r = jnp.where(lo, lax.rsqrt(s_lo * (1.0 / HD) + EPS), lax.rsqrt(s_hi * (1.0 / HD) + EPS))
        ka = ka * r * gk_ref[...]
        cosn = cosn_ref[rows, :]
        sina = sina_ref[rows, :]
        sinb = sinb_ref[rows, :]
        for j in range(N_K // LANES):
            t = ka if j == KA_TILE else y[:, N_Q + j * LANES:N_Q + (j + 1) * LANES]
            t = t * cosn + pltpu.roll(t, LANES - 16, 1) * sina + pltpu.roll(t, 16, 1) * sinb
            k_ref[rows, j * LANES:(j + 1) * LANES] = t.astype(BF16)

        vt_ref[:, rows] = y[:, N_Q + N_K:].T.astype(BF16)

    y_prev = project(0)
    for i in range(1, n_sub):
        y = project(i)
        finish(i - 1, y_prev)
        y_prev = y
    finish(n_sub - 1, y_prev)


def _token_specs(x_src, ctx_src, n_sub, sub, s_len):
    last = s_len // sub - 1
    specs = [pl.BlockSpec((None, sub, D), functools.partial(lambda i, b, t: (b, jnp.minimum(n_sub * t + i, last), 0), i))
             for i in range(n_sub)]
    operands = [x_src] * n_sub
    if ctx_src is not None:
        arr, blk = ctx_src
        specs.append(pl.BlockSpec((None, sub, D), lambda b, t: (b, blk, 0)))
        operands.append(arr)
    return specs, operands


def _proj_call(x_src, ctx_src, mod4, g_pre, wp, tabs, gq_t, gk_n, layer, s_len, t_all, tm, sub):
    nb = x_src.shape[0]
    cost, sint, cosn, sina, sinb = tabs
    n_sub = tm // sub
    tok_specs, tok_args = _token_specs(x_src, ctx_src, n_sub, sub, s_len)
    return pl.pallas_call(
        functools.partial(_proj_kernel, sub=sub, n_sub=n_sub, ctx_step=t_all // tm - 1),
        out_shape=(jax.ShapeDtypeStruct((nb, N_Q, t_all), BF16),
                   jax.ShapeDtypeStruct((nb, t_all, N_K), BF16),
                   jax.ShapeDtypeStruct((nb, N_V, t_all), BF16)),
        grid=(nb, t_all // tm),
        in_specs=tok_specs + [
            pl.BlockSpec((None, None, 1, 3 * D), lambda b, t: (layer, b, 0, 0)),
            pl.BlockSpec((None, None, 1, 3 * D), lambda b, t: (layer, nb, 0, 0)),
            pl.BlockSpec((None, 1, D), lambda b, t: (layer, 0, 0)),
            pl.BlockSpec((None, D, N_P), lambda b, t: (layer, 0, 0)),
            pl.BlockSpec((HD, tm), lambda b, t: (0, t)),
            pl.BlockSpec((HD, tm), lambda b, t: (0, t)),
            pl.BlockSpec((tm, LANES), lambda b, t: (t, 0)),
            pl.BlockSpec((tm, LANES), lambda b, t: (t, 0)),
            pl.BlockSpec((tm, LANES), lambda b, t: (t, 0)),
            pl.BlockSpec((None, HD, sub), lambda b, t: (layer, 0, 0)),
            pl.BlockSpec((None, 1, LANES), lambda b, t: (layer, 0, 0)),
        ],
        out_specs=(pl.BlockSpec((None, N_Q, tm), lambda b, t: (b, 0, t)),
                   pl.BlockSpec((None, tm, N_K), lambda b, t: (b, t, 0)),
                   pl.BlockSpec((None, N_V, tm), lambda b, t: (b, 0, t))),
        compiler_params=_cparams(2),
        name="proj",
    )(*tok_args, mod4, mod4, g_pre, wp, cost, sint, cosn, sina, sinb, gq_t, gk_n)


KCHUNK = 256
ONES_ROWS = 16
UNITS_PER_BODY = 2


def _small_softmax_pv(problems):
    ss = [jnp.dot(k, rhs, preferred_element_type=F32) for k, _, rhs, _ in problems]
    ms = [jnp.max(s, axis=0, keepdims=True) for s in ss]
    ms = [m if pr[3] is None else jnp.maximum(m, pr[3]) for m, pr in zip(ms, problems)]
    ps = [jnp.exp2(s - m).astype(BF16) for s, m in zip(ss, ms)]
    outs = []
    for p, m, (_, vt, _, extra) in zip(ps, ms, problems):
        dv = vt.shape[0]
        vt_aug = jnp.concatenate([vt, jnp.ones((ONES_ROWS, vt.shape[1]), BF16)], axis=0)
        pv = jnp.dot(vt_aug, p, preferred_element_type=F32)
        l = pv[dv:dv + 1]
        if extra is not None:
            l = l + jnp.exp2(extra - m)
        outs.append(pv[:dv] * (1.0 / l))
    return outs


def _gqa_rhs(qt, j, tq):
    z = jnp.zeros((HD, tq), qt.dtype)
    first = j == 0
    cols = []
    for g in range(4):
        qg = qt[g * HD:(g + 1) * HD, :]
        cols.append(jnp.concatenate([jnp.where(first, qg, z), jnp.where(first, z, qg)], axis=0))
    return jnp.concatenate(cols, axis=1)


def _store_heads(o_ref, ot, tq, nheads):
    o = jnp.concatenate([ot[:, g * tq:(g + 1) * tq] for g in range(nheads)], axis=0)
    o_ref[...] = o.T.astype(o_ref.dtype)


def _sink_row(sink_ref, base, tq):
    blk = lax.broadcasted_iota(jnp.int32, (1, 4 * tq), 1) // tq
    row = jnp.zeros((1, 4 * tq), F32)
    for g in range(4):
        row = jnp.where(blk == g, sink_ref[base + g], row)
    return row * LOG2E


def _tile(t, size):
    if isinstance(t, int):
        return pl.ds(t * size, size)
    return pl.ds(pl.multiple_of(t * size, size), size)


def _two_stage(score_chunks, value_chunks, rhs_next, s_next, s_cur, m_cur, extra_next, extra_cur, lag=2,
               zero=None):
    n_groups = len(rhs_next if rhs_next is not None else m_cur)
    cm = [None] * n_groups
    acc = [None] * n_groups
    parts = []
    row = 0
    for c in range(len(score_chunks or value_chunks)):
        if score_chunks is not None:
            k, bias, g = score_chunks[c]()
            rows = slice(row, row + k.shape[0])
            s = jnp.dot(k, rhs_next[g], preferred_element_type=F32)
            if bias is not None:
                s = s + bias
            s_next[rows, :s.shape[1]] = s
            parts.append(jnp.max(s, axis=0, keepdims=True))
            cm[g] = parts[-1] if cm[g] is None else jnp.maximum(cm[g], parts[-1])
        if value_chunks is not None:
            vt, g = value_chunks[c]()
            m_c = m_cur[g]
            if score_chunks is not None and c >= lag:
                m_c = jnp.maximum(m_c, jnp.minimum(parts[c - lag], m_c))
            rows = slice(row, row + vt.shape[1])
            src_rows = rows if zero is None else pl.ds(pl.multiple_of(row + zero, LANES), vt.shape[1])
            p = jnp.exp2(s_cur[src_rows, :m_c.shape[1]] - m_c).astype(BF16)
            vt_aug = jnp.concatenate([vt, jnp.ones((ONES_ROWS, vt.shape[1]), BF16)], axis=0)
            pv = jnp.dot(vt_aug, p, preferred_element_type=F32)
            acc[g] = pv if acc[g] is None else acc[g] + pv
        row = rows.stop
    ots = None
    if score_chunks is not None:
        cm = [m if e is None else jnp.maximum(m, e) for m, e in zip(cm, extra_next)]
    if value_chunks is not None:
        ots = []
        for g in range(n_groups):
            dv = acc[g].shape[0] - ONES_ROWS
            l = acc[g][dv:dv + 1]
            if extra_cur[g] is not None:
                l = l + jnp.exp2(extra_cur[g] - m_cur[g])
            ots.append(acc[g][:dv] * (1.0 / l))
    return tuple(cm), ots


def _pipeline(make_rhs, score_chunks, value_chunks, finish, out_shapes, s0, s1, n_units, extra=None, lag=2,
              zero=None):
    rhs0 = make_rhs(0)
    ex = (lambda u: [None] * len(rhs0)) if extra is None else extra
    m0, _ = _two_stage(score_chunks(0), None, rhs0, s0, None, None, ex(0), None)

    bufs = (s0, s1)

    def body(i, carry):
        m_cur, ot = carry
        u0 = UNITS_PER_BODY * i
        finish(jnp.maximum(u0 - 1, 0), ot)
        for h in range(UNITS_PER_BODY):
            u = u0 + h
            if h:
                finish(u - 1, ot)
            nxt = u + 1 if h < UNITS_PER_BODY - 1 else jnp.minimum(u + 1, n_units - 1)
            m_cur, ot = _two_stage(score_chunks(nxt), value_chunks(u), make_rhs(nxt), bufs[(h + 1) % 2],
                                   bufs[h % 2], m_cur, ex(nxt), ex(u), lag, zero)
        return m_cur, ot

    assert n_units % UNITS_PER_BODY == 0
    _, ot_last = lax.fori_loop(0, n_units // UNITS_PER_BODY, body,
                               (m0, [jnp.zeros(s, F32) for s in out_shapes]))
    finish(n_units - 1, ot_last)


def _split_unit(u, n_tiles):
    if isinstance(u, int):
        return u // n_tiles, u % n_tiles
    return lax.div(u, n_tiles), lax.rem(u, n_tiles)


def _rows(g, size):
    return _tile(g, size)


def _attn_ac_kernel(zero_ref, sink_ref, bias_ref, qa_ref, qc_ref, ka_ref, kc_ref, vta_ref, vtc_ref, oa_ref, oc_ref,
                    s0, s1, *, tq, n_lat, n_ctx, s_len):
    t_all = ka_ref.shape[0]
    wk = tq + 2 * WINDOW
    spans = [(lo, min(lo + KCHUNK, wk)) for lo in range(0, wk, KCHUNK)]

    def rhs_of(q_ref, j, t):
        return _gqa_rhs(q_ref[_rows(j, 4 * HD), _tile(t, tq)], j, tq)

    def store(o_ref, j, t, ot):
        _store_heads(o_ref.at[_tile(t, tq), _rows(j, 4 * HD)], ot, tq, 4)

    def window(t):
        q0 = t * tq
        start = jnp.clip(q0 - WINDOW, 0, s_len - wk)
        return start, (q0 - start) // WINDOW

    def local(start, lo, hi):
        return pl.ds(pl.multiple_of(start + lo, LANES), hi - lo)

    def score_chunks(u):
        _, t = _split_unit(u, n_lat)
        start, variant = window(t)

        def loc(lo, hi):
            b = bias_ref[variant, lo:hi, :]
            return kc_ref[local(start, lo, hi), :], jnp.concatenate([b] * 4, axis=1), 1

        return ([functools.partial(lambda c: (ka_ref[c * KCHUNK:(c + 1) * KCHUNK, :], None, 0), c)
                 for c in range(t_all // KCHUNK)]
                + [lambda: (kc_ref[s_len:s_len + KCHUNK, :], None, 1)]
                + [functools.partial(loc, lo, hi) for lo, hi in spans])

    def value_chunks(u):
        j, t = _split_unit(u, n_lat)
        start, _ = window(t)
        return ([functools.partial(lambda c: (vta_ref[_rows(j, HD), c * KCHUNK:(c + 1) * KCHUNK], 0), c)
                 for c in range(t_all // KCHUNK)]
                + [lambda: (vtc_ref[_rows(j, HD), s_len:s_len + KCHUNK], 1)]
                + [functools.partial(lambda lo, hi: (vtc_ref[_rows(j, HD), local(start, lo, hi)], 1), lo, hi)
                   for lo, hi in spans])

    def make_rhs(u):
        j, t = _split_unit(u, n_lat)
        return [rhs_of(qa_ref, j, t), rhs_of(qc_ref, j, t)]

    def extra(u):
        j, _ = _split_unit(u, n_lat)
        return [None, _sink_row(sink_ref, j * 4, tq)]

    def finish(u, ots):
        j, t = _split_unit(u, n_lat)
        store(oa_ref, j, t, ots[0])
        store(oc_ref, j, t, ots[1])

    _pipeline(make_rhs, score_chunks, value_chunks, finish, [(HD, 4 * tq)] * 2, s0, s1, 2 * n_lat, extra=extra,
              lag=6, zero=zero_ref[0])
    ctx_tiles = range(n_lat, n_lat + n_ctx)
    n = 4 * tq
    problems, targets = [], []
    for j in range(2 if n_ctx else 0):
        for q_ref, k_ref, vt_ref, o_ref, sink in ((qa_ref, ka_ref, vta_ref, oa_ref, None),
                                                  (qc_ref, kc_ref, vtc_ref, oc_ref, _sink_row(sink_ref, j * 4, tq))):
            rhs = jnp.concatenate([rhs_of(q_ref, j, t) for t in ctx_tiles], axis=1)
            extra = None if sink is None else jnp.concatenate([sink] * n_ctx, axis=1)
            problems.append((k_ref[s_len:, :], vt_ref[_rows(j, HD), s_len:], rhs, extra))
            targets.append((o_ref, j))
    for (o_ref, j), ot in zip(targets, _small_softmax_pv(problems)):
        for i, t in enumerate(ctx_tiles):
            store(o_ref, j, t, ot[:, i * n:(i + 1) * n])


def _attn_b_kernel(zero_ref, lamv_ref, subln_ref, qt_ref, k_ref, vt_ref, o_ref, s0, s1, *, tq, n_lat, n_ctx, s_len,
                   lam_init):
    lv = lamv_ref[...]
    lam = (jnp.exp(jnp.sum(lv[0:1] * lv[1:2], axis=-1, keepdims=True))
           - jnp.exp(jnp.sum(lv[2:3] * lv[3:4], axis=-1, keepdims=True)) + lam_init)
    t_all = k_ref.shape[0]
    n_chunks = t_all // KCHUNK
    hw = 2 * HD

    def rhs_of(h, t):
        qt = qt_ref[_rows(h, hw), _tile(t, tq)]
        z = jnp.zeros((HD, tq), qt.dtype)
        return jnp.concatenate([jnp.concatenate([qt[:HD], z], axis=0),
                                jnp.concatenate([z, qt[HD:]], axis=0)], axis=1)

    def store(h, t, o2):
        o = o2[:, :tq] - lam * o2[:, tq:]
        ms = jnp.mean(o * o, axis=0, keepdims=True)
        o = o * lax.rsqrt(ms + SUBLN_EPS) * subln_ref[...] * (1.0 - lam_init)
        o_ref[_tile(t, tq), _rows(h, hw)] = o.T.astype(o_ref.dtype)

    def score_chunks(u):
        h, _ = _split_unit(u, n_lat)
        return [functools.partial(lambda c: (k_ref[c * KCHUNK:(c + 1) * KCHUNK, _rows(h, hw)], None, 0), c)
                for c in range(n_chunks)]

    def value_chunks(u):
        h, _ = _split_unit(u, n_lat)
        return [functools.partial(lambda c: (vt_ref[_rows(h, hw), c * KCHUNK:(c + 1) * KCHUNK], 0), c)
                for c in range(n_chunks)]

    _pipeline(lambda u: [rhs_of(*_split_unit(u, n_lat))], score_chunks, value_chunks,
              lambda u, ots: store(*_split_unit(u, n_lat), ots[0]), [(hw, 2 * tq)], s0, s1, 4 * n_lat, lag=3,
              zero=zero_ref[0])
    units = [(h, t) for h in range(4) for t in range(n_lat, n_lat + n_ctx)]
    problems = [(k_ref[s_len:, _rows(h, hw)], vt_ref[_rows(h, hw), s_len:], rhs_of(h, t), None) for h, t in units]
    for (h, t), o2 in zip(units, _small_softmax_pv(problems)):
        store(h, t, o2)


def _window_bias(tq):
    wk = tq + 2 * WINDOW
    r = np.arange(wk)[:, None]
    c = np.arange(tq)[None, :]
    out = np.stack([np.where(np.abs(c - r + v * WINDOW) <= WINDOW, 0.0, -np.inf) for v in range(3)])
    return jnp.asarray(out, F32)


def _attn_calls(qt_all, k_all, vt_all, sink, lamv, subln_t, layer, with_ctx, s_len, lam_init,
                tq_a, tq_b):
    nb, _, t_all = qt_all.shape
    c_len = t_all - s_len
    o_rows = t_all if with_ctx else s_len
    o_shape = jax.ShapeDtypeStruct((nb, o_rows, QW), BF16)

    def steps(tq):
        n_lat = s_len // tq
        return n_lat, n_lat + (c_len // tq if with_ctx else 0)

    whole = lambda rows, cols, r, c: pl.BlockSpec((None, rows, cols), lambda b: (b, r, c))
    o_spec = pl.BlockSpec((None, o_rows, QW), lambda b: (b, 0, 0))
    score_scratch = lambda rows, n: [pltpu.VMEM((rows, n + LANES), F32), pltpu.VMEM((rows, n + LANES), F32)]
    zero = jnp.zeros((1,), jnp.int32)

    n_lat, n_all = steps(tq_a)
    wk = tq_a + 2 * WINDOW
    oa, oc = pl.pallas_call(
        functools.partial(_attn_ac_kernel, tq=tq_a, n_lat=n_lat, n_ctx=n_all - n_lat, s_len=s_len),
        out_shape=(o_shape, o_shape),
        grid=(nb,),
        in_specs=[
            pl.BlockSpec(memory_space=pltpu.SMEM),
            pl.BlockSpec(memory_space=pltpu.SMEM),
            pl.BlockSpec((3, wk, tq_a), lambda b: (0, 0, 0)),
            whole(QW, t_all, 0, 0), whole(QW, t_all, 2, 0),
            whole(t_all, LANES, 0, KA_TILE), whole(t_all, LANES, 0, KA_TILE + 1),
            whole(LANES, t_all, KA_TILE, 0), whole(LANES, t_all, KA_TILE + 1, 0),
        ],
        out_specs=(o_spec, o_spec),
        scratch_shapes=score_scratch(t_all + c_len + wk, 4 * tq_a),
        compiler_params=_cparams(1),
        name="attn_ac",
    )(zero, sink[layer], _window_bias(tq_a), qt_all, qt_all, k_all, k_all, vt_all, vt_all)

    n_lat, n_all = steps(tq_b)
    ob = pl.pallas_call(
        functools.partial(_attn_b_kernel, tq=tq_b, n_lat=n_lat, n_ctx=n_all - n_lat, s_len=s_len,
                          lam_init=lam_init),
        out_shape=o_shape,
        grid=(nb,),
        in_specs=[
            pl.BlockSpec(memory_space=pltpu.SMEM),
            pl.BlockSpec((None, 4, HD), lambda b: (layer, 0, 0)),
            pl.BlockSpec((None, 2 * HD, tq_b), lambda b: (layer, 0, 0)),
            whole(QW, t_all, 1, 0), whole(t_all, QW, 0, 0), whole(QW, t_all, 0, 0),
        ],
        out_specs=o_spec,
        scratch_shapes=score_scratch(t_all, 2 * tq_b),
        compiler_params=_cparams(1),
        name="attn_b",
    )(zero, lamv, subln_t, qt_all, k_all, vt_all)

    return oa, ob, oc


def _merge_kernel(*refs, sub, n_sub, ctx_step):
    x_refs = refs[:n_sub]
    rest = refs[n_sub:]
    ctx_ref = None
    if ctx_step is not None:
        ctx_ref, rest = rest[0], rest[1:]
    (oa_ref, ob_ref, oc_ref, modb_ref, modc_ref, gpre_ref, gpost_ref, wgm_ref, bmg_ref, wbr_ref, wout_ref,
     out_ref) = rest

    def gates(i):
        x, is_ctx = _sub_tile_input(x_refs, ctx_ref, i, ctx_step)
        mod = jnp.where(is_ctx, modc_ref[...], modb_ref[...])
        h = _modulated_norm(x, mod, gpre_ref[...])
        gm = jnp.dot(h.astype(BF16), wgm_ref[...], preferred_element_type=F32)
        return x, mod, gm

    def finish(i, x, mod, gm):
        rows = slice(i * sub, (i + 1) * sub)
        z = None
        for j, o_ref in enumerate((oa_ref, ob_ref, oc_ref)):
            g = gm[:, j * QW:(j + 1) * QW]
            u = (o_ref[rows, :].astype(F32) * (g * _sigmoid(g))).astype(BF16)
            p = jnp.dot(u, wbr_ref[j], preferred_element_type=F32)
            mg = _sigmoid(gm[:, N_G + j * D:N_G + (j + 1) * D] + bmg_ref[:, j * D:(j + 1) * D])
            z = mg * p if z is None else z + mg * p
        y = jnp.dot(z.astype(BF16), wout_ref[...], preferred_element_type=F32)
        ms = jnp.mean(y * y, axis=-1, keepdims=True)
        gate = mod[:, 2 * D:]
        out_ref[rows, :] = x + gate * (y * lax.rsqrt(ms + EPS) * gpost_ref[...])

    prev = gates(0)
    for i in range(1, n_sub):
        cur = gates(i)
        finish(i - 1, *prev)
        prev = cur
    finish(n_sub - 1, *prev)


def _merge_call(x_src, ctx_src, oa, ob, oc, mod4, g_pre, g_post, wgm, b_mg, wbr, wout, layer, s_len, rows_out,
                tm, sub):
    nb = x_src.shape[0]
    n_sub = tm // sub
    tok_specs, tok_args = _token_specs(x_src, ctx_src, n_sub, sub, s_len)
    ctx_step = None if ctx_src is None else rows_out // tm - 1
    tok = lambda b, t: (b, t, 0)
    lay2 = lambda b, t: (layer, 0, 0)
    const = dict(pipeline_mode=pl.Buffered(1))
    return pl.pallas_call(
        functools.partial(_merge_kernel, sub=sub, n_sub=n_sub, ctx_step=ctx_step),
        out_shape=jax.ShapeDtypeStruct((nb, rows_out, D), F32),
        grid=(nb, rows_out // tm),
        in_specs=tok_specs + [
            pl.BlockSpec((None, tm, QW), tok),
            pl.BlockSpec((None, tm, QW), tok),
            pl.BlockSpec((None, tm, QW), tok),
            pl.BlockSpec((None, None, 1, 3 * D), lambda b, t: (layer, b, 0, 0)),
            pl.BlockSpec((None, None, 1, 3 * D), lambda b, t: (layer, nb, 0, 0)),
            pl.BlockSpec((None, 1, D), lay2),
            pl.BlockSpec((None, 1, D), lay2),
            pl.BlockSpec((None, D, N_G + N_M), lay2, **const),
            pl.BlockSpec((None, 1, N_M), lay2),
            pl.BlockSpec((None, 3, QW, D), lambda b, t: (layer, 0, 0, 0), **const),
            pl.BlockSpec((None, D, D), lay2, **const),
        ],
        out_specs=pl.BlockSpec((None, tm, D), tok),
        compiler_params=_cparams(2),
        name="merge",
    )(*tok_args, oa, ob, oc, mod4, mod4, g_pre, g_post, wgm, b_mg, wbr, wout)


def _rope_tables(s_len, c_len):
    rows = s_len // GRID_W
    row = np.repeat(np.arange(rows), GRID_W).astype(np.float32)
    col = np.tile(np.arange(GRID_W), rows).astype(np.float32)
    freqs = (np.float32(ROPE_THETA) ** (-np.arange(ROPE_PAIRS, dtype=np.float32) / ROPE_PAIRS)).astype(np.float32)
    ang_r = row[:, None] * freqs
    ang_c = col[:, None] * freqs
    ang = np.concatenate([ang_r, ang_r, ang_c, ang_c], axis=-1)
    cos = np.concatenate([np.cos(ang), np.ones((c_len, HD), np.float32)], axis=0).astype(np.float32)
    sin = np.concatenate([np.sin(ang), np.zeros((c_len, HD), np.float32)], axis=0).astype(np.float32)
    first = (np.arange(HD) % 32) < 16
    sin_a = np.where(first, -sin, np.float32(0.0))
    sin_b = np.where(first, np.float32(0.0), sin)
    tile2 = lambda a: np.concatenate([a, a], axis=-1)
    cost = np.ascontiguousarray(cos.T)
    sint = np.ascontiguousarray((sin_a + sin_b).T)
    return tuple(jnp.asarray(a, F32) for a in (cost, sint, tile2(cos), tile2(sin_a), tile2(sin_b)))


def _cols(w, names):
    return jnp.concatenate([w[..., _IN[n][0]:_IN[n][1]] for n in names], axis=-1)


def kernel(x, c, ctx, c_ctx, w_ada, b_ada, g_pre, g_post, w_in, q_norm, k_norm, lam_q1, lam_k1, lam_q2,
           lam_k2, subln, sink, w_br_a, w_br_b, w_br_c, w_mg, b_mg, w_out):
    nb, s_len, _ = x.shape
    c_len = ctx.shape[1]
    depth = w_in.shape[0]
    sub = 256
    tm_all = 3 * sub
    tm_lat = 4 * sub
    tq_a, tq_b = 128, 256
    assert (s_len + c_len) % tm_all == 0 and s_len % tm_lat == 0 and s_len % sub == 0
    assert c_len == KCHUNK == sub

    wp = _cols(w_in, ("qa", "qb", "qc", "kb", "ka", "kc", "vb", "va", "vc")).astype(BF16)
    wgm = jnp.concatenate([_cols(w_in, ("ga", "gb", "gc")), w_mg], axis=-1).astype(BF16)
    wbr = jnp.stack([w_br_a, w_br_b, w_br_c], axis=1).astype(BF16)
    wout = w_out.astype(BF16)

    tabs = _rope_tables(s_len, c_len)
    gq_t = jnp.broadcast_to(q_norm[:, :, None], (depth, HD, sub))
    gk_n = jnp.concatenate([k_norm, k_norm], axis=-1)[:, None, :]
    lamv = jnp.stack([lam_q1, lam_k1, lam_q2, lam_k2], axis=1)
    subln_t = jnp.broadcast_to(subln[:, :, None], (depth, 2 * HD, tq_b))
    g_pre3 = g_pre[:, None, :]
    g_post3 = g_post[:, None, :]
    b_mg3 = b_mg[:, None, :]

    rows = ((nb + 1 + 7) // 8) * 8
    sc_in = jnp.concatenate([c, c_ctx[None, :], jnp.zeros((rows - nb - 1, D), F32)], axis=0)
    mod = _ada_call(sc_in, w_ada, b_ada)
    mod4 = mod[:, :, None, :]

    t_all = s_len + c_len
    x_src, ctx_src = x, (ctx, 0)
    for layer in range(depth):
        last = layer == depth - 1
        lam_init = 0.8 - 0.6 * math.exp(-0.3 * layer)
        qt_all, k_all, vt_all = _proj_call(x_src, ctx_src, mod4, g_pre3, wp, tabs, gq_t, gk_n, layer, s_len,
                                           t_all, tm_all, sub)
        oa, ob, oc = _attn_calls(qt_all, k_all, vt_all, sink, lamv, subln_t, layer, not last, s_len,
                                 lam_init, tq_a, tq_b)
        rows_out, tm = (s_len, tm_lat) if last else (t_all, tm_all)
        xs = _merge_call(x_src, None if last else ctx_src, oa, ob, oc, mod4, g_pre3, g_post3, wgm, b_mg3, wbr,
                         wout, layer, s_len, rows_out, tm, sub)
        x_src, ctx_src = xs, (xs, s_len // sub)
    return xs
```

```python
import functools
import math

import jax
import jax.numpy as jnp
import numpy as np
from jax import lax
from jax.experimental import pallas as pl
from jax.experimental.pallas import tpu as pltpu

F32 = jnp.float32
BF16 = jnp.bfloat16

D = 1024
HD = 64
GRID_W = 64
WINDOW = 128
ROPE_THETA = 10000.0
ROPE_PAIRS = HD // 4
EPS = 1e-6
SUBLN_EPS = 1e-5
ATTN_SCALE = HD ** -0.5
LOG2E = math.log2(math.e)

QW = 512
N_Q = 3 * QW
N_K = 512 + 128 + 128
N_V = 512 + 128 + 128
KA_TILE = 4
N_P = N_Q + N_K + N_V
N_G = 3 * QW
N_M = 3 * D

LANES = 128
VMEM_LIMIT = 56 * 1024 * 1024

_IN = dict(qa=(0, 512), ka=(512, 640), va=(640, 768), ga=(768, 1280),
           qb=(1280, 1792), kb=(1792, 2304), vb=(2304, 2816), gb=(2816, 3328),
           qc=(3328, 3840), kc=(3840, 3968), vc=(3968, 4096), gc=(4096, 4608))


def _sigmoid(v):
    return 1.0 / (1.0 + jnp.exp(-v))


def _cparams(n_axes):
    return pltpu.CompilerParams(dimension_semantics=("arbitrary",) * n_axes,
                                vmem_limit_bytes=VMEM_LIMIT)


def _ada_kernel(sc_ref, w_ref, b_ref, o_ref):
    v = sc_ref[...]
    s = (v * _sigmoid(v)).astype(BF16)
    o_ref[...] = jnp.dot(s, w_ref[...].astype(BF16), preferred_element_type=F32) + b_ref[...]


def _ada_call(sc_in, w_ada, b_ada):
    depth = w_ada.shape[0]
    rows = sc_in.shape[0]
    nblk = 3
    return pl.pallas_call(
        _ada_kernel,
        out_shape=jax.ShapeDtypeStruct((depth, rows, 3 * D), F32),
        grid=(depth, nblk),
        in_specs=[
            pl.BlockSpec((rows, D), lambda l, n: (0, 0)),
            pl.BlockSpec((None, D, D), lambda l, n: (l, 0, n)),
            pl.BlockSpec((None, 1, D), lambda l, n: (l, 0, n)),
        ],
        out_specs=pl.BlockSpec((None, rows, D), lambda l, n: (l, 0, n)),
        compiler_params=_cparams(2),
        name="adaln",
    )(sc_in, w_ada, b_ada.reshape(depth, 1, 3 * D))


def _modulated_norm(x, mod, gpre):
    shift = mod[:, :D]
    scale = mod[:, D:2 * D]
    ms = jnp.mean(x * x, axis=-1, keepdims=True)
    return x * lax.rsqrt(ms + EPS) * gpre * (1.0 + scale) + shift


def _sub_tile_input(x_refs, ctx_ref, i, ctx_step):
    x = x_refs[i][...]
    if ctx_ref is None or i != len(x_refs) - 1:
        return x, False
    is_ctx = pl.program_id(1) == ctx_step
    return jnp.where(is_ctx, ctx_ref[...], x), is_ctx


def _proj_kernel(*refs, sub, n_sub, ctx_step):
    x_refs, ctx_ref = refs[:n_sub], refs[n_sub]
    (modb_ref, modc_ref, gpre_ref, w_ref, cost_ref, sint_ref, cosn_ref, sina_ref, sinb_ref, gq_ref, gk_ref,
     qt_ref, k_ref, vt_ref) = refs[n_sub + 1:]

    def project(i):
        x, is_ctx = _sub_tile_input(x_refs, ctx_ref, i, ctx_step)
        mod = jnp.where(is_ctx, modc_ref[...], modb_ref[...])
        h = _modulated_norm(x, mod, gpre_ref[...])
        return jnp.dot(h.astype(BF16), w_ref[...], preferred_element_type=F32)

    def finish(i, y):
        rows = slice(i * sub, (i + 1) * sub)
        q3 = y[:, :N_Q].T.reshape(N_Q // HD, HD, sub)
        qa = q3[:8]
        ss = jnp.sum(qa * qa, axis=1, keepdims=True)
        qa = qa * lax.rsqrt(ss * (1.0 / HD) + EPS) * gq_ref[...][None]
        q3 = jnp.concatenate([qa, q3[8:]], axis=0)
        rot = jnp.concatenate([q3[:, 16:32], q3[:, 0:16], q3[:, 48:64], q3[:, 32:48]], axis=1)
        q3 = (q3 * cost_ref[:, rows][None] + rot * sint_ref[:, rows][None]) * (ATTN_SCALE * LOG2E)
        qt_ref[:, rows] = q3.reshape(N_Q, sub).astype(BF16)

        ka = y[:, N_Q + KA_TILE * LANES:N_Q + (KA_TILE + 1) * LANES]
        lane = lax.broadcasted_iota(jnp.int32, (1, LANES), 1)
        lo = lane < HD
        sq = ka * ka
        s_lo = jnp.sum(jnp.where(lo, sq, 0.0), axis=-1, keepdims=True)
        s_hi = jnp.sum(jnp.where(lo, 0.0, sq), axis=-1, keepdims=True)
        r = jnp.where(lo, lax.rsqrt(s_lo * (1.0 / HD) + EPS), lax.rsqrt(s_hi * (1.0 / HD) + EPS))
        ka = ka * r * gk_ref[...]
        cosn = cosn_ref[rows, :]
        sina = sina_ref[rows, :]
        sinb = sinb_ref[rows, :]
        for j in range(N_K // LANES):
            t = ka if j == KA_TILE else y[:, N_Q + j * LANES:N_Q + (j + 1) * LANES]
            t = t * cosn + pltpu.roll(t, LANES - 16, 1) * sina + pltpu.roll(t, 16, 1) * sinb
            k_ref[rows, j * LANES:(j + 1) * LANES] = t.astype(BF16)

        vt_ref[:, rows] = y[:, N_Q + N_K:].T.astype(BF16)

    y_prev = project(0)
    for i in range(1, n_sub):
        y = project(i)
        finish(i - 1, y_prev)
        y_prev = y
    finish(n_sub - 1, y_prev)


def _token_specs(x_src, ctx_src, n_sub, sub, s_len):
    last = s_len // sub - 1
    specs = [pl.BlockSpec((None, sub, D), functools.partial(lambda i, b, t: (b, jnp.minimum(n_sub * t + i, last), 0), i))
             for i in range(n_sub)]
    operands = [x_src] * n_sub
    if ctx_src is not None:
        arr, blk = ctx_src
        specs.append(pl.BlockSpec((None, sub, D), lambda b, t: (b, blk, 0)))
        operands.append(arr)
    return specs, operands


def _proj_call(x_src, ctx_src, mod4, g_pre, wp, tabs, gq_t, gk_n, layer, s_len, t_all, tm, sub):
    nb = x_src.shape[0]
    cost, sint, cosn, sina, sinb = tabs
    n_sub = tm // sub
    tok_specs, tok_args = _token_specs(x_src, ctx_src, n_sub, sub, s_len)
    return pl.pallas_call(
        functools.partial(_proj_kernel, sub=sub, n_sub=n_sub, ctx_step=t_all // tm - 1),
        out_shape=(jax.ShapeDtypeStruct((nb, N_Q, t_all), BF16),
                   jax.ShapeDtypeStruct((nb, t_all, N_K), BF16),
                   jax.ShapeDtypeStruct((nb, N_V, t_all), BF16)),
        grid=(nb, t_all // tm),
        in_specs=tok_specs + [
            pl.BlockSpec((None, None, 1, 3 * D), lambda b, t: (layer, b, 0, 0)),
            pl.BlockSpec((None, None, 1, 3 * D), lambda b, t: (layer, nb, 0, 0)),
            pl.BlockSpec((None, 1, D), lambda b, t: (layer, 0, 0)),
            pl.BlockSpec((None, D, N_P), lambda b, t: (layer, 0, 0)),
            pl.BlockSpec((HD, tm), lambda b, t: (0, t)),
            pl.BlockSpec((HD, tm), lambda b, t: (0, t)),
            pl.BlockSpec((tm, LANES), lambda b, t: (t, 0)),
            pl.BlockSpec((tm, LANES), lambda b, t: (t, 0)),
            pl.BlockSpec((tm, LANES), lambda b, t: (t, 0)),
            pl.BlockSpec((None, HD, sub), lambda b, t: (layer, 0, 0)),
            pl.BlockSpec((None, 1, LANES), lambda b, t: (layer, 0, 0)),
        ],
        out_specs=(pl.BlockSpec((None, N_Q, tm), lambda b, t: (b, 0, t)),
                   pl.BlockSpec((None, tm, N_K), lambda b, t: (b, t, 0)),
                   pl.BlockSpec((None, N_V, tm), lambda b, t: (b, 0, t))),
        compiler_params=_cparams(2),
        name="proj",
    )(*tok_args, mod4, mod4, g_pre, wp, cost, sint, cosn, sina, sinb, gq_t, gk_n)


KCHUNK = 256
ONES_ROWS = 16


def _small_softmax_pv(problems):
    ss = [jnp.dot(k, rhs, preferred_element_type=F32) for k, _, rhs, _ in problems]
    ms = [jnp.max(s, axis=0, keepdims=True) for s in ss]
    ms = [m if pr[3] is None else jnp.maximum(m, pr[3]) for m, pr in zip(ms, problems)]
    ps = [jnp.exp2(s - m).astype(BF16) for s, m in zip(ss, ms)]
    outs = []
    for p, m, (_, vt, _, extra) in zip(ps, ms, problems):
        dv = vt.shape[0]
        vt_aug = jnp.concatenate([vt, jnp.ones((ONES_ROWS, vt.shape[1]), BF16)], axis=0)
        pv = jnp.dot(vt_aug, p, preferred_element_type=F32)
        l = pv[dv:dv + 1]
        if extra is not None:
            l = l + jnp.exp2(extra - m)
        outs.append(pv[:dv] * (1.0 / l))
    return outs


def _gqa_rhs(qt, j, tq):
    z = jnp.zeros((HD, tq), qt.dtype)
    first = j == 0
    cols = []
    for g in range(4):
        qg = qt[g * HD:(g + 1) * HD, :]
        cols.append(jnp.concatenate([jnp.where(first, qg, z), jnp.where(first, z, qg)], axis=0))
    return jnp.concatenate(cols, axis=1)


def _store_heads(o_ref, ot, tq, nheads):
    o = jnp.concatenate([ot[:, g * tq:(g + 1) * tq] for g in range(nheads)], axis=0)
    o_ref[...] = o.T.astype(o_ref.dtype)


def _sink_row(sink_ref, base, tq):
    blk = lax.broadcasted_iota(jnp.int32, (1, 4 * tq), 1) // tq
    row = jnp.zeros((1, 4 * tq), F32)
    for g in range(4):
        row = jnp.where(blk == g, sink_ref[base + g], row)
    return row * LOG2E


def _tile(t, size):
    if isinstance(t, int):
        return pl.ds(t * size, size)
    return pl.ds(pl.multiple_of(t * size, size), size)


def _pipeline(make_rhs, score_chunks, value_chunks, finish, chunk_rows, out_shapes, scratch, n_units, zero,
              extra=None, lead=4, pace=2):
    main, head = scratch[:2], scratch[2:]
    n = len(chunk_rows)
    n_groups = len(out_shapes)
    last = n_units - 1
    assert n >= 2 * lead and n_units % 2 == 0
    ex = (lambda u: [None] * n_groups) if extra is None else extra

    def place(c, parity):
        if c < lead:
            return head[parity], sum(chunk_rows[:c])
        return main[parity], sum(chunk_rows[lead:c])

    def score(thunk, rhs, c, parity, cm):
        k, bias, g = thunk()
        s = jnp.dot(k, rhs[g], preferred_element_type=F32)
        if bias is not None:
            s = s + bias
        buf, row = place(c, parity)
        buf[row:row + chunk_rows[c], :] = s
        part = jnp.max(s, axis=0, keepdims=True)
        cm = list(cm)
        cm[g] = jnp.maximum(cm[g], part)
        return cm, part

    def with_extra(cm, u):
        return [m if e is None else jnp.maximum(m, e) for m, e in zip(cm, ex(u))]

    neg = [jnp.full((1, s[1]), -jnp.inf, F32) for s in out_shapes]

    rhs, sc, cm = make_rhs(0), score_chunks(0), neg
    for c in range(n):
        cm, _ = score(sc[c], rhs, c, 0, cm)
    m0 = with_extra(cm, 0)
    rhs, sc, cm1 = make_rhs(1), score_chunks(1), neg
    for c in range(lead):
        cm1, _ = score(sc[c], rhs, c, 1, cm1)

    def half(u, parity, m_u, cm1):
        u1 = jnp.minimum(u + 1, last)
        u2 = jnp.minimum(u + 2, last)
        rhs1, sc1 = make_rhs(u1), score_chunks(u1)
        rhs2, sc2 = make_rhs(u2), score_chunks(u2)
        vc, ex_u = value_chunks(u), ex(u)
        cm2 = neg
        acc = [None] * n_groups
        parts = []
        for c in range(n):
            if c + lead < n:
                cm1, part = score(sc1[c + lead], rhs1, c + lead, 1 - parity, cm1)
            else:
                cm2, part = score(sc2[c + lead - n], rhs2, c + lead - n, parity, cm2)
            parts.append(part)
            vt, g = vc[c]()
            m_c = m_u[g]
            if c >= pace:
                m_c = jnp.maximum(m_c, jnp.minimum(parts[c - pace], m_c))
            buf, row = place(c, parity)
            rows = pl.ds(pl.multiple_of(row + zero, LANES), chunk_rows[c])
            p = jnp.exp2(buf[rows, :] - m_c).astype(BF16)
            vt_aug = jnp.concatenate([vt, jnp.ones((ONES_ROWS, vt.shape[1]), BF16)], axis=0)
            pv = jnp.dot(vt_aug, p, preferred_element_type=F32)
            acc[g] = pv if acc[g] is None else acc[g] + pv
        ots = []
        for g in range(n_groups):
            dv = acc[g].shape[0] - ONES_ROWS
            l = acc[g][dv:dv + 1]
            if ex_u[g] is not None:
                l = l + jnp.exp2(ex_u[g] - m_u[g])
            ots.append(acc[g][:dv] * (1.0 / l))
        return with_extra(cm1, u1), cm2, ots

    def body(i, carry):
        m_u, cm_next, ot = carry
        u0 = 2 * i
        finish(jnp.maximum(u0 - 1, 0), ot)
        m_u, cm_next, ot = half(u0, 0, m_u, cm_next)
        finish(u0, ot)
        return half(u0 + 1, 1, m_u, cm_next)

    carry = (m0, cm1, [jnp.zeros(s, F32) for s in out_shapes])
    _, _, ot_last = lax.fori_loop(0, n_units // 2, body, carry)
    finish(last, ot_last)


def _score_scratch(chunk_rows, n_cols, lead=4):
    main = pltpu.VMEM((sum(chunk_rows[lead:]), n_cols), F32)
    head = pltpu.VMEM((sum(chunk_rows[:lead]), n_cols), F32)
    return [main, main, head, head]


def _split_unit(u, n_tiles):
    if isinstance(u, int):
        return u // n_tiles, u % n_tiles
    return lax.div(u, n_tiles), lax.rem(u, n_tiles)


def _rows(g, size):
    return _tile(g, size)


def _ac_chunk_rows(t_all, tq):
    wk = tq + 2 * WINDOW
    return ([KCHUNK] * (t_all // KCHUNK) + [KCHUNK]
            + [min(lo + KCHUNK, wk) - lo for lo in range(0, wk, KCHUNK)])


def _attn_ac_kernel(zero_ref, sink_ref, bias_ref, qa_ref, qc_ref, ka_ref, kc_ref, vta_ref, vtc_ref, oa_ref, oc_ref,
                    *scratch, tq, n_lat, n_ctx, s_len):
    t_all = ka_ref.shape[0]
    wk = tq + 2 * WINDOW
    spans = [(lo, min(lo + KCHUNK, wk)) for lo in range(0, wk, KCHUNK)]

    def rhs_of(q_ref, j, t):
        return _gqa_rhs(q_ref[_rows(j, 4 * HD), _tile(t, tq)], j, tq)

    def store(o_ref, j, t, ot):
        _store_heads(o_ref.at[_tile(t, tq), _rows(j, 4 * HD)], ot, tq, 4)

    def window(t):
        q0 = t * tq
        start = jnp.clip(q0 - WINDOW, 0, s_len - wk)
        return start, (q0 - start) // WINDOW

    def local(start, lo, hi):
        return pl.ds(pl.multiple_of(start + lo, LANES), hi - lo)

    def score_chunks(u):
        _, t = _split_unit(u, n_lat)
        start, variant = window(t)

        def loc(lo, hi):
            b = bias_ref[variant, lo:hi, :]
            return kc_ref[local(start, lo, hi), :], jnp.concatenate([b] * 4, axis=1), 1

        return ([functools.partial(lambda c: (ka_ref[c * KCHUNK:(c + 1) * KCHUNK, :], None, 0), c)
                 for c in range(t_all // KCHUNK)]
                + [lambda: (kc_ref[s_len:s_len + KCHUNK, :], None, 1)]
                + [functools.partial(loc, lo, hi) for lo, hi in spans])

    def value_chunks(u):
        j, t = _split_unit(u, n_lat)
        start, _ = window(t)
        return ([functools.partial(lambda c: (vta_ref[_rows(j, HD), c * KCHUNK:(c + 1) * KCHUNK], 0), c)
                 for c in range(t_all // KCHUNK)]
                + [lambda: (vtc_ref[_rows(j, HD), s_len:s_len + KCHUNK], 1)]
                + [functools.partial(lambda lo, hi: (vtc_ref[_rows(j, HD), local(start, lo, hi)], 1), lo, hi)
                   for lo, hi in spans])

    def make_rhs(u):
        j, t = _split_unit(u, n_lat)
        return [rhs_of(qa_ref, j, t), rhs_of(qc_ref, j, t)]

    def extra(u):
        j, _ = _split_unit(u, n_lat)
        return [None, _sink_row(sink_ref, j * 4, tq)]

    def finish(u, ots):
        j, t = _split_unit(u, n_lat)
        store(oa_ref, j, t, ots[0])
        store(oc_ref, j, t, ots[1])

    _pipeline(make_rhs, score_chunks, value_chunks, finish, _ac_chunk_rows(t_all, tq), [(HD, 4 * tq)] * 2, scratch,
              2 * n_lat, zero_ref[0], extra=extra, lead=2, pace=6)
    ctx_tiles = range(n_lat, n_lat + n_ctx)
    n = 4 * tq
    problems, targets = [], []
    for j in range(2 if n_ctx else 0):
        for q_ref, k_ref, vt_ref, o_ref, sink in ((qa_ref, ka_ref, vta_ref, oa_ref, None),
                                                  (qc_ref, kc_ref, vtc_ref, oc_ref, _sink_row(sink_ref, j * 4, tq))):
            rhs = jnp.concatenate([rhs_of(q_ref, j, t) for t in ctx_tiles], axis=1)
            extra = None if sink is None else jnp.concatenate([sink] * n_ctx, axis=1)
            problems.append((k_ref[s_len:, :], vt_ref[_rows(j, HD), s_len:], rhs, extra))
            targets.append((o_ref, j))
    for (o_ref, j), ot in zip(targets, _small_softmax_pv(problems)):
        for i, t in enumerate(ctx_tiles):
            store(o_ref, j, t, ot[:, i * n:(i + 1) * n])


def _attn_b_kernel(zero_ref, lamv_ref, subln_ref, qt_ref, k_ref, vt_ref, o_ref, *scratch, tq, n_lat, n_ctx, s_len,
                   lam_init):
    lv = lamv_ref[...]
    lam = (jnp.exp(jnp.sum(lv[0:1] * lv[1:2], axis=-1, keepdims=True))
           - jnp.exp(jnp.sum(lv[2:3] * lv[3:4], axis=-1, keepdims=True)) + lam_init)
    t_all = k_ref.shape[0]
    n_chunks = t_all // KCHUNK
    hw = 2 * HD

    def rhs_of(h, t):
        qt = qt_ref[_rows(h, hw), _tile(t, tq)]
        z = jnp.zeros((HD, tq), qt.dtype)
        return jnp.concatenate([jnp.concatenate([qt[:HD], z], axis=0),
                                jnp.concatenate([z, qt[HD:]], axis=0)], axis=1)

    def store(h, t, o2):
        o = o2[:, :tq] - lam * o2[:, tq:]
        ms = jnp.mean(o * o, axis=0, keepdims=True)
        o = o * lax.rsqrt(ms + SUBLN_EPS) * subln_ref[...] * (1.0 - lam_init)
        o_ref[_tile(t, tq), _rows(h, hw)] = o.T.astype(o_ref.dtype)

    def score_chunks(u):
        h, _ = _split_unit(u, n_lat)
        return [functools.partial(lambda c: (k_ref[c * KCHUNK:(c + 1) * KCHUNK, _rows(h, hw)], None, 0), c)
                for c in range(n_chunks)]

    def value_chunks(u):
        h, _ = _split_unit(u, n_lat)
        return [functools.partial(lambda c: (vt_ref[_rows(h, hw), c * KCHUNK:(c + 1) * KCHUNK], 0), c)
                for c in range(n_chunks)]

    _pipeline(lambda u: [rhs_of(*_split_unit(u, n_lat))], score_chunks, value_chunks,
              lambda u, ots: store(*_split_unit(u, n_lat), ots[0]), [KCHUNK] * n_chunks, [(hw, 2 * tq)], scratch,
              4 * n_lat, zero_ref[0])
    units = [(h, t) for h in range(4) for t in range(n_lat, n_lat + n_ctx)]
    problems = [(k_ref[s_len:, _rows(h, hw)], vt_ref[_rows(h, hw), s_len:], rhs_of(h, t), None) for h, t in units]
    for (h, t), o2 in zip(units, _small_softmax_pv(problems)):
        store(h, t, o2)


def _window_bias(tq):
    wk = tq + 2 * WINDOW
    r = np.arange(wk)[:, None]
    c = np.arange(tq)[None, :]
    out = np.stack([np.where(np.abs(c - r + v * WINDOW) <= WINDOW, 0.0, -np.inf) for v in range(3)])
    return jnp.asarray(out, F32)


def _attn_calls(qt_all, k_all, vt_all, sink, lamv, subln_t, layer, with_ctx, s_len, lam_init,
                tq_a, tq_b):
    nb, _, t_all = qt_all.shape
    c_len = t_all - s_len
    o_rows = t_all if with_ctx else s_len
    o_shape = jax.ShapeDtypeStruct((nb, o_rows, QW), BF16)

    def steps(tq):
        n_lat = s_len // tq
        return n_lat, n_lat + (c_len // tq if with_ctx else 0)

    whole = lambda rows, cols, r, c: pl.BlockSpec((None, rows, cols), lambda b: (b, r, c))
    o_spec = pl.BlockSpec((None, o_rows, QW), lambda b: (b, 0, 0))
    zero = jnp.zeros((1,), jnp.int32)

    n_lat, n_all = steps(tq_a)
    wk = tq_a + 2 * WINDOW
    oa, oc = pl.pallas_call(
        functools.partial(_attn_ac_kernel, tq=tq_a, n_lat=n_lat, n_ctx=n_all - n_lat, s_len=s_len),
        out_shape=(o_shape, o_shape),
        grid=(nb,),
        in_specs=[
            pl.BlockSpec(memory_space=pltpu.SMEM),
            pl.BlockSpec(memory_space=pltpu.SMEM),
            pl.BlockSpec((3, wk, tq_a), lambda b: (0, 0, 0)),
            whole(QW, t_all, 0, 0), whole(QW, t_all, 2, 0),
            whole(t_all, LANES, 0, KA_TILE), whole(t_all, LANES, 0, KA_TILE + 1),
            whole(LANES, t_all, KA_TILE, 0), whole(LANES, t_all, KA_TILE + 1, 0),
        ],
        out_specs=(o_spec, o_spec),
        scratch_shapes=_score_scratch(_ac_chunk_rows(t_all, tq_a), 4 * tq_a, lead=2),
        compiler_params=_cparams(1),
        name="attn_ac",
    )(zero, sink[layer], _window_bias(tq_a), qt_all, qt_all, k_all, k_all, vt_all, vt_all)

    n_lat, n_all = steps(tq_b)
    ob = pl.pallas_call(
        functools.partial(_attn_b_kernel, tq=tq_b, n_lat=n_lat, n_ctx=n_all - n_lat, s_len=s_len,
                          lam_init=lam_init),
        out_shape=o_shape,
        grid=(nb,),
        in_specs=[
            pl.BlockSpec(memory_space=pltpu.SMEM),
            pl.BlockSpec((None, 4, HD), lambda b: (layer, 0, 0)),
            pl.BlockSpec((None, 2 * HD, tq_b), lambda b: (layer, 0, 0)),
            whole(QW, t_all, 1, 0), whole(t_all, QW, 0, 0), whole(QW, t_all, 0, 0),
        ],
        out_specs=o_spec,
        scratch_shapes=_score_scratch([KCHUNK] * (t_all // KCHUNK), 2 * tq_b),
        compiler_params=_cparams(1),
        name="attn_b",
    )(zero, lamv, subln_t, qt_all, k_all, vt_all)

    return oa, ob, oc


def _merge_kernel(*refs, sub, n_sub, ctx_step):
    x_refs = refs[:n_sub]
    rest = refs[n_sub:]
    ctx_ref = None
    if ctx_step is not None:
        ctx_ref, rest = rest[0], rest[1:]
    (oa_ref, ob_ref, oc_ref, modb_ref, modc_ref, gpre_ref, gpost_ref, wgm_ref, bmg_ref, wbr_ref, wout_ref,
     out_ref) = rest

    def gates(i):
        x, is_ctx = _sub_tile_input(x_refs, ctx_ref, i, ctx_step)
        mod = jnp.where(is_ctx, modc_ref[...], modb_ref[...])
        h = _modulated_norm(x, mod, gpre_ref[...])
        gm = jnp.dot(h.astype(BF16), wgm_ref[...], preferred_element_type=F32)
        return x, mod, gm

    def finish(i, x, mod, gm):
        rows = slice(i * sub, (i + 1) * sub)
        z = None
        for j, o_ref in enumerate((oa_ref, ob_ref, oc_ref)):
            g = gm[:, j * QW:(j + 1) * QW]
            u = (o_ref[rows, :].astype(F32) * (g * _sigmoid(g))).astype(BF16)
            p = jnp.dot(u, wbr_ref[j], preferred_element_type=F32)
            mg = _sigmoid(gm[:, N_G + j * D:N_G + (j + 1) * D] + bmg_ref[:, j * D:(j + 1) * D])
            z = mg * p if z is None else z + mg * p
        y = jnp.dot(z.astype(BF16), wout_ref[...], preferred_element_type=F32)
        ms = jnp.mean(y * y, axis=-1, keepdims=True)
        gate = mod[:, 2 * D:]
        out_ref[rows, :] = x + gate * (y * lax.rsqrt(ms + EPS) * gpost_ref[...])

    prev = gates(0)
    for i in range(1, n_sub):
        cur = gates(i)
        finish(i - 1, *prev)
        prev = cur
    finish(n_sub - 1, *prev)


def _merge_call(x_src, ctx_src, oa, ob, oc, mod4, g_pre, g_post, wgm, b_mg, wbr, wout, layer, s_len, rows_out,
                tm, sub):
    nb = x_src.shape[0]
    n_sub = tm // sub
    tok_specs, tok_args = _token_specs(x_src, ctx_src, n_sub, sub, s_len)
    ctx_step = None if ctx_src is None else rows_out // tm - 1
    tok = lambda b, t: (b, t, 0)
    lay2 = lambda b, t: (layer, 0, 0)
    const = dict(pipeline_mode=pl.Buffered(1))
    return pl.pallas_call(
        functools.partial(_merge_kernel, sub=sub, n_sub=n_sub, ctx_step=ctx_step),
        out_shape=jax.ShapeDtypeStruct((nb, rows_out, D), F32),
        grid=(nb, rows_out // tm),
        in_specs=tok_specs + [
            pl.BlockSpec((None, tm, QW), tok),
            pl.BlockSpec((None, tm, QW), tok),
            pl.BlockSpec((None, tm, QW), tok),
            pl.BlockSpec((None, None, 1, 3 * D), lambda b, t: (layer, b, 0, 0)),
            pl.BlockSpec((None, None, 1, 3 * D), lambda b, t: (layer, nb, 0, 0)),
            pl.BlockSpec((None, 1, D), lay2),
            pl.BlockSpec((None, 1, D), lay2),
            pl.BlockSpec((None, D, N_G + N_M), lay2, **const),
            pl.BlockSpec((None, 1, N_M), lay2),
            pl.BlockSpec((None, 3, QW, D), lambda b, t: (layer, 0, 0, 0), **const),
            pl.BlockSpec((None, D, D), lay2, **const),
        ],
        out_specs=pl.BlockSpec((None, tm, D), tok),
        compiler_params=_cparams(2),
        name="merge",
    )(*tok_args, oa, ob, oc, mod4, mod4, g_pre, g_post, wgm, b_mg, wbr, wout)


def _rope_tables(s_len, c_len):
    rows = s_len // GRID_W
    row = np.repeat(np.arange(rows), GRID_W).astype(np.float32)
    col = np.tile(np.arange(GRID_W), rows).astype(np.float32)
    freqs = (np.float32(ROPE_THETA) ** (-np.arange(ROPE_PAIRS, dtype=np.float32) / ROPE_PAIRS)).astype(np.float32)
    ang_r = row[:, None] * freqs
    ang_c = col[:, None] * freqs
    ang = np.concatenate([ang_r, ang_r, ang_c, ang_c], axis=-1)
    cos = np.concatenate([np.cos(ang), np.ones((c_len, HD), np.float32)], axis=0).astype(np.float32)
    sin = np.concatenate([np.sin(ang), np.zeros((c_len, HD), np.float32)], axis=0).astype(np.float32)
    first = (np.arange(HD) % 32) < 16
    sin_a = np.where(first, -sin, np.float32(0.0))
    sin_b = np.where(first, np.float32(0.0), sin)
    tile2 = lambda a: np.concatenate([a, a], axis=-1)
    cost = np.ascontiguousarray(cos.T)
    sint = np.ascontiguousarray((sin_a + sin_b).T)
    return tuple(jnp.asarray(a, F32) for a in (cost, sint, tile2(cos), tile2(sin_a), tile2(sin_b)))


def _cols(w, names):
    return jnp.concatenate([w[..., _IN[n][0]:_IN[n][1]] for n in names], axis=-1)


def kernel(x, c, ctx, c_ctx, w_ada, b_ada, g_pre, g_post, w_in, q_norm, k_norm, lam_q1, lam_k1, lam_q2,
           lam_k2, subln, sink, w_br_a, w_br_b, w_br_c, w_mg, b_mg, w_out):
    nb, s_len, _ = x.shape
    c_len = ctx.shape[1]
    depth = w_in.shape[0]
    sub = 256
    tm_all = 3 * sub
    tm_lat = 4 * sub
    tq_a, tq_b = 128, 256
    assert (s_len + c_len) % tm_all == 0 and s_len % tm_lat == 0 and s_len % sub == 0
    assert c_len == KCHUNK == sub

    wp = _cols(w_in, ("qa", "qb", "qc", "kb", "ka", "kc", "vb", "va", "vc")).astype(BF16)
    wgm = jnp.concatenate([_cols(w_in, ("ga", "gb", "gc")), w_mg], axis=-1).astype(BF16)
    wbr = jnp.stack([w_br_a, w_br_b, w_br_c], axis=1).astype(BF16)
    wout = w_out.astype(BF16)

    tabs = _rope_tables(s_len, c_len)
    gq_t = jnp.broadcast_to(q_norm[:, :, None], (depth, HD, sub))
    gk_n = jnp.concatenate([k_norm, k_norm], axis=-1)[:, None, :]
    lamv = jnp.stack([lam_q1, lam_k1, lam_q2, lam_k2], axis=1)
    subln_t = jnp.broadcast_to(subln[:, :, None], (depth, 2 * HD, tq_b))
    g_pre3 = g_pre[:, None, :]
    g_post3 = g_post[:, None, :]
    b_mg3 = b_mg[:, None, :]

    rows = ((nb + 1 + 7) // 8) * 8
    sc_in = jnp.concatenate([c, c_ctx[None, :], jnp.zeros((rows - nb - 1, D), F32)], axis=0)
    mod = _ada_call(sc_in, w_ada, b_ada)
    mod4 = mod[:, :, None, :]

    t_all = s_len + c_len
    x_src, ctx_src = x, (ctx, 0)
    for layer in range(depth):
        last = layer == depth - 1
        lam_init = 0.8 - 0.6 * math.exp(-0.3 * layer)
        qt_all, k_all, vt_all = _proj_call(x_src, ctx_src, mod4, g_pre3, wp, tabs, gq_t, gk_n, layer, s_len,
                                           t_all, tm_all, sub)
        oa, ob, oc = _attn_calls(qt_all, k_all, vt_all, sink, lamv, subln_t, layer, not last, s_len,
                                 lam_init, tq_a, tq_b)
        rows_out, tm = (s_len, tm_lat) if last else (t_all, tm_all)
        xs = _merge_call(x_src, None if last else ctx_src, oa, ob, oc, mod4, g_pre3, g_post3, wgm, b_mg3, wbr,
                         wout, layer, s_len, rows_out, tm, sub)
        x_src, ctx_src = xs, (xs, s_len // sub)
    return xs
```

```python
import functools
import math

import jax
import jax.numpy as jnp
import numpy as np
from jax import lax
from jax.experimental import pallas as pl
from jax.experimental.pallas import tpu as pltpu

F32 = jnp.float32
BF16 = jnp.bfloat16

D = 1024
HD = 64
GRID_W = 64
WINDOW = 128
ROPE_THETA = 10000.0
ROPE_PAIRS = HD // 4
EPS = 1e-6
SUBLN_EPS = 1e-5
ATTN_SCALE = HD ** -0.5
LOG2E = math.log2(math.e)

QW = 512
N_Q = 3 * QW
N_K = 512 + 128 + 128
N_V = 512 + 128 + 128
KA_TILE = 4
N_P = N_Q + N_K + N_V
N_G = 3 * QW
N_M = 3 * D

LANES = 128
VMEM_LIMIT = 56 * 1024 * 1024

_IN = dict(qa=(0, 512), ka=(512, 640), va=(640, 768), ga=(768, 1280),
           qb=(1280, 1792), kb=(1792, 2304), vb=(2304, 2816), gb=(2816, 3328),
           qc=(3328, 3840), kc=(3840, 3968), vc=(3968, 4096), gc=(4096, 4608))


def _sigmoid(v):
    return 1.0 / (1.0 + jnp.exp(-v))


def _cparams(n_axes):
    return pltpu.CompilerParams(dimension_semantics=("arbitrary",) * n_axes,
                                vmem_limit_bytes=VMEM_LIMIT)


def _ada_kernel(sc_ref, w_ref, b_ref, o_ref):
    v = sc_ref[...]
    s = (v * _sigmoid(v)).astype(BF16)
    o_ref[...] = jnp.dot(s, w_ref[...].astype(BF16), preferred_element_type=F32) + b_ref[...]


def _ada_call(sc_in, w_ada, b_ada):
    depth = w_ada.shape[0]
    rows = sc_in.shape[0]
    nblk = 3
    return pl.pallas_call(
        _ada_kernel,
        out_shape=jax.ShapeDtypeStruct((depth, rows, 3 * D), F32),
        grid=(depth, nblk),
        in_specs=[
            pl.BlockSpec((rows, D), lambda l, n: (0, 0)),
            pl.BlockSpec((None, D, D), lambda l, n: (l, 0, n)),
            pl.BlockSpec((None, 1, D), lambda l, n: (l, 0, n)),
        ],
        out_specs=pl.BlockSpec((None, rows, D), lambda l, n: (l, 0, n)),
        compiler_params=_cparams(2),
        name="adaln",
    )(sc_in, w_ada, b_ada.reshape(depth, 1, 3 * D))


def _modulated_norm(x, mod, gpre):
    shift = mod[:, :D]
    scale = mod[:, D:2 * D]
    ms = jnp.mean(x * x, axis=-1, keepdims=True)
    return x * lax.rsqrt(ms + EPS) * gpre * (1.0 + scale) + shift


def _sub_tile_input(x_refs, ctx_ref, i, ctx_step):
    x = x_refs[i][...]
    if ctx_ref is None or i != len(x_refs) - 1:
        return x, False
    is_ctx = pl.program_id(1) == ctx_step
    return jnp.where(is_ctx, ctx_ref[...], x), is_ctx


def _proj_kernel(*refs, sub, n_sub, ctx_step):
    x_refs, ctx_ref = refs[:n_sub], refs[n_sub]
    (modb_ref, modc_ref, gpre_ref, w_ref, cost_ref, sint_ref, cosn_ref, sina_ref, sinb_ref, gq_ref, gk_ref,
     qt_ref, k_ref, vt_ref) = refs[n_sub + 1:]

    def project(i):
        x, is_ctx = _sub_tile_input(x_refs, ctx_ref, i, ctx_step)
        mod = jnp.where(is_ctx, modc_ref[...], modb_ref[...])
        h = _modulated_norm(x, mod, gpre_ref[...])
        return jnp.dot(h.astype(BF16), w_ref[...], preferred_element_type=F32)

    def finish(i, y):
        rows = slice(i * sub, (i + 1) * sub)
        q3 = y[:, :N_Q].T.reshape(N_Q // HD, HD, sub)
        qa = q3[:8]
        ss = jnp.sum(qa * qa, axis=1, keepdims=True)
        qa = qa * lax.rsqrt(ss * (1.0 / HD) + EPS) * gq_ref[...][None]
        q3 = jnp.concatenate([qa, q3[8:]], axis=0)
        rot = jnp.concatenate([q3[:, 16:32], q3[:, 0:16], q3[:, 48:64], q3[:, 32:48]], axis=1)
        q3 = (q3 * cost_ref[:, rows][None] + rot * sint_ref[:, rows][None]) * (ATTN_SCALE * LOG2E)
        qt_ref[:, rows] = q3.reshape(N_Q, sub).astype(BF16)

        ka = y[:, N_Q + KA_TILE * LANES:N_Q + (KA_TILE + 1) * LANES]
        lane = lax.broadcasted_iota(jnp.int32, (1, LANES), 1)
        lo = lane < HD
        sq = ka * ka
        s_lo = jnp.sum(jnp.where(lo, sq, 0.0), axis=-1, keepdims=True)
        s_hi = jnp.sum(jnp.where(lo, 0.0, sq), axis=-1, keepdims=True)
        r = jnp.where(lo, lax.rsqrt(s_lo * (1.0 / HD) + EPS), lax.rsqrt(s_hi * (1.0 / HD) + EPS))
        ka = ka * r * gk_ref[...]
        cosn = cosn_ref[rows, :]
        sina = sina_ref[rows, :]
        sinb = sinb_ref[rows, :]
        for j in range(N_K // LANES):
            t = ka if j == KA_TILE else y[:, N_Q + j * LANES:N_Q + (j + 1) * LANES]
            t = t * cosn + pltpu.roll(t, LANES - 16, 1) * sina + pltpu.roll(t, 16, 1) * sinb
            k_ref[rows, j * LANES:(j + 1) * LANES] = t.astype(BF16)

        vt_ref[:, rows] = y[:, N_Q + N_K:].T.astype(BF16)

    y_prev = project(0)
    for i in range(1, n_sub):
        y = project(i)
        finish(i - 1, y_prev)
        y_prev = y
    finish(n_sub - 1, y_prev)


def _token_specs(x_src, ctx_src, n_sub, sub, s_len):
    last = s_len // sub - 1
    specs = [pl.BlockSpec((None, sub, D), functools.partial(lambda i, b, t: (b, jnp.minimum(n_sub * t + i, last), 0), i))
             for i in range(n_sub)]
    operands = [x_src] * n_sub
    if ctx_src is not None:
        arr, blk = ctx_src
        specs.append(pl.BlockSpec((None, sub, D), lambda b, t: (b, blk, 0)))
        operands.append(arr)
    return specs, operands


def _proj_call(x_src, ctx_src, mod4, g_pre, wp, tabs, gq_t, gk_n, layer, s_len, t_all, tm, sub):
    nb = x_src.shape[0]
    cost, sint, cosn, sina, sinb = tabs
    n_sub = tm // sub
    tok_specs, tok_args = _token_specs(x_src, ctx_src, n_sub, sub, s_len)
    return pl.pallas_call(
        functools.partial(_proj_kernel, sub=sub, n_sub=n_sub, ctx_step=t_all // tm - 1),
        out_shape=(jax.ShapeDtypeStruct((nb, N_Q, t_all), BF16),
                   jax.ShapeDtypeStruct((nb, t_all, N_K), BF16),
                   jax.ShapeDtypeStruct((nb, N_V, t_all), BF16)),
        grid=(nb, t_all // tm),
        in_specs=tok_specs + [
            pl.BlockSpec((None, None, 1, 3 * D), lambda b, t: (layer, b, 0, 0)),
            pl.BlockSpec((None, None, 1, 3 * D), lambda b, t: (layer, nb, 0, 0)),
            pl.BlockSpec((None, 1, D), lambda b, t: (layer, 0, 0)),
            pl.BlockSpec((None, D, N_P), lambda b, t: (layer, 0, 0)),
            pl.BlockSpec((HD, tm), lambda b, t: (0, t)),
            pl.BlockSpec((HD, tm), lambda b, t: (0, t)),
            pl.BlockSpec((tm, LANES), lambda b, t: (t, 0)),
            pl.BlockSpec((tm, LANES), lambda b, t: (t, 0)),
            pl.BlockSpec((tm, LANES), lambda b, t: (t, 0)),
            pl.BlockSpec((None, HD, sub), lambda b, t: (layer, 0, 0)),
            pl.BlockSpec((None, 1, LANES), lambda b, t: (layer, 0, 0)),
        ],
        out_specs=(pl.BlockSpec((None, N_Q, tm), lambda b, t: (b, 0, t)),
                   pl.BlockSpec((None, tm, N_K), lambda b, t: (b, t, 0)),
                   pl.BlockSpec((None, N_V, tm), lambda b, t: (b, 0, t))),
        compiler_params=_cparams(2),
        name="proj",
    )(*tok_args, mod4, mod4, g_pre, wp, cost, sint, cosn, sina, sinb, gq_t, gk_n)


KCHUNK = 256
ONES_ROWS = 16
PIPE_AC = dict(lead=3, pace=6)
PIPE_B = dict(lead=4, pace=2)


def _small_softmax_pv(problems):
    ss = [jnp.dot(k, rhs, preferred_element_type=F32) for k, _, rhs, _ in problems]
    ms = [jnp.max(s, axis=0, keepdims=True) for s in ss]
    ms = [m if pr[3] is None else jnp.maximum(m, pr[3]) for m, pr in zip(ms, problems)]
    ps = [jnp.exp2(s - m).astype(BF16) for s, m in zip(ss, ms)]
    outs = []
    for p, m, (_, vt, _, extra) in zip(ps, ms, problems):
        dv = vt.shape[0]
        vt_aug = jnp.concatenate([vt, jnp.ones((ONES_ROWS, vt.shape[1]), BF16)], axis=0)
        pv = jnp.dot(vt_aug, p, preferred_element_type=F32)
        l = pv[dv:dv + 1]
        if extra is not None:
            l = l + jnp.exp2(extra - m)
        outs.append(pv[:dv] * (1.0 / l))
    return outs


def _gqa_rhs(qt, j, tq):
    z = jnp.zeros((HD, tq), qt.dtype)
    first = j == 0
    cols = []
    for g in range(4):
        qg = qt[g * HD:(g + 1) * HD, :]
        cols.append(jnp.concatenate([jnp.where(first, qg, z), jnp.where(first, z, qg)], axis=0))
    return jnp.concatenate(cols, axis=1)


def _store_heads(o_ref, ot, tq, nheads):
    o = jnp.concatenate([ot[:, g * tq:(g + 1) * tq] for g in range(nheads)], axis=0)
    o_ref[...] = o.T.astype(o_ref.dtype)


def _sink_row(sink_ref, base, tq):
    blk = lax.broadcasted_iota(jnp.int32, (1, 4 * tq), 1) // tq
    row = jnp.zeros((1, 4 * tq), F32)
    for g in range(4):
        row = jnp.where(blk == g, sink_ref[base + g], row)
    return row * LOG2E


def _tile(t, size):
    if isinstance(t, int):
        return pl.ds(t * size, size)
    return pl.ds(pl.multiple_of(t * size, size), size)


def _pipeline(make_rhs, score_chunks, value_chunks, finish, chunk_rows, out_shapes, scratch, n_units, zero,
              extra=None, *, lead, pace):
    main, head = scratch[:2], scratch[2:]
    n = len(chunk_rows)
    n_groups = len(out_shapes)
    last = n_units - 1
    assert n >= 2 * lead and n_units % 2 == 0
    ex = (lambda u: [None] * n_groups) if extra is None else extra

    def place(c, parity):
        if c < lead:
            return head[parity], sum(chunk_rows[:c])
        return main[parity], sum(chunk_rows[lead:c])

    def score(thunk, rhs, c, parity, cm):
        k, bias, g = thunk()
        s = jnp.dot(k, rhs[g], preferred_element_type=F32)
        if bias is not None:
            s = s + bias
        buf, row = place(c, parity)
        buf[row:row + chunk_rows[c], :] = s
        part = jnp.max(s, axis=0, keepdims=True)
        cm = list(cm)
        cm[g] = jnp.maximum(cm[g], part)
        return cm, part

    def with_extra(cm, u):
        return [m if e is None else jnp.maximum(m, e) for m, e in zip(cm, ex(u))]

    neg = [jnp.full((1, s[1]), -jnp.inf, F32) for s in out_shapes]

    rhs, sc, cm = make_rhs(0), score_chunks(0), neg
    for c in range(n):
        cm, _ = score(sc[c], rhs, c, 0, cm)
    m0 = with_extra(cm, 0)
    rhs, sc, cm1 = make_rhs(1), score_chunks(1), neg
    for c in range(lead):
        cm1, _ = score(sc[c], rhs, c, 1, cm1)

    def half(u, parity, m_u, cm1):
        u1 = jnp.minimum(u + 1, last)
        u2 = jnp.minimum(u + 2, last)
        rhs1, sc1 = make_rhs(u1), score_chunks(u1)
        rhs2, sc2 = make_rhs(u2), score_chunks(u2)
        vc, ex_u = value_chunks(u), ex(u)
        cm2 = neg
        acc = [None] * n_groups
        parts = []
        for c in range(n):
            if c + lead < n:
                cm1, part = score(sc1[c + lead], rhs1, c + lead, 1 - parity, cm1)
            else:
                cm2, part = score(sc2[c + lead - n], rhs2, c + lead - n, parity, cm2)
            parts.append(part)
            vt, g = vc[c]()
            m_c = m_u[g]
            if c >= pace:
                m_c = jnp.maximum(m_c, jnp.minimum(parts[c - pace], m_c))
            buf, row = place(c, parity)
            rows = pl.ds(pl.multiple_of(row + zero, LANES), chunk_rows[c])
            p = jnp.exp2(buf[rows, :] - m_c).astype(BF16)
            vt_aug = jnp.concatenate([vt, jnp.ones((ONES_ROWS, vt.shape[1]), BF16)], axis=0)
            pv = jnp.dot(vt_aug, p, preferred_element_type=F32)
            acc[g] = pv if acc[g] is None else acc[g] + pv
        ots = []
        for g in range(n_groups):
            dv = acc[g].shape[0] - ONES_ROWS
            l = acc[g][dv:dv + 1]
            if ex_u[g] is not None:
                l = l + jnp.exp2(ex_u[g] - m_u[g])
            ots.append(acc[g][:dv] * (1.0 / l))
        return with_extra(cm1, u1), cm2, ots

    def body(i, carry):
        m_u, cm_next, ot = carry
        u0 = 2 * i
        finish(jnp.maximum(u0 - 1, 0), ot)
        m_u, cm_next, ot = half(u0, 0, m_u, cm_next)
        finish(u0, ot)
        return half(u0 + 1, 1, m_u, cm_next)

    carry = (m0, cm1, [jnp.zeros(s, F32) for s in out_shapes])
    _, _, ot_last = lax.fori_loop(0, n_units // 2, body, carry)
    finish(last, ot_last)


def _score_scratch(chunk_rows, n_cols, lead):
    main = pltpu.VMEM((sum(chunk_rows[lead:]), n_cols), F32)
    head = pltpu.VMEM((sum(chunk_rows[:lead]), n_cols), F32)
    return [main, main, head, head]


def _split_unit(u, n_tiles):
    if isinstance(u, int):
        return u // n_tiles, u % n_tiles
    return lax.div(u, n_tiles), lax.rem(u, n_tiles)


def _rows(g, size):
    return _tile(g, size)


def _ac_chunk_rows(t_all, tq):
    wk = tq + 2 * WINDOW
    return ([KCHUNK] * (t_all // KCHUNK) + [KCHUNK]
            + [min(lo + KCHUNK, wk) - lo for lo in range(0, wk, KCHUNK)])


def _attn_ac_kernel(zero_ref, sink_ref, bias_ref, qa_ref, qc_ref, ka_ref, kc_ref, vta_ref, vtc_ref, oa_ref, oc_ref,
                    *scratch, tq, n_lat, n_ctx, s_len):
    t_all = ka_ref.shape[0]
    wk = tq + 2 * WINDOW
    spans = [(lo, min(lo + KCHUNK, wk)) for lo in range(0, wk, KCHUNK)]

    def rhs_of(q_ref, j, t):
        return _gqa_rhs(q_ref[_rows(j, 4 * HD), _tile(t, tq)], j, tq)

    def store(o_ref, j, t, ot):
        _store_heads(o_ref.at[_tile(t, tq), _rows(j, 4 * HD)], ot, tq, 4)

    def window(t):
        q0 = t * tq
        start = jnp.clip(q0 - WINDOW, 0, s_len - wk)
        return start, (q0 - start) // WINDOW

    def local(start, lo, hi):
        return pl.ds(pl.multiple_of(start + lo, LANES), hi - lo)

    def score_chunks(u):
        _, t = _split_unit(u, n_lat)
        start, variant = window(t)

        def loc(lo, hi):
            b = bias_ref[variant, lo:hi, :]
            return kc_ref[local(start, lo, hi), :], jnp.concatenate([b] * 4, axis=1), 1

        return ([functools.partial(lambda c: (ka_ref[c * KCHUNK:(c + 1) * KCHUNK, :], None, 0), c)
                 for c in range(t_all // KCHUNK)]
                + [lambda: (kc_ref[s_len:s_len + KCHUNK, :], None, 1)]
                + [functools.partial(loc, lo, hi) for lo, hi in spans])

    def value_chunks(u):
        j, t = _split_unit(u, n_lat)
        start, _ = window(t)
        return ([functools.partial(lambda c: (vta_ref[_rows(j, HD), c * KCHUNK:(c + 1) * KCHUNK], 0), c)
                 for c in range(t_all // KCHUNK)]
                + [lambda: (vtc_ref[_rows(j, HD), s_len:s_len + KCHUNK], 1)]
                + [functools.partial(lambda lo, hi: (vtc_ref[_rows(j, HD), local(start, lo, hi)], 1), lo, hi)
                   for lo, hi in spans])

    def make_rhs(u):
        j, t = _split_unit(u, n_lat)
        return [rhs_of(qa_ref, j, t), rhs_of(qc_ref, j, t)]

    def extra(u):
        j, _ = _split_unit(u, n_lat)
        return [None, _sink_row(sink_ref, j * 4, tq)]

    def finish(u, ots):
        j, t = _split_unit(u, n_lat)
        store(oa_ref, j, t, ots[0])
        store(oc_ref, j, t, ots[1])

    ctx_tiles = range(n_lat, n_lat + n_ctx)
    n = 4 * tq
    problems, targets = [], []
    for j in range(2 if n_ctx else 0):
        for q_ref, k_ref, vt_ref, o_ref, sink in ((qa_ref, ka_ref, vta_ref, oa_ref, None),
                                                  (qc_ref, kc_ref, vtc_ref, oc_ref, _sink_row(sink_ref, j * 4, tq))):
            rhs = jnp.concatenate([rhs_of(q_ref, j, t) for t in ctx_tiles], axis=1)
            sink_cols = None if sink is None else jnp.concatenate([sink] * n_ctx, axis=1)
            problems.append((k_ref[s_len:, :], vt_ref[_rows(j, HD), s_len:], rhs, sink_cols))
            targets.append((o_ref, j))
    for (o_ref, j), ot in zip(targets, _small_softmax_pv(problems)):
        for i, t in enumerate(ctx_tiles):
            store(o_ref, j, t, ot[:, i * n:(i + 1) * n])

    _pipeline(make_rhs, score_chunks, value_chunks, finish, _ac_chunk_rows(t_all, tq), [(HD, 4 * tq)] * 2, scratch,
              2 * n_lat, zero_ref[0], extra=extra, **PIPE_AC)


def _attn_b_kernel(zero_ref, lamv_ref, subln_ref, qt_ref, k_ref, vt_ref, o_ref, *scratch, tq, n_lat, n_ctx, s_len,
                   lam_init):
    lv = lamv_ref[...]
    lam = (jnp.exp(jnp.sum(lv[0:1] * lv[1:2], axis=-1, keepdims=True))
           - jnp.exp(jnp.sum(lv[2:3] * lv[3:4], axis=-1, keepdims=True)) + lam_init)
    t_all = k_ref.shape[0]
    n_chunks = t_all // KCHUNK
    hw = 2 * HD

    def rhs_of(h, t):
        qt = qt_ref[_rows(h, hw), _tile(t, tq)]
        z = jnp.zeros((HD, tq), qt.dtype)
        return jnp.concatenate([jnp.concatenate([qt[:HD], z], axis=0),
                                jnp.concatenate([z, qt[HD:]], axis=0)], axis=1)

    def store(h, t, o2):
        o = o2[:, :tq] - lam * o2[:, tq:]
        ms = jnp.mean(o * o, axis=0, keepdims=True)
        o = o * lax.rsqrt(ms + SUBLN_EPS) * subln_ref[...] * (1.0 - lam_init)
        o_ref[_tile(t, tq), _rows(h, hw)] = o.T.astype(o_ref.dtype)

    def score_chunks(u):
        h, _ = _split_unit(u, n_lat)
        return [functools.partial(lambda c: (k_ref[c * KCHUNK:(c + 1) * KCHUNK, _rows(h, hw)], None, 0), c)
                for c in range(n_chunks)]

    def value_chunks(u):
        h, _ = _split_unit(u, n_lat)
        return [functools.partial(lambda c: (vt_ref[_rows(h, hw), c * KCHUNK:(c + 1) * KCHUNK], 0), c)
                for c in range(n_chunks)]

    units = [(h, t) for h in range(4) for t in range(n_lat, n_lat + n_ctx)]
    problems = [(k_ref[s_len:, _rows(h, hw)], vt_ref[_rows(h, hw), s_len:], rhs_of(h, t), None) for h, t in units]
    for (h, t), o2 in zip(units, _small_softmax_pv(problems)):
        store(h, t, o2)

    _pipeline(lambda u: [rhs_of(*_split_unit(u, n_lat))], score_chunks, value_chunks,
              lambda u, ots: store(*_split_unit(u, n_lat), ots[0]), [KCHUNK] * n_chunks, [(hw, 2 * tq)], scratch,
              4 * n_lat, zero_ref[0], **PIPE_B)


def _window_bias(tq):
    wk = tq + 2 * WINDOW
    r = np.arange(wk)[:, None]
    c = np.arange(tq)[None, :]
    out = np.stack([np.where(np.abs(c - r + v * WINDOW) <= WINDOW, 0.0, -np.inf) for v in range(3)])
    return jnp.asarray(out, F32)


def _attn_calls(qt_all, k_all, vt_all, sink, lamv, subln_t, layer, with_ctx, s_len, lam_init,
                tq_a, tq_b):
    nb, _, t_all = qt_all.shape
    c_len = t_all - s_len
    o_rows = t_all if with_ctx else s_len
    o_shape = jax.ShapeDtypeStruct((nb, o_rows, QW), BF16)

    def steps(tq):
        n_lat = s_len // tq
        return n_lat, n_lat + (c_len // tq if with_ctx else 0)

    whole = lambda rows, cols, r, c: pl.BlockSpec((None, rows, cols), lambda b: (b, r, c))
    o_spec = pl.BlockSpec((None, o_rows, QW), lambda b: (b, 0, 0))
    zero = jnp.zeros((1,), jnp.int32)

    n_lat, n_all = steps(tq_a)
    wk = tq_a + 2 * WINDOW
    oa, oc = pl.pallas_call(
        functools.partial(_attn_ac_kernel, tq=tq_a, n_lat=n_lat, n_ctx=n_all - n_lat, s_len=s_len),
        out_shape=(o_shape, o_shape),
        grid=(nb,),
        in_specs=[
            pl.BlockSpec(memory_space=pltpu.SMEM),
            pl.BlockSpec(memory_space=pltpu.SMEM),
            pl.BlockSpec((3, wk, tq_a), lambda b: (0, 0, 0)),
            whole(QW, t_all, 0, 0), whole(QW, t_all, 2, 0),
            whole(t_all, LANES, 0, KA_TILE), whole(t_all, LANES, 0, KA_TILE + 1),
            whole(LANES, t_all, KA_TILE, 0), whole(LANES, t_all, KA_TILE + 1, 0),
        ],
        out_specs=(o_spec, o_spec),
        scratch_shapes=_score_scratch(_ac_chunk_rows(t_all, tq_a), 4 * tq_a, PIPE_AC["lead"]),
        compiler_params=_cparams(1),
        name="attn_ac",
    )(zero, sink[layer], _window_bias(tq_a), qt_all, qt_all, k_all, k_all, vt_all, vt_all)

    n_lat, n_all = steps(tq_b)
    ob = pl.pallas_call(
        functools.partial(_attn_b_kernel, tq=tq_b, n_lat=n_lat, n_ctx=n_all - n_lat, s_len=s_len,
                          lam_init=lam_init),
        out_shape=o_shape,
        grid=(nb,),
        in_specs=[
            pl.BlockSpec(memory_space=pltpu.SMEM),
            pl.BlockSpec((None, 4, HD), lambda b: (layer, 0, 0)),
            pl.BlockSpec((None, 2 * HD, tq_b), lambda b: (layer, 0, 0)),
            whole(QW, t_all, 1, 0), whole(t_all, QW, 0, 0), whole(QW, t_all, 0, 0),
        ],
        out_specs=o_spec,
        scratch_shapes=_score_scratch([KCHUNK] * (t_all // KCHUNK), 2 * tq_b, PIPE_B["lead"]),
        compiler_params=_cparams(1),
        name="attn_b",
    )(zero, lamv, subln_t, qt_all, k_all, vt_all)

    return oa, ob, oc


def _merge_kernel(*refs, sub, n_sub, ctx_step):
    x_refs = refs[:n_sub]
    rest = refs[n_sub:]
    ctx_ref = None
    if ctx_step is not None:
        ctx_ref, rest = rest[0], rest[1:]
    (oa_ref, ob_ref, oc_ref, modb_ref, modc_ref, gpre_ref, gpost_ref, wgm_ref, bmg_ref, wbr_ref, wout_ref,
     out_ref) = rest

    def gates(i):
        x, is_ctx = _sub_tile_input(x_refs, ctx_ref, i, ctx_step)
        mod = jnp.where(is_ctx, modc_ref[...], modb_ref[...])
        h = _modulated_norm(x, mod, gpre_ref[...])
        gm = jnp.dot(h.astype(BF16), wgm_ref[...], preferred_element_type=F32)
        return x, mod, gm

    def finish(i, x, mod, gm):
        rows = slice(i * sub, (i + 1) * sub)
        z = None
        for j, o_ref in enumerate((oa_ref, ob_ref, oc_ref)):
            g = gm[:, j * QW:(j + 1) * QW]
            u = (o_ref[rows, :].astype(F32) * (g * _sigmoid(g))).astype(BF16)
            p = jnp.dot(u, wbr_ref[j], preferred_element_type=F32)
            mg = _sigmoid(gm[:, N_G + j * D:N_G + (j + 1) * D] + bmg_ref[:, j * D:(j + 1) * D])
            z = mg * p if z is None else z + mg * p
        y = jnp.dot(z.astype(BF16), wout_ref[...], preferred_element_type=F32)
        ms = jnp.mean(y * y, axis=-1, keepdims=True)
        gate = mod[:, 2 * D:]
        out_ref[rows, :] = x + gate * (y * lax.rsqrt(ms + EPS) * gpost_ref[...])

    prev = gates(0)
    for i in range(1, n_sub):
        cur = gates(i)
        finish(i - 1, *prev)
        prev = cur
    finish(n_sub - 1, *prev)


def _merge_call(x_src, ctx_src, oa, ob, oc, mod4, g_pre, g_post, wgm, b_mg, wbr, wout, layer, s_len, rows_out,
                tm, sub):
    nb = x_src.shape[0]
    n_sub = tm // sub
    tok_specs, tok_args = _token_specs(x_src, ctx_src, n_sub, sub, s_len)
    ctx_step = None if ctx_src is None else rows_out // tm - 1
    tok = lambda b, t: (b, t, 0)
    lay2 = lambda b, t: (layer, 0, 0)
    const = dict(pipeline_mode=pl.Buffered(1))
    return pl.pallas_call(
        functools.partial(_merge_kernel, sub=sub, n_sub=n_sub, ctx_step=ctx_step),
        out_shape=jax.ShapeDtypeStruct((nb, rows_out, D), F32),
        grid=(nb, rows_out // tm),
        in_specs=tok_specs + [
            pl.BlockSpec((None, tm, QW), tok),
            pl.BlockSpec((None, tm, QW), tok),
            pl.BlockSpec((None, tm, QW), tok),
            pl.BlockSpec((None, None, 1, 3 * D), lambda b, t: (layer, b, 0, 0)),
            pl.BlockSpec((None, None, 1, 3 * D), lambda b, t: (layer, nb, 0, 0)),
            pl.BlockSpec((None, 1, D), lay2),
            pl.BlockSpec((None, 1, D), lay2),
            pl.BlockSpec((None, D, N_G + N_M), lay2, **const),
            pl.BlockSpec((None, 1, N_M), lay2),
            pl.BlockSpec((None, 3, QW, D), lambda b, t: (layer, 0, 0, 0), **const),
            pl.BlockSpec((None, D, D), lay2, **const),
        ],
        out_specs=pl.BlockSpec((None, tm, D), tok),
        compiler_params=_cparams(2),
        name="merge",
    )(*tok_args, oa, ob, oc, mod4, mod4, g_pre, g_post, wgm, b_mg, wbr, wout)


def _rope_tables(s_len, c_len):
    rows = s_len // GRID_W
    row = np.repeat(np.arange(rows), GRID_W).astype(np.float32)
    col = np.tile(np.arange(GRID_W), rows).astype(np.float32)
    freqs = (np.float32(ROPE_THETA) ** (-np.arange(ROPE_PAIRS, dtype=np.float32) / ROPE_PAIRS)).astype(np.float32)
    ang_r = row[:, None] * freqs
    ang_c = col[:, None] * freqs
    ang = np.concatenate([ang_r, ang_r, ang_c, ang_c], axis=-1)
    cos = np.concatenate([np.cos(ang), np.ones((c_len, HD), np.float32)], axis=0).astype(np.float32)
    sin = np.concatenate([np.sin(ang), np.zeros((c_len, HD), np.float32)], axis=0).astype(np.float32)
    first = (np.arange(HD) % 32) < 16
    sin_a = np.where(first, -sin, np.float32(0.0))
    sin_b = np.where(first, np.float32(0.0), sin)
    tile2 = lambda a: np.concatenate([a, a], axis=-1)
    cost = np.ascontiguousarray(cos.T)
    sint = np.ascontiguousarray((sin_a + sin_b).T)
    return tuple(jnp.asarray(a, F32) for a in (cost, sint, tile2(cos), tile2(sin_a), tile2(sin_b)))


def _cols(w, names):
    return jnp.concatenate([w[..., _IN[n][0]:_IN[n][1]] for n in names], axis=-1)


def kernel(x, c, ctx, c_ctx, w_ada, b_ada, g_pre, g_post, w_in, q_norm, k_norm, lam_q1, lam_k1, lam_q2,
           lam_k2, subln, sink, w_br_a, w_br_b, w_br_c, w_mg, b_mg, w_out):
    nb, s_len, _ = x.shape
    c_len = ctx.shape[1]
    depth = w_in.shape[0]
    sub = 256
    tm_all = 3 * sub
    tm_lat = 4 * sub
    tq_a, tq_b = 128, 256
    assert (s_len + c_len) % tm_all == 0 and s_len % tm_lat == 0 and s_len % sub == 0
    assert c_len == KCHUNK == sub

    wp = _cols(w_in, ("qa", "qb", "qc", "kb", "ka", "kc", "vb", "va", "vc")).astype(BF16)
    wgm = jnp.concatenate([_cols(w_in, ("ga", "gb", "gc")), w_mg], axis=-1).astype(BF16)
    wbr = jnp.stack([w_br_a, w_br_b, w_br_c], axis=1).astype(BF16)
    wout = w_out.astype(BF16)

    tabs = _rope_tables(s_len, c_len)
    gq_t = jnp.broadcast_to(q_norm[:, :, None], (depth, HD, sub))
    gk_n = jnp.concatenate([k_norm, k_norm], axis=-1)[:, None, :]
    lamv = jnp.stack([lam_q1, lam_k1, lam_q2, lam_k2], axis=1)
    subln_t = jnp.broadcast_to(subln[:, :, None], (depth, 2 * HD, tq_b))
    g_pre3 = g_pre[:, None, :]
    g_post3 = g_post[:, None, :]
    b_mg3 = b_mg[:, None, :]

    rows = ((nb + 1 + 7) // 8) * 8
    sc_in = jnp.concatenate([c, c_ctx[None, :], jnp.zeros((rows - nb - 1, D), F32)], axis=0)
    mod = _ada_call(sc_in, w_ada, b_ada)
    mod4 = mod[:, :, None, :]

    t_all = s_len + c_len
    x_src, ctx_src = x, (ctx, 0)
    for layer in range(depth):
        last = layer == depth - 1
        lam_init = 0.8 - 0.6 * math.exp(-0.3 * layer)
        qt_all, k_all, vt_all = _proj_call(x_src, ctx_src, mod4, g_pre3, wp, tabs, gq_t, gk_n, layer, s_len,
                                           t_all, tm_all, sub)
        oa, ob, oc = _attn_calls(qt_all, k_all, vt_all, sink, lamv, subln_t, layer, not last, s_len,
                                 lam_init, tq_a, tq_b)
        rows_out, tm = (s_len, tm_lat) if last else (t_all, tm_all)
        xs = _merge_call(x_src, None if last else ctx_src, oa, ob, oc, mod4, g_pre3, g_post3, wgm, b_mg3, wbr,
                         wout, layer, s_len, rows_out, tm, sub)
        x_src, ctx_src = xs, (xs, s_len // sub)
    return xs
```

```python
import functools
import math

import jax
import jax.numpy as jnp
import numpy as np
from jax import lax
from jax.experimental import pallas as pl
from jax.experimental.pallas import tpu as pltpu

F32 = jnp.float32
BF16 = jnp.bfloat16

D = 1024
HD = 64
GRID_W = 64
WINDOW = 128
ROPE_THETA = 10000.0
ROPE_PAIRS = HD // 4
EPS = 1e-6
SUBLN_EPS = 1e-5
ATTN_SCALE = HD ** -0.5
LOG2E = math.log2(math.e)

QW = 512
N_Q = 3 * QW
N_K = 512 + 128 + 128
N_V = 512 + 128 + 128
KA_TILE = 4
N_P = N_Q + N_K + N_V
N_G = 3 * QW
N_M = 3 * D

LANES = 128
VMEM_LIMIT = 56 * 1024 * 1024

_IN = dict(qa=(0, 512), ka=(512, 640), va=(640, 768), ga=(768, 1280),
           qb=(1280, 1792), kb=(1792, 2304), vb=(2304, 2816), gb=(2816, 3328),
           qc=(3328, 3840), kc=(3840, 3968), vc=(3968, 4096), gc=(4096, 4608))


def _sigmoid(v):
    return 1.0 / (1.0 + jnp.exp(-v))


def _cparams(n_axes):
    return pltpu.CompilerParams(dimension_semantics=("arbitrary",) * n_axes,
                                vmem_limit_bytes=VMEM_LIMIT)


def _ada_kernel(sc_ref, w_ref, b_ref, o_ref):
    v = sc_ref[...]
    s = (v * _sigmoid(v)).astype(BF16)
    o_ref[...] = jnp.dot(s, w_ref[...].astype(BF16), preferred_element_type=F32) + b_ref[...]


def _ada_call(sc_in, w_ada, b_ada):
    depth = w_ada.shape[0]
    rows = sc_in.shape[0]
    nblk = 3
    return pl.pallas_call(
        _ada_kernel,
        out_shape=jax.ShapeDtypeStruct((depth, rows, 3 * D), F32),
        grid=(depth, nblk),
        in_specs=[
            pl.BlockSpec((rows, D), lambda l, n: (0, 0)),
            pl.BlockSpec((None, D, D), lambda l, n: (l, 0, n)),
            pl.BlockSpec((None, 1, D), lambda l, n: (l, 0, n)),
        ],
        out_specs=pl.BlockSpec((None, rows, D), lambda l, n: (l, 0, n)),
        compiler_params=_cparams(2),
        name="adaln",
    )(sc_in, w_ada, b_ada.reshape(depth, 1, 3 * D))


def _modulated_norm(x, mod, gpre):
    shift = mod[:, :D]
    scale = mod[:, D:2 * D]
    ms = jnp.mean(x * x, axis=-1, keepdims=True)
    return x * lax.rsqrt(ms + EPS) * gpre * (1.0 + scale) + shift


def _sub_tile_input(x_refs, ctx_ref, i, ctx_step):
    x = x_refs[i][...]
    if ctx_ref is None or i != len(x_refs) - 1:
        return x, False
    is_ctx = pl.program_id(1) == ctx_step
    return jnp.where(is_ctx, ctx_ref[...], x), is_ctx


def _proj_kernel(*refs, sub, n_sub, ctx_step):
    x_refs, ctx_ref = refs[:n_sub], refs[n_sub]
    (modb_ref, modc_ref, gpre_ref, w_ref, cost_ref, sint_ref, cosn_ref, sina_ref, sinb_ref, gq_ref, gk_ref,
     qt_ref, k_ref, vt_ref) = refs[n_sub + 1:]

    def project(i):
        x, is_ctx = _sub_tile_input(x_refs, ctx_ref, i, ctx_step)
        mod = jnp.where(is_ctx, modc_ref[...], modb_ref[...])
        h = _modulated_norm(x, mod, gpre_ref[...])
        return jnp.dot(h.astype(BF16), w_ref[...], preferred_element_type=F32)

    def finish(i, y):
        rows = slice(i * sub, (i + 1) * sub)
        q3 = y[:, :N_Q].T.reshape(N_Q // HD, HD, sub)
        qa = q3[:8]
        ss = jnp.sum(qa * qa, axis=1, keepdims=True)
        qa = qa * lax.rsqrt(ss * (1.0 / HD) + EPS) * gq_ref[...][None]
        q3 = jnp.concatenate([qa, q3[8:]], axis=0)
        rot = jnp.concatenate([q3[:, 16:32], q3[:, 0:16], q3[:, 48:64], q3[:, 32:48]], axis=1)
        q3 = (q3 * cost_ref[:, rows][None] + rot * sint_ref[:, rows][None]) * (ATTN_SCALE * LOG2E)
        qt_ref[:, rows] = q3.reshape(N_Q, sub).astype(BF16)

        ka = y[:, N_Q + KA_TILE * LANES:N_Q + (KA_TILE + 1) * LANES]
        lane = lax.broadcasted_iota(jnp.int32, (1, LANES), 1)
        lo = lane < HD
        sq = ka * ka
        s_lo = jnp.sum(jnp.where(lo, sq, 0.0), axis=-1, keepdims=True)
        s_hi = jnp.sum(jnp.where(lo, 0.0, sq), axis=-1, keepdims=True)
        r = jnp.where(lo, lax.rsqrt(s_lo * (1.0 / HD) + EPS), lax.rsqrt(s_hi * (1.0 / HD) + EPS))
        ka = ka * r * gk_ref[...]
        cosn = cosn_ref[rows, :]
        sina = sina_ref[rows, :]
        sinb = sinb_ref[rows, :]
        for j in range(N_K // LANES):
            t = ka if j == KA_TILE else y[:, N_Q + j * LANES:N_Q + (j + 1) * LANES]
            t = t * cosn + pltpu.roll(t, LANES - 16, 1) * sina + pltpu.roll(t, 16, 1) * sinb
            k_ref[rows, j * LANES:(j + 1) * LANES] = t.astype(BF16)

        vt_ref[:, rows] = y[:, N_Q + N_K:].T.astype(BF16)

    y_prev = project(0)
    for i in range(1, n_sub):
        y = project(i)
        finish(i - 1, y_prev)
        y_prev = y
    finish(n_sub - 1, y_prev)


def _token_specs(x_src, ctx_src, n_sub, sub, s_len):
    last = s_len // sub - 1
    specs = [pl.BlockSpec((None, sub, D), functools.partial(lambda i, b, t: (b, jnp.minimum(n_sub * t + i, last), 0), i))
             for i in range(n_sub)]
    operands = [x_src] * n_sub
    if ctx_src is not None:
        arr, blk = ctx_src
        specs.append(pl.BlockSpec((None, sub, D), lambda b, t: (b, blk, 0)))
        operands.append(arr)
    return specs, operands


def _proj_call(x_src, ctx_src, mod4, g_pre, wp, tabs, gq_t, gk_n, layer, s_len, t_all, tm, sub):
    nb = x_src.shape[0]
    cost, sint, cosn, sina, sinb = tabs
    n_sub = tm // sub
    tok_specs, tok_args = _token_specs(x_src, ctx_src, n_sub, sub, s_len)
    return pl.pallas_call(
        functools.partial(_proj_kernel, sub=sub, n_sub=n_sub, ctx_step=t_all // tm - 1),
        out_shape=(jax.ShapeDtypeStruct((nb, N_Q, t_all), BF16),
                   jax.ShapeDtypeStruct((nb, t_all, N_K), BF16),
                   jax.ShapeDtypeStruct((nb, N_V, t_all), BF16)),
        grid=(nb, t_all // tm),
        in_specs=tok_specs + [
            pl.BlockSpec((None, None, 1, 3 * D), lambda b, t: (layer, b, 0, 0)),
            pl.BlockSpec((None, None, 1, 3 * D), lambda b, t: (layer, nb, 0, 0)),
            pl.BlockSpec((None, 1, D), lambda b, t: (layer, 0, 0)),
            pl.BlockSpec((None, D, N_P), lambda b, t: (layer, 0, 0)),
            pl.BlockSpec((HD, tm), lambda b, t: (0, t)),
            pl.BlockSpec((HD, tm), lambda b, t: (0, t)),
            pl.BlockSpec((tm, LANES), lambda b, t: (t, 0)),
            pl.BlockSpec((tm, LANES), lambda b, t: (t, 0)),
            pl.BlockSpec((tm, LANES), lambda b, t: (t, 0)),
            pl.BlockSpec((None, HD, sub), lambda b, t: (layer, 0, 0)),
            pl.BlockSpec((None, 1, LANES), lambda b, t: (layer, 0, 0)),
        ],
        out_specs=(pl.BlockSpec((None, N_Q, tm), lambda b, t: (b, 0, t)),
                   pl.BlockSpec((None, tm, N_K), lambda b, t: (b, t, 0)),
                   pl.BlockSpec((None, N_V, tm), lambda b, t: (b, 0, t))),
        compiler_params=_cparams(2),
        name="proj",
    )(*tok_args, mod4, mod4, g_pre, wp, cost, sint, cosn, sina, sinb, gq_t, gk_n)


KCHUNK = 256
ONES_ROWS = 16
UNITS_PER_BODY = 4
PIPE_AC = dict(lead=3, pace=6)
PIPE_B = dict(lead=4, pace=2)


def _small_softmax_pv(problems):
    ss = [jnp.dot(k, rhs, preferred_element_type=F32) for k, _, rhs, _ in problems]
    ms = [jnp.max(s, axis=0, keepdims=True) for s in ss]
    ms = [m if pr[3] is None else jnp.maximum(m, pr[3]) for m, pr in zip(ms, problems)]
    ps = [jnp.exp2(s - m).astype(BF16) for s, m in zip(ss, ms)]
    outs = []
    for p, m, (_, vt, _, extra) in zip(ps, ms, problems):
        dv = vt.shape[0]
        vt_aug = jnp.concatenate([vt, jnp.ones((ONES_ROWS, vt.shape[1]), BF16)], axis=0)
        pv = jnp.dot(vt_aug, p, preferred_element_type=F32)
        l = pv[dv:dv + 1]
        if extra is not None:
            l = l + jnp.exp2(extra - m)
        outs.append(pv[:dv] * (1.0 / l))
    return outs


def _gqa_rhs(qt, j, tq):
    z = jnp.zeros((HD, tq), qt.dtype)
    first = j == 0
    cols = []
    for g in range(4):
        qg = qt[g * HD:(g + 1) * HD, :]
        cols.append(jnp.concatenate([jnp.where(first, qg, z), jnp.where(first, z, qg)], axis=0))
    return jnp.concatenate(cols, axis=1)


def _store_heads(o_ref, ot, tq, nheads):
    o = jnp.concatenate([ot[:, g * tq:(g + 1) * tq] for g in range(nheads)], axis=0)
    o_ref[...] = o.T.astype(o_ref.dtype)


def _sink_row(sink_ref, base, tq):
    blk = lax.broadcasted_iota(jnp.int32, (1, 4 * tq), 1) // tq
    row = jnp.zeros((1, 4 * tq), F32)
    for g in range(4):
        row = jnp.where(blk == g, sink_ref[base + g], row)
    return row * LOG2E


def _tile(t, size):
    if isinstance(t, int):
        return pl.ds(t * size, size)
    return pl.ds(pl.multiple_of(t * size, size), size)


def _pipeline(make_rhs, score_chunks, value_chunks, finish, chunk_rows, out_shapes, scratch, n_units, zero,
              extra=None, *, lead, pace):
    main, head = scratch[:2], scratch[2:]
    n = len(chunk_rows)
    n_groups = len(out_shapes)
    last = n_units - 1
    assert n >= 2 * lead and n_units % 2 == 0
    ex = (lambda u: [None] * n_groups) if extra is None else extra

    def place(c, parity):
        if c < lead:
            return head[parity], sum(chunk_rows[:c])
        return main[parity], sum(chunk_rows[lead:c])

    def score(thunk, rhs, c, parity, cm):
        k, bias, g = thunk()
        s = jnp.dot(k, rhs[g], preferred_element_type=F32)
        if bias is not None:
            s = s + bias
        buf, row = place(c, parity)
        buf[row:row + chunk_rows[c], :] = s
        part = jnp.max(s, axis=0, keepdims=True)
        cm = list(cm)
        cm[g] = jnp.maximum(cm[g], part)
        return cm, part

    def with_extra(cm, u):
        return [m if e is None else jnp.maximum(m, e) for m, e in zip(cm, ex(u))]

    neg = [jnp.full((1, s[1]), -jnp.inf, F32) for s in out_shapes]

    rhs, sc, cm = make_rhs(0), score_chunks(0), neg
    for c in range(n):
        cm, _ = score(sc[c], rhs, c, 0, cm)
    m0 = with_extra(cm, 0)
    rhs, sc, cm1 = make_rhs(1), score_chunks(1), neg
    for c in range(lead):
        cm1, _ = score(sc[c], rhs, c, 1, cm1)

    def half(u, parity, m_u, cm1):
        u1 = jnp.minimum(u + 1, last)
        u2 = jnp.minimum(u + 2, last)
        rhs1, sc1 = make_rhs(u1), score_chunks(u1)
        rhs2, sc2 = make_rhs(u2), score_chunks(u2)
        vc, ex_u = value_chunks(u), ex(u)
        cm2 = neg
        acc = [None] * n_groups
        parts = []
        for c in range(n):
            if c + lead < n:
                cm1, part = score(sc1[c + lead], rhs1, c + lead, 1 - parity, cm1)
            else:
                cm2, part = score(sc2[c + lead - n], rhs2, c + lead - n, parity, cm2)
            parts.append(part)
            vt, g = vc[c]()
            m_c = m_u[g]
            if c >= pace:
                m_c = jnp.maximum(m_c, jnp.minimum(parts[c - pace], m_c))
            buf, row = place(c, parity)
            rows = pl.ds(pl.multiple_of(row + zero, LANES), chunk_rows[c])
            p = jnp.exp2(buf[rows, :] - m_c).astype(BF16)
            vt_aug = jnp.concatenate([vt, jnp.ones((ONES_ROWS, vt.shape[1]), BF16)], axis=0)
            pv = jnp.dot(vt_aug, p, preferred_element_type=F32)
            acc[g] = pv if acc[g] is None else acc[g] + pv
        ots = []
        for g in range(n_groups):
            dv = acc[g].shape[0] - ONES_ROWS
            l = acc[g][dv:dv + 1]
            if ex_u[g] is not None:
                l = l + jnp.exp2(ex_u[g] - m_u[g])
            ots.append(acc[g][:dv] * (1.0 / l))
        return with_extra(cm1, u1), cm2, ots

    def body(i, carry):
        m_u, cm_next, ot = carry
        u0 = UNITS_PER_BODY * i
        finish(jnp.maximum(u0 - 1, 0), ot)
        for h in range(UNITS_PER_BODY):
            if h:
                finish(u0 + h - 1, ot)
            m_u, cm_next, ot = half(u0 + h, h % 2, m_u, cm_next)
        return m_u, cm_next, ot

    assert n_units % UNITS_PER_BODY == 0
    carry = (m0, cm1, [jnp.zeros(s, F32) for s in out_shapes])
    _, _, ot_last = lax.fori_loop(0, n_units // UNITS_PER_BODY, body, carry)
    finish(last, ot_last)


def _score_scratch(chunk_rows, n_cols, lead):
    main = pltpu.VMEM((sum(chunk_rows[lead:]), n_cols), F32)
    head = pltpu.VMEM((sum(chunk_rows[:lead]), n_cols), F32)
    return [main, main, head, head]


def _split_unit(u, n_tiles):
    if isinstance(u, int):
        return u // n_tiles, u % n_tiles
    return lax.div(u, n_tiles), lax.rem(u, n_tiles)


def _rows(g, size):
    return _tile(g, size)


def _ac_chunk_rows(t_all, tq):
    wk = tq + 2 * WINDOW
    return ([KCHUNK] * (t_all // KCHUNK) + [KCHUNK]
            + [min(lo + KCHUNK, wk) - lo for lo in range(0, wk, KCHUNK)])


def _attn_ac_kernel(zero_ref, sink_ref, bias_ref, qa_ref, qc_ref, ka_ref, kc_ref, vta_ref, vtc_ref, oa_ref, oc_ref,
                    *scratch, tq, n_lat, n_ctx, s_len):
    t_all = ka_ref.shape[0]
    wk = tq + 2 * WINDOW
    spans = [(lo, min(lo + KCHUNK, wk)) for lo in range(0, wk, KCHUNK)]

    def rhs_of(q_ref, j, t):
        return _gqa_rhs(q_ref[_rows(j, 4 * HD), _tile(t, tq)], j, tq)

    def store(o_ref, j, t, ot):
        _store_heads(o_ref.at[_tile(t, tq), _rows(j, 4 * HD)], ot, tq, 4)

    def window(t):
        q0 = t * tq
        start = jnp.clip(q0 - WINDOW, 0, s_len - wk)
        return start, (q0 - start) // WINDOW

    def local(start, lo, hi):
        return pl.ds(pl.multiple_of(start + lo, LANES), hi - lo)

    def score_chunks(u):
        _, t = _split_unit(u, n_lat)
        start, variant = window(t)

        def loc(lo, hi):
            b = bias_ref[variant, lo:hi, :]
            return kc_ref[local(start, lo, hi), :], jnp.concatenate([b] * 4, axis=1), 1

        return ([functools.partial(lambda c: (ka_ref[c * KCHUNK:(c + 1) * KCHUNK, :], None, 0), c)
                 for c in range(t_all // KCHUNK)]
                + [lambda: (kc_ref[s_len:s_len + KCHUNK, :], None, 1)]
                + [functools.partial(loc, lo, hi) for lo, hi in spans])

    def value_chunks(u):
        j, t = _split_unit(u, n_lat)
        start, _ = window(t)
        return ([functools.partial(lambda c: (vta_ref[_rows(j, HD), c * KCHUNK:(c + 1) * KCHUNK], 0), c)
                 for c in range(t_all // KCHUNK)]
                + [lambda: (vtc_ref[_rows(j, HD), s_len:s_len + KCHUNK], 1)]
                + [functools.partial(lambda lo, hi: (vtc_ref[_rows(j, HD), local(start, lo, hi)], 1), lo, hi)
                   for lo, hi in spans])

    def make_rhs(u):
        j, t = _split_unit(u, n_lat)
        return [rhs_of(qa_ref, j, t), rhs_of(qc_ref, j, t)]

    def extra(u):
        j, _ = _split_unit(u, n_lat)
        return [None, _sink_row(sink_ref, j * 4, tq)]

    def finish(u, ots):
        j, t = _split_unit(u, n_lat)
        store(oa_ref, j, t, ots[0])
        store(oc_ref, j, t, ots[1])

    ctx_tiles = range(n_lat, n_lat + n_ctx)
    n = 4 * tq
    problems, targets = [], []
    for j in range(2 if n_ctx else 0):
        for q_ref, k_ref, vt_ref, o_ref, sink in ((qa_ref, ka_ref, vta_ref, oa_ref, None),
                                                  (qc_ref, kc_ref, vtc_ref, oc_ref, _sink_row(sink_ref, j * 4, tq))):
            rhs = jnp.concatenate([rhs_of(q_ref, j, t) for t in ctx_tiles], axis=1)
            sink_cols = None if sink is None else jnp.concatenate([sink] * n_ctx, axis=1)
            problems.append((k_ref[s_len:, :], vt_ref[_rows(j, HD), s_len:], rhs, sink_cols))
            targets.append((o_ref, j))
    for (o_ref, j), ot in zip(targets, _small_softmax_pv(problems)):
        for i, t in enumerate(ctx_tiles):
            store(o_ref, j, t, ot[:, i * n:(i + 1) * n])

    _pipeline(make_rhs, score_chunks, value_chunks, finish, _ac_chunk_rows(t_all, tq), [(HD, 4 * tq)] * 2, scratch,
              2 * n_lat, zero_ref[0], extra=extra, **PIPE_AC)


def _attn_b_kernel(zero_ref, lamv_ref, subln_ref, qt_ref, k_ref, vt_ref, o_ref, *scratch, tq, n_lat, n_ctx, s_len,
                   lam_init):
    lv = lamv_ref[...]
    lam = (jnp.exp(jnp.sum(lv[0:1] * lv[1:2], axis=-1, keepdims=True))
           - jnp.exp(jnp.sum(lv[2:3] * lv[3:4], axis=-1, keepdims=True)) + lam_init)
    t_all = k_ref.shape[0]
    n_chunks = t_all // KCHUNK
    hw = 2 * HD

    def rhs_of(h, t):
        qt = qt_ref[_rows(h, hw), _tile(t, tq)]
        z = jnp.zeros((HD, tq), qt.dtype)
        return jnp.concatenate([jnp.concatenate([qt[:HD], z], axis=0),
                                jnp.concatenate([z, qt[HD:]], axis=0)], axis=1)

    def store(h, t, o2):
        o = o2[:, :tq] - lam * o2[:, tq:]
        ms = jnp.mean(o * o, axis=0, keepdims=True)
        o = o * lax.rsqrt(ms + SUBLN_EPS) * subln_ref[...] * (1.0 - lam_init)
        o_ref[_tile(t, tq), _rows(h, hw)] = o.T.astype(o_ref.dtype)

    def score_chunks(u):
        h, _ = _split_unit(u, n_lat)
        return [functools.partial(lambda c: (k_ref[c * KCHUNK:(c + 1) * KCHUNK, _rows(h, hw)], None, 0), c)
                for c in range(n_chunks)]

    def value_chunks(u):
        h, _ = _split_unit(u, n_lat)
        return [functools.partial(lambda c: (vt_ref[_rows(h, hw), c * KCHUNK:(c + 1) * KCHUNK], 0), c)
                for c in range(n_chunks)]

    units = [(h, t) for h in range(4) for t in range(n_lat, n_lat + n_ctx)]
    problems = [(k_ref[s_len:, _rows(h, hw)], vt_ref[_rows(h, hw), s_len:], rhs_of(h, t), None) for h, t in units]
    for (h, t), o2 in zip(units, _small_softmax_pv(problems)):
        store(h, t, o2)

    _pipeline(lambda u: [rhs_of(*_split_unit(u, n_lat))], score_chunks, value_chunks,
              lambda u, ots: store(*_split_unit(u, n_lat), ots[0]), [KCHUNK] * n_chunks, [(hw, 2 * tq)], scratch,
              4 * n_lat, zero_ref[0], **PIPE_B)


def _window_bias(tq):
    wk = tq + 2 * WINDOW
    r = np.arange(wk)[:, None]
    c = np.arange(tq)[None, :]
    out = np.stack([np.where(np.abs(c - r + v * WINDOW) <= WINDOW, 0.0, -np.inf) for v in range(3)])
    return jnp.asarray(out, F32)


def _attn_calls(qt_all, k_all, vt_all, sink, lamv, subln_t, layer, with_ctx, s_len, lam_init,
                tq_a, tq_b):
    nb, _, t_all = qt_all.shape
    c_len = t_all - s_len
    o_rows = t_all if with_ctx else s_len
    o_shape = jax.ShapeDtypeStruct((nb, o_rows, QW), BF16)

    def steps(tq):
        n_lat = s_len // tq
        return n_lat, n_lat + (c_len // tq if with_ctx else 0)

    whole = lambda rows, cols, r, c: pl.BlockSpec((None, rows, cols), lambda b: (b, r, c))
    o_spec = pl.BlockSpec((None, o_rows, QW), lambda b: (b, 0, 0))
    zero = jnp.zeros((1,), jnp.int32)

    n_lat, n_all = steps(tq_a)
    wk = tq_a + 2 * WINDOW
    oa, oc = pl.pallas_call(
        functools.partial(_attn_ac_kernel, tq=tq_a, n_lat=n_lat, n_ctx=n_all - n_lat, s_len=s_len),
        out_shape=(o_shape, o_shape),
        grid=(nb,),
        in_specs=[
            pl.BlockSpec(memory_space=pltpu.SMEM),
            pl.BlockSpec(memory_space=pltpu.SMEM),
            pl.BlockSpec((3, wk, tq_a), lambda b: (0, 0, 0)),
            whole(QW, t_all, 0, 0), whole(QW, t_all, 2, 0),
            whole(t_all, LANES, 0, KA_TILE), whole(t_all, LANES, 0, KA_TILE + 1),
            whole(LANES, t_all, KA_TILE, 0), whole(LANES, t_all, KA_TILE + 1, 0),
        ],
        out_specs=(o_spec, o_spec),
        scratch_shapes=_score_scratch(_ac_chunk_rows(t_all, tq_a), 4 * tq_a, PIPE_AC["lead"]),
        compiler_params=_cparams(1),
        name="attn_ac",
    )(zero, sink[layer], _window_bias(tq_a), qt_all, qt_all, k_all, k_all, vt_all, vt_all)

    n_lat, n_all = steps(tq_b)
    ob = pl.pallas_call(
        functools.partial(_attn_b_kernel, tq=tq_b, n_lat=n_lat, n_ctx=n_all - n_lat, s_len=s_len,
                          lam_init=lam_init),
        out_shape=o_shape,
        grid=(nb,),
        in_specs=[
            pl.BlockSpec(memory_space=pltpu.SMEM),
            pl.BlockSpec((None, 4, HD), lambda b: (layer, 0, 0)),
            pl.BlockSpec((None, 2 * HD, tq_b), lambda b: (layer, 0, 0)),
            whole(QW, t_all, 1, 0), whole(t_all, QW, 0, 0), whole(QW, t_all, 0, 0),
        ],
        out_specs=o_spec,
        scratch_shapes=_score_scratch([KCHUNK] * (t_all // KCHUNK), 2 * tq_b, PIPE_B["lead"]),
        compiler_params=_cparams(1),
        name="attn_b",
    )(zero, lamv, subln_t, qt_all, k_all, vt_all)

    return oa, ob, oc


def _merge_kernel(*refs, sub, n_sub, ctx_step):
    x_refs = refs[:n_sub]
    rest = refs[n_sub:]
    ctx_ref = None
    if ctx_step is not None:
        ctx_ref, rest = rest[0], rest[1:]
    (oa_ref, ob_ref, oc_ref, modb_ref, modc_ref, gpre_ref, gpost_ref, wgm_ref, bmg_ref, wbr_ref, wout_ref,
     out_ref) = rest

    def gates(i):
        x, is_ctx = _sub_tile_input(x_refs, ctx_ref, i, ctx_step)
        mod = jnp.where(is_ctx, modc_ref[...], modb_ref[...])
        h = _modulated_norm(x, mod, gpre_ref[...])
        gm = jnp.dot(h.astype(BF16), wgm_ref[...], preferred_element_type=F32)
        return x, mod, gm

    def finish(i, x, mod, gm):
        rows = slice(i * sub, (i + 1) * sub)
        z = None
        for j, o_ref in enumerate((oa_ref, ob_ref, oc_ref)):
            g = gm[:, j * QW:(j + 1) * QW]
            u = (o_ref[rows, :].astype(F32) * (g * _sigmoid(g))).astype(BF16)
            p = jnp.dot(u, wbr_ref[j], preferred_element_type=F32)
            mg = _sigmoid(gm[:, N_G + j * D:N_G + (j + 1) * D] + bmg_ref[:, j * D:(j + 1) * D])
            z = mg * p if z is None else z + mg * p
        y = jnp.dot(z.astype(BF16), wout_ref[...], preferred_element_type=F32)
        ms = jnp.mean(y * y, axis=-1, keepdims=True)
        gate = mod[:, 2 * D:]
        out_ref[rows, :] = x + gate * (y * lax.rsqrt(ms + EPS) * gpost_ref[...])

    prev = gates(0)
    for i in range(1, n_sub):
        cur = gates(i)
        finish(i - 1, *prev)
        prev = cur
    finish(n_sub - 1, *prev)


def _merge_call(x_src, ctx_src, oa, ob, oc, mod4, g_pre, g_post, wgm, b_mg, wbr, wout, layer, s_len, rows_out,
                tm, sub):
    nb = x_src.shape[0]
    n_sub = tm // sub
    tok_specs, tok_args = _token_specs(x_src, ctx_src, n_sub, sub, s_len)
    ctx_step = None if ctx_src is None else rows_out // tm - 1
    tok = lambda b, t: (b, t, 0)
    lay2 = lambda b, t: (layer, 0, 0)
    const = dict(pipeline_mode=pl.Buffered(1))
    return pl.pallas_call(
        functools.partial(_merge_kernel, sub=sub, n_sub=n_sub, ctx_step=ctx_step),
        out_shape=jax.ShapeDtypeStruct((nb, rows_out, D), F32),
        grid=(nb, rows_out // tm),
        in_specs=tok_specs + [
            pl.BlockSpec((None, tm, QW), tok),
            pl.BlockSpec((None, tm, QW), tok),
            pl.BlockSpec((None, tm, QW), tok),
            pl.BlockSpec((None, None, 1, 3 * D), lambda b, t: (layer, b, 0, 0)),
            pl.BlockSpec((None, None, 1, 3 * D), lambda b, t: (layer, nb, 0, 0)),
            pl.BlockSpec((None, 1, D), lay2),
            pl.BlockSpec((None, 1, D), lay2),
            pl.BlockSpec((None, D, N_G + N_M), lay2, **const),
            pl.BlockSpec((None, 1, N_M), lay2),
            pl.BlockSpec((None, 3, QW, D), lambda b, t: (layer, 0, 0, 0), **const),
            pl.BlockSpec((None, D, D), lay2, **const),
        ],
        out_specs=pl.BlockSpec((None, tm, D), tok),
        compiler_params=_cparams(2),
        name="merge",
    )(*tok_args, oa, ob, oc, mod4, mod4, g_pre, g_post, wgm, b_mg, wbr, wout)


def _rope_tables(s_len, c_len):
    rows = s_len // GRID_W
    row = np.repeat(np.arange(rows), GRID_W).astype(np.float32)
    col = np.tile(np.arange(GRID_W), rows).astype(np.float32)
    freqs = (np.float32(ROPE_THETA) ** (-np.arange(ROPE_PAIRS, dtype=np.float32) / ROPE_PAIRS)).astype(np.float32)
    ang_r = row[:, None] * freqs
    ang_c = col[:, None] * freqs
    ang = np.concatenate([ang_r, ang_r, ang_c, ang_c], axis=-1)
    cos = np.concatenate([np.cos(ang), np.ones((c_len, HD), np.float32)], axis=0).astype(np.float32)
    sin = np.concatenate([np.sin(ang), np.zeros((c_len, HD), np.float32)], axis=0).astype(np.float32)
    first = (np.arange(HD) % 32) < 16
    sin_a = np.where(first, -sin, np.float32(0.0))
    sin_b = np.where(first, np.float32(0.0), sin)
    tile2 = lambda a: np.concatenate([a, a], axis=-1)
    cost = np.ascontiguousarray(cos.T)
    sint = np.ascontiguousarray((sin_a + sin_b).T)
    return tuple(jnp.asarray(a, F32) for a in (cost, sint, tile2(cos), tile2(sin_a), tile2(sin_b)))


def _cols(w, names):
    return jnp.concatenate([w[..., _IN[n][0]:_IN[n][1]] for n in names], axis=-1)


def kernel(x, c, ctx, c_ctx, w_ada, b_ada, g_pre, g_post, w_in, q_norm, k_norm, lam_q1, lam_k1, lam_q2,
           lam_k2, subln, sink, w_br_a, w_br_b, w_br_c, w_mg, b_mg, w_out):
    nb, s_len, _ = x.shape
    c_len = ctx.shape[1]
    depth = w_in.shape[0]
    sub = 256
    tm_all = 3 * sub
    tm_lat = 4 * sub
    tq_a, tq_b = 128, 256
    assert (s_len + c_len) % tm_all == 0 and s_len % tm_lat == 0 and s_len % sub == 0
    assert c_len == KCHUNK == sub

    wp = _cols(w_in, ("qa", "qb", "qc", "kb", "ka", "kc", "vb", "va", "vc")).astype(BF16)
    wgm = jnp.concatenate([_cols(w_in, ("ga", "gb", "gc")), w_mg], axis=-1).astype(BF16)
    wbr = jnp.stack([w_br_a, w_br_b, w_br_c], axis=1).astype(BF16)
    wout = w_out.astype(BF16)

    tabs = _rope_tables(s_len, c_len)
    gq_t = jnp.broadcast_to(q_norm[:, :, None], (depth, HD, sub))
    gk_n = jnp.concatenate([k_norm, k_norm], axis=-1)[:, None, :]
    lamv = jnp.stack([lam_q1, lam_k1, lam_q2, lam_k2], axis=1)
    subln_t = jnp.broadcast_to(subln[:, :, None], (depth, 2 * HD, tq_b))
    g_pre3 = g_pre[:, None, :]
    g_post3 = g_post[:, None, :]
    b_mg3 = b_mg[:, None, :]

    rows = ((nb + 1 + 7) // 8) * 8
    sc_in = jnp.concatenate([c, c_ctx[None, :], jnp.zeros((rows - nb - 1, D), F32)], axis=0)
    mod = _ada_call(sc_in, w_ada, b_ada)
    mod4 = mod[:, :, None, :]

    t_all = s_len + c_len
    x_src, ctx_src = x, (ctx, 0)
    for layer in range(depth):
        last = layer == depth - 1
        lam_init = 0.8 - 0.6 * math.exp(-0.3 * layer)
        qt_all, k_all, vt_all = _proj_call(x_src, ctx_src, mod4, g_pre3, wp, tabs, gq_t, gk_n, layer, s_len,
                                           t_all, tm_all, sub)
        oa, ob, oc = _attn_calls(qt_all, k_all, vt_all, sink, lamv, subln_t, layer, not last, s_len,
                                 lam_init, tq_a, tq_b)
        rows_out, tm = (s_len, tm_lat) if last else (t_all, tm_all)
        xs = _merge_call(x_src, None if last else ctx_src, oa, ob, oc, mod4, g_pre3, g_post3, wgm, b_mg3, wbr,
                         wout, layer, s_len, rows_out, tm, sub)
        x_src, ctx_src = xs, (xs, s_len // sub)
    return xs
```

```python
import functools
import math

import jax
import jax.numpy as jnp
import numpy as np
from jax import lax
from jax.experimental import pallas as pl
from jax.experimental.pallas import tpu as pltpu

F32 = jnp.float32
BF16 = jnp.bfloat16

D = 1024
HD = 64
GRID_W = 64
WINDOW = 128
ROPE_THETA = 10000.0
ROPE_PAIRS = HD // 4
EPS = 1e-6
SUBLN_EPS = 1e-5
ATTN_SCALE = HD ** -0.5
LOG2E = math.log2(math.e)

QW = 512
N_Q = 3 * QW
N_K = 512 + 128 + 128
N_V = 512 + 128 + 128
KA_TILE = 4
N_P = N_Q + N_K + N_V
N_G = 3 * QW
N_M = 3 * D

LANES = 128
VMEM_LIMIT = 56 * 1024 * 1024

_IN = dict(qa=(0, 512), ka=(512, 640), va=(640, 768), ga=(768, 1280),
           qb=(1280, 1792), kb=(1792, 2304), vb=(2304, 2816), gb=(2816, 3328),
           qc=(3328, 3840), kc=(3840, 3968), vc=(3968, 4096), gc=(4096, 4608))


def _sigmoid(v):
    return 1.0 / (1.0 + jnp.exp(-v))


def _cparams(n_axes):
    return pltpu.CompilerParams(dimension_semantics=("arbitrary",) * n_axes,
                                vmem_limit_bytes=VMEM_LIMIT)


def _ada_kernel(sc_ref, w_ref, b_ref, o_ref):
    v = sc_ref[...]
    s = (v * _sigmoid(v)).astype(BF16)
    o_ref[...] = jnp.dot(s, w_ref[...].astype(BF16), preferred_element_type=F32) + b_ref[...]


def _ada_call(sc_in, w_ada, b_ada):
    depth = w_ada.shape[0]
    rows = sc_in.shape[0]
    nblk = 3
    return pl.pallas_call(
        _ada_kernel,
        out_shape=jax.ShapeDtypeStruct((depth, rows, 3 * D), F32),
        grid=(depth, nblk),
        in_specs=[
            pl.BlockSpec((rows, D), lambda l, n: (0, 0)),
            pl.BlockSpec((None, D, D), lambda l, n: (l, 0, n)),
            pl.BlockSpec((None, 1, D), lambda l, n: (l, 0, n)),
        ],
        out_specs=pl.BlockSpec((None, rows, D), lambda l, n: (l, 0, n)),
        compiler_params=_cparams(2),
        name="adaln",
    )(sc_in, w_ada, b_ada.reshape(depth, 1, 3 * D))


def _modulated_norm(x, mod, gpre):
    shift = mod[:, :D]
    scale = mod[:, D:2 * D]
    ms = jnp.mean(x * x, axis=-1, keepdims=True)
    return x * lax.rsqrt(ms + EPS) * gpre * (1.0 + scale) + shift


def _sub_tile_input(x_refs, ctx_ref, i, ctx_step):
    x = x_refs[i][...]
    if ctx_ref is None or i != len(x_refs) - 1:
        return x, False
    is_ctx = pl.program_id(1) == ctx_step
    return jnp.where(is_ctx, ctx_ref[...], x), is_ctx


def _proj_kernel(*refs, sub, n_sub, ctx_step):
    x_refs, ctx_ref = refs[:n_sub], refs[n_sub]
    (modb_ref, modc_ref, gpre_ref, w_ref, cost_ref, sint_ref, cosn_ref, sina_ref, sinb_ref, gq_ref, gk_ref,
     qt_ref, k_ref, vt_ref) = refs[n_sub + 1:]

    def project(i):
        x, is_ctx = _sub_tile_input(x_refs, ctx_ref, i, ctx_step)
        mod = jnp.where(is_ctx, modc_ref[...], modb_ref[...])
        h = _modulated_norm(x, mod, gpre_ref[...])
        return jnp.dot(h.astype(BF16), w_ref[...], preferred_element_type=F32)

    def finish(i, y):
        rows = slice(i * sub, (i + 1) * sub)
        q3 = y[:, :N_Q].T.reshape(N_Q // HD, HD, sub)
        qa = q3[:8]
        ss = jnp.sum(qa * qa, axis=1, keepdims=True)
        qa = qa * lax.rsqrt(ss * (1.0 / HD) + EPS) * gq_ref[...][None]
        q3 = jnp.concatenate([qa, q3[8:]], axis=0)
        rot = jnp.concatenate([q3[:, 16:32], q3[:, 0:16], q3[:, 48:64], q3[:, 32:48]], axis=1)
        q3 = (q3 * cost_ref[:, rows][None] + rot * sint_ref[:, rows][None]) * (ATTN_SCALE * LOG2E)
        qt_ref[:, rows] = q3.reshape(N_Q, sub).astype(BF16)

        ka = y[:, N_Q + KA_TILE * LANES:N_Q + (KA_TILE + 1) * LANES]
        lane = lax.broadcasted_iota(jnp.int32, (1, LANES), 1)
        lo = lane < HD
        sq = ka * ka
        s_lo = jnp.sum(jnp.where(lo, sq, 0.0), axis=-1, keepdims=True)
        s_hi = jnp.sum(jnp.where(lo, 0.0, sq), axis=-1, keepdims=True)
        r = jnp.where(lo, lax.rsqrt(s_lo * (1.0 / HD) + EPS), lax.rsqrt(s_hi * (1.0 / HD) + EPS))
        ka = ka * r * gk_ref[...]
        cosn = cosn_ref[rows, :]
        sina = sina_ref[rows, :]
        sinb = sinb_ref[rows, :]
        for j in range(N_K // LANES):
            t = ka if j == KA_TILE else y[:, N_Q + j * LANES:N_Q + (j + 1) * LANES]
            t = t * cosn + pltpu.roll(t, LANES - 16, 1) * sina + pltpu.roll(t, 16, 1) * sinb
            k_ref[rows, j * LANES:(j + 1) * LANES] = t.astype(BF16)

        vt_ref[:, rows] = y[:, N_Q + N_K:].T.astype(BF16)

    y_prev = project(0)
    for i in range(1, n_sub):
        y = project(i)
        finish(i - 1, y_prev)
        y_prev = y
    finish(n_sub - 1, y_prev)


def _token_specs(x_src, ctx_src, n_sub, sub, s_len):
    last = s_len // sub - 1
    specs = [pl.BlockSpec((None, sub, D), functools.partial(lambda i, b, t: (b, jnp.minimum(n_sub * t + i, last), 0), i))
             for i in range(n_sub)]
    operands = [x_src] * n_sub
    if ctx_src is not None:
        arr, blk = ctx_src
        specs.append(pl.BlockSpec((None, sub, D), lambda b, t: (b, blk, 0)))
        operands.append(arr)
    return specs, operands


def _proj_call(x_src, ctx_src, mod4, g_pre, wp, tabs, gq_t, gk_n, layer, s_len, t_all, tm, sub):
    nb = x_src.shape[0]
    cost, sint, cosn, sina, sinb = tabs
    n_sub = tm // sub
    tok_specs, tok_args = _token_specs(x_src, ctx_src, n_sub, sub, s_len)
    return pl.pallas_call(
        functools.partial(_proj_kernel, sub=sub, n_sub=n_sub, ctx_step=t_all // tm - 1),
        out_shape=(jax.ShapeDtypeStruct((nb, N_Q, t_all), BF16),
                   jax.ShapeDtypeStruct((nb, t_all, N_K), BF16),
                   jax.ShapeDtypeStruct((nb, N_V, t_all), BF16)),
        grid=(nb, t_all // tm),
        in_specs=tok_specs + [
            pl.BlockSpec((None, None, 1, 3 * D), lambda b, t: (layer, b, 0, 0)),
            pl.BlockSpec((None, None, 1, 3 * D), lambda b, t: (layer, nb, 0, 0)),
            pl.BlockSpec((None, 1, D), lambda b, t: (layer, 0, 0)),
            pl.BlockSpec((None, D, N_P), lambda b, t: (layer, 0, 0)),
            pl.BlockSpec((HD, tm), lambda b, t: (0, t)),
            pl.BlockSpec((HD, tm), lambda b, t: (0, t)),
            pl.BlockSpec((tm, LANES), lambda b, t: (t, 0)),
            pl.BlockSpec((tm, LANES), lambda b, t: (t, 0)),
            pl.BlockSpec((tm, LANES), lambda b, t: (t, 0)),
            pl.BlockSpec((None, HD, sub), lambda b, t: (layer, 0, 0)),
            pl.BlockSpec((None, 1, LANES), lambda b, t: (layer, 0, 0)),
        ],
        out_specs=(pl.BlockSpec((None, N_Q, tm), lambda b, t: (b, 0, t)),
                   pl.BlockSpec((None, tm, N_K), lambda b, t: (b, t, 0)),
                   pl.BlockSpec((None, N_V, tm), lambda b, t: (b, 0, t))),
        compiler_params=_cparams(2),
        name="proj",
    )(*tok_args, mod4, mod4, g_pre, wp, cost, sint, cosn, sina, sinb, gq_t, gk_n)


KCHUNK = 256
ONES_ROWS = 16
UNITS_PER_BODY = 8
PIPE_AC = dict(lead=3, pace=6)
PIPE_B = dict(lead=4, pace=2)


def _small_softmax_pv(problems):
    ss = [jnp.dot(k, rhs, preferred_element_type=F32) for k, _, rhs, _ in problems]
    ms = [jnp.max(s, axis=0, keepdims=True) for s in ss]
    ms = [m if pr[3] is None else jnp.maximum(m, pr[3]) for m, pr in zip(ms, problems)]
    ps = [jnp.exp2(s - m).astype(BF16) for s, m in zip(ss, ms)]
    outs = []
    for p, m, (_, vt, _, extra) in zip(ps, ms, problems):
        dv = vt.shape[0]
        vt_aug = jnp.concatenate([vt, jnp.ones((ONES_ROWS, vt.shape[1]), BF16)], axis=0)
        pv = jnp.dot(vt_aug, p, preferred_element_type=F32)
        l = pv[dv:dv + 1]
        if extra is not None:
            l = l + jnp.exp2(extra - m)
        outs.append(pv[:dv] * (1.0 / l))
    return outs


def _gqa_rhs(qt, j, tq):
    z = jnp.zeros((HD, tq), qt.dtype)
    first = j == 0
    cols = []
    for g in range(4):
        qg = qt[g * HD:(g + 1) * HD, :]
        cols.append(jnp.concatenate([jnp.where(first, qg, z), jnp.where(first, z, qg)], axis=0))
    return jnp.concatenate(cols, axis=1)


def _store_heads(o_ref, ot, tq, nheads):
    o = jnp.concatenate([ot[:, g * tq:(g + 1) * tq] for g in range(nheads)], axis=0)
    o_ref[...] = o.T.astype(o_ref.dtype)


def _sink_row(sink_ref, base, tq):
    blk = lax.broadcasted_iota(jnp.int32, (1, 4 * tq), 1) // tq
    row = jnp.zeros((1, 4 * tq), F32)
    for g in range(4):
        row = jnp.where(blk == g, sink_ref[base + g], row)
    return row * LOG2E


def _tile(t, size):
    if isinstance(t, int):
        return pl.ds(t * size, size)
    return pl.ds(pl.multiple_of(t * size, size), size)


def _pipeline(make_rhs, score_chunks, value_chunks, finish, chunk_rows, out_shapes, scratch, n_units, zero,
              extra=None, *, lead, pace):
    main, head = scratch[:2], scratch[2:]
    n = len(chunk_rows)
    n_groups = len(out_shapes)
    last = n_units - 1
    assert n >= 2 * lead and n_units % 2 == 0
    ex = (lambda u: [None] * n_groups) if extra is None else extra

    def place(c, parity):
        if c < lead:
            return head[parity], sum(chunk_rows[:c])
        return main[parity], sum(chunk_rows[lead:c])

    def score(thunk, rhs, c, parity, cm):
        k, bias, g = thunk()
        s = jnp.dot(k, rhs[g], preferred_element_type=F32)
        if bias is not None:
            s = s + bias
        buf, row = place(c, parity)
        buf[row:row + chunk_rows[c], :] = s
        part = jnp.max(s, axis=0, keepdims=True)
        cm = list(cm)
        cm[g] = jnp.maximum(cm[g], part)
        return cm, part

    def with_extra(cm, u):
        return [m if e is None else jnp.maximum(m, e) for m, e in zip(cm, ex(u))]

    neg = [jnp.full((1, s[1]), -jnp.inf, F32) for s in out_shapes]

    rhs, sc, cm = make_rhs(0), score_chunks(0), neg
    for c in range(n):
        cm, _ = score(sc[c], rhs, c, 0, cm)
    m0 = with_extra(cm, 0)
    rhs, sc, cm1 = make_rhs(1), score_chunks(1), neg
    for c in range(lead):
        cm1, _ = score(sc[c], rhs, c, 1, cm1)

    def half(u, parity, m_u, cm1):
        u1 = jnp.minimum(u + 1, last)
        u2 = jnp.minimum(u + 2, last)
        rhs1, sc1 = make_rhs(u1), score_chunks(u1)
        rhs2, sc2 = make_rhs(u2), score_chunks(u2)
        vc, ex_u = value_chunks(u), ex(u)
        cm2 = neg
        acc = [None] * n_groups
        parts = []
        for c in range(n):
            if c + lead < n:
                cm1, part = score(sc1[c + lead], rhs1, c + lead, 1 - parity, cm1)
            else:
                cm2, part = score(sc2[c + lead - n], rhs2, c + lead - n, parity, cm2)
            parts.append(part)
            vt, g = vc[c]()
            m_c = m_u[g]
            if c >= pace:
                m_c = jnp.maximum(m_c, jnp.minimum(parts[c - pace], m_c))
            buf, row = place(c, parity)
            rows = pl.ds(pl.multiple_of(row + zero, LANES), chunk_rows[c])
            p = jnp.exp2(buf[rows, :] - m_c).astype(BF16)
            vt_aug = jnp.concatenate([vt, jnp.ones((ONES_ROWS, vt.shape[1]), BF16)], axis=0)
            pv = jnp.dot(vt_aug, p, preferred_element_type=F32)
            acc[g] = pv if acc[g] is None else acc[g] + pv
        ots = []
        for g in range(n_groups):
            dv = acc[g].shape[0] - ONES_ROWS
            l = acc[g][dv:dv + 1]
            if ex_u[g] is not None:
                l = l + jnp.exp2(ex_u[g] - m_u[g])
            ots.append(acc[g][:dv] * (1.0 / l))
        return with_extra(cm1, u1), cm2, ots

    def body(i, carry):
        m_u, cm_next, ot = carry
        u0 = UNITS_PER_BODY * i
        finish(jnp.maximum(u0 - 1, 0), ot)
        for h in range(UNITS_PER_BODY):
            if h:
                finish(u0 + h - 1, ot)
            m_u, cm_next, ot = half(u0 + h, h % 2, m_u, cm_next)
        return m_u, cm_next, ot

    assert n_units % UNITS_PER_BODY == 0
    carry = (m0, cm1, [jnp.zeros(s, F32) for s in out_shapes])
    _, _, ot_last = lax.fori_loop(0, n_units // UNITS_PER_BODY, body, carry)
    finish(last, ot_last)


def _score_scratch(chunk_rows, n_cols, lead):
    main = pltpu.VMEM((sum(chunk_rows[lead:]), n_cols), F32)
    head = pltpu.VMEM((sum(chunk_rows[:lead]), n_cols), F32)
    return [main, main, head, head]


def _split_unit(u, n_tiles):
    if isinstance(u, int):
        return u // n_tiles, u % n_tiles
    return lax.div(u, n_tiles), lax.rem(u, n_tiles)


def _rows(g, size):
    return _tile(g, size)


def _ac_chunk_rows(t_all, tq):
    wk = tq + 2 * WINDOW
    return ([KCHUNK] * (t_all // KCHUNK) + [KCHUNK]
            + [min(lo + KCHUNK, wk) - lo for lo in range(0, wk, KCHUNK)])


def _attn_ac_kernel(zero_ref, sink_ref, bias_ref, qa_ref, qc_ref, ka_ref, kc_ref, vta_ref, vtc_ref, oa_ref, oc_ref,
                    *scratch, tq, n_lat, n_ctx, s_len):
    t_all = ka_ref.shape[0]
    wk = tq + 2 * WINDOW
    spans = [(lo, min(lo + KCHUNK, wk)) for lo in range(0, wk, KCHUNK)]

    def rhs_of(q_ref, j, t):
        return _gqa_rhs(q_ref[_rows(j, 4 * HD), _tile(t, tq)], j, tq)

    def store(o_ref, j, t, ot):
        _store_heads(o_ref.at[_tile(t, tq), _rows(j, 4 * HD)], ot, tq, 4)

    def window(t):
        q0 = t * tq
        start = jnp.clip(q0 - WINDOW, 0, s_len - wk)
        return start, (q0 - start) // WINDOW

    def local(start, lo, hi):
        return pl.ds(pl.multiple_of(start + lo, LANES), hi - lo)

    def score_chunks(u):
        _, t = _split_unit(u, n_lat)
        start, variant = window(t)

        def loc(lo, hi):
            b = bias_ref[variant, lo:hi, :]
            return kc_ref[local(start, lo, hi), :], jnp.concatenate([b] * 4, axis=1), 1

        return ([functools.partial(lambda c: (ka_ref[c * KCHUNK:(c + 1) * KCHUNK, :], None, 0), c)
                 for c in range(t_all // KCHUNK)]
                + [lambda: (kc_ref[s_len:s_len + KCHUNK, :], None, 1)]
                + [functools.partial(loc, lo, hi) for lo, hi in spans])

    def value_chunks(u):
        j, t = _split_unit(u, n_lat)
        start, _ = window(t)
        return ([functools.partial(lambda c: (vta_ref[_rows(j, HD), c * KCHUNK:(c + 1) * KCHUNK], 0), c)
                 for c in range(t_all // KCHUNK)]
                + [lambda: (vtc_ref[_rows(j, HD), s_len:s_len + KCHUNK], 1)]
                + [functools.partial(lambda lo, hi: (vtc_ref[_rows(j, HD), local(start, lo, hi)], 1), lo, hi)
                   for lo, hi in spans])

    def make_rhs(u):
        j, t = _split_unit(u, n_lat)
        return [rhs_of(qa_ref, j, t), rhs_of(qc_ref, j, t)]

    def extra(u):
        j, _ = _split_unit(u, n_lat)
        return [None, _sink_row(sink_ref, j * 4, tq)]

    def finish(u, ots):
        j, t = _split_unit(u, n_lat)
        store(oa_ref, j, t, ots[0])
        store(oc_ref, j, t, ots[1])

    ctx_tiles = range(n_lat, n_lat + n_ctx)
    n = 4 * tq
    problems, targets = [], []
    for j in range(2 if n_ctx else 0):
        for q_ref, k_ref, vt_ref, o_ref, sink in ((qa_ref, ka_ref, vta_ref, oa_ref, None),
                                                  (qc_ref, kc_ref, vtc_ref, oc_ref, _sink_row(sink_ref, j * 4, tq))):
            rhs = jnp.concatenate([rhs_of(q_ref, j, t) for t in ctx_tiles], axis=1)
            sink_cols = None if sink is None else jnp.concatenate([sink] * n_ctx, axis=1)
            problems.append((k_ref[s_len:, :], vt_ref[_rows(j, HD), s_len:], rhs, sink_cols))
            targets.append((o_ref, j))
    for (o_ref, j), ot in zip(targets, _small_softmax_pv(problems)):
        for i, t in enumerate(ctx_tiles):
            store(o_ref, j, t, ot[:, i * n:(i + 1) * n])

    _pipeline(make_rhs, score_chunks, value_chunks, finish, _ac_chunk_rows(t_all, tq), [(HD, 4 * tq)] * 2, scratch,
              2 * n_lat, zero_ref[0], extra=extra, **PIPE_AC)


def _attn_b_kernel(zero_ref, lamv_ref, subln_ref, qt_ref, k_ref, vt_ref, o_ref, *scratch, tq, n_lat, n_ctx, s_len,
                   lam_init):
    lv = lamv_ref[...]
    lam = (jnp.exp(jnp.sum(lv[0:1] * lv[1:2], axis=-1, keepdims=True))
           - jnp.exp(jnp.sum(lv[2:3] * lv[3:4], axis=-1, keepdims=True)) + lam_init)
    t_all = k_ref.shape[0]
    n_chunks = t_all // KCHUNK
    hw = 2 * HD

    def rhs_of(h, t):
        qt = qt_ref[_rows(h, hw), _tile(t, tq)]
        z = jnp.zeros((HD, tq), qt.dtype)
        return jnp.concatenate([jnp.concatenate([qt[:HD], z], axis=0),
                                jnp.concatenate([z, qt[HD:]], axis=0)], axis=1)

    def store(h, t, o2):
        o = o2[:, :tq] - lam * o2[:, tq:]
        ms = jnp.mean(o * o, axis=0, keepdims=True)
        o = o * lax.rsqrt(ms + SUBLN_EPS) * subln_ref[...] * (1.0 - lam_init)
        o_ref[_tile(t, tq), _rows(h, hw)] = o.T.astype(o_ref.dtype)

    def score_chunks(u):
        h, _ = _split_unit(u, n_lat)
        return [functools.partial(lambda c: (k_ref[c * KCHUNK:(c + 1) * KCHUNK, _rows(h, hw)], None, 0), c)
                for c in range(n_chunks)]

    def value_chunks(u):
        h, _ = _split_unit(u, n_lat)
        return [functools.partial(lambda c: (vt_ref[_rows(h, hw), c * KCHUNK:(c + 1) * KCHUNK], 0), c)
                for c in range(n_chunks)]

    units = [(h, t) for h in range(4) for t in range(n_lat, n_lat + n_ctx)]
    problems = [(k_ref[s_len:, _rows(h, hw)], vt_ref[_rows(h, hw), s_len:], rhs_of(h, t), None) for h, t in units]
    for (h, t), o2 in zip(units, _small_softmax_pv(problems)):
        store(h, t, o2)

    _pipeline(lambda u: [rhs_of(*_split_unit(u, n_lat))], score_chunks, value_chunks,
              lambda u, ots: store(*_split_unit(u, n_lat), ots[0]), [KCHUNK] * n_chunks, [(hw, 2 * tq)], scratch,
              4 * n_lat, zero_ref[0], **PIPE_B)


def _window_bias(tq):
    wk = tq + 2 * WINDOW
    r = np.arange(wk)[:, None]
    c = np.arange(tq)[None, :]
    out = np.stack([np.where(np.abs(c - r + v * WINDOW) <= WINDOW, 0.0, -np.inf) for v in range(3)])
    return jnp.asarray(out, F32)


def _attn_calls(qt_all, k_all, vt_all, sink, lamv, subln_t, layer, with_ctx, s_len, lam_init,
                tq_a, tq_b):
    nb, _, t_all = qt_all.shape
    c_len = t_all - s_len
    o_rows = t_all if with_ctx else s_len
    o_shape = jax.ShapeDtypeStruct((nb, o_rows, QW), BF16)

    def steps(tq):
        n_lat = s_len // tq
        return n_lat, n_lat + (c_len // tq if with_ctx else 0)

    whole = lambda rows, cols, r, c: pl.BlockSpec((None, rows, cols), lambda b: (b, r, c))
    o_spec = pl.BlockSpec((None, o_rows, QW), lambda b: (b, 0, 0))
    zero = jnp.zeros((1,), jnp.int32)

    n_lat, n_all = steps(tq_a)
    wk = tq_a + 2 * WINDOW
    oa, oc = pl.pallas_call(
        functools.partial(_attn_ac_kernel, tq=tq_a, n_lat=n_lat, n_ctx=n_all - n_lat, s_len=s_len),
        out_shape=(o_shape, o_shape),
        grid=(nb,),
        in_specs=[
            pl.BlockSpec(memory_space=pltpu.SMEM),
            pl.BlockSpec(memory_space=pltpu.SMEM),
            pl.BlockSpec((3, wk, tq_a), lambda b: (0, 0, 0)),
            whole(QW, t_all, 0, 0), whole(QW, t_all, 2, 0),
            whole(t_all, LANES, 0, KA_TILE), whole(t_all, LANES, 0, KA_TILE + 1),
            whole(LANES, t_all, KA_TILE, 0), whole(LANES, t_all, KA_TILE + 1, 0),
        ],
        out_specs=(o_spec, o_spec),
        scratch_shapes=_score_scratch(_ac_chunk_rows(t_all, tq_a), 4 * tq_a, PIPE_AC["lead"]),
        compiler_params=_cparams(1),
        name="attn_ac",
    )(zero, sink[layer], _window_bias(tq_a), qt_all, qt_all, k_all, k_all, vt_all, vt_all)

    n_lat, n_all = steps(tq_b)
    ob = pl.pallas_call(
        functools.partial(_attn_b_kernel, tq=tq_b, n_lat=n_lat, n_ctx=n_all - n_lat, s_len=s_len,
                          lam_init=lam_init),
        out_shape=o_shape,
        grid=(nb,),
        in_specs=[
            pl.BlockSpec(memory_space=pltpu.SMEM),
            pl.BlockSpec((None, 4, HD), lambda b: (layer, 0, 0)),
            pl.BlockSpec((None, 2 * HD, tq_b), lambda b: (layer, 0, 0)),
            whole(QW, t_all, 1, 0), whole(t_all, QW, 0, 0), whole(QW, t_all, 0, 0),
        ],
        out_specs=o_spec,
        scratch_shapes=_score_scratch([KCHUNK] * (t_all // KCHUNK), 2 * tq_b, PIPE_B["lead"]),
        compiler_params=_cparams(1),
        name="attn_b",
    )(zero, lamv, subln_t, qt_all, k_all, vt_all)

    return oa, ob, oc


def _merge_kernel(*refs, sub, n_sub, ctx_step):
    x_refs = refs[:n_sub]
    rest = refs[n_sub:]
    ctx_ref = None
    if ctx_step is not None:
        ctx_ref, rest = rest[0], rest[1:]
    (oa_ref, ob_ref, oc_ref, modb_ref, modc_ref, gpre_ref, gpost_ref, wgm_ref, bmg_ref, wbr_ref, wout_ref,
     out_ref) = rest

    def gates(i):
        x, is_ctx = _sub_tile_input(x_refs, ctx_ref, i, ctx_step)
        mod = jnp.where(is_ctx, modc_ref[...], modb_ref[...])
        h = _modulated_norm(x, mod, gpre_ref[...])
        gm = jnp.dot(h.astype(BF16), wgm_ref[...], preferred_element_type=F32)
        return x, mod, gm

    def finish(i, x, mod, gm):
        rows = slice(i * sub, (i + 1) * sub)
        z = None
        for j, o_ref in enumerate((oa_ref, ob_ref, oc_ref)):
            g = gm[:, j * QW:(j + 1) * QW]
            u = (o_ref[rows, :].astype(F32) * (g * _sigmoid(g))).astype(BF16)
            p = jnp.dot(u, wbr_ref[j], preferred_element_type=F32)
            mg = _sigmoid(gm[:, N_G + j * D:N_G + (j + 1) * D] + bmg_ref[:, j * D:(j + 1) * D])
            z = mg * p if z is None else z + mg * p
        y = jnp.dot(z.astype(BF16), wout_ref[...], preferred_element_type=F32)
        ms = jnp.mean(y * y, axis=-1, keepdims=True)
        gate = mod[:, 2 * D:]
        out_ref[rows, :] = x + gate * (y * lax.rsqrt(ms + EPS) * gpost_ref[...])

    prev = gates(0)
    for i in range(1, n_sub):
        cur = gates(i)
        finish(i - 1, *prev)
        prev = cur
    finish(n_sub - 1, *prev)


def _merge_call(x_src, ctx_src, oa, ob, oc, mod4, g_pre, g_post, wgm, b_mg, wbr, wout, layer, s_len, rows_out,
                tm, sub):
    nb = x_src.shape[0]
    n_sub = tm // sub
    tok_specs, tok_args = _token_specs(x_src, ctx_src, n_sub, sub, s_len)
    ctx_step = None if ctx_src is None else rows_out // tm - 1
    tok = lambda b, t: (b, t, 0)
    lay2 = lambda b, t: (layer, 0, 0)
    const = dict(pipeline_mode=pl.Buffered(1))
    return pl.pallas_call(
        functools.partial(_merge_kernel, sub=sub, n_sub=n_sub, ctx_step=ctx_step),
        out_shape=jax.ShapeDtypeStruct((nb, rows_out, D), F32),
        grid=(nb, rows_out // tm),
        in_specs=tok_specs + [
            pl.BlockSpec((None, tm, QW), tok),
            pl.BlockSpec((None, tm, QW), tok),
            pl.BlockSpec((None, tm, QW), tok),
            pl.BlockSpec((None, None, 1, 3 * D), lambda b, t: (layer, b, 0, 0)),
            pl.BlockSpec((None, None, 1, 3 * D), lambda b, t: (layer, nb, 0, 0)),
            pl.BlockSpec((None, 1, D), lay2),
            pl.BlockSpec((None, 1, D), lay2),
            pl.BlockSpec((None, D, N_G + N_M), lay2, **const),
            pl.BlockSpec((None, 1, N_M), lay2),
            pl.BlockSpec((None, 3, QW, D), lambda b, t: (layer, 0, 0, 0), **const),
            pl.BlockSpec((None, D, D), lay2, **const),
        ],
        out_specs=pl.BlockSpec((None, tm, D), tok),
        compiler_params=_cparams(2),
        name="merge",
    )(*tok_args, oa, ob, oc, mod4, mod4, g_pre, g_post, wgm, b_mg, wbr, wout)


def _rope_tables(s_len, c_len):
    rows = s_len // GRID_W
    row = np.repeat(np.arange(rows), GRID_W).astype(np.float32)
    col = np.tile(np.arange(GRID_W), rows).astype(np.float32)
    freqs = (np.float32(ROPE_THETA) ** (-np.arange(ROPE_PAIRS, dtype=np.float32) / ROPE_PAIRS)).astype(np.float32)
    ang_r = row[:, None] * freqs
    ang_c = col[:, None] * freqs
    ang = np.concatenate([ang_r, ang_r, ang_c, ang_c], axis=-1)
    cos = np.concatenate([np.cos(ang), np.ones((c_len, HD), np.float32)], axis=0).astype(np.float32)
    sin = np.concatenate([np.sin(ang), np.zeros((c_len, HD), np.float32)], axis=0).astype(np.float32)
    first = (np.arange(HD) % 32) < 16
    sin_a = np.where(first, -sin, np.float32(0.0))
    sin_b = np.where(first, np.float32(0.0), sin)
    tile2 = lambda a: np.concatenate([a, a], axis=-1)
    cost = np.ascontiguousarray(cos.T)
    sint = np.ascontiguousarray((sin_a + sin_b).T)
    return tuple(jnp.asarray(a, F32) for a in (cost, sint, tile2(cos), tile2(sin_a), tile2(sin_b)))


def _cols(w, names):
    return jnp.concatenate([w[..., _IN[n][0]:_IN[n][1]] for n in names], axis=-1)


def kernel(x, c, ctx, c_ctx, w_ada, b_ada, g_pre, g_post, w_in, q_norm, k_norm, lam_q1, lam_k1, lam_q2,
           lam_k2, subln, sink, w_br_a, w_br_b, w_br_c, w_mg, b_mg, w_out):
    nb, s_len, _ = x.shape
    c_len = ctx.shape[1]
    depth = w_in.shape[0]
    sub = 256
    tm_all = 3 * sub
    tm_lat = 4 * sub
    tq_a, tq_b = 128, 256
    assert (s_len + c_len) % tm_all == 0 and s_len % tm_lat == 0 and s_len % sub == 0
    assert c_len == KCHUNK == sub

    wp = _cols(w_in, ("qa", "qb", "qc", "kb", "ka", "kc", "vb", "va", "vc")).astype(BF16)
    wgm = jnp.concatenate([_cols(w_in, ("ga", "gb", "gc")), w_mg], axis=-1).astype(BF16)
    wbr = jnp.stack([w_br_a, w_br_b, w_br_c], axis=1).astype(BF16)
    wout = w_out.astype(BF16)

    tabs = _rope_tables(s_len, c_len)
    gq_t = jnp.broadcast_to(q_norm[:, :, None], (depth, HD, sub))
    gk_n = jnp.concatenate([k_norm, k_norm], axis=-1)[:, None, :]
    lamv = jnp.stack([lam_q1, lam_k1, lam_q2, lam_k2], axis=1)
    subln_t = jnp.broadcast_to(subln[:, :, None], (depth, 2 * HD, tq_b))
    g_pre3 = g_pre[:, None, :]
    g_post3 = g_post[:, None, :]
    b_mg3 = b_mg[:, None, :]

    rows = ((nb + 1 + 7) // 8) * 8
    sc_in = jnp.concatenate([c, c_ctx[None, :], jnp.zeros((rows - nb - 1, D), F32)], axis=0)
    mod = _ada_call(sc_in, w_ada, b_ada)
    mod4 = mod[:, :, None, :]

    t_all = s_len + c_len
    x_src, ctx_src = x, (ctx, 0)
    for layer in range(depth):
        last = layer == depth - 1
        lam_init = 0.8 - 0.6 * math.exp(-0.3 * layer)
        qt_all, k_all, vt_all = _proj_call(x_src, ctx_src, mod4, g_pre3, wp, tabs, gq_t, gk_n, layer, s_len,
                                           t_all, tm_all, sub)
        oa, ob, oc = _attn_calls(qt_all, k_all, vt_all, sink, lamv, subln_t, layer, not last, s_len,
                                 lam_init, tq_a, tq_b)
        rows_out, tm = (s_len, tm_lat) if last else (t_all, tm_all)
        xs = _merge_call(x_src, None if last else ctx_src, oa, ob, oc, mod4, g_pre3, g_post3, wgm, b_mg3, wbr,
                         wout, layer, s_len, rows_out, tm, sub)
        x_src, ctx_src = xs, (xs, s_len // sub)
    return xs
```

```python
import functools
import math

import jax
import jax.numpy as jnp
import numpy as np
from jax import lax
from jax.experimental import pallas as pl
from jax.experimental.pallas import tpu as pltpu

F32 = jnp.float32
BF16 = jnp.bfloat16

D = 1024
HD = 64
GRID_W = 64
WINDOW = 128
ROPE_THETA = 10000.0
ROPE_PAIRS = HD // 4
EPS = 1e-6
SUBLN_EPS = 1e-5
ATTN_SCALE = HD ** -0.5
LOG2E = math.log2(math.e)

QW = 512
N_Q = 3 * QW
N_K = 512 + 128 + 128
N_V = 512 + 128 + 128
KA_TILE = 4
N_P = N_Q + N_K + N_V
N_G = 3 * QW
N_M = 3 * D

LANES = 128
VMEM_LIMIT = 56 * 1024 * 1024

_IN = dict(qa=(0, 512), ka=(512, 640), va=(640, 768), ga=(768, 1280),
           qb=(1280, 1792), kb=(1792, 2304), vb=(2304, 2816), gb=(2816, 3328),
           qc=(3328, 3840), kc=(3840, 3968), vc=(3968, 4096), gc=(4096, 4608))


def _sigmoid(v):
    return 1.0 / (1.0 + jnp.exp(-v))


def _cparams(n_axes):
    return pltpu.CompilerParams(dimension_semantics=("arbitrary",) * n_axes,
                                vmem_limit_bytes=VMEM_LIMIT)


def _ada_kernel(sc_ref, w_ref, b_ref, o_ref):
    v = sc_ref[...]
    s = (v * _sigmoid(v)).astype(BF16)
    o_ref[...] = jnp.dot(s, w_ref[...].astype(BF16), preferred_element_type=F32) + b_ref[...]


def _ada_call(sc_in, w_ada, b_ada):
    depth = w_ada.shape[0]
    rows = sc_in.shape[0]
    nblk = 3
    return pl.pallas_call(
        _ada_kernel,
        out_shape=jax.ShapeDtypeStruct((depth, rows, 3 * D), F32),
        grid=(depth, nblk),
        in_specs=[
            pl.BlockSpec((rows, D), lambda l, n: (0, 0)),
            pl.BlockSpec((None, D, D), lambda l, n: (l, 0, n)),
            pl.BlockSpec((None, 1, D), lambda l, n: (l, 0, n)),
        ],
        out_specs=pl.BlockSpec((None, rows, D), lambda l, n: (l, 0, n)),
        compiler_params=_cparams(2),
        name="adaln",
    )(sc_in, w_ada, b_ada.reshape(depth, 1, 3 * D))


def _modulated_norm(x, mod, gpre):
    shift = mod[:, :D]
    scale = mod[:, D:2 * D]
    ms = jnp.mean(x * x, axis=-1, keepdims=True)
    return x * lax.rsqrt(ms + EPS) * gpre * (1.0 + scale) + shift


def _sub_tile_input(x_refs, ctx_ref, i, ctx_step):
    x = x_refs[i][...]
    if ctx_ref is None or i != len(x_refs) - 1:
        return x, False
    is_ctx = pl.program_id(1) == ctx_step
    return jnp.where(is_ctx, ctx_ref[...], x), is_ctx


def _proj_kernel(*refs, sub, n_sub, ctx_step):
    x_refs, ctx_ref = refs[:n_sub], refs[n_sub]
    (modb_ref, modc_ref, gpre_ref, w_ref, cost_ref, sint_ref, cosn_ref, sina_ref, sinb_ref, gq_ref, gk_ref,
     qt_ref, k_ref, vt_ref) = refs[n_sub + 1:]

    def project(i):
        x, is_ctx = _sub_tile_input(x_refs, ctx_ref, i, ctx_step)
        mod = jnp.where(is_ctx, modc_ref[...], modb_ref[...])
        h = _modulated_norm(x, mod, gpre_ref[...])
        return jnp.dot(h.astype(BF16), w_ref[...], preferred_element_type=F32)

    def finish(i, y):
        rows = slice(i * sub, (i + 1) * sub)
        q3 = y[:, :N_Q].T.reshape(N_Q // HD, HD, sub)
        qa = q3[:8]
        ss = jnp.sum(qa * qa, axis=1, keepdims=True)
        qa = qa * lax.rsqrt(ss * (1.0 / HD) + EPS) * gq_ref[...][None]
        q3 = jnp.concatenate([qa, q3[8:]], axis=0)
        rot = jnp.concatenate([q3[:, 16:32], q3[:, 0:16], q3[:, 48:64], q3[:, 32:48]], axis=1)
        q3 = (q3 * cost_ref[:, rows][None] + rot * sint_ref[:, rows][None]) * (ATTN_SCALE * LOG2E)
        qt_ref[:, rows] = q3.reshape(N_Q, sub).astype(BF16)

        ka = y[:, N_Q + KA_TILE * LANES:N_Q + (KA_TILE + 1) * LANES]
        lane = lax.broadcasted_iota(jnp.int32, (1, LANES), 1)
        lo = lane < HD
        sq = ka * ka
        s_lo = jnp.sum(jnp.where(lo, sq, 0.0), axis=-1, keepdims=True)
        s_hi = jnp.sum(jnp.where(lo, 0.0, sq), axis=-1, keepdims=True)
        r = jnp.where(lo, lax.rsqrt(s_lo * (1.0 / HD) + EPS), lax.rsqrt(s_hi * (1.0 / HD) + EPS))
        ka = ka * r * gk_ref[...]
        cosn = cosn_ref[rows, :]
        sina = sina_ref[rows, :]
        sinb = sinb_ref[rows, :]
        for j in range(N_K // LANES):
            t = ka if j == KA_TILE else y[:, N_Q + j * LANES:N_Q + (j + 1) * LANES]
            t = t * cosn + pltpu.roll(t, LANES - 16, 1) * sina + pltpu.roll(t, 16, 1) * sinb
            k_ref[rows, j * LANES:(j + 1) * LANES] = t.astype(BF16)

        vt_ref[:, rows] = y[:, N_Q + N_K:].T.astype(BF16)

    y_prev = project(0)
    for i in range(1, n_sub):
        y = project(i)
        finish(i - 1, y_prev)
        y_prev = y
    finish(n_sub - 1, y_prev)


def _token_specs(x_src, ctx_src, n_sub, sub, s_len):
    last = s_len // sub - 1
    specs = [pl.BlockSpec((None, sub, D), functools.partial(lambda i, b, t: (b, jnp.minimum(n_sub * t + i, last), 0), i))
             for i in range(n_sub)]
    operands = [x_src] * n_sub
    if ctx_src is not None:
        arr, blk = ctx_src
        specs.append(pl.BlockSpec((None, sub, D), lambda b, t: (b, blk, 0)))
        operands.append(arr)
    return specs, operands


def _proj_call(x_src, ctx_src, mod4, g_pre, wp, tabs, gq_t, gk_n, layer, s_len, t_all, tm, sub):
    nb = x_src.shape[0]
    cost, sint, cosn, sina, sinb = tabs
    n_sub = tm // sub
    tok_specs, tok_args = _token_specs(x_src, ctx_src, n_sub, sub, s_len)
    return pl.pallas_call(
        functools.partial(_proj_kernel, sub=sub, n_sub=n_sub, ctx_step=t_all // tm - 1),
        out_shape=(jax.ShapeDtypeStruct((nb, N_Q, t_all), BF16),
                   jax.ShapeDtypeStruct((nb, t_all, N_K), BF16),
                   jax.ShapeDtypeStruct((nb, N_V, t_all), BF16)),
        grid=(nb, t_all // tm),
        in_specs=tok_specs + [
            pl.BlockSpec((None, None, 1, 3 * D), lambda b, t: (layer, b, 0, 0)),
            pl.BlockSpec((None, None, 1, 3 * D), lambda b, t: (layer, nb, 0, 0)),
            pl.BlockSpec((None, 1, D), lambda b, t: (layer, 0, 0)),
            pl.BlockSpec((None, D, N_P), lambda b, t: (layer, 0, 0)),
            pl.BlockSpec((HD, tm), lambda b, t: (0, t)),
            pl.BlockSpec((HD, tm), lambda b, t: (0, t)),
            pl.BlockSpec((tm, LANES), lambda b, t: (t, 0)),
            pl.BlockSpec((tm, LANES), lambda b, t: (t, 0)),
            pl.BlockSpec((tm, LANES), lambda b, t: (t, 0)),
            pl.BlockSpec((None, HD, sub), lambda b, t: (layer, 0, 0)),
            pl.BlockSpec((None, 1, LANES), lambda b, t: (layer, 0, 0)),
        ],
        out_specs=(pl.BlockSpec((None, N_Q, tm), lambda b, t: (b, 0, t)),
                   pl.BlockSpec((None, tm, N_K), lambda b, t: (b, t, 0)),
                   pl.BlockSpec((None, N_V, tm), lambda b, t: (b, 0, t))),
        compiler_params=_cparams(2),
        name="proj",
    )(*tok_args, mod4, mod4, g_pre, wp, cost, sint, cosn, sina, sinb, gq_t, gk_n)


KCHUNK = 256
ONES_ROWS = 16
UNITS_PER_BODY = 4
PIPE_AC = dict(lead=3, pace=6)
PIPE_B = dict(lead=3, pace=2)


def _small_softmax_pv(problems):
    ss = [jnp.dot(k, rhs, preferred_element_type=F32) for k, _, rhs, _ in problems]
    ms = [jnp.max(s, axis=0, keepdims=True) for s in ss]
    ms = [m if pr[3] is None else jnp.maximum(m, pr[3]) for m, pr in zip(ms, problems)]
    ps = [jnp.exp2(s - m).astype(BF16) for s, m in zip(ss, ms)]
    outs = []
    for p, m, (_, vt, _, extra) in zip(ps, ms, problems):
        dv = vt.shape[0]
        vt_aug = jnp.concatenate([vt, jnp.ones((ONES_ROWS, vt.shape[1]), BF16)], axis=0)
        pv = jnp.dot(vt_aug, p, preferred_element_type=F32)
        l = pv[dv:dv + 1]
        if extra is not None:
            l = l + jnp.exp2(extra - m)
        outs.append(pv[:dv] * (1.0 / l))
    return outs


def _gqa_rhs(qt, j, tq):
    z = jnp.zeros((HD, tq), qt.dtype)
    first = j == 0
    cols = []
    for g in range(4):
        qg = qt[g * HD:(g + 1) * HD, :]
        cols.append(jnp.concatenate([jnp.where(first, qg, z), jnp.where(first, z, qg)], axis=0))
    return jnp.concatenate(cols, axis=1)


def _store_heads(o_ref, ot, tq, nheads):
    o = jnp.concatenate([ot[:, g * tq:(g + 1) * tq] for g in range(nheads)], axis=0)
    o_ref[...] = o.T.astype(o_ref.dtype)


def _sink_row(sink_ref, base, tq):
    blk = lax.broadcasted_iota(jnp.int32, (1, 4 * tq), 1) // tq
    row = jnp.zeros((1, 4 * tq), F32)
    for g in range(4):
        row = jnp.where(blk == g, sink_ref[base + g], row)
    return row * LOG2E


def _tile(t, size):
    if isinstance(t, int):
        return pl.ds(t * size, size)
    return pl.ds(pl.multiple_of(t * size, size), size)


def _pipeline(make_rhs, score_chunks, value_chunks, finish, chunk_rows, out_shapes, scratch, n_units, zero,
              extra=None, *, lead, pace):
    main, head = scratch[:2], scratch[2:]
    n = len(chunk_rows)
    n_groups = len(out_shapes)
    last = n_units - 1
    assert n >= 2 * lead and n_units % 2 == 0
    ex = (lambda u: [None] * n_groups) if extra is None else extra

    def place(c, parity):
        if c < lead:
            return head[parity], sum(chunk_rows[:c])
        return main[parity], sum(chunk_rows[lead:c])

    def score(thunk, rhs, c, parity, cm):
        k, bias, g = thunk()
        s = jnp.dot(k, rhs[g], preferred_element_type=F32)
        if bias is not None:
            s = s + bias
        buf, row = place(c, parity)
        buf[row:row + chunk_rows[c], :] = s
        part = jnp.max(s, axis=0, keepdims=True)
        cm = list(cm)
        cm[g] = jnp.maximum(cm[g], part)
        return cm, part

    def with_extra(cm, u):
        return [m if e is None else jnp.maximum(m, e) for m, e in zip(cm, ex(u))]

    neg = [jnp.full((1, s[1]), -jnp.inf, F32) for s in out_shapes]

    rhs, sc, cm = make_rhs(0), score_chunks(0), neg
    for c in range(n):
        cm, _ = score(sc[c], rhs, c, 0, cm)
    m0 = with_extra(cm, 0)
    rhs, sc, cm1 = make_rhs(1), score_chunks(1), neg
    for c in range(lead):
        cm1, _ = score(sc[c], rhs, c, 1, cm1)

    def half(u, parity, m_u, cm1):
        u1 = jnp.minimum(u + 1, last)
        u2 = jnp.minimum(u + 2, last)
        rhs1, sc1 = make_rhs(u1), score_chunks(u1)
        rhs2, sc2 = make_rhs(u2), score_chunks(u2)
        vc, ex_u = value_chunks(u), ex(u)
        cm2 = neg
        acc = [None] * n_groups
        parts = []
        for c in range(n):
            if c + lead < n:
                cm1, part = score(sc1[c + lead], rhs1, c + lead, 1 - parity, cm1)
            else:
                cm2, part = score(sc2[c + lead - n], rhs2, c + lead - n, parity, cm2)
            parts.append(part)
            vt, g = vc[c]()
            m_c = m_u[g]
            if c >= pace:
                m_c = jnp.maximum(m_c, jnp.minimum(parts[c - pace], m_c))
            buf, row = place(c, parity)
            rows = pl.ds(pl.multiple_of(row + zero, LANES), chunk_rows[c])
            p = jnp.exp2(buf[rows, :] - m_c).astype(BF16)
            vt_aug = jnp.concatenate([vt, jnp.ones((ONES_ROWS, vt.shape[1]), BF16)], axis=0)
            pv = jnp.dot(vt_aug, p, preferred_element_type=F32)
            acc[g] = pv if acc[g] is None else acc[g] + pv
        ots = []
        for g in range(n_groups):
            dv = acc[g].shape[0] - ONES_ROWS
            l = acc[g][dv:dv + 1]
            if ex_u[g] is not None:
                l = l + jnp.exp2(ex_u[g] - m_u[g])
            ots.append(acc[g][:dv] * (1.0 / l))
        return with_extra(cm1, u1), cm2, ots

    def body(i, carry):
        m_u, cm_next, ot = carry
        u0 = UNITS_PER_BODY * i
        finish(jnp.maximum(u0 - 1, 0), ot)
        for h in range(UNITS_PER_BODY):
            if h:
                finish(u0 + h - 1, ot)
            m_u, cm_next, ot = half(u0 + h, h % 2, m_u, cm_next)
        return m_u, cm_next, ot

    assert n_units % UNITS_PER_BODY == 0
    carry = (m0, cm1, [jnp.zeros(s, F32) for s in out_shapes])
    _, _, ot_last = lax.fori_loop(0, n_units // UNITS_PER_BODY, body, carry)
    finish(last, ot_last)


def _score_scratch(chunk_rows, n_cols, lead):
    main = pltpu.VMEM((sum(chunk_rows[lead:]), n_cols), F32)
    head = pltpu.VMEM((sum(chunk_rows[:lead]), n_cols), F32)
    return [main, main, head, head]


def _split_unit(u, n_tiles):
    if isinstance(u, int):
        return u // n_tiles, u % n_tiles
    return lax.div(u, n_tiles), lax.rem(u, n_tiles)


def _rows(g, size):
    return _tile(g, size)


def _ac_chunk_rows(t_all, tq):
    wk = tq + 2 * WINDOW
    return ([KCHUNK] * (t_all // KCHUNK) + [KCHUNK]
            + [min(lo + KCHUNK, wk) - lo for lo in range(0, wk, KCHUNK)])


def _attn_ac_kernel(zero_ref, sink_ref, bias_ref, qa_ref, qc_ref, ka_ref, kc_ref, vta_ref, vtc_ref, oa_ref, oc_ref,
                    *scratch, tq, n_lat, n_ctx, s_len):
    t_all = ka_ref.shape[0]
    wk = tq + 2 * WINDOW
    spans = [(lo, min(lo + KCHUNK, wk)) for lo in range(0, wk, KCHUNK)]

    def rhs_of(q_ref, j, t):
        return _gqa_rhs(q_ref[_rows(j, 4 * HD), _tile(t, tq)], j, tq)

    def store(o_ref, j, t, ot):
        _store_heads(o_ref.at[_tile(t, tq), _rows(j, 4 * HD)], ot, tq, 4)

    def window(t):
        q0 = t * tq
        start = jnp.clip(q0 - WINDOW, 0, s_len - wk)
        return start, (q0 - start) // WINDOW

    def local(start, lo, hi):
        return pl.ds(pl.multiple_of(start + lo, LANES), hi - lo)

    def score_chunks(u):
        _, t = _split_unit(u, n_lat)
        start, variant = window(t)

        def loc(lo, hi):
            b = bias_ref[variant, lo:hi, :]
            return kc_ref[local(start, lo, hi), :], jnp.concatenate([b] * 4, axis=1), 1

        return ([functools.partial(lambda c: (ka_ref[c * KCHUNK:(c + 1) * KCHUNK, :], None, 0), c)
                 for c in range(t_all // KCHUNK)]
                + [lambda: (kc_ref[s_len:s_len + KCHUNK, :], None, 1)]
                + [functools.partial(loc, lo, hi) for lo, hi in spans])

    def value_chunks(u):
        j, t = _split_unit(u, n_lat)
        start, _ = window(t)
        return ([functools.partial(lambda c: (vta_ref[_rows(j, HD), c * KCHUNK:(c + 1) * KCHUNK], 0), c)
                 for c in range(t_all // KCHUNK)]
                + [lambda: (vtc_ref[_rows(j, HD), s_len:s_len + KCHUNK], 1)]
                + [functools.partial(lambda lo, hi: (vtc_ref[_rows(j, HD), local(start, lo, hi)], 1), lo, hi)
                   for lo, hi in spans])

    def make_rhs(u):
        j, t = _split_unit(u, n_lat)
        return [rhs_of(qa_ref, j, t), rhs_of(qc_ref, j, t)]

    def extra(u):
        j, _ = _split_unit(u, n_lat)
        return [None, _sink_row(sink_ref, j * 4, tq)]

    def finish(u, ots):
        j, t = _split_unit(u, n_lat)
        store(oa_ref, j, t, ots[0])
        store(oc_ref, j, t, ots[1])

    ctx_tiles = range(n_lat, n_lat + n_ctx)
    n = 4 * tq
    problems, targets = [], []
    for j in range(2 if n_ctx else 0):
        for q_ref, k_ref, vt_ref, o_ref, sink in ((qa_ref, ka_ref, vta_ref, oa_ref, None),
                                                  (qc_ref, kc_ref, vtc_ref, oc_ref, _sink_row(sink_ref, j * 4, tq))):
            rhs = jnp.concatenate([rhs_of(q_ref, j, t) for t in ctx_tiles], axis=1)
            sink_cols = None if sink is None else jnp.concatenate([sink] * n_ctx, axis=1)
            problems.append((k_ref[s_len:, :], vt_ref[_rows(j, HD), s_len:], rhs, sink_cols))
            targets.append((o_ref, j))
    for (o_ref, j), ot in zip(targets, _small_softmax_pv(problems)):
        for i, t in enumerate(ctx_tiles):
            store(o_ref, j, t, ot[:, i * n:(i + 1) * n])

    _pipeline(make_rhs, score_chunks, value_chunks, finish, _ac_chunk_rows(t_all, tq), [(HD, 4 * tq)] * 2, scratch,
              2 * n_lat, zero_ref[0], extra=extra, **PIPE_AC)


def _attn_b_kernel(zero_ref, lamv_ref, subln_ref, qt_ref, k_ref, vt_ref, o_ref, *scratch, tq, n_lat, n_ctx, s_len,
                   lam_init):
    lv = lamv_ref[...]
    lam = (jnp.exp(jnp.sum(lv[0:1] * lv[1:2], axis=-1, keepdims=True))
           - jnp.exp(jnp.sum(lv[2:3] * lv[3:4], axis=-1, keepdims=True)) + lam_init)
    t_all = k_ref.shape[0]
    n_chunks = t_all // KCHUNK
    hw = 2 * HD

    def rhs_of(h, t):
        qt = qt_ref[_rows(h, hw), _tile(t, tq)]
        z = jnp.zeros((HD, tq), qt.dtype)
        return jnp.concatenate([jnp.concatenate([qt[:HD], z], axis=0),
                                jnp.concatenate([z, qt[HD:]], axis=0)], axis=1)

    def store(h, t, o2):
        o = o2[:, :tq] - lam * o2[:, tq:]
        ms = jnp.mean(o * o, axis=0, keepdims=True)
        o = o * lax.rsqrt(ms + SUBLN_EPS) * subln_ref[...] * (1.0 - lam_init)
        o_ref[_tile(t, tq), _rows(h, hw)] = o.T.astype(o_ref.dtype)

    def score_chunks(u):
        h, _ = _split_unit(u, n_lat)
        return [functools.partial(lambda c: (k_ref[c * KCHUNK:(c + 1) * KCHUNK, _rows(h, hw)], None, 0), c)
                for c in range(n_chunks)]

    def value_chunks(u):
        h, _ = _split_unit(u, n_lat)
        return [functools.partial(lambda c: (vt_ref[_rows(h, hw), c * KCHUNK:(c + 1) * KCHUNK], 0), c)
                for c in range(n_chunks)]

    units = [(h, t) for h in range(4) for t in range(n_lat, n_lat + n_ctx)]
    problems = [(k_ref[s_len:, _rows(h, hw)], vt_ref[_rows(h, hw), s_len:], rhs_of(h, t), None) for h, t in units]
    for (h, t), o2 in zip(units, _small_softmax_pv(problems)):
        store(h, t, o2)

    _pipeline(lambda u: [rhs_of(*_split_unit(u, n_lat))], score_chunks, value_chunks,
              lambda u, ots: store(*_split_unit(u, n_lat), ots[0]), [KCHUNK] * n_chunks, [(hw, 2 * tq)], scratch,
              4 * n_lat, zero_ref[0], **PIPE_B)


def _window_bias(tq):
    wk = tq + 2 * WINDOW
    r = np.arange(wk)[:, None]
    c = np.arange(tq)[None, :]
    out = np.stack([np.where(np.abs(c - r + v * WINDOW) <= WINDOW, 0.0, -np.inf) for v in range(3)])
    return jnp.asarray(out, F32)


def _attn_calls(qt_all, k_all, vt_all, sink, lamv, subln_t, layer, with_ctx, s_len, lam_init,
                tq_a, tq_b):
    nb, _, t_all = qt_all.shape
    c_len = t_all - s_len
    o_rows = t_all if with_ctx else s_len
    o_shape = jax.ShapeDtypeStruct((nb, o_rows, QW), BF16)

    def steps(tq):
        n_lat = s_len // tq
        return n_lat, n_lat + (c_len // tq if with_ctx else 0)

    whole = lambda rows, cols, r, c: pl.BlockSpec((None, rows, cols), lambda b: (b, r, c))
    o_spec = pl.BlockSpec((None, o_rows, QW), lambda b: (b, 0, 0))
    zero = jnp.zeros((1,), jnp.int32)

    n_lat, n_all = steps(tq_a)
    wk = tq_a + 2 * WINDOW
    oa, oc = pl.pallas_call(
        functools.partial(_attn_ac_kernel, tq=tq_a, n_lat=n_lat, n_ctx=n_all - n_lat, s_len=s_len),
        out_shape=(o_shape, o_shape),
        grid=(nb,),
        in_specs=[
            pl.BlockSpec(memory_space=pltpu.SMEM),
            pl.BlockSpec(memory_space=pltpu.SMEM),
            pl.BlockSpec((3, wk, tq_a), lambda b: (0, 0, 0)),
            whole(QW, t_all, 0, 0), whole(QW, t_all, 2, 0),
            whole(t_all, LANES, 0, KA_TILE), whole(t_all, LANES, 0, KA_TILE + 1),
            whole(LANES, t_all, KA_TILE, 0), whole(LANES, t_all, KA_TILE + 1, 0),
        ],
        out_specs=(o_spec, o_spec),
        scratch_shapes=_score_scratch(_ac_chunk_rows(t_all, tq_a), 4 * tq_a, PIPE_AC["lead"]),
        compiler_params=_cparams(1),
        name="attn_ac",
    )(zero, sink[layer], _window_bias(tq_a), qt_all, qt_all, k_all, k_all, vt_all, vt_all)

    n_lat, n_all = steps(tq_b)
    ob = pl.pallas_call(
        functools.partial(_attn_b_kernel, tq=tq_b, n_lat=n_lat, n_ctx=n_all - n_lat, s_len=s_len,
                          lam_init=lam_init),
        out_shape=o_shape,
        grid=(nb,),
        in_specs=[
            pl.BlockSpec(memory_space=pltpu.SMEM),
            pl.BlockSpec((None, 4, HD), lambda b: (layer, 0, 0)),
            pl.BlockSpec((None, 2 * HD, tq_b), lambda b: (layer, 0, 0)),
            whole(QW, t_all, 1, 0), whole(t_all, QW, 0, 0), whole(QW, t_all, 0, 0),
        ],
        out_specs=o_spec,
        scratch_shapes=_score_scratch([KCHUNK] * (t_all // KCHUNK), 2 * tq_b, PIPE_B["lead"]),
        compiler_params=_cparams(1),
        name="attn_b",
    )(zero, lamv, subln_t, qt_all, k_all, vt_all)

    return oa, ob, oc


def _merge_kernel(*refs, sub, n_sub, ctx_step):
    x_refs = refs[:n_sub]
    rest = refs[n_sub:]
    ctx_ref = None
    if ctx_step is not None:
        ctx_ref, rest = rest[0], rest[1:]
    (oa_ref, ob_ref, oc_ref, modb_ref, modc_ref, gpre_ref, gpost_ref, wgm_ref, bmg_ref, wbr_ref, wout_ref,
     out_ref) = rest

    def gates(i):
        x, is_ctx = _sub_tile_input(x_refs, ctx_ref, i, ctx_step)
        mod = jnp.where(is_ctx, modc_ref[...], modb_ref[...])
        h = _modulated_norm(x, mod, gpre_ref[...])
        gm = jnp.dot(h.astype(BF16), wgm_ref[...], preferred_element_type=F32)
        return x, mod, gm

    def finish(i, x, mod, gm):
        rows = slice(i * sub, (i + 1) * sub)
        z = None
        for j, o_ref in enumerate((oa_ref, ob_ref, oc_ref)):
            g = gm[:, j * QW:(j + 1) * QW]
            u = (o_ref[rows, :].astype(F32) * (g * _sigmoid(g))).astype(BF16)
            p = jnp.dot(u, wbr_ref[j], preferred_element_type=F32)
            mg = _sigmoid(gm[:, N_G + j * D:N_G + (j + 1) * D] + bmg_ref[:, j * D:(j + 1) * D])
            z = mg * p if z is None else z + mg * p
        y = jnp.dot(z.astype(BF16), wout_ref[...], preferred_element_type=F32)
        ms = jnp.mean(y * y, axis=-1, keepdims=True)
        gate = mod[:, 2 * D:]
        out_ref[rows, :] = x + gate * (y * lax.rsqrt(ms + EPS) * gpost_ref[...])

    prev = gates(0)
    for i in range(1, n_sub):
        cur = gates(i)
        finish(i - 1, *prev)
        prev = cur
    finish(n_sub - 1, *prev)


def _merge_call(x_src, ctx_src, oa, ob, oc, mod4, g_pre, g_post, wgm, b_mg, wbr, wout, layer, s_len, rows_out,
                tm, sub):
    nb = x_src.shape[0]
    n_sub = tm // sub
    tok_specs, tok_args = _token_specs(x_src, ctx_src, n_sub, sub, s_len)
    ctx_step = None if ctx_src is None else rows_out // tm - 1
    tok = lambda b, t: (b, t, 0)
    lay2 = lambda b, t: (layer, 0, 0)
    const = dict(pipeline_mode=pl.Buffered(1))
    return pl.pallas_call(
        functools.partial(_merge_kernel, sub=sub, n_sub=n_sub, ctx_step=ctx_step),
        out_shape=jax.ShapeDtypeStruct((nb, rows_out, D), F32),
        grid=(nb, rows_out // tm),
        in_specs=tok_specs + [
            pl.BlockSpec((None, tm, QW), tok),
            pl.BlockSpec((None, tm, QW), tok),
            pl.BlockSpec((None, tm, QW), tok),
            pl.BlockSpec((None, None, 1, 3 * D), lambda b, t: (layer, b, 0, 0)),
            pl.BlockSpec((None, None, 1, 3 * D), lambda b, t: (layer, nb, 0, 0)),
            pl.BlockSpec((None, 1, D), lay2),
            pl.BlockSpec((None, 1, D), lay2),
            pl.BlockSpec((None, D, N_G + N_M), lay2, **const),
            pl.BlockSpec((None, 1, N_M), lay2),
            pl.BlockSpec((None, 3, QW, D), lambda b, t: (layer, 0, 0, 0), **const),
            pl.BlockSpec((None, D, D), lay2, **const),
        ],
        out_specs=pl.BlockSpec((None, tm, D), tok),
        compiler_params=_cparams(2),
        name="merge",
    )(*tok_args, oa, ob, oc, mod4, mod4, g_pre, g_post, wgm, b_mg, wbr, wout)


def _rope_tables(s_len, c_len):
    rows = s_len // GRID_W
    row = np.repeat(np.arange(rows), GRID_W).astype(np.float32)
    col = np.tile(np.arange(GRID_W), rows).astype(np.float32)
    freqs = (np.float32(ROPE_THETA) ** (-np.arange(ROPE_PAIRS, dtype=np.float32) / ROPE_PAIRS)).astype(np.float32)
    ang_r = row[:, None] * freqs
    ang_c = col[:, None] * freqs
    ang = np.concatenate([ang_r, ang_r, ang_c, ang_c], axis=-1)
    cos = np.concatenate([np.cos(ang), np.ones((c_len, HD), np.float32)], axis=0).astype(np.float32)
    sin = np.concatenate([np.sin(ang), np.zeros((c_len, HD), np.float32)], axis=0).astype(np.float32)
    first = (np.arange(HD) % 32) < 16
    sin_a = np.where(first, -sin, np.float32(0.0))
    sin_b = np.where(first, np.float32(0.0), sin)
    tile2 = lambda a: np.concatenate([a, a], axis=-1)
    cost = np.ascontiguousarray(cos.T)
    sint = np.ascontiguousarray((sin_a + sin_b).T)
    return tuple(jnp.asarray(a, F32) for a in (cost, sint, tile2(cos), tile2(sin_a), tile2(sin_b)))


def _cols(w, names):
    return jnp.concatenate([w[..., _IN[n][0]:_IN[n][1]] for n in names], axis=-1)


def kernel(x, c, ctx, c_ctx, w_ada, b_ada, g_pre, g_post, w_in, q_norm, k_norm, lam_q1, lam_k1, lam_q2,
           lam_k2, subln, sink, w_br_a, w_br_b, w_br_c, w_mg, b_mg, w_out):
    nb, s_len, _ = x.shape
    c_len = ctx.shape[1]
    depth = w_in.shape[0]
    sub = 256
    tm_all = 3 * sub
    tm_lat = 4 * sub
    tq_a, tq_b = 128, 256
    assert (s_len + c_len) % tm_all == 0 and s_len % tm_lat == 0 and s_len % sub == 0
    assert c_len == KCHUNK == sub

    wp = _cols(w_in, ("qa", "qb", "qc", "kb", "ka", "kc", "vb", "va", "vc")).astype(BF16)
    wgm = jnp.concatenate([_cols(w_in, ("ga", "gb", "gc")), w_mg], axis=-1).astype(BF16)
    wbr = jnp.stack([w_br_a, w_br_b, w_br_c], axis=1).astype(BF16)
    wout = w_out.astype(BF16)

    tabs = _rope_tables(s_len, c_len)
    gq_t = jnp.broadcast_to(q_norm[:, :, None], (depth, HD, sub))
    gk_n = jnp.concatenate([k_norm, k_norm], axis=-1)[:, None, :]
    lamv = jnp.stack([lam_q1, lam_k1, lam_q2, lam_k2], axis=1)
    subln_t = jnp.broadcast_to(subln[:, :, None], (depth, 2 * HD, tq_b))
    g_pre3 = g_pre[:, None, :]
    g_post3 = g_post[:, None, :]
    b_mg3 = b_mg[:, None, :]

    rows = ((nb + 1 + 7) // 8) * 8
    sc_in = jnp.concatenate([c, c_ctx[None, :], jnp.zeros((rows - nb - 1, D), F32)], axis=0)
    mod = _ada_call(sc_in, w_ada, b_ada)
    mod4 = mod[:, :, None, :]

    t_all = s_len + c_len
    x_src, ctx_src = x, (ctx, 0)
    for layer in range(depth):
        last = layer == depth - 1
        lam_init = 0.8 - 0.6 * math.exp(-0.3 * layer)
        qt_all, k_all, vt_all = _proj_call(x_src, ctx_src, mod4, g_pre3, wp, tabs, gq_t, gk_n, layer, s_len,
                                           t_all, tm_all, sub)
        oa, ob, oc = _attn_calls(qt_all, k_all, vt_all, sink, lamv, subln_t, layer, not last, s_len,
                                 lam_init, tq_a, tq_b)
        rows_out, tm = (s_len, tm_lat) if last else (t_all, tm_all)
        xs = _merge_call(x_src, None if last else ctx_src, oa, ob, oc, mod4, g_pre3, g_post3, wgm, b_mg3, wbr,
                         wout, layer, s_len, rows_out, tm, sub)
        x_src, ctx_src = xs, (xs, s_len // sub)
    return xs
```

```python
import functools
import math

import jax
import jax.numpy as jnp
import numpy as np
from jax import lax
from jax.experimental import pallas as pl
from jax.experimental.pallas import tpu as pltpu

F32 = jnp.float32
BF16 = jnp.bfloat16

D = 1024
HD = 64
GRID_W = 64
WINDOW = 128
ROPE_THETA = 10000.0
ROPE_PAIRS = HD // 4
EPS = 1e-6
SUBLN_EPS = 1e-5
ATTN_SCALE = HD ** -0.5
LOG2E = math.log2(math.e)

QW = 512
N_Q = 3 * QW
N_K = 512 + 128 + 128
N_V = 512 + 128 + 128
KA_TILE = 4
N_P = N_Q + N_K + N_V
N_G = 3 * QW
N_M = 3 * D

LANES = 128
VMEM_LIMIT = 56 * 1024 * 1024

_IN = dict(qa=(0, 512), ka=(512, 640), va=(640, 768), ga=(768, 1280),
           qb=(1280, 1792), kb=(1792, 2304), vb=(2304, 2816), gb=(2816, 3328),
           qc=(3328, 3840), kc=(3840, 3968), vc=(3968, 4096), gc=(4096, 4608))


def _sigmoid(v):
    return 1.0 / (1.0 + jnp.exp(-v))


def _cparams(n_axes):
    return pltpu.CompilerParams(dimension_semantics=("arbitrary",) * n_axes,
                                vmem_limit_bytes=VMEM_LIMIT)


def _ada_kernel(sc_ref, w_ref, b_ref, o_ref):
    v = sc_ref[...]
    s = (v * _sigmoid(v)).astype(BF16)
    o_ref[...] = jnp.dot(s, w_ref[...].astype(BF16), preferred_element_type=F32) + b_ref[...]


def _ada_call(sc_in, w_ada, b_ada):
    depth = w_ada.shape[0]
    rows = sc_in.shape[0]
    nblk = 3
    return pl.pallas_call(
        _ada_kernel,
        out_shape=jax.ShapeDtypeStruct((depth, rows, 3 * D), F32),
        grid=(depth, nblk),
        in_specs=[
            pl.BlockSpec((rows, D), lambda l, n: (0, 0)),
            pl.BlockSpec((None, D, D), lambda l, n: (l, 0, n)),
            pl.BlockSpec((None, 1, D), lambda l, n: (l, 0, n)),
        ],
        out_specs=pl.BlockSpec((None, rows, D), lambda l, n: (l, 0, n)),
        compiler_params=_cparams(2),
        name="adaln",
    )(sc_in, w_ada, b_ada.reshape(depth, 1, 3 * D))


def _modulated_norm(x, mod, gpre):
    shift = mod[:, :D]
    scale = mod[:, D:2 * D]
    ms = jnp.mean(x * x, axis=-1, keepdims=True)
    return x * lax.rsqrt(ms + EPS) * gpre * (1.0 + scale) + shift


def _sub_tile_input(x_refs, ctx_ref, i, ctx_step):
    x = x_refs[i][...]
    if ctx_ref is None or i != len(x_refs) - 1:
        return x, False
    is_ctx = pl.program_id(1) == ctx_step
    return jnp.where(is_ctx, ctx_ref[...], x), is_ctx


def _proj_kernel(*refs, sub, n_sub, ctx_step):
    x_refs, ctx_ref = refs[:n_sub], refs[n_sub]
    (modb_ref, modc_ref, gpre_ref, w_ref, cost_ref, sint_ref, cosn_ref, sina_ref, sinb_ref, gq_ref, gk_ref,
     qt_ref, k_ref, vt_ref) = refs[n_sub + 1:]

    def project(i):
        x, is_ctx = _sub_tile_input(x_refs, ctx_ref, i, ctx_step)
        mod = jnp.where(is_ctx, modc_ref[...], modb_ref[...])
        h = _modulated_norm(x, mod, gpre_ref[...])
        return jnp.dot(h.astype(BF16), w_ref[...], preferred_element_type=F32)

    def finish(i, y):
        rows = slice(i * sub, (i + 1) * sub)
        q3 = y[:, :N_Q].T.reshape(N_Q // HD, HD, sub)
        qa = q3[:8]
        ss = jnp.sum(qa * qa, axis=1, keepdims=True)
        qa = qa * lax.rsqrt(ss * (1.0 / HD) + EPS) * gq_ref[...][None]
        q3 = jnp.concatenate([qa, q3[8:]], axis=0)
        rot = jnp.concatenate([q3[:, 16:32], q3[:, 0:16], q3[:, 48:64], q3[:, 32:48]], axis=1)
        q3 = (q3 * cost_ref[:, rows][None] + rot * sint_ref[:, rows][None]) * (ATTN_SCALE * LOG2E)
        qt_ref[:, rows] = q3.reshape(N_Q, sub).astype(BF16)

        ka = y[:, N_Q + KA_TILE * LANES:N_Q + (KA_TILE + 1) * LANES]
        lane = lax.broadcasted_iota(jnp.int32, (1, LANES), 1)
        lo = lane < HD
        sq = ka * ka
        s_lo = jnp.sum(jnp.where(lo, sq, 0.0), axis=-1, keepdims=True)
        s_hi = jnp.sum(jnp.where(lo, 0.0, sq), axis=-1, keepdims=True)
        r = jnp.where(lo, lax.rsqrt(s_lo * (1.0 / HD) + EPS), lax.rsqrt(s_hi * (1.0 / HD) + EPS))
        ka = ka * r * gk_ref[...]
        cosn = cosn_ref[rows, :]
        sina = sina_ref[rows, :]
        sinb = sinb_ref[rows, :]
        for j in range(N_K // LANES):
            t = ka if j == KA_TILE else y[:, N_Q + j * LANES:N_Q + (j + 1) * LANES]
            t = t * cosn + pltpu.roll(t, LANES - 16, 1) * sina + pltpu.roll(t, 16, 1) * sinb
            k_ref[rows, j * LANES:(j + 1) * LANES] = t.astype(BF16)

        vt_ref[:, rows] = y[:, N_Q + N_K:].T.astype(BF16)

    y_prev = project(0)
    for i in range(1, n_sub):
        y = project(i)
        finish(i - 1, y_prev)
        y_prev = y
    finish(n_sub - 1, y_prev)


def _token_specs(x_src, ctx_src, n_sub, sub, s_len):
    last = s_len // sub - 1
    specs = [pl.BlockSpec((None, sub, D), functools.partial(lambda i, b, t: (b, jnp.minimum(n_sub * t + i, last), 0), i))
             for i in range(n_sub)]
    operands = [x_src] * n_sub
    if ctx_src is not None:
        arr, blk = ctx_src
        specs.append(pl.BlockSpec((None, sub, D), lambda b, t: (b, blk, 0)))
        operands.append(arr)
    return specs, operands


def _proj_call(x_src, ctx_src, mod4, g_pre, wp, tabs, gq_t, gk_n, layer, s_len, t_all, tm, sub):
    nb = x_src.shape[0]
    cost, sint, cosn, sina, sinb = tabs
    n_sub = tm // sub
    tok_specs, tok_args = _token_specs(x_src, ctx_src, n_sub, sub, s_len)
    return pl.pallas_call(
        functools.partial(_proj_kernel, sub=sub, n_sub=n_sub, ctx_step=t_all // tm - 1),
        out_shape=(jax.ShapeDtypeStruct((nb, N_Q, t_all), BF16),
                   jax.ShapeDtypeStruct((nb, t_all, N_K), BF16),
                   jax.ShapeDtypeStruct((nb, N_V, t_all), BF16)),
        grid=(nb, t_all // tm),
        in_specs=tok_specs + [
            pl.BlockSpec((None, None, 1, 3 * D), lambda b, t: (layer, b, 0, 0)),
            pl.BlockSpec((None, None, 1, 3 * D), lambda b, t: (layer, nb, 0, 0)),
            pl.BlockSpec((None, 1, D), lambda b, t: (layer, 0, 0)),
            pl.BlockSpec((None, D, N_P), lambda b, t: (layer, 0, 0)),
            pl.BlockSpec((HD, tm), lambda b, t: (0, t)),
            pl.BlockSpec((HD, tm), lambda b, t: (0, t)),
            pl.BlockSpec((tm, LANES), lambda b, t: (t, 0)),
            pl.BlockSpec((tm, LANES), lambda b, t: (t, 0)),
            pl.BlockSpec((tm, LANES), lambda b, t: (t, 0)),
            pl.BlockSpec((None, HD, sub), lambda b, t: (layer, 0, 0)),
            pl.BlockSpec((None, 1, LANES), lambda b, t: (layer, 0, 0)),
        ],
        out_specs=(pl.BlockSpec((None, N_Q, tm), lambda b, t: (b, 0, t)),
                   pl.BlockSpec((None, tm, N_K), lambda b, t: (b, t, 0)),
                   pl.BlockSpec((None, N_V, tm), lambda b, t: (b, 0, t))),
        compiler_params=_cparams(2),
        name="proj",
    )(*tok_args, mod4, mod4, g_pre, wp, cost, sint, cosn, sina, sinb, gq_t, gk_n)


KCHUNK = 256
ONES_ROWS = 16
UNITS_PER_BODY = 4
PIPE_AC = dict(lead=3, pace=4)
PIPE_B = dict(lead=4, pace=2)


def _small_softmax_pv(problems):
    ss = [jnp.dot(k, rhs, preferred_element_type=F32) for k, _, rhs, _ in problems]
    ms = [jnp.max(s, axis=0, keepdims=True) for s in ss]
    ms = [m if pr[3] is None else jnp.maximum(m, pr[3]) for m, pr in zip(ms, problems)]
    ps = [jnp.exp2(s - m).astype(BF16) for s, m in zip(ss, ms)]
    outs = []
    for p, m, (_, vt, _, extra) in zip(ps, ms, problems):
        dv = vt.shape[0]
        vt_aug = jnp.concatenate([vt, jnp.ones((ONES_ROWS, vt.shape[1]), BF16)], axis=0)
        pv = jnp.dot(vt_aug, p, preferred_element_type=F32)
        l = pv[dv:dv + 1]
        if extra is not None:
            l = l + jnp.exp2(extra - m)
        outs.append(pv[:dv] * (1.0 / l))
    return outs


def _gqa_rhs(qt, j, tq):
    z = jnp.zeros((HD, tq), qt.dtype)
    first = j == 0
    cols = []
    for g in range(4):
        qg = qt[g * HD:(g + 1) * HD, :]
        cols.append(jnp.concatenate([jnp.where(first, qg, z), jnp.where(first, z, qg)], axis=0))
    return jnp.concatenate(cols, axis=1)


def _store_heads(o_ref, ot, tq, nheads):
    o = jnp.concatenate([ot[:, g * tq:(g + 1) * tq] for g in range(nheads)], axis=0)
    o_ref[...] = o.T.astype(o_ref.dtype)


def _sink_row(sink_ref, base, tq):
    blk = lax.broadcasted_iota(jnp.int32, (1, 4 * tq), 1) // tq
    row = jnp.zeros((1, 4 * tq), F32)
    for g in range(4):
        row = jnp.where(blk == g, sink_ref[base + g], row)
    return row * LOG2E


def _tile(t, size):
    if isinstance(t, int):
        return pl.ds(t * size, size)
    return pl.ds(pl.multiple_of(t * size, size), size)


def _pipeline(make_rhs, score_chunks, value_chunks, finish, chunk_rows, out_shapes, scratch, n_units, zero,
              extra=None, *, lead, pace):
    main, head = scratch[:2], scratch[2:]
    n = len(chunk_rows)
    n_groups = len(out_shapes)
    last = n_units - 1
    assert n >= 2 * lead and n_units % 2 == 0
    ex = (lambda u: [None] * n_groups) if extra is None else extra

    def place(c, parity):
        if c < lead:
            return head[parity], sum(chunk_rows[:c])
        return main[parity], sum(chunk_rows[lead:c])

    def score(thunk, rhs, c, parity, cm):
        k, bias, g = thunk()
        s = jnp.dot(k, rhs[g], preferred_element_type=F32)
        if bias is not None:
            s = s + bias
        buf, row = place(c, parity)
        buf[row:row + chunk_rows[c], :] = s
        part = jnp.max(s, axis=0, keepdims=True)
        cm = list(cm)
        cm[g] = jnp.maximum(cm[g], part)
        return cm, part

    def with_extra(cm, u):
        return [m if e is None else jnp.maximum(m, e) for m, e in zip(cm, ex(u))]

    neg = [jnp.full((1, s[1]), -jnp.inf, F32) for s in out_shapes]

    rhs, sc, cm = make_rhs(0), score_chunks(0), neg
    for c in range(n):
        cm, _ = score(sc[c], rhs, c, 0, cm)
    m0 = with_extra(cm, 0)
    rhs, sc, cm1 = make_rhs(1), score_chunks(1), neg
    for c in range(lead):
        cm1, _ = score(sc[c], rhs, c, 1, cm1)

    def half(u, parity, m_u, cm1):
        u1 = jnp.minimum(u + 1, last)
        u2 = jnp.minimum(u + 2, last)
        rhs1, sc1 = make_rhs(u1), score_chunks(u1)
        rhs2, sc2 = make_rhs(u2), score_chunks(u2)
        vc, ex_u = value_chunks(u), ex(u)
        cm2 = neg
        acc = [None] * n_groups
        parts = []
        for c in range(n):
            if c + lead < n:
                cm1, part = score(sc1[c + lead], rhs1, c + lead, 1 - parity, cm1)
            else:
                cm2, part = score(sc2[c + lead - n], rhs2, c + lead - n, parity, cm2)
            parts.append(part)
            vt, g = vc[c]()
            m_c = m_u[g]
            if c >= pace:
                m_c = jnp.maximum(m_c, jnp.minimum(parts[c - pace], m_c))
            buf, row = place(c, parity)
            rows = pl.ds(pl.multiple_of(row + zero, LANES), chunk_rows[c])
            p = jnp.exp2(buf[rows, :] - m_c).astype(BF16)
            vt_aug = jnp.concatenate([vt, jnp.ones((ONES_ROWS, vt.shape[1]), BF16)], axis=0)
            pv = jnp.dot(vt_aug, p, preferred_element_type=F32)
            acc[g] = pv if acc[g] is None else acc[g] + pv
        ots = []
        for g in range(n_groups):
            dv = acc[g].shape[0] - ONES_ROWS
            l = acc[g][dv:dv + 1]
            if ex_u[g] is not None:
                l = l + jnp.exp2(ex_u[g] - m_u[g])
            ots.append(acc[g][:dv] * (1.0 / l))
        return with_extra(cm1, u1), cm2, ots

    def body(i, carry):
        m_u, cm_next, ot = carry
        u0 = UNITS_PER_BODY * i
        finish(jnp.maximum(u0 - 1, 0), ot)
        for h in range(UNITS_PER_BODY):
            if h:
                finish(u0 + h - 1, ot)
            m_u, cm_next, ot = half(u0 + h, h % 2, m_u, cm_next)
        return m_u, cm_next, ot

    assert n_units % UNITS_PER_BODY == 0
    carry = (m0, cm1, [jnp.zeros(s, F32) for s in out_shapes])
    _, _, ot_last = lax.fori_loop(0, n_units // UNITS_PER_BODY, body, carry)
    finish(last, ot_last)


def _score_scratch(chunk_rows, n_cols, lead):
    main = pltpu.VMEM((sum(chunk_rows[lead:]), n_cols), F32)
    head = pltpu.VMEM((sum(chunk_rows[:lead]), n_cols), F32)
    return [main, main, head, head]


def _split_unit(u, n_tiles):
    if isinstance(u, int):
        return u // n_tiles, u % n_tiles
    return lax.div(u, n_tiles), lax.rem(u, n_tiles)


def _rows(g, size):
    return _tile(g, size)


def _ac_chunk_rows(t_all, tq):
    wk = tq + 2 * WINDOW
    return ([KCHUNK] * (t_all // KCHUNK) + [KCHUNK]
            + [min(lo + KCHUNK, wk) - lo for lo in range(0, wk, KCHUNK)])


def _attn_ac_kernel(zero_ref, sink_ref, bias_ref, qa_ref, qc_ref, ka_ref, kc_ref, vta_ref, vtc_ref, oa_ref, oc_ref,
                    *scratch, tq, n_lat, n_ctx, s_len):
    t_all = ka_ref.shape[0]
    wk = tq + 2 * WINDOW
    spans = [(lo, min(lo + KCHUNK, wk)) for lo in range(0, wk, KCHUNK)]

    def rhs_of(q_ref, j, t):
        return _gqa_rhs(q_ref[_rows(j, 4 * HD), _tile(t, tq)], j, tq)

    def store(o_ref, j, t, ot):
        _store_heads(o_ref.at[_tile(t, tq), _rows(j, 4 * HD)], ot, tq, 4)

    def window(t):
        q0 = t * tq
        start = jnp.clip(q0 - WINDOW, 0, s_len - wk)
        return start, (q0 - start) // WINDOW

    def local(start, lo, hi):
        return pl.ds(pl.multiple_of(start + lo, LANES), hi - lo)

    def score_chunks(u):
        _, t = _split_unit(u, n_lat)
        start, variant = window(t)

        def loc(lo, hi):
            b = bias_ref[variant, lo:hi, :]
            return kc_ref[local(start, lo, hi), :], jnp.concatenate([b] * 4, axis=1), 1

        return ([functools.partial(lambda c: (ka_ref[c * KCHUNK:(c + 1) * KCHUNK, :], None, 0), c)
                 for c in range(t_all // KCHUNK)]
                + [lambda: (kc_ref[s_len:s_len + KCHUNK, :], None, 1)]
                + [functools.partial(loc, lo, hi) for lo, hi in spans])

    def value_chunks(u):
        j, t = _split_unit(u, n_lat)
        start, _ = window(t)
        return ([functools.partial(lambda c: (vta_ref[_rows(j, HD), c * KCHUNK:(c + 1) * KCHUNK], 0), c)
                 for c in range(t_all // KCHUNK)]
                + [lambda: (vtc_ref[_rows(j, HD), s_len:s_len + KCHUNK], 1)]
                + [functools.partial(lambda lo, hi: (vtc_ref[_rows(j, HD), local(start, lo, hi)], 1), lo, hi)
                   for lo, hi in spans])

    def make_rhs(u):
        j, t = _split_unit(u, n_lat)
        return [rhs_of(qa_ref, j, t), rhs_of(qc_ref, j, t)]

    def extra(u):
        j, _ = _split_unit(u, n_lat)
        return [None, _sink_row(sink_ref, j * 4, tq)]

    def finish(u, ots):
        j, t = _split_unit(u, n_lat)
        store(oa_ref, j, t, ots[0])
        store(oc_ref, j, t, ots[1])

    ctx_tiles = range(n_lat, n_lat + n_ctx)
    n = 4 * tq
    problems, targets = [], []
    for j in range(2 if n_ctx else 0):
        for q_ref, k_ref, vt_ref, o_ref, sink in ((qa_ref, ka_ref, vta_ref, oa_ref, None),
                                                  (qc_ref, kc_ref, vtc_ref, oc_ref, _sink_row(sink_ref, j * 4, tq))):
            rhs = jnp.concatenate([rhs_of(q_ref, j, t) for t in ctx_tiles], axis=1)
            sink_cols = None if sink is None else jnp.concatenate([sink] * n_ctx, axis=1)
            problems.append((k_ref[s_len:, :], vt_ref[_rows(j, HD), s_len:], rhs, sink_cols))
            targets.append((o_ref, j))
    for (o_ref, j), ot in zip(targets, _small_softmax_pv(problems)):
        for i, t in enumerate(ctx_tiles):
            store(o_ref, j, t, ot[:, i * n:(i + 1) * n])

    _pipeline(make_rhs, score_chunks, value_chunks, finish, _ac_chunk_rows(t_all, tq), [(HD, 4 * tq)] * 2, scratch,
              2 * n_lat, zero_ref[0], extra=extra, **PIPE_AC)


def _attn_b_kernel(zero_ref, lamv_ref, subln_ref, qt_ref, k_ref, vt_ref, o_ref, *scratch, tq, n_lat, n_ctx, s_len,
                   lam_init):
    lv = lamv_ref[...]
    lam = (jnp.exp(jnp.sum(lv[0:1] * lv[1:2], axis=-1, keepdims=True))
           - jnp.exp(jnp.sum(lv[2:3] * lv[3:4], axis=-1, keepdims=True)) + lam_init)
    t_all = k_ref.shape[0]
    n_chunks = t_all // KCHUNK
    hw = 2 * HD

    def rhs_of(h, t):
        qt = qt_ref[_rows(h, hw), _tile(t, tq)]
        z = jnp.zeros((HD, tq), qt.dtype)
        return jnp.concatenate([jnp.concatenate([qt[:HD], z], axis=0),
                                jnp.concatenate([z, qt[HD:]], axis=0)], axis=1)

    def store(h, t, o2):
        o = o2[:, :tq] - lam * o2[:, tq:]
        ms = jnp.mean(o * o, axis=0, keepdims=True)
        o = o * lax.rsqrt(ms + SUBLN_EPS) * subln_ref[...] * (1.0 - lam_init)
        o_ref[_tile(t, tq), _rows(h, hw)] = o.T.astype(o_ref.dtype)

    def score_chunks(u):
        h, _ = _split_unit(u, n_lat)
        return [functools.partial(lambda c: (k_ref[c * KCHUNK:(c + 1) * KCHUNK, _rows(h, hw)], None, 0), c)
                for c in range(n_chunks)]

    def value_chunks(u):
        h, _ = _split_unit(u, n_lat)
        return [functools.partial(lambda c: (vt_ref[_rows(h, hw), c * KCHUNK:(c + 1) * KCHUNK], 0), c)
                for c in range(n_chunks)]

    units = [(h, t) for h in range(4) for t in range(n_lat, n_lat + n_ctx)]
    problems = [(k_ref[s_len:, _rows(h, hw)], vt_ref[_rows(h, hw), s_len:], rhs_of(h, t), None) for h, t in units]
    for (h, t), o2 in zip(units, _small_softmax_pv(problems)):
        store(h, t, o2)

    _pipeline(lambda u: [rhs_of(*_split_unit(u, n_lat))], score_chunks, value_chunks,
              lambda u, ots: store(*_split_unit(u, n_lat), ots[0]), [KCHUNK] * n_chunks, [(hw, 2 * tq)], scratch,
              4 * n_lat, zero_ref[0], **PIPE_B)


def _window_bias(tq):
    wk = tq + 2 * WINDOW
    r = np.arange(wk)[:, None]
    c = np.arange(tq)[None, :]
    out = np.stack([np.where(np.abs(c - r + v * WINDOW) <= WINDOW, 0.0, -np.inf) for v in range(3)])
    return jnp.asarray(out, F32)


def _attn_calls(qt_all, k_all, vt_all, sink, lamv, subln_t, layer, with_ctx, s_len, lam_init,
                tq_a, tq_b):
    nb, _, t_all = qt_all.shape
    c_len = t_all - s_len
    o_rows = t_all if with_ctx else s_len
    o_shape = jax.ShapeDtypeStruct((nb, o_rows, QW), BF16)

    def steps(tq):
        n_lat = s_len // tq
        return n_lat, n_lat + (c_len // tq if with_ctx else 0)

    whole = lambda rows, cols, r, c: pl.BlockSpec((None, rows, cols), lambda b: (b, r, c))
    o_spec = pl.BlockSpec((None, o_rows, QW), lambda b: (b, 0, 0))
    zero = jnp.zeros((1,), jnp.int32)

    n_lat, n_all = steps(tq_a)
    wk = tq_a + 2 * WINDOW
    oa, oc = pl.pallas_call(
        functools.partial(_attn_ac_kernel, tq=tq_a, n_lat=n_lat, n_ctx=n_all - n_lat, s_len=s_len),
        out_shape=(o_shape, o_shape),
        grid=(nb,),
        in_specs=[
            pl.BlockSpec(memory_space=pltpu.SMEM),
            pl.BlockSpec(memory_space=pltpu.SMEM),
            pl.BlockSpec((3, wk, tq_a), lambda b: (0, 0, 0)),
            whole(QW, t_all, 0, 0), whole(QW, t_all, 2, 0),
            whole(t_all, LANES, 0, KA_TILE), whole(t_all, LANES, 0, KA_TILE + 1),
            whole(LANES, t_all, KA_TILE, 0), whole(LANES, t_all, KA_TILE + 1, 0),
        ],
        out_specs=(o_spec, o_spec),
        scratch_shapes=_score_scratch(_ac_chunk_rows(t_all, tq_a), 4 * tq_a, PIPE_AC["lead"]),
        compiler_params=_cparams(1),
        name="attn_ac",
    )(zero, sink[layer], _window_bias(tq_a), qt_all, qt_all, k_all, k_all, vt_all, vt_all)

    n_lat, n_all = steps(tq_b)
    ob = pl.pallas_call(
        functools.partial(_attn_b_kernel, tq=tq_b, n_lat=n_lat, n_ctx=n_all - n_lat, s_len=s_len,
                          lam_init=lam_init),
        out_shape=o_shape,
        grid=(nb,),
        in_specs=[
            pl.BlockSpec(memory_space=pltpu.SMEM),
            pl.BlockSpec((None, 4, HD), lambda b: (layer, 0, 0)),
            pl.BlockSpec((None, 2 * HD, tq_b), lambda b: (layer, 0, 0)),
            whole(QW, t_all, 1, 0), whole(t_all, QW, 0, 0), whole(QW, t_all, 0, 0),
        ],
        out_specs=o_spec,
        scratch_shapes=_score_scratch([KCHUNK] * (t_all // KCHUNK), 2 * tq_b, PIPE_B["lead"]),
        compiler_params=_cparams(1),
        name="attn_b",
    )(zero, lamv, subln_t, qt_all, k_all, vt_all)

    return oa, ob, oc


def _merge_kernel(*refs, sub, n_sub, ctx_step):
    x_refs = refs[:n_sub]
    rest = refs[n_sub:]
    ctx_ref = None
    if ctx_step is not None:
        ctx_ref, rest = rest[0], rest[1:]
    (oa_ref, ob_ref, oc_ref, modb_ref, modc_ref, gpre_ref, gpost_ref, wgm_ref, bmg_ref, wbr_ref, wout_ref,
     out_ref) = rest

    def gates(i):
        x, is_ctx = _sub_tile_input(x_refs, ctx_ref, i, ctx_step)
        mod = jnp.where(is_ctx, modc_ref[...], modb_ref[...])
        h = _modulated_norm(x, mod, gpre_ref[...])
        gm = jnp.dot(h.astype(BF16), wgm_ref[...], preferred_element_type=F32)
        return x, mod, gm

    def finish(i, x, mod, gm):
        rows = slice(i * sub, (i + 1) * sub)
        z = None
        for j, o_ref in enumerate((oa_ref, ob_ref, oc_ref)):
            g = gm[:, j * QW:(j + 1) * QW]
            u = (o_ref[rows, :].astype(F32) * (g * _sigmoid(g))).astype(BF16)
            p = jnp.dot(u, wbr_ref[j], preferred_element_type=F32)
            mg = _sigmoid(gm[:, N_G + j * D:N_G + (j + 1) * D] + bmg_ref[:, j * D:(j + 1) * D])
            z = mg * p if z is None else z + mg * p
        y = jnp.dot(z.astype(BF16), wout_ref[...], preferred_element_type=F32)
        ms = jnp.mean(y * y, axis=-1, keepdims=True)
        gate = mod[:, 2 * D:]
        out_ref[rows, :] = x + gate * (y * lax.rsqrt(ms + EPS) * gpost_ref[...])

    prev = gates(0)
    for i in range(1, n_sub):
        cur = gates(i)
        finish(i - 1, *prev)
        prev = cur
    finish(n_sub - 1, *prev)


def _merge_call(x_src, ctx_src, oa, ob, oc, mod4, g_pre, g_post, wgm, b_mg, wbr, wout, layer, s_len, rows_out,
                tm, sub):
    nb = x_src.shape[0]
    n_sub = tm // sub
    tok_specs, tok_args = _token_specs(x_src, ctx_src, n_sub, sub, s_len)
    ctx_step = None if ctx_src is None else rows_out // tm - 1
    tok = lambda b, t: (b, t, 0)
    lay2 = lambda b, t: (layer, 0, 0)
    const = dict(pipeline_mode=pl.Buffered(1))
    return pl.pallas_call(
        functools.partial(_merge_kernel, sub=sub, n_sub=n_sub, ctx_step=ctx_step),
        out_shape=jax.ShapeDtypeStruct((nb, rows_out, D), F32),
        grid=(nb, rows_out // tm),
        in_specs=tok_specs + [
            pl.BlockSpec((None, tm, QW), tok),
            pl.BlockSpec((None, tm, QW), tok),
            pl.BlockSpec((None, tm, QW), tok),
            pl.BlockSpec((None, None, 1, 3 * D), lambda b, t: (layer, b, 0, 0)),
            pl.BlockSpec((None, None, 1, 3 * D), lambda b, t: (layer, nb, 0, 0)),
            pl.BlockSpec((None, 1, D), lay2),
            pl.BlockSpec((None, 1, D), lay2),
            pl.BlockSpec((None, D, N_G + N_M), lay2, **const),
            pl.BlockSpec((None, 1, N_M), lay2),
            pl.BlockSpec((None, 3, QW, D), lambda b, t: (layer, 0, 0, 0), **const),
            pl.BlockSpec((None, D, D), lay2, **const),
        ],
        out_specs=pl.BlockSpec((None, tm, D), tok),
        compiler_params=_cparams(2),
        name="merge",
    )(*tok_args, oa, ob, oc, mod4, mod4, g_pre, g_post, wgm, b_mg, wbr, wout)


def _rope_tables(s_len, c_len):
    rows = s_len // GRID_W
    row = np.repeat(np.arange(rows), GRID_W).astype(np.float32)
    col = np.tile(np.arange(GRID_W), rows).astype(np.float32)
    freqs = (np.float32(ROPE_THETA) ** (-np.arange(ROPE_PAIRS, dtype=np.float32) / ROPE_PAIRS)).astype(np.float32)
    ang_r = row[:, None] * freqs
    ang_c = col[:, None] * freqs
    ang = np.concatenate([ang_r, ang_r, ang_c, ang_c], axis=-1)
    cos = np.concatenate([np.cos(ang), np.ones((c_len, HD), np.float32)], axis=0).astype(np.float32)
    sin = np.concatenate([np.sin(ang), np.zeros((c_len, HD), np.float32)], axis=0).astype(np.float32)
    first = (np.arange(HD) % 32) < 16
    sin_a = np.where(first, -sin, np.float32(0.0))
    sin_b = np.where(first, np.float32(0.0), sin)
    tile2 = lambda a: np.concatenate([a, a], axis=-1)
    cost = np.ascontiguousarray(cos.T)
    sint = np.ascontiguousarray((sin_a + sin_b).T)
    return tuple(jnp.asarray(a, F32) for a in (cost, sint, tile2(cos), tile2(sin_a), tile2(sin_b)))


def _cols(w, names):
    return jnp.concatenate([w[..., _IN[n][0]:_IN[n][1]] for n in names], axis=-1)


def kernel(x, c, ctx, c_ctx, w_ada, b_ada, g_pre, g_post, w_in, q_norm, k_norm, lam_q1, lam_k1, lam_q2,
           lam_k2, subln, sink, w_br_a, w_br_b, w_br_c, w_mg, b_mg, w_out):
    nb, s_len, _ = x.shape
    c_len = ctx.shape[1]
    depth = w_in.shape[0]
    sub = 256
    tm_all = 3 * sub
    tm_lat = 4 * sub
    tq_a, tq_b = 128, 256
    assert (s_len + c_len) % tm_all == 0 and s_len % tm_lat == 0 and s_len % sub == 0
    assert c_len == KCHUNK == sub

    wp = _cols(w_in, ("qa", "qb", "qc", "kb", "ka", "kc", "vb", "va", "vc")).astype(BF16)
    wgm = jnp.concatenate([_cols(w_in, ("ga", "gb", "gc")), w_mg], axis=-1).astype(BF16)
    wbr = jnp.stack([w_br_a, w_br_b, w_br_c], axis=1).astype(BF16)
    wout = w_out.astype(BF16)

    tabs = _rope_tables(s_len, c_len)
    gq_t = jnp.broadcast_to(q_norm[:, :, None], (depth, HD, sub))
    gk_n = jnp.concatenate([k_norm, k_norm], axis=-1)[:, None, :]
    lamv = jnp.stack([lam_q1, lam_k1, lam_q2, lam_k2], axis=1)
    subln_t = jnp.broadcast_to(subln[:, :, None], (depth, 2 * HD, tq_b))
    g_pre3 = g_pre[:, None, :]
    g_post3 = g_post[:, None, :]
    b_mg3 = b_mg[:, None, :]

    rows = ((nb + 1 + 7) // 8) * 8
    sc_in = jnp.concatenate([c, c_ctx[None, :], jnp.zeros((rows - nb - 1, D), F32)], axis=0)
    mod = _ada_call(sc_in, w_ada, b_ada)
    mod4 = mod[:, :, None, :]

    t_all = s_len + c_len
    x_src, ctx_src = x, (ctx, 0)
    for layer in range(depth):
        last = layer == depth - 1
        lam_init = 0.8 - 0.6 * math.exp(-0.3 * layer)
        qt_all, k_all, vt_all = _proj_call(x_src, ctx_src, mod4, g_pre3, wp, tabs, gq_t, gk_n, layer, s_len,
                                           t_all, tm_all, sub)
        oa, ob, oc = _attn_calls(qt_all, k_all, vt_all, sink, lamv, subln_t, layer, not last, s_len,
                                 lam_init, tq_a, tq_b)
        rows_out, tm = (s_len, tm_lat) if last else (t_all, tm_all)
        xs = _merge_call(x_src, None if last else ctx_src, oa, ob, oc, mod4, g_pre3, g_post3, wgm, b_mg3, wbr,
                         wout, layer, s_len, rows_out, tm, sub)
        x_src, ctx_src = xs, (xs, s_len // sub)
    return xs
```

```python
import functools
import math

import jax
import jax.numpy as jnp
import numpy as np
from jax import lax
from jax.experimental import pallas as pl
from jax.experimental.pallas import tpu as pltpu

F32 = jnp.float32
BF16 = jnp.bfloat16

D = 1024
HD = 64
GRID_W = 64
WINDOW = 128
ROPE_THETA = 10000.0
ROPE_PAIRS = HD // 4
EPS = 1e-6
SUBLN_EPS = 1e-5
ATTN_SCALE = HD ** -0.5
LOG2E = math.log2(math.e)

QW = 512
N_Q = 3 * QW
N_K = 512 + 128 + 128
N_V = 512 + 128 + 128
KA_TILE = 4
N_P = N_Q + N_K + N_V
N_G = 3 * QW
N_M = 3 * D

LANES = 128
VMEM_LIMIT = 56 * 1024 * 1024

_IN = dict(qa=(0, 512), ka=(512, 640), va=(640, 768), ga=(768, 1280),
           qb=(1280, 1792), kb=(1792, 2304), vb=(2304, 2816), gb=(2816, 3328),
           qc=(3328, 3840), kc=(3840, 3968), vc=(3968, 4096), gc=(4096, 4608))


def _sigmoid(v):
    return 1.0 / (1.0 + jnp.exp(-v))


def _cparams(n_axes):
    return pltpu.CompilerParams(dimension_semantics=("arbitrary",) * n_axes,
                                vmem_limit_bytes=VMEM_LIMIT)


def _ada_kernel(sc_ref, w_ref, b_ref, o_ref):
    v = sc_ref[...]
    s = (v * _sigmoid(v)).astype(BF16)
    o_ref[...] = jnp.dot(s, w_ref[...].astype(BF16), preferred_element_type=F32) + b_ref[...]


def _ada_call(sc_in, w_ada, b_ada):
    depth = w_ada.shape[0]
    rows = sc_in.shape[0]
    nblk = 3
    return pl.pallas_call(
        _ada_kernel,
        out_shape=jax.ShapeDtypeStruct((depth, rows, 3 * D), F32),
        grid=(depth, nblk),
        in_specs=[
            pl.BlockSpec((rows, D), lambda l, n: (0, 0)),
            pl.BlockSpec((None, D, D), lambda l, n: (l, 0, n)),
            pl.BlockSpec((None, 1, D), lambda l, n: (l, 0, n)),
        ],
        out_specs=pl.BlockSpec((None, rows, D), lambda l, n: (l, 0, n)),
        compiler_params=_cparams(2),
        name="adaln",
    )(sc_in, w_ada, b_ada.reshape(depth, 1, 3 * D))


def _modulated_norm(x, mod, gpre):
    shift = mod[:, :D]
    scale = mod[:, D:2 * D]
    ms = jnp.mean(x * x, axis=-1, keepdims=True)
    return x * lax.rsqrt(ms + EPS) * gpre * (1.0 + scale) + shift


def _sub_tile_input(x_refs, ctx_ref, i, ctx_step):
    x = x_refs[i][...]
    if ctx_ref is None or i != len(x_refs) - 1:
        return x, False
    is_ctx = pl.program_id(1) == ctx_step
    return jnp.where(is_ctx, ctx_ref[...], x), is_ctx


def _proj_kernel(*refs, sub, n_sub, ctx_step):
    x_refs, ctx_ref = refs[:n_sub], refs[n_sub]
    (modb_ref, modc_ref, gpre_ref, w_ref, cost_ref, sint_ref, cosn_ref, sina_ref, sinb_ref, gq_ref, gk_ref,
     qt_ref, k_ref, vt_ref) = refs[n_sub + 1:]

    def project(i):
        x, is_ctx = _sub_tile_input(x_refs, ctx_ref, i, ctx_step)
        mod = jnp.where(is_ctx, modc_ref[...], modb_ref[...])
        h = _modulated_norm(x, mod, gpre_ref[...])
        return jnp.dot(h.astype(BF16), w_ref[...], preferred_element_type=F32)

    def finish(i, y):
        rows = slice(i * sub, (i + 1) * sub)
        q3 = y[:, :N_Q].T.reshape(N_Q // HD, HD, sub)
        qa = q3[:8]
        ss = jnp.sum(qa * qa, axis=1, keepdims=True)
        qa = qa * lax.rsqrt(ss * (1.0 / HD) + EPS) * gq_ref[...][None]
        q3 = jnp.concatenate([qa, q3[8:]], axis=0)
        rot = jnp.concatenate([q3[:, 16:32], q3[:, 0:16], q3[:, 48:64], q3[:, 32:48]], axis=1)
        q3 = (q3 * cost_ref[:, rows][None] + rot * sint_ref[:, rows][None]) * (ATTN_SCALE * LOG2E)
        qt_ref[:, rows] = q3.reshape(N_Q, sub).astype(BF16)

        ka = y[:, N_Q + KA_TILE * LANES:N_Q + (KA_TILE + 1) * LANES]
        lane = lax.broadcasted_iota(jnp.int32, (1, LANES), 1)
        lo = lane < HD
        sq = ka * ka
        s_lo = jnp.sum(jnp.where(lo, sq, 0.0), axis=-1, keepdims=True)
        s_hi = jnp.sum(jnp.where(lo, 0.0, sq), axis=-1, keepdims=True)
        r = jnp.where(lo, lax.rsqrt(s_lo * (1.0 / HD) + EPS), lax.rsqrt(s_hi * (1.0 / HD) + EPS))
        ka = ka * r * gk_ref[...]
        cosn = cosn_ref[rows, :]
        sina = sina_ref[rows, :]
        sinb = sinb_ref[rows, :]
        for j in range(N_K // LANES):
            t = ka if j == KA_TILE else y[:, N_Q + j * LANES:N_Q + (j + 1) * LANES]
            t = t * cosn + pltpu.roll(t, LANES - 16, 1) * sina + pltpu.roll(t, 16, 1) * sinb
            k_ref[rows, j * LANES:(j + 1) * LANES] = t.astype(BF16)

        vt_ref[:, rows] = y[:, N_Q + N_K:].T.astype(BF16)

    y_prev = project(0)
    for i in range(1, n_sub):
        y = project(i)
        finish(i - 1, y_prev)
        y_prev = y
    finish(n_sub - 1, y_prev)


def _token_specs(x_src, ctx_src, n_sub, sub, s_len):
    last = s_len // sub - 1
    specs = [pl.BlockSpec((None, sub, D), functools.partial(lambda i, b, t: (b, jnp.minimum(n_sub * t + i, last), 0), i))
             for i in range(n_sub)]
    operands = [x_src] * n_sub
    if ctx_src is not None:
        arr, blk = ctx_src
        specs.append(pl.BlockSpec((None, sub, D), lambda b, t: (b, blk, 0)))
        operands.append(arr)
    return specs, operands


def _proj_call(x_src, ctx_src, mod4, g_pre, wp, tabs, gq_t, gk_n, layer, s_len, t_all, tm, sub):
    nb = x_src.shape[0]
    cost, sint, cosn, sina, sinb = tabs
    n_sub = tm // sub
    tok_specs, tok_args = _token_specs(x_src, ctx_src, n_sub, sub, s_len)
    return pl.pallas_call(
        functools.partial(_proj_kernel, sub=sub, n_sub=n_sub, ctx_step=t_all // tm - 1),
        out_shape=(jax.ShapeDtypeStruct((nb, N_Q, t_all), BF16),
                   jax.ShapeDtypeStruct((nb, t_all, N_K), BF16),
                   jax.ShapeDtypeStruct((nb, N_V, t_all), BF16)),
        grid=(nb, t_all // tm),
        in_specs=tok_specs + [
            pl.BlockSpec((None, None, 1, 3 * D), lambda b, t: (layer, b, 0, 0)),
            pl.BlockSpec((None, None, 1, 3 * D), lambda b, t: (layer, nb, 0, 0)),
            pl.BlockSpec((None, 1, D), lambda b, t: (layer, 0, 0)),
            pl.BlockSpec((None, D, N_P), lambda b, t: (layer, 0, 0)),
            pl.BlockSpec((HD, tm), lambda b, t: (0, t)),
            pl.BlockSpec((HD, tm), lambda b, t: (0, t)),
            pl.BlockSpec((tm, LANES), lambda b, t: (t, 0)),
            pl.BlockSpec((tm, LANES), lambda b, t: (t, 0)),
            pl.BlockSpec((tm, LANES), lambda b, t: (t, 0)),
            pl.BlockSpec((None, HD, sub), lambda b, t: (layer, 0, 0)),
            pl.BlockSpec((None, 1, LANES), lambda b, t: (layer, 0, 0)),
        ],
        out_specs=(pl.BlockSpec((None, N_Q, tm), lambda b, t: (b, 0, t)),
                   pl.BlockSpec((None, tm, N_K), lambda b, t: (b, t, 0)),
                   pl.BlockSpec((None, N_V, tm), lambda b, t: (b, 0, t))),
        compiler_params=_cparams(2),
        name="proj",
    )(*tok_args, mod4, mod4, g_pre, wp, cost, sint, cosn, sina, sinb, gq_t, gk_n)


KCHUNK = 256
ONES_ROWS = 16
UNITS_PER_BODY = 4
PIPE_AC = dict(lead=3, pace=8)
PIPE_B = dict(lead=4, pace=2)


def _small_softmax_pv(problems):
    ss = [jnp.dot(k, rhs, preferred_element_type=F32) for k, _, rhs, _ in problems]
    ms = [jnp.max(s, axis=0, keepdims=True) for s in ss]
    ms = [m if pr[3] is None else jnp.maximum(m, pr[3]) for m, pr in zip(ms, problems)]
    ps = [jnp.exp2(s - m).astype(BF16) for s, m in zip(ss, ms)]
    outs = []
    for p, m, (_, vt, _, extra) in zip(ps, ms, problems):
        dv = vt.shape[0]
        vt_aug = jnp.concatenate([vt, jnp.ones((ONES_ROWS, vt.shape[1]), BF16)], axis=0)
        pv = jnp.dot(vt_aug, p, preferred_element_type=F32)
        l = pv[dv:dv + 1]
        if extra is not None:
            l = l + jnp.exp2(extra - m)
        outs.append(pv[:dv] * (1.0 / l))
    return outs


def _gqa_rhs(qt, j, tq):
    z = jnp.zeros((HD, tq), qt.dtype)
    first = j == 0
    cols = []
    for g in range(4):
        qg = qt[g * HD:(g + 1) * HD, :]
        cols.append(jnp.concatenate([jnp.where(first, qg, z), jnp.where(first, z, qg)], axis=0))
    return jnp.concatenate(cols, axis=1)


def _store_heads(o_ref, ot, tq, nheads):
    o = jnp.concatenate([ot[:, g * tq:(g + 1) * tq] for g in range(nheads)], axis=0)
    o_ref[...] = o.T.astype(o_ref.dtype)


def _sink_row(sink_ref, base, tq):
    blk = lax.broadcasted_iota(jnp.int32, (1, 4 * tq), 1) // tq
    row = jnp.zeros((1, 4 * tq), F32)
    for g in range(4):
        row = jnp.where(blk == g, sink_ref[base + g], row)
    return row * LOG2E


def _tile(t, size):
    if isinstance(t, int):
        return pl.ds(t * size, size)
    return pl.ds(pl.multiple_of(t * size, size), size)


def _pipeline(make_rhs, score_chunks, value_chunks, finish, chunk_rows, out_shapes, scratch, n_units, zero,
              extra=None, *, lead, pace):
    main, head = scratch[:2], scratch[2:]
    n = len(chunk_rows)
    n_groups = len(out_shapes)
    last = n_units - 1
    assert n >= 2 * lead and n_units % 2 == 0
    ex = (lambda u: [None] * n_groups) if extra is None else extra

    def place(c, parity):
        if c < lead:
            return head[parity], sum(chunk_rows[:c])
        return main[parity], sum(chunk_rows[lead:c])

    def score(thunk, rhs, c, parity, cm):
        k, bias, g = thunk()
        s = jnp.dot(k, rhs[g], preferred_element_type=F32)
        if bias is not None:
            s = s + bias
        buf, row = place(c, parity)
        buf[row:row + chunk_rows[c], :] = s
        part = jnp.max(s, axis=0, keepdims=True)
        cm = list(cm)
        cm[g] = jnp.maximum(cm[g], part)
        return cm, part

    def with_extra(cm, u):
        return [m if e is None else jnp.maximum(m, e) for m, e in zip(cm, ex(u))]

    neg = [jnp.full((1, s[1]), -jnp.inf, F32) for s in out_shapes]

    rhs, sc, cm = make_rhs(0), score_chunks(0), neg
    for c in range(n):
        cm, _ = score(sc[c], rhs, c, 0, cm)
    m0 = with_extra(cm, 0)
    rhs, sc, cm1 = make_rhs(1), score_chunks(1), neg
    for c in range(lead):
        cm1, _ = score(sc[c], rhs, c, 1, cm1)

    def half(u, parity, m_u, cm1):
        u1 = jnp.minimum(u + 1, last)
        u2 = jnp.minimum(u + 2, last)
        rhs1, sc1 = make_rhs(u1), score_chunks(u1)
        rhs2, sc2 = make_rhs(u2), score_chunks(u2)
        vc, ex_u = value_chunks(u), ex(u)
        cm2 = neg
        acc = [None] * n_groups
        parts = []
        for c in range(n):
            if c + lead < n:
                cm1, part = score(sc1[c + lead], rhs1, c + lead, 1 - parity, cm1)
            else:
                cm2, part = score(sc2[c + lead - n], rhs2, c + lead - n, parity, cm2)
            parts.append(part)
            vt, g = vc[c]()
            m_c = m_u[g]
            if c >= pace:
                m_c = jnp.maximum(m_c, jnp.minimum(parts[c - pace], m_c))
            buf, row = place(c, parity)
            rows = pl.ds(pl.multiple_of(row + zero, LANES), chunk_rows[c])
            p = jnp.exp2(buf[rows, :] - m_c).astype(BF16)
            vt_aug = jnp.concatenate([vt, jnp.ones((ONES_ROWS, vt.shape[1]), BF16)], axis=0)
            pv = jnp.dot(vt_aug, p, preferred_element_type=F32)
            acc[g] = pv if acc[g] is None else acc[g] + pv
        ots = []
        for g in range(n_groups):
            dv = acc[g].shape[0] - ONES_ROWS
            l = acc[g][dv:dv + 1]
            if ex_u[g] is not None:
                l = l + jnp.exp2(ex_u[g] - m_u[g])
            ots.append(acc[g][:dv] * (1.0 / l))
        return with_extra(cm1, u1), cm2, ots

    def body(i, carry):
        m_u, cm_next, ot = carry
        u0 = UNITS_PER_BODY * i
        finish(jnp.maximum(u0 - 1, 0), ot)
        for h in range(UNITS_PER_BODY):
            if h:
                finish(u0 + h - 1, ot)
            m_u, cm_next, ot = half(u0 + h, h % 2, m_u, cm_next)
        return m_u, cm_next, ot

    assert n_units % UNITS_PER_BODY == 0
    carry = (m0, cm1, [jnp.zeros(s, F32) for s in out_shapes])
    _, _, ot_last = lax.fori_loop(0, n_units // UNITS_PER_BODY, body, carry)
    finish(last, ot_last)


def _score_scratch(chunk_rows, n_cols, lead):
    main = pltpu.VMEM((sum(chunk_rows[lead:]), n_cols), F32)
    head = pltpu.VMEM((sum(chunk_rows[:lead]), n_cols), F32)
    return [main, main, head, head]


def _split_unit(u, n_tiles):
    if isinstance(u, int):
        return u // n_tiles, u % n_tiles
    return lax.div(u, n_tiles), lax.rem(u, n_tiles)


def _rows(g, size):
    return _tile(g, size)


def _ac_chunk_rows(t_all, tq):
    wk = tq + 2 * WINDOW
    return ([KCHUNK] * (t_all // KCHUNK) + [KCHUNK]
            + [min(lo + KCHUNK, wk) - lo for lo in range(0, wk, KCHUNK)])


def _attn_ac_kernel(zero_ref, sink_ref, bias_ref, qa_ref, qc_ref, ka_ref, kc_ref, vta_ref, vtc_ref, oa_ref, oc_ref,
                    *scratch, tq, n_lat, n_ctx, s_len):
    t_all = ka_ref.shape[0]
    wk = tq + 2 * WINDOW
    spans = [(lo, min(lo + KCHUNK, wk)) for lo in range(0, wk, KCHUNK)]

    def rhs_of(q_ref, j, t):
        return _gqa_rhs(q_ref[_rows(j, 4 * HD), _tile(t, tq)], j, tq)

    def store(o_ref, j, t, ot):
        _store_heads(o_ref.at[_tile(t, tq), _rows(j, 4 * HD)], ot, tq, 4)

    def window(t):
        q0 = t * tq
        start = jnp.clip(q0 - WINDOW, 0, s_len - wk)
        return start, (q0 - start) // WINDOW

    def local(start, lo, hi):
        return pl.ds(pl.multiple_of(start + lo, LANES), hi - lo)

    def score_chunks(u):
        _, t = _split_unit(u, n_lat)
        start, variant = window(t)

        def loc(lo, hi):
            b = bias_ref[variant, lo:hi, :]
            return kc_ref[local(start, lo, hi), :], jnp.concatenate([b] * 4, axis=1), 1

        return ([functools.partial(lambda c: (ka_ref[c * KCHUNK:(c + 1) * KCHUNK, :], None, 0), c)
                 for c in range(t_all // KCHUNK)]
                + [lambda: (kc_ref[s_len:s_len + KCHUNK, :], None, 1)]
                + [functools.partial(loc, lo, hi) for lo, hi in spans])

    def value_chunks(u):
        j, t = _split_unit(u, n_lat)
        start, _ = window(t)
        return ([functools.partial(lambda c: (vta_ref[_rows(j, HD), c * KCHUNK:(c + 1) * KCHUNK], 0), c)
                 for c in range(t_all // KCHUNK)]
                + [lambda: (vtc_ref[_rows(j, HD), s_len:s_len + KCHUNK], 1)]
                + [functools.partial(lambda lo, hi: (vtc_ref[_rows(j, HD), local(start, lo, hi)], 1), lo, hi)
                   for lo, hi in spans])

    def make_rhs(u):
        j, t = _split_unit(u, n_lat)
        return [rhs_of(qa_ref, j, t), rhs_of(qc_ref, j, t)]

    def extra(u):
        j, _ = _split_unit(u, n_lat)
        return [None, _sink_row(sink_ref, j * 4, tq)]

    def finish(u, ots):
        j, t = _split_unit(u, n_lat)
        store(oa_ref, j, t, ots[0])
        store(oc_ref, j, t, ots[1])

    ctx_tiles = range(n_lat, n_lat + n_ctx)
    n = 4 * tq
    problems, targets = [], []
    for j in range(2 if n_ctx else 0):
        for q_ref, k_ref, vt_ref, o_ref, sink in ((qa_ref, ka_ref, vta_ref, oa_ref, None),
                                                  (qc_ref, kc_ref, vtc_ref, oc_ref, _sink_row(sink_ref, j * 4, tq))):
            rhs = jnp.concatenate([rhs_of(q_ref, j, t) for t in ctx_tiles], axis=1)
            sink_cols = None if sink is None else jnp.concatenate([sink] * n_ctx, axis=1)
            problems.append((k_ref[s_len:, :], vt_ref[_rows(j, HD), s_len:], rhs, sink_cols))
            targets.append((o_ref, j))
    for (o_ref, j), ot in zip(targets, _small_softmax_pv(problems)):
        for i, t in enumerate(ctx_tiles):
            store(o_ref, j, t, ot[:, i * n:(i + 1) * n])

    _pipeline(make_rhs, score_chunks, value_chunks, finish, _ac_chunk_rows(t_all, tq), [(HD, 4 * tq)] * 2, scratch,
              2 * n_lat, zero_ref[0], extra=extra, **PIPE_AC)


def _attn_b_kernel(zero_ref, lamv_ref, subln_ref, qt_ref, k_ref, vt_ref, o_ref, *scratch, tq, n_lat, n_ctx, s_len,
                   lam_init):
    lv = lamv_ref[...]
    lam = (jnp.exp(jnp.sum(lv[0:1] * lv[1:2], axis=-1, keepdims=True))
           - jnp.exp(jnp.sum(lv[2:3] * lv[3:4], axis=-1, keepdims=True)) + lam_init)
    t_all = k_ref.shape[0]
    n_chunks = t_all // KCHUNK
    hw = 2 * HD

    def rhs_of(h, t):
        qt = qt_ref[_rows(h, hw), _tile(t, tq)]
        z = jnp.zeros((HD, tq), qt.dtype)
        return jnp.concatenate([jnp.concatenate([qt[:HD], z], axis=0),
                                jnp.concatenate([z, qt[HD:]], axis=0)], axis=1)

    def store(h, t, o2):
        o = o2[:, :tq] - lam * o2[:, tq:]
        ms = jnp.mean(o * o, axis=0, keepdims=True)
        o = o * lax.rsqrt(ms + SUBLN_EPS) * subln_ref[...] * (1.0 - lam_init)
        o_ref[_tile(t, tq), _rows(h, hw)] = o.T.astype(o_ref.dtype)

    def score_chunks(u):
        h, _ = _split_unit(u, n_lat)
        return [functools.partial(lambda c: (k_ref[c * KCHUNK:(c + 1) * KCHUNK, _rows(h, hw)], None, 0), c)
                for c in range(n_chunks)]

    def value_chunks(u):
        h, _ = _split_unit(u, n_lat)
        return [functools.partial(lambda c: (vt_ref[_rows(h, hw), c * KCHUNK:(c + 1) * KCHUNK], 0), c)
                for c in range(n_chunks)]

    units = [(h, t) for h in range(4) for t in range(n_lat, n_lat + n_ctx)]
    problems = [(k_ref[s_len:, _rows(h, hw)], vt_ref[_rows(h, hw), s_len:], rhs_of(h, t), None) for h, t in units]
    for (h, t), o2 in zip(units, _small_softmax_pv(problems)):
        store(h, t, o2)

    _pipeline(lambda u: [rhs_of(*_split_unit(u, n_lat))], score_chunks, value_chunks,
              lambda u, ots: store(*_split_unit(u, n_lat), ots[0]), [KCHUNK] * n_chunks, [(hw, 2 * tq)], scratch,
              4 * n_lat, zero_ref[0], **PIPE_B)


def _window_bias(tq):
    wk = tq + 2 * WINDOW
    r = np.arange(wk)[:, None]
    c = np.arange(tq)[None, :]
    out = np.stack([np.where(np.abs(c - r + v * WINDOW) <= WINDOW, 0.0, -np.inf) for v in range(3)])
    return jnp.asarray(out, F32)


def _attn_calls(qt_all, k_all, vt_all, sink, lamv, subln_t, layer, with_ctx, s_len, lam_init,
                tq_a, tq_b):
    nb, _, t_all = qt_all.shape
    c_len = t_all - s_len
    o_rows = t_all if with_ctx else s_len
    o_shape = jax.ShapeDtypeStruct((nb, o_rows, QW), BF16)

    def steps(tq):
        n_lat = s_len // tq
        return n_lat, n_lat + (c_len // tq if with_ctx else 0)

    whole = lambda rows, cols, r, c: pl.BlockSpec((None, rows, cols), lambda b: (b, r, c))
    o_spec = pl.BlockSpec((None, o_rows, QW), lambda b: (b, 0, 0))
    zero = jnp.zeros((1,), jnp.int32)

    n_lat, n_all = steps(tq_a)
    wk = tq_a + 2 * WINDOW
    oa, oc = pl.pallas_call(
        functools.partial(_attn_ac_kernel, tq=tq_a, n_lat=n_lat, n_ctx=n_all - n_lat, s_len=s_len),
        out_shape=(o_shape, o_shape),
        grid=(nb,),
        in_specs=[
            pl.BlockSpec(memory_space=pltpu.SMEM),
            pl.BlockSpec(memory_space=pltpu.SMEM),
            pl.BlockSpec((3, wk, tq_a), lambda b: (0, 0, 0)),
            whole(QW, t_all, 0, 0), whole(QW, t_all, 2, 0),
            whole(t_all, LANES, 0, KA_TILE), whole(t_all, LANES, 0, KA_TILE + 1),
            whole(LANES, t_all, KA_TILE, 0), whole(LANES, t_all, KA_TILE + 1, 0),
        ],
        out_specs=(o_spec, o_spec),
        scratch_shapes=_score_scratch(_ac_chunk_rows(t_all, tq_a), 4 * tq_a, PIPE_AC["lead"]),
        compiler_params=_cparams(1),
        name="attn_ac",
    )(zero, sink[layer], _window_bias(tq_a), qt_all, qt_all, k_all, k_all, vt_all, vt_all)

    n_lat, n_all = steps(tq_b)
    ob = pl.pallas_call(
        functools.partial(_attn_b_kernel, tq=tq_b, n_lat=n_lat, n_ctx=n_all - n_lat, s_len=s_len,
                          lam_init=lam_init),
        out_shape=o_shape,
        grid=(nb,),
        in_specs=[
            pl.BlockSpec(memory_space=pltpu.SMEM),
            pl.BlockSpec((None, 4, HD), lambda b: (layer, 0, 0)),
            pl.BlockSpec((None, 2 * HD, tq_b), lambda b: (layer, 0, 0)),
            whole(QW, t_all, 1, 0), whole(t_all, QW, 0, 0), whole(QW, t_all, 0, 0),
        ],
        out_specs=o_spec,
        scratch_shapes=_score_scratch([KCHUNK] * (t_all // KCHUNK), 2 * tq_b, PIPE_B["lead"]),
        compiler_params=_cparams(1),
        name="attn_b",
    )(zero, lamv, subln_t, qt_all, k_all, vt_all)

    return oa, ob, oc


def _merge_kernel(*refs, sub, n_sub, ctx_step):
    x_refs = refs[:n_sub]
    rest = refs[n_sub:]
    ctx_ref = None
    if ctx_step is not None:
        ctx_ref, rest = rest[0], rest[1:]
    (oa_ref, ob_ref, oc_ref, modb_ref, modc_ref, gpre_ref, gpost_ref, wgm_ref, bmg_ref, wbr_ref, wout_ref,
     out_ref) = rest

    def gates(i):
        x, is_ctx = _sub_tile_input(x_refs, ctx_ref, i, ctx_step)
        mod = jnp.where(is_ctx, modc_ref[...], modb_ref[...])
        h = _modulated_norm(x, mod, gpre_ref[...])
        gm = jnp.dot(h.astype(BF16), wgm_ref[...], preferred_element_type=F32)
        return x, mod, gm

    def finish(i, x, mod, gm):
        rows = slice(i * sub, (i + 1) * sub)
        z = None
        for j, o_ref in enumerate((oa_ref, ob_ref, oc_ref)):
            g = gm[:, j * QW:(j + 1) * QW]
            u = (o_ref[rows, :].astype(F32) * (g * _sigmoid(g))).astype(BF16)
            p = jnp.dot(u, wbr_ref[j], preferred_element_type=F32)
            mg = _sigmoid(gm[:, N_G + j * D:N_G + (j + 1) * D] + bmg_ref[:, j * D:(j + 1) * D])
            z = mg * p if z is None else z + mg * p
        y = jnp.dot(z.astype(BF16), wout_ref[...], preferred_element_type=F32)
        ms = jnp.mean(y * y, axis=-1, keepdims=True)
        gate = mod[:, 2 * D:]
        out_ref[rows, :] = x + gate * (y * lax.rsqrt(ms + EPS) * gpost_ref[...])

    prev = gates(0)
    for i in range(1, n_sub):
        cur = gates(i)
        finish(i - 1, *prev)
        prev = cur
    finish(n_sub - 1, *prev)


def _merge_call(x_src, ctx_src, oa, ob, oc, mod4, g_pre, g_post, wgm, b_mg, wbr, wout, layer, s_len, rows_out,
                tm, sub):
    nb = x_src.shape[0]
    n_sub = tm // sub
    tok_specs, tok_args = _token_specs(x_src, ctx_src, n_sub, sub, s_len)
    ctx_step = None if ctx_src is None else rows_out // tm - 1
    tok = lambda b, t: (b, t, 0)
    lay2 = lambda b, t: (layer, 0, 0)
    const = dict(pipeline_mode=pl.Buffered(1))
    return pl.pallas_call(
        functools.partial(_merge_kernel, sub=sub, n_sub=n_sub, ctx_step=ctx_step),
        out_shape=jax.ShapeDtypeStruct((nb, rows_out, D), F32),
        grid=(nb, rows_out // tm),
        in_specs=tok_specs + [
            pl.BlockSpec((None, tm, QW), tok),
            pl.BlockSpec((None, tm, QW), tok),
            pl.BlockSpec((None, tm, QW), tok),
            pl.BlockSpec((None, None, 1, 3 * D), lambda b, t: (layer, b, 0, 0)),
            pl.BlockSpec((None, None, 1, 3 * D), lambda b, t: (layer, nb, 0, 0)),
            pl.BlockSpec((None, 1, D), lay2),
            pl.BlockSpec((None, 1, D), lay2),
            pl.BlockSpec((None, D, N_G + N_M), lay2, **const),
            pl.BlockSpec((None, 1, N_M), lay2),
            pl.BlockSpec((None, 3, QW, D), lambda b, t: (layer, 0, 0, 0), **const),
            pl.BlockSpec((None, D, D), lay2, **const),
        ],
        out_specs=pl.BlockSpec((None, tm, D), tok),
        compiler_params=_cparams(2),
        name="merge",
    )(*tok_args, oa, ob, oc, mod4, mod4, g_pre, g_post, wgm, b_mg, wbr, wout)


def _rope_tables(s_len, c_len):
    rows = s_len // GRID_W
    row = np.repeat(np.arange(rows), GRID_W).astype(np.float32)
    col = np.tile(np.arange(GRID_W), rows).astype(np.float32)
    freqs = (np.float32(ROPE_THETA) ** (-np.arange(ROPE_PAIRS, dtype=np.float32) / ROPE_PAIRS)).astype(np.float32)
    ang_r = row[:, None] * freqs
    ang_c = col[:, None] * freqs
    ang = np.concatenate([ang_r, ang_r, ang_c, ang_c], axis=-1)
    cos = np.concatenate([np.cos(ang), np.ones((c_len, HD), np.float32)], axis=0).astype(np.float32)
    sin = np.concatenate([np.sin(ang), np.zeros((c_len, HD), np.float32)], axis=0).astype(np.float32)
    first = (np.arange(HD) % 32) < 16
    sin_a = np.where(first, -sin, np.float32(0.0))
    sin_b = np.where(first, np.float32(0.0), sin)
    tile2 = lambda a: np.concatenate([a, a], axis=-1)
    cost = np.ascontiguousarray(cos.T)
    sint = np.ascontiguousarray((sin_a + sin_b).T)
    return tuple(jnp.asarray(a, F32) for a in (cost, sint, tile2(cos), tile2(sin_a), tile2(sin_b)))


def _cols(w, names):
    return jnp.concatenate([w[..., _IN[n][0]:_IN[n][1]] for n in names], axis=-1)


def kernel(x, c, ctx, c_ctx, w_ada, b_ada, g_pre, g_post, w_in, q_norm, k_norm, lam_q1, lam_k1, lam_q2,
           lam_k2, subln, sink, w_br_a, w_br_b, w_br_c, w_mg, b_mg, w_out):
    nb, s_len, _ = x.shape
    c_len = ctx.shape[1]
    depth = w_in.shape[0]
    sub = 256
    tm_all = 3 * sub
    tm_lat = 4 * sub
    tq_a, tq_b = 128, 256
    assert (s_len + c_len) % tm_all == 0 and s_len % tm_lat == 0 and s_len % sub == 0
    assert c_len == KCHUNK == sub

    wp = _cols(w_in, ("qa", "qb", "qc", "kb", "ka", "kc", "vb", "va", "vc")).astype(BF16)
    wgm = jnp.concatenate([_cols(w_in, ("ga", "gb", "gc")), w_mg], axis=-1).astype(BF16)
    wbr = jnp.stack([w_br_a, w_br_b, w_br_c], axis=1).astype(BF16)
    wout = w_out.astype(BF16)

    tabs = _rope_tables(s_len, c_len)
    gq_t = jnp.broadcast_to(q_norm[:, :, None], (depth, HD, sub))
    gk_n = jnp.concatenate([k_norm, k_norm], axis=-1)[:, None, :]
    lamv = jnp.stack([lam_q1, lam_k1, lam_q2, lam_k2], axis=1)
    subln_t = jnp.broadcast_to(subln[:, :, None], (depth, 2 * HD, tq_b))
    g_pre3 = g_pre[:, None, :]
    g_post3 = g_post[:, None, :]
    b_mg3 = b_mg[:, None, :]

    rows = ((nb + 1 + 7) // 8) * 8
    sc_in = jnp.concatenate([c, c_ctx[None, :], jnp.zeros((rows - nb - 1, D), F32)], axis=0)
    mod = _ada_call(sc_in, w_ada, b_ada)
    mod4 = mod[:, :, None, :]

    t_all = s_len + c_len
    x_src, ctx_src = x, (ctx, 0)
    for layer in range(depth):
        last = layer == depth - 1
        lam_init = 0.8 - 0.6 * math.exp(-0.3 * layer)
        qt_all, k_all, vt_all = _proj_call(x_src, ctx_src, mod4, g_pre3, wp, tabs, gq_t, gk_n, layer, s_len,
                                           t_all, tm_all, sub)
        oa, ob, oc = _attn_calls(qt_all, k_all, vt_all, sink, lamv, subln_t, layer, not last, s_len,
                                 lam_init, tq_a, tq_b)
        rows_out, tm = (s_len, tm_lat) if last else (t_all, tm_all)
        xs = _merge_call(x_src, None if last else ctx_src, oa, ob, oc, mod4, g_pre3, g_post3, wgm, b_mg3, wbr,
                         wout, layer, s_len, rows_out, tm, sub)
        x_src, ctx_src = xs, (xs, s_len // sub)
    return xs
```

```python
import functools
import math

import jax
import jax.numpy as jnp
import numpy as np
from jax import lax
from jax.experimental import pallas as pl
from jax.experimental.pallas import tpu as pltpu

F32 = jnp.float32
BF16 = jnp.bfloat16

D = 1024
HD = 64
GRID_W = 64
WINDOW = 128
ROPE_THETA = 10000.0
ROPE_PAIRS = HD // 4
EPS = 1e-6
SUBLN_EPS = 1e-5
ATTN_SCALE = HD ** -0.5
LOG2E = math.log2(math.e)

QW = 512
N_Q = 3 * QW
N_K = 512 + 128 + 128
N_V = 512 + 128 + 128
KA_TILE = 4
N_P = N_Q + N_K + N_V
N_G = 3 * QW
N_M = 3 * D

LANES = 128
VMEM_LIMIT = 56 * 1024 * 1024

_IN = dict(qa=(0, 512), ka=(512, 640), va=(640, 768), ga=(768, 1280),
           qb=(1280, 1792), kb=(1792, 2304), vb=(2304, 2816), gb=(2816, 3328),
           qc=(3328, 3840), kc=(3840, 3968), vc=(3968, 4096), gc=(4096, 4608))


def _sigmoid(v):
    return 1.0 / (1.0 + jnp.exp(-v))


def _cparams(n_axes):
    return pltpu.CompilerParams(dimension_semantics=("arbitrary",) * n_axes,
                                vmem_limit_bytes=VMEM_LIMIT)


def _ada_kernel(sc_ref, w_ref, b_ref, o_ref):
    v = sc_ref[...]
    s = (v * _sigmoid(v)).astype(BF16)
    o_ref[...] = jnp.dot(s, w_ref[...].astype(BF16), preferred_element_type=F32) + b_ref[...]


def _ada_call(sc_in, w_ada, b_ada):
    depth = w_ada.shape[0]
    rows = sc_in.shape[0]
    nblk = 3
    return pl.pallas_call(
        _ada_kernel,
        out_shape=jax.ShapeDtypeStruct((depth, rows, 3 * D), F32),
        grid=(depth, nblk),
        in_specs=[
            pl.BlockSpec((rows, D), lambda l, n: (0, 0)),
            pl.BlockSpec((None, D, D), lambda l, n: (l, 0, n)),
            pl.BlockSpec((None, 1, D), lambda l, n: (l, 0, n)),
        ],
        out_specs=pl.BlockSpec((None, rows, D), lambda l, n: (l, 0, n)),
        compiler_params=_cparams(2),
        name="adaln",
    )(sc_in, w_ada, b_ada.reshape(depth, 1, 3 * D))


def _modulated_norm(x, mod, gpre):
    shift = mod[:, :D]
    scale = mod[:, D:2 * D]
    ms = jnp.mean(x * x, axis=-1, keepdims=True)
    return x * lax.rsqrt(ms + EPS) * gpre * (1.0 + scale) + shift


def _sub_tile_input(x_refs, ctx_ref, i, ctx_step):
    x = x_refs[i][...]
    if ctx_ref is None or i != len(x_refs) - 1:
        return x, False
    is_ctx = pl.program_id(1) == ctx_step
    return jnp.where(is_ctx, ctx_ref[...], x), is_ctx


def _proj_kernel(*refs, sub, n_sub, ctx_step):
    x_refs, ctx_ref = refs[:n_sub], refs[n_sub]
    (modb_ref, modc_ref, gpre_ref, w_ref, cost_ref, sint_ref, cosn_ref, sina_ref, sinb_ref, gq_ref, gk_ref,
     qt_ref, k_ref, vt_ref) = refs[n_sub + 1:]

    def project(i):
        x, is_ctx = _sub_tile_input(x_refs, ctx_ref, i, ctx_step)
        mod = jnp.where(is_ctx, modc_ref[...], modb_ref[...])
        h = _modulated_norm(x, mod, gpre_ref[...])
        return jnp.dot(h.astype(BF16), w_ref[...], preferred_element_type=F32)

    def finish(i, y):
        rows = slice(i * sub, (i + 1) * sub)
        q3 = y[:, :N_Q].T.reshape(N_Q // HD, HD, sub)
        qa = q3[:8]
        ss = jnp.sum(qa * qa, axis=1, keepdims=True)
        qa = qa * lax.rsqrt(ss * (1.0 / HD) + EPS) * gq_ref[...][None]
        q3 = jnp.concatenate([qa, q3[8:]], axis=0)
        rot = jnp.concatenate([q3[:, 16:32], q3[:, 0:16], q3[:, 48:64], q3[:, 32:48]], axis=1)
        q3 = (q3 * cost_ref[:, rows][None] + rot * sint_ref[:, rows][None]) * (ATTN_SCALE * LOG2E)
        qt_ref[:, rows] = q3.reshape(N_Q, sub).astype(BF16)

        ka = y[:, N_Q + KA_TILE * LANES:N_Q + (KA_TILE + 1) * LANES]
        lane = lax.broadcasted_iota(jnp.int32, (1, LANES), 1)
        lo = lane < HD
        sq = ka * ka
        s_lo = jnp.sum(jnp.where(lo, sq, 0.0), axis=-1, keepdims=True)
        s_hi = jnp.sum(jnp.where(lo, 0.0, sq), axis=-1, keepdims=True)
        r = jnp.where(lo, lax.rsqrt(s_lo * (1.0 / HD) + EPS), lax.rsqrt(s_hi * (1.0 / HD) + EPS))
        ka = ka * r * gk_ref[...]
        cosn = cosn_ref[rows, :]
        sina = sina_ref[rows, :]
        sinb = sinb_ref[rows, :]
        for j in range(N_K // LANES):
            t = ka if j == KA_TILE else y[:, N_Q + j * LANES:N_Q + (j + 1) * LANES]
            t = t * cosn + pltpu.roll(t, LANES - 16, 1) * sina + pltpu.roll(t, 16, 1) * sinb
            k_ref[rows, j * LANES:(j + 1) * LANES] = t.astype(BF16)

        vt_ref[:, rows] = y[:, N_Q + N_K:].T.astype(BF16)

    y_prev = project(0)
    for i in range(1, n_sub):
        y = project(i)
        finish(i - 1, y_prev)
        y_prev = y
    finish(n_sub - 1, y_prev)


def _token_specs(x_src, ctx_src, n_sub, sub, s_len):
    last = s_len // sub - 1
    specs = [pl.BlockSpec((None, sub, D), functools.partial(lambda i, b, t: (b, jnp.minimum(n_sub * t + i, last), 0), i))
             for i in range(n_sub)]
    operands = [x_src] * n_sub
    if ctx_src is not None:
        arr, blk = ctx_src
        specs.append(pl.BlockSpec((None, sub, D), lambda b, t: (b, blk, 0)))
        operands.append(arr)
    return specs, operands


def _proj_call(x_src, ctx_src, mod4, g_pre, wp, tabs, gq_t, gk_n, layer, s_len, t_all, tm, sub):
    nb = x_src.shape[0]
    cost, sint, cosn, sina, sinb = tabs
    n_sub = tm // sub
    tok_specs, tok_args = _token_specs(x_src, ctx_src, n_sub, sub, s_len)
    return pl.pallas_call(
        functools.partial(_proj_kernel, sub=sub, n_sub=n_sub, ctx_step=t_all // tm - 1),
        out_shape=(jax.ShapeDtypeStruct((nb, N_Q, t_all), BF16),
                   jax.ShapeDtypeStruct((nb, t_all, N_K), BF16),
                   jax.ShapeDtypeStruct((nb, N_V, t_all), BF16)),
        grid=(nb, t_all // tm),
        in_specs=tok_specs + [
            pl.BlockSpec((None, None, 1, 3 * D), lambda b, t: (layer, b, 0, 0)),
            pl.BlockSpec((None, None, 1, 3 * D), lambda b, t: (layer, nb, 0, 0)),
            pl.BlockSpec((None, 1, D), lambda b, t: (layer, 0, 0)),
            pl.BlockSpec((None, D, N_P), lambda b, t: (layer, 0, 0)),
            pl.BlockSpec((HD, tm), lambda b, t: (0, t)),
            pl.BlockSpec((HD, tm), lambda b, t: (0, t)),
            pl.BlockSpec((tm, LANES), lambda b, t: (t, 0)),
            pl.BlockSpec((tm, LANES), lambda b, t: (t, 0)),
            pl.BlockSpec((tm, LANES), lambda b, t: (t, 0)),
            pl.BlockSpec((None, HD, sub), lambda b, t: (layer, 0, 0)),
            pl.BlockSpec((None, 1, LANES), lambda b, t: (layer, 0, 0)),
        ],
        out_specs=(pl.BlockSpec((None, N_Q, tm), lambda b, t: (b, 0, t)),
                   pl.BlockSpec((None, tm, N_K), lambda b, t: (b, t, 0)),
                   pl.BlockSpec((None, N_V, tm), lambda b, t: (b, 0, t))),
        compiler_params=_cparams(2),
        name="proj",
    )(*tok_args, mod4, mod4, g_pre, wp, cost, sint, cosn, sina, sinb, gq_t, gk_n)


KCHUNK = 256
ONES_ROWS = 16
UNITS_PER_BODY = 4
PIPE_AC = dict(lead=3, pace=6)
PIPE_B = dict(lead=4, pace=2)


def _small_softmax_pv(problems):
    ss = [jnp.dot(k, rhs, preferred_element_type=F32) for k, _, rhs, _ in problems]
    ms = [jnp.max(s, axis=0, keepdims=True) for s in ss]
    ms = [m if pr[3] is None else jnp.maximum(m, pr[3]) for m, pr in zip(ms, problems)]
    ps = [jnp.exp2(s - m).astype(BF16) for s, m in zip(ss, ms)]
    outs = []
    for p, m, (_, vt, _, extra) in zip(ps, ms, problems):
        dv = vt.shape[0]
        vt_aug = jnp.concatenate([vt, jnp.ones((ONES_ROWS, vt.shape[1]), BF16)], axis=0)
        pv = jnp.dot(vt_aug, p, preferred_element_type=F32)
        l = pv[dv:dv + 1]
        if extra is not None:
            l = l + jnp.exp2(extra - m)
        outs.append(pv[:dv] * (1.0 / l))
    return outs


def _gqa_rhs(qt, j, tq):
    z = jnp.zeros((HD, tq), qt.dtype)
    first = j == 0
    cols = []
    for g in range(4):
        qg = qt[g * HD:(g + 1) * HD, :]
        cols.append(jnp.concatenate([jnp.where(first, qg, z), jnp.where(first, z, qg)], axis=0))
    return jnp.concatenate(cols, axis=1)


def _store_heads(o_ref, ot, tq, nheads):
    o = jnp.concatenate([ot[:, g * tq:(g + 1) * tq] for g in range(nheads)], axis=0)
    o_ref[...] = o.T.astype(o_ref.dtype)


def _sink_row(sink_ref, base, tq):
    blk = lax.broadcasted_iota(jnp.int32, (1, 4 * tq), 1) // tq
    row = jnp.zeros((1, 4 * tq), F32)
    for g in range(4):
        row = jnp.where(blk == g, sink_ref[base + g], row)
    return row * LOG2E


def _tile(t, size):
    if isinstance(t, int):
        return pl.ds(t * size, size)
    return pl.ds(pl.multiple_of(t * size, size), size)


def _pipeline(make_rhs, score_chunks, value_chunks, finish, chunk_rows, out_shapes, scratch, n_units, zero,
              extra=None, *, lead, pace):
    main, head = scratch[:2], scratch[2:]
    n = len(chunk_rows)
    n_groups = len(out_shapes)
    last = n_units - 1
    assert n >= 2 * lead and n_units % 2 == 0
    ex = (lambda u: [None] * n_groups) if extra is None else extra

    def place(c, parity):
        if c < lead:
            return head[parity], sum(chunk_rows[:c])
        return main[parity], sum(chunk_rows[lead:c])

    def score(thunk, rhs, c, parity, cm):
        k, bias, g = thunk()
        s = jnp.dot(k, rhs[g], preferred_element_type=F32)
        if bias is not None:
            s = s + bias
        buf, row = place(c, parity)
        buf[row:row + chunk_rows[c], :] = s
        part = jnp.max(s, axis=0, keepdims=True)
        cm = list(cm)
        cm[g] = jnp.maximum(cm[g], part)
        return cm, part

    def with_extra(cm, u):
        return [m if e is None else jnp.maximum(m, e) for m, e in zip(cm, ex(u))]

    neg = [jnp.full((1, s[1]), -jnp.inf, F32) for s in out_shapes]

    rhs, sc, cm = make_rhs(0), score_chunks(0), neg
    for c in range(n):
        cm, _ = score(sc[c], rhs, c, 0, cm)
    m0 = with_extra(cm, 0)
    rhs, sc, cm1 = make_rhs(1), score_chunks(1), neg
    for c in range(lead):
        cm1, _ = score(sc[c], rhs, c, 1, cm1)

    def half(u, parity, m_u, cm1):
        u1 = jnp.minimum(u + 1, last)
        u2 = jnp.minimum(u + 2, last)
        rhs1, sc1 = make_rhs(u1), score_chunks(u1)
        rhs2, sc2 = make_rhs(u2), score_chunks(u2)
        vc, ex_u = value_chunks(u), ex(u)
        cm2 = neg
        acc = [None] * n_groups
        parts = []
        for c in range(n):
            if c + lead < n:
                cm1, part = score(sc1[c + lead], rhs1, c + lead, 1 - parity, cm1)
            else:
                cm2, part = score(sc2[c + lead - n], rhs2, c + lead - n, parity, cm2)
            parts.append(part)
            vt, g = vc[c]()
            m_c = m_u[g]
            if c >= pace:
                m_c = jnp.maximum(m_c, jnp.minimum(parts[c - pace], m_c))
            buf, row = place(c, parity)
            rows = pl.ds(pl.multiple_of(row + zero, LANES), chunk_rows[c])
            p = jnp.exp2(buf[rows, :] - m_c).astype(BF16)
            vt_aug = jnp.concatenate([vt, jnp.ones((ONES_ROWS, vt.shape[1]), BF16)], axis=0)
            pv = jnp.dot(vt_aug, p, preferred_element_type=F32)
            acc[g] = pv if acc[g] is None else acc[g] + pv
        ots = []
        for g in range(n_groups):
            dv = acc[g].shape[0] - ONES_ROWS
            l = acc[g][dv:dv + 1]
            if ex_u[g] is not None:
                l = l + jnp.exp2(ex_u[g] - m_u[g])
            ots.append(acc[g][:dv] * (1.0 / l))
        return with_extra(cm1, u1), cm2, ots

    def body(i, carry):
        m_u, cm_next, ot = carry
        u0 = UNITS_PER_BODY * i
        finish(jnp.maximum(u0 - 1, 0), ot)
        for h in range(UNITS_PER_BODY):
            if h:
                finish(u0 + h - 1, ot)
            m_u, cm_next, ot = half(u0 + h, h % 2, m_u, cm_next)
        return m_u, cm_next, ot

    assert n_units % UNITS_PER_BODY == 0
    carry = (m0, cm1, [jnp.zeros(s, F32) for s in out_shapes])
    _, _, ot_last = lax.fori_loop(0, n_units // UNITS_PER_BODY, body, carry)
    finish(last, ot_last)


def _score_scratch(chunk_rows, n_cols, lead):
    main = pltpu.VMEM((sum(chunk_rows[lead:]), n_cols), F32)
    head = pltpu.VMEM((sum(chunk_rows[:lead]), n_cols), F32)
    return [main, main, head, head]


def _split_unit(u, n_tiles):
    if isinstance(u, int):
        return u // n_tiles, u % n_tiles
    return lax.div(u, n_tiles), lax.rem(u, n_tiles)


def _rows(g, size):
    return _tile(g, size)


def _ac_chunk_rows(t_all, tq):
    wk = tq + 2 * WINDOW
    return ([KCHUNK] * (t_all // KCHUNK) + [KCHUNK]
            + [min(lo + KCHUNK, wk) - lo for lo in range(0, wk, KCHUNK)])


def _attn_ac_kernel(zero_ref, sink_ref, bias_ref, qa_ref, qc_ref, ka_ref, kc_ref, vta_ref, vtc_ref, oa_ref, oc_ref,
                    *scratch, tq, n_lat, n_ctx, s_len):
    t_all = ka_ref.shape[0]
    wk = tq + 2 * WINDOW
    spans = [(lo, min(lo + KCHUNK, wk)) for lo in range(0, wk, KCHUNK)]

    def rhs_of(q_ref, j, t):
        return _gqa_rhs(q_ref[_rows(j, 4 * HD), _tile(t, tq)], j, tq)

    def store(o_ref, j, t, ot):
        _store_heads(o_ref.at[_tile(t, tq), _rows(j, 4 * HD)], ot, tq, 4)

    def window(t):
        q0 = t * tq
        start = jnp.clip(q0 - WINDOW, 0, s_len - wk)
        return start, (q0 - start) // WINDOW

    def local(start, lo, hi):
        return pl.ds(pl.multiple_of(start + lo, LANES), hi - lo)

    def score_chunks(u):
        _, t = _split_unit(u, n_lat)
        start, variant = window(t)

        def loc(lo, hi):
            b = bias_ref[variant, lo:hi, :]
            return kc_ref[local(start, lo, hi), :], jnp.concatenate([b] * 4, axis=1), 1

        return ([functools.partial(lambda c: (ka_ref[c * KCHUNK:(c + 1) * KCHUNK, :], None, 0), c)
                 for c in range(t_all // KCHUNK)]
                + [lambda: (kc_ref[s_len:s_len + KCHUNK, :], None, 1)]
                + [functools.partial(loc, lo, hi) for lo, hi in spans])

    def value_chunks(u):
        j, t = _split_unit(u, n_lat)
        start, _ = window(t)
        return ([functools.partial(lambda c: (vta_ref[_rows(j, HD), c * KCHUNK:(c + 1) * KCHUNK], 0), c)
                 for c in range(t_all // KCHUNK)]
                + [lambda: (vtc_ref[_rows(j, HD), s_len:s_len + KCHUNK], 1)]
                + [functools.partial(lambda lo, hi: (vtc_ref[_rows(j, HD), local(start, lo, hi)], 1), lo, hi)
                   for lo, hi in spans])

    def make_rhs(u):
        j, t = _split_unit(u, n_lat)
        return [rhs_of(qa_ref, j, t), rhs_of(qc_ref, j, t)]

    def extra(u):
        j, _ = _split_unit(u, n_lat)
        return [None, _sink_row(sink_ref, j * 4, tq)]

    def finish(u, ots):
        j, t = _split_unit(u, n_lat)
        store(oa_ref, j, t, ots[0])
        store(oc_ref, j, t, ots[1])

    ctx_tiles = range(n_lat, n_lat + n_ctx)
    n = 4 * tq
    problems, targets = [], []
    for j in range(2 if n_ctx else 0):
        for q_ref, k_ref, vt_ref, o_ref, sink in ((qa_ref, ka_ref, vta_ref, oa_ref, None),
                                                  (qc_ref, kc_ref, vtc_ref, oc_ref, _sink_row(sink_ref, j * 4, tq))):
            rhs = jnp.concatenate([rhs_of(q_ref, j, t) for t in ctx_tiles], axis=1)
            sink_cols = None if sink is None else jnp.concatenate([sink] * n_ctx, axis=1)
            problems.append((k_ref[s_len:, :], vt_ref[_rows(j, HD), s_len:], rhs, sink_cols))
            targets.append((o_ref, j))
    for (o_ref, j), ot in zip(targets, _small_softmax_pv(problems)):
        for i, t in enumerate(ctx_tiles):
            store(o_ref, j, t, ot[:, i * n:(i + 1) * n])

    _pipeline(make_rhs, score_chunks, value_chunks, finish, _ac_chunk_rows(t_all, tq), [(HD, 4 * tq)] * 2, scratch,
              2 * n_lat, zero_ref[0], extra=extra, **PIPE_AC)


def _attn_b_kernel(zero_ref, lamv_ref, subln_ref, qt_ref, k_ref, vt_ref, o_ref, *scratch, tq, n_lat, n_ctx, s_len,
                   lam_init):
    lv = lamv_ref[...]
    lam = (jnp.exp(jnp.sum(lv[0:1] * lv[1:2], axis=-1, keepdims=True))
           - jnp.exp(jnp.sum(lv[2:3] * lv[3:4], axis=-1, keepdims=True)) + lam_init)
    t_all = k_ref.shape[0]
    n_chunks = t_all // KCHUNK
    hw = 2 * HD

    def rhs_of(h, t):
        qt = qt_ref[_rows(h, hw), _tile(t, tq)]
        z = jnp.zeros((HD, tq), qt.dtype)
        return jnp.concatenate([jnp.concatenate([qt[:HD], z], axis=0),
                                jnp.concatenate([z, qt[HD:]], axis=0)], axis=1)

    def store(h, t, o2):
        o = o2[:, :tq] - lam * o2[:, tq:]
        ms = jnp.mean(o * o, axis=0, keepdims=True)
        o = o * lax.rsqrt(ms + SUBLN_EPS) * subln_ref[...] * (1.0 - lam_init)
        o_ref[_tile(t, tq), _rows(h, hw)] = o.T.astype(o_ref.dtype)

    def score_chunks(u):
        h, _ = _split_unit(u, n_lat)
        return [functools.partial(lambda c: (k_ref[c * KCHUNK:(c + 1) * KCHUNK, _rows(h, hw)], None, 0), c)
                for c in range(n_chunks)]

    def value_chunks(u):
        h, _ = _split_unit(u, n_lat)
        return [functools.partial(lambda c: (vt_ref[_rows(h, hw), c * KCHUNK:(c + 1) * KCHUNK], 0), c)
                for c in range(n_chunks)]

    units = [(h, t) for h in range(4) for t in range(n_lat, n_lat + n_ctx)]
    problems = [(k_ref[s_len:, _rows(h, hw)], vt_ref[_rows(h, hw), s_len:], rhs_of(h, t), None) for h, t in units]
    for (h, t), o2 in zip(units, _small_softmax_pv(problems)):
        store(h, t, o2)

    _pipeline(lambda u: [rhs_of(*_split_unit(u, n_lat))], score_chunks, value_chunks,
              lambda u, ots: store(*_split_unit(u, n_lat), ots[0]), [KCHUNK] * n_chunks, [(hw, 2 * tq)], scratch,
              4 * n_lat, zero_ref[0], **PIPE_B)


def _window_bias(tq):
    wk = tq + 2 * WINDOW
    r = np.arange(wk)[:, None]
    c = np.arange(tq)[None, :]
    out = np.stack([np.where(np.abs(c - r + v * WINDOW) <= WINDOW, 0.0, -np.inf) for v in range(3)])
    return jnp.asarray(out, F32)


def _attn_calls(qt_all, k_all, vt_all, sink, lamv, subln_t, layer, with_ctx, s_len, lam_init,
                tq_a, tq_b):
    nb, _, t_all = qt_all.shape
    c_len = t_all - s_len
    o_rows = t_all if with_ctx else s_len
    o_shape = jax.ShapeDtypeStruct((nb, o_rows, QW), BF16)

    def steps(tq):
        n_lat = s_len // tq
        return n_lat, n_lat + (c_len // tq if with_ctx else 0)

    whole = lambda rows, cols, r, c: pl.BlockSpec((None, rows, cols), lambda b: (b, r, c))
    o_spec = pl.BlockSpec((None, o_rows, QW), lambda b: (b, 0, 0))
    zero = jnp.zeros((1,), jnp.int32)

    n_lat, n_all = steps(tq_a)
    wk = tq_a + 2 * WINDOW
    oa, oc = pl.pallas_call(
        functools.partial(_attn_ac_kernel, tq=tq_a, n_lat=n_lat, n_ctx=n_all - n_lat, s_len=s_len),
        out_shape=(o_shape, o_shape),
        grid=(nb,),
        in_specs=[
            pl.BlockSpec(memory_space=pltpu.SMEM),
            pl.BlockSpec(memory_space=pltpu.SMEM),
            pl.BlockSpec((3, wk, tq_a), lambda b: (0, 0, 0)),
            whole(QW, t_all, 0, 0), whole(QW, t_all, 2, 0),
            whole(t_all, LANES, 0, KA_TILE), whole(t_all, LANES, 0, KA_TILE + 1),
            whole(LANES, t_all, KA_TILE, 0), whole(LANES, t_all, KA_TILE + 1, 0),
        ],
        out_specs=(o_spec, o_spec),
        scratch_shapes=_score_scratch(_ac_chunk_rows(t_all, tq_a), 4 * tq_a, PIPE_AC["lead"]),
        compiler_params=_cparams(1),
        name="attn_ac",
    )(zero, sink[layer], _window_bias(tq_a), qt_all, qt_all, k_all, k_all, vt_all, vt_all)

    n_lat, n_all = steps(tq_b)
    ob = pl.pallas_call(
        functools.partial(_attn_b_kernel, tq=tq_b, n_lat=n_lat, n_ctx=n_all - n_lat, s_len=s_len,
                          lam_init=lam_init),
        out_shape=o_shape,
        grid=(nb,),
        in_specs=[
            pl.BlockSpec(memory_space=pltpu.SMEM),
            pl.BlockSpec((None, 4, HD), lambda b: (layer, 0, 0)),
            pl.BlockSpec((None, 2 * HD, tq_b), lambda b: (layer, 0, 0)),
            whole(QW, t_all, 1, 0), whole(t_all, QW, 0, 0), whole(QW, t_all, 0, 0),
        ],
        out_specs=o_spec,
        scratch_shapes=_score_scratch([KCHUNK] * (t_all // KCHUNK), 2 * tq_b, PIPE_B["lead"]),
        compiler_params=_cparams(1),
        name="attn_b",
    )(zero, lamv, subln_t, qt_all, k_all, vt_all)

    return oa, ob, oc


def _merge_kernel(*refs, sub, n_sub, ctx_step):
    x_refs = refs[:n_sub]
    rest = refs[n_sub:]
    ctx_ref = None
    if ctx_step is not None:
        ctx_ref, rest = rest[0], rest[1:]
    (oa_ref, ob_ref, oc_ref, modb_ref, modc_ref, gpre_ref, gpost_ref, wgm_ref, bmg_ref, wbr_ref, wout_ref,
     out_ref) = rest

    def gates(i):
        x, is_ctx = _sub_tile_input(x_refs, ctx_ref, i, ctx_step)
        mod = jnp.where(is_ctx, modc_ref[...], modb_ref[...])
        h = _modulated_norm(x, mod, gpre_ref[...])
        h = h.astype(BF16)
        gs = jnp.dot(h, wgm_ref[:, :N_G], preferred_element_type=F32)
        return x, mod, h, gs

    def finish(i, x, mod, h, gs):
        rows = slice(i * sub, (i + 1) * sub)
        z = None
        for j, o_ref in enumerate((oa_ref, ob_ref, oc_ref)):
            g = gs[:, j * QW:(j + 1) * QW]
            u = (o_ref[rows, :].astype(F32) * (g * _sigmoid(g))).astype(BF16)
            p = jnp.dot(u, wbr_ref[j], preferred_element_type=F32)
            mg = jnp.dot(h, wgm_ref[:, N_G + j * D:N_G + (j + 1) * D], preferred_element_type=F32)
            mg = _sigmoid(mg + bmg_ref[:, j * D:(j + 1) * D])
            z = mg * p if z is None else z + mg * p
        y = jnp.dot(z.astype(BF16), wout_ref[...], preferred_element_type=F32)
        ms = jnp.mean(y * y, axis=-1, keepdims=True)
        gate = mod[:, 2 * D:]
        out_ref[rows, :] = x + gate * (y * lax.rsqrt(ms + EPS) * gpost_ref[...])

    prev = gates(0)
    for i in range(1, n_sub):
        cur = gates(i)
        finish(i - 1, *prev)
        prev = cur
    finish(n_sub - 1, *prev)


def _merge_call(x_src, ctx_src, oa, ob, oc, mod4, g_pre, g_post, wgm, b_mg, wbr, wout, layer, s_len, rows_out,
                tm, sub):
    nb = x_src.shape[0]
    n_sub = tm // sub
    tok_specs, tok_args = _token_specs(x_src, ctx_src, n_sub, sub, s_len)
    ctx_step = None if ctx_src is None else rows_out // tm - 1
    tok = lambda b, t: (b, t, 0)
    lay2 = lambda b, t: (layer, 0, 0)
    const = dict(pipeline_mode=pl.Buffered(1))
    return pl.pallas_call(
        functools.partial(_merge_kernel, sub=sub, n_sub=n_sub, ctx_step=ctx_step),
        out_shape=jax.ShapeDtypeStruct((nb, rows_out, D), F32),
        grid=(nb, rows_out // tm),
        in_specs=tok_specs + [
            pl.BlockSpec((None, tm, QW), tok),
            pl.BlockSpec((None, tm, QW), tok),
            pl.BlockSpec((None, tm, QW), tok),
            pl.BlockSpec((None, None, 1, 3 * D), lambda b, t: (layer, b, 0, 0)),
            pl.BlockSpec((None, None, 1, 3 * D), lambda b, t: (layer, nb, 0, 0)),
            pl.BlockSpec((None, 1, D), lay2),
            pl.BlockSpec((None, 1, D), lay2),
            pl.BlockSpec((None, D, N_G + N_M), lay2, **const),
            pl.BlockSpec((None, 1, N_M), lay2),
            pl.BlockSpec((None, 3, QW, D), lambda b, t: (layer, 0, 0, 0), **const),
            pl.BlockSpec((None, D, D), lay2, **const),
        ],
        out_specs=pl.BlockSpec((None, tm, D), tok),
        compiler_params=_cparams(2),
        name="merge",
    )(*tok_args, oa, ob, oc, mod4, mod4, g_pre, g_post, wgm, b_mg, wbr, wout)


def _rope_tables(s_len, c_len):
    rows = s_len // GRID_W
    row = np.repeat(np.arange(rows), GRID_W).astype(np.float32)
    col = np.tile(np.arange(GRID_W), rows).astype(np.float32)
    freqs = (np.float32(ROPE_THETA) ** (-np.arange(ROPE_PAIRS, dtype=np.float32) / ROPE_PAIRS)).astype(np.float32)
    ang_r = row[:, None] * freqs
    ang_c = col[:, None] * freqs
    ang = np.concatenate([ang_r, ang_r, ang_c, ang_c], axis=-1)
    cos = np.concatenate([np.cos(ang), np.ones((c_len, HD), np.float32)], axis=0).astype(np.float32)
    sin = np.concatenate([np.sin(ang), np.zeros((c_len, HD), np.float32)], axis=0).astype(np.float32)
    first = (np.arange(HD) % 32) < 16
    sin_a = np.where(first, -sin, np.float32(0.0))
    sin_b = np.where(first, np.float32(0.0), sin)
    tile2 = lambda a: np.concatenate([a, a], axis=-1)
    cost = np.ascontiguousarray(cos.T)
    sint = np.ascontiguousarray((sin_a + sin_b).T)
    return tuple(jnp.asarray(a, F32) for a in (cost, sint, tile2(cos), tile2(sin_a), tile2(sin_b)))


def _cols(w, names):
    return jnp.concatenate([w[..., _IN[n][0]:_IN[n][1]] for n in names], axis=-1)


def kernel(x, c, ctx, c_ctx, w_ada, b_ada, g_pre, g_post, w_in, q_norm, k_norm, lam_q1, lam_k1, lam_q2,
           lam_k2, subln, sink, w_br_a, w_br_b, w_br_c, w_mg, b_mg, w_out):
    nb, s_len, _ = x.shape
    c_len = ctx.shape[1]
    depth = w_in.shape[0]
    sub = 256
    tm_all = 3 * sub
    tm_lat = 4 * sub
    tq_a, tq_b = 128, 256
    assert (s_len + c_len) % tm_all == 0 and s_len % tm_lat == 0 and s_len % sub == 0
    assert c_len == KCHUNK == sub

    wp = _cols(w_in, ("qa", "qb", "qc", "kb", "ka", "kc", "vb", "va", "vc")).astype(BF16)
    wgm = jnp.concatenate([_cols(w_in, ("ga", "gb", "gc")), w_mg], axis=-1).astype(BF16)
    wbr = jnp.stack([w_br_a, w_br_b, w_br_c], axis=1).astype(BF16)
    wout = w_out.astype(BF16)

    tabs = _rope_tables(s_len, c_len)
    gq_t = jnp.broadcast_to(q_norm[:, :, None], (depth, HD, sub))
    gk_n = jnp.concatenate([k_norm, k_norm], axis=-1)[:, None, :]
    lamv = jnp.stack([lam_q1, lam_k1, lam_q2, lam_k2], axis=1)
    subln_t = jnp.broadcast_to(subln[:, :, None], (depth, 2 * HD, tq_b))
    g_pre3 = g_pre[:, None, :]
    g_post3 = g_post[:, None, :]
    b_mg3 = b_mg[:, None, :]

    rows = ((nb + 1 + 7) // 8) * 8
    sc_in = jnp.concatenate([c, c_ctx[None, :], jnp.zeros((rows - nb - 1, D), F32)], axis=0)
    mod = _ada_call(sc_in, w_ada, b_ada)
    mod4 = mod[:, :, None, :]

    t_all = s_len + c_len
    x_src, ctx_src = x, (ctx, 0)
    for layer in range(depth):
        last = layer == depth - 1
        lam_init = 0.8 - 0.6 * math.exp(-0.3 * layer)
        qt_all, k_all, vt_all = _proj_call(x_src, ctx_src, mod4, g_pre3, wp, tabs, gq_t, gk_n, layer, s_len,
                                           t_all, tm_all, sub)
        oa, ob, oc = _attn_calls(qt_all, k_all, vt_all, sink, lamv, subln_t, layer, not last, s_len,
                                 lam_init, tq_a, tq_b)
        rows_out, tm = (s_len, tm_lat) if last else (t_all, tm_all)
        xs = _merge_call(x_src, None if last else ctx_src, oa, ob, oc, mod4, g_pre3, g_post3, wgm, b_mg3, wbr,
                         wout, layer, s_len, rows_out, tm, sub)
        x_src, ctx_src = xs, (xs, s_len // sub)
    return xs
```

```python
import functools
import math

import jax
import jax.numpy as jnp
import numpy as np
from jax import lax
from jax.experimental import pallas as pl
from jax.experimental.pallas import tpu as pltpu

F32 = jnp.float32
BF16 = jnp.bfloat16

D = 1024
HD = 64
GRID_W = 64
WINDOW = 128
ROPE_THETA = 10000.0
ROPE_PAIRS = HD // 4
EPS = 1e-6
SUBLN_EPS = 1e-5
ATTN_SCALE = HD ** -0.5
LOG2E = math.log2(math.e)

QW = 512
N_Q = 3 * QW
N_K = 512 + 128 + 128
N_V = 512 + 128 + 128
KA_TILE = 4
N_P = N_Q + N_K + N_V
N_G = 3 * QW
N_M = 3 * D

LANES = 128
VMEM_LIMIT = 56 * 1024 * 1024

_IN = dict(qa=(0, 512), ka=(512, 640), va=(640, 768), ga=(768, 1280),
           qb=(1280, 1792), kb=(1792, 2304), vb=(2304, 2816), gb=(2816, 3328),
           qc=(3328, 3840), kc=(3840, 3968), vc=(3968, 4096), gc=(4096, 4608))


def _sigmoid(v):
    return 1.0 / (1.0 + jnp.exp(-v))


def _cparams(n_axes):
    return pltpu.CompilerParams(dimension_semantics=("arbitrary",) * n_axes,
                                vmem_limit_bytes=VMEM_LIMIT)


def _ada_kernel(sc_ref, w_ref, b_ref, o_ref):
    v = sc_ref[...]
    s = (v * _sigmoid(v)).astype(BF16)
    o_ref[...] = jnp.dot(s, w_ref[...].astype(BF16), preferred_element_type=F32) + b_ref[...]


def _ada_call(sc_in, w_ada, b_ada):
    depth = w_ada.shape[0]
    rows = sc_in.shape[0]
    nblk = 3
    return pl.pallas_call(
        _ada_kernel,
        out_shape=jax.ShapeDtypeStruct((depth, rows, 3 * D), F32),
        grid=(depth, nblk),
        in_specs=[
            pl.BlockSpec((rows, D), lambda l, n: (0, 0)),
            pl.BlockSpec((None, D, D), lambda l, n: (l, 0, n)),
            pl.BlockSpec((None, 1, D), lambda l, n: (l, 0, n)),
        ],
        out_specs=pl.BlockSpec((None, rows, D), lambda l, n: (l, 0, n)),
        compiler_params=_cparams(2),
        name="adaln",
    )(sc_in, w_ada, b_ada.reshape(depth, 1, 3 * D))


def _modulated_norm(x, mod, gpre):
    shift = mod[:, :D]
    scale = mod[:, D:2 * D]
    ms = jnp.mean(x * x, axis=-1, keepdims=True)
    return x * lax.rsqrt(ms + EPS) * gpre * (1.0 + scale) + shift


def _sub_tile_input(x_refs, ctx_ref, i, ctx_step):
    x = x_refs[i][...]
    if ctx_ref is None or i != len(x_refs) - 1:
        return x, False
    is_ctx = pl.program_id(1) == ctx_step
    return jnp.where(is_ctx, ctx_ref[...], x), is_ctx


def _proj_kernel(*refs, sub, n_sub, ctx_step):
    x_refs, ctx_ref = refs[:n_sub], refs[n_sub]
    (modb_ref, modc_ref, gpre_ref, w_ref, cost_ref, sint_ref, cosn_ref, sina_ref, sinb_ref, gq_ref, gk_ref,
     qt_ref, k_ref, vt_ref) = refs[n_sub + 1:]

    def project(i):
        x, is_ctx = _sub_tile_input(x_refs, ctx_ref, i, ctx_step)
        mod = jnp.where(is_ctx, modc_ref[...], modb_ref[...])
        h = _modulated_norm(x, mod, gpre_ref[...])
        return jnp.dot(h.astype(BF16), w_ref[...], preferred_element_type=F32)

    def finish(i, y):
        rows = slice(i * sub, (i + 1) * sub)
        q3 = y[:, :N_Q].T.reshape(N_Q // HD, HD, sub)
        qa = q3[:8]
        ss = jnp.sum(qa * qa, axis=1, keepdims=True)
        qa = qa * lax.rsqrt(ss * (1.0 / HD) + EPS) * gq_ref[...][None]
        q3 = jnp.concatenate([qa, q3[8:]], axis=0)
        rot = jnp.concatenate([q3[:, 16:32], q3[:, 0:16], q3[:, 48:64], q3[:, 32:48]], axis=1)
        q3 = (q3 * cost_ref[:, rows][None] + rot * sint_ref[:, rows][None]) * (ATTN_SCALE * LOG2E)
        qt_ref[:, rows] = q3.reshape(N_Q, sub).astype(BF16)

        ka = y[:, N_Q + KA_TILE * LANES:N_Q + (KA_TILE + 1) * LANES]
        lane = lax.broadcasted_iota(jnp.int32, (1, LANES), 1)
        lo = lane < HD
        sq = ka * ka
        s_lo = jnp.sum(jnp.where(lo, sq, 0.0), axis=-1, keepdims=True)
        s_hi = jnp.sum(jnp.where(lo, 0.0, sq), axis=-1, keepdims=True)
        r = jnp.where(lo, lax.rsqrt(s_lo * (1.0 / HD) + EPS), lax.rsqrt(s_hi * (1.0 / HD) + EPS))
        ka = ka * r * gk_ref[...]
        cosn = cosn_ref[rows, :]
        sina = sina_ref[rows, :]
        sinb = sinb_ref[rows, :]
        for j in range(N_K // LANES):
            t = ka if j == KA_TILE else y[:, N_Q + j * LANES:N_Q + (j + 1) * LANES]
            t = t * cosn + pltpu.roll(t, LANES - 16, 1) * sina + pltpu.roll(t, 16, 1) * sinb
            k_ref[rows, j * LANES:(j + 1) * LANES] = t.astype(BF16)

        vt_ref[:, rows] = y[:, N_Q + N_K:].T.astype(BF16)

    y_prev = project(0)
    for i in range(1, n_sub):
        y = project(i)
        finish(i - 1, y_prev)
        y_prev = y
    finish(n_sub - 1, y_prev)


def _token_specs(x_src, ctx_src, n_sub, sub, s_len):
    last = s_len // sub - 1
    specs = [pl.BlockSpec((None, sub, D), functools.partial(lambda i, b, t: (b, jnp.minimum(n_sub * t + i, last), 0), i))
             for i in range(n_sub)]
    operands = [x_src] * n_sub
    if ctx_src is not None:
        arr, blk = ctx_src
        specs.append(pl.BlockSpec((None, sub, D), lambda b, t: (b, blk, 0)))
        operands.append(arr)
    return specs, operands


def _proj_call(x_src, ctx_src, mod4, g_pre, wp, tabs, gq_t, gk_n, layer, s_len, t_all, tm, sub):
    nb = x_src.shape[0]
    cost, sint, cosn, sina, sinb = tabs
    n_sub = tm // sub
    tok_specs, tok_args = _token_specs(x_src, ctx_src, n_sub, sub, s_len)
    return pl.pallas_call(
        functools.partial(_proj_kernel, sub=sub, n_sub=n_sub, ctx_step=t_all // tm - 1),
        out_shape=(jax.ShapeDtypeStruct((nb, N_Q, t_all), BF16),
                   jax.ShapeDtypeStruct((nb, t_all, N_K), BF16),
                   jax.ShapeDtypeStruct((nb, N_V, t_all), BF16)),
        grid=(nb, t_all // tm),
        in_specs=tok_specs + [
            pl.BlockSpec((None, None, 1, 3 * D), lambda b, t: (layer, b, 0, 0)),
            pl.BlockSpec((None, None, 1, 3 * D), lambda b, t: (layer, nb, 0, 0)),
            pl.BlockSpec((None, 1, D), lambda b, t: (layer, 0, 0)),
            pl.BlockSpec((None, D, N_P), lambda b, t: (layer, 0, 0)),
            pl.BlockSpec((HD, tm), lambda b, t: (0, t)),
            pl.BlockSpec((HD, tm), lambda b, t: (0, t)),
            pl.BlockSpec((tm, LANES), lambda b, t: (t, 0)),
            pl.BlockSpec((tm, LANES), lambda b, t: (t, 0)),
            pl.BlockSpec((tm, LANES), lambda b, t: (t, 0)),
            pl.BlockSpec((None, HD, sub), lambda b, t: (layer, 0, 0)),
            pl.BlockSpec((None, 1, LANES), lambda b, t: (layer, 0, 0)),
        ],
        out_specs=(pl.BlockSpec((None, N_Q, tm), lambda b, t: (b, 0, t)),
                   pl.BlockSpec((None, tm, N_K), lambda b, t: (b, t, 0)),
                   pl.BlockSpec((None, N_V, tm), lambda b, t: (b, 0, t))),
        compiler_params=_cparams(2),
        name="proj",
    )(*tok_args, mod4, mod4, g_pre, wp, cost, sint, cosn, sina, sinb, gq_t, gk_n)


KCHUNK = 256
ONES_ROWS = 16
UNITS_PER_BODY = 4
PIPE_AC = dict(lead=3, pace=6)
PIPE_B = dict(lead=4, pace=2)


def _small_softmax_pv(problems):
    ss = [jnp.dot(k, rhs, preferred_element_type=F32) for k, _, rhs, _ in problems]
    ms = [jnp.max(s, axis=0, keepdims=True) for s in ss]
    ms = [m if pr[3] is None else jnp.maximum(m, pr[3]) for m, pr in zip(ms, problems)]
    ps = [jnp.exp2(s - m).astype(BF16) for s, m in zip(ss, ms)]
    outs = []
    for p, m, (_, vt, _, extra) in zip(ps, ms, problems):
        dv = vt.shape[0]
        vt_aug = jnp.concatenate([vt, jnp.ones((ONES_ROWS, vt.shape[1]), BF16)], axis=0)
        pv = jnp.dot(vt_aug, p, preferred_element_type=F32)
        l = pv[dv:dv + 1]
        if extra is not None:
            l = l + jnp.exp2(extra - m)
        outs.append(pv[:dv] * (1.0 / l))
    return outs


def _gqa_rhs(qt, j, tq):
    z = jnp.zeros((HD, tq), qt.dtype)
    first = j == 0
    cols = []
    for g in range(4):
        qg = qt[g * HD:(g + 1) * HD, :]
        cols.append(jnp.concatenate([jnp.where(first, qg, z), jnp.where(first, z, qg)], axis=0))
    return jnp.concatenate(cols, axis=1)


def _store_heads(o_ref, ot, tq, nheads):
    o = jnp.concatenate([ot[:, g * tq:(g + 1) * tq] for g in range(nheads)], axis=0)
    o_ref[...] = o.T.astype(o_ref.dtype)


def _sink_row(sink_ref, base, tq):
    blk = lax.broadcasted_iota(jnp.int32, (1, 4 * tq), 1) // tq
    row = jnp.zeros((1, 4 * tq), F32)
    for g in range(4):
        row = jnp.where(blk == g, sink_ref[base + g], row)
    return row * LOG2E


def _tile(t, size):
    if isinstance(t, int):
        return pl.ds(t * size, size)
    return pl.ds(pl.multiple_of(t * size, size), size)


def _pipeline(make_rhs, score_chunks, value_chunks, finish, chunk_rows, out_shapes, scratch, n_units, zero,
              extra=None, *, lead, pace):
    main, head = scratch[:2], scratch[2:]
    n = len(chunk_rows)
    n_groups = len(out_shapes)
    last = n_units - 1
    assert n >= 2 * lead and n_units % 2 == 0
    ex = (lambda u: [None] * n_groups) if extra is None else extra

    def place(c, parity):
        if c < lead:
            return head[parity], sum(chunk_rows[:c])
        return main[parity], sum(chunk_rows[lead:c])

    def score(thunk, rhs, c, parity, cm):
        k, bias, g = thunk()
        s = jnp.dot(k, rhs[g], preferred_element_type=F32)
        if bias is not None:
            s = s + bias
        buf, row = place(c, parity)
        buf[row:row + chunk_rows[c], :] = s
        part = jnp.max(s, axis=0, keepdims=True)
        cm = list(cm)
        cm[g] = jnp.maximum(cm[g], part)
        return cm, part

    def with_extra(cm, u):
        return [m if e is None else jnp.maximum(m, e) for m, e in zip(cm, ex(u))]

    neg = [jnp.full((1, s[1]), -jnp.inf, F32) for s in out_shapes]

    rhs, sc, cm = make_rhs(0), score_chunks(0), neg
    for c in range(n):
        cm, _ = score(sc[c], rhs, c, 0, cm)
    m0 = with_extra(cm, 0)
    rhs, sc, cm1 = make_rhs(1), score_chunks(1), neg
    for c in range(lead):
        cm1, _ = score(sc[c], rhs, c, 1, cm1)

    def half(u, parity, m_u, cm1):
        u1 = jnp.minimum(u + 1, last)
        u2 = jnp.minimum(u + 2, last)
        rhs1, sc1 = make_rhs(u1), score_chunks(u1)
        rhs2, sc2 = make_rhs(u2), score_chunks(u2)
        vc, ex_u = value_chunks(u), ex(u)
        cm2 = neg
        acc = [None] * n_groups
        parts = []
        for c in range(n):
            if c + lead < n:
                cm1, part = score(sc1[c + lead], rhs1, c + lead, 1 - parity, cm1)
            else:
                cm2, part = score(sc2[c + lead - n], rhs2, c + lead - n, parity, cm2)
            parts.append(part)
            vt, g = vc[c]()
            m_c = m_u[g]
            if c >= pace:
                m_c = jnp.maximum(m_c, jnp.minimum(parts[c - pace], m_c))
            buf, row = place(c, parity)
            rows = pl.ds(pl.multiple_of(row + zero, LANES), chunk_rows[c])
            p = jnp.exp2((buf[rows, :] - m_c).astype(BF16))
            vt_aug = jnp.concatenate([vt, jnp.ones((ONES_ROWS, vt.shape[1]), BF16)], axis=0)
            pv = jnp.dot(vt_aug, p, preferred_element_type=F32)
            acc[g] = pv if acc[g] is None else acc[g] + pv
        ots = []
        for g in range(n_groups):
            dv = acc[g].shape[0] - ONES_ROWS
            l = acc[g][dv:dv + 1]
            if ex_u[g] is not None:
                l = l + jnp.exp2(ex_u[g] - m_u[g])
            ots.append(acc[g][:dv] * (1.0 / l))
        return with_extra(cm1, u1), cm2, ots

    def body(i, carry):
        m_u, cm_next, ot = carry
        u0 = UNITS_PER_BODY * i
        finish(jnp.maximum(u0 - 1, 0), ot)
        for h in range(UNITS_PER_BODY):
            if h:
                finish(u0 + h - 1, ot)
            m_u, cm_next, ot = half(u0 + h, h % 2, m_u, cm_next)
        return m_u, cm_next, ot

    assert n_units % UNITS_PER_BODY == 0
    carry = (m0, cm1, [jnp.zeros(s, F32) for s in out_shapes])
    _, _, ot_last = lax.fori_loop(0, n_units // UNITS_PER_BODY, body, carry)
    finish(last, ot_last)


def _score_scratch(chunk_rows, n_cols, lead):
    main = pltpu.VMEM((sum(chunk_rows[lead:]), n_cols), F32)
    head = pltpu.VMEM((sum(chunk_rows[:lead]), n_cols), F32)
    return [main, main, head, head]


def _split_unit(u, n_tiles):
    if isinstance(u, int):
        return u // n_tiles, u % n_tiles
    return lax.div(u, n_tiles), lax.rem(u, n_tiles)


def _rows(g, size):
    return _tile(g, size)


def _ac_chunk_rows(t_all, tq):
    wk = tq + 2 * WINDOW
    return ([KCHUNK] * (t_all // KCHUNK) + [KCHUNK]
            + [min(lo + KCHUNK, wk) - lo for lo in range(0, wk, KCHUNK)])


def _attn_ac_kernel(zero_ref, sink_ref, bias_ref, qa_ref, qc_ref, ka_ref, kc_ref, vta_ref, vtc_ref, oa_ref, oc_ref,
                    *scratch, tq, n_lat, n_ctx, s_len):
    t_all = ka_ref.shape[0]
    wk = tq + 2 * WINDOW
    spans = [(lo, min(lo + KCHUNK, wk)) for lo in range(0, wk, KCHUNK)]

    def rhs_of(q_ref, j, t):
        return _gqa_rhs(q_ref[_rows(j, 4 * HD), _tile(t, tq)], j, tq)

    def store(o_ref, j, t, ot):
        _store_heads(o_ref.at[_tile(t, tq), _rows(j, 4 * HD)], ot, tq, 4)

    def window(t):
        q0 = t * tq
        start = jnp.clip(q0 - WINDOW, 0, s_len - wk)
        return start, (q0 - start) // WINDOW

    def local(start, lo, hi):
        return pl.ds(pl.multiple_of(start + lo, LANES), hi - lo)

    def score_chunks(u):
        _, t = _split_unit(u, n_lat)
        start, variant = window(t)

        def loc(lo, hi):
            b = bias_ref[variant, lo:hi, :]
            return kc_ref[local(start, lo, hi), :], jnp.concatenate([b] * 4, axis=1), 1

        return ([functools.partial(lambda c: (ka_ref[c * KCHUNK:(c + 1) * KCHUNK, :], None, 0), c)
                 for c in range(t_all // KCHUNK)]
                + [lambda: (kc_ref[s_len:s_len + KCHUNK, :], None, 1)]
                + [functools.partial(loc, lo, hi) for lo, hi in spans])

    def value_chunks(u):
        j, t = _split_unit(u, n_lat)
        start, _ = window(t)
        return ([functools.partial(lambda c: (vta_ref[_rows(j, HD), c * KCHUNK:(c + 1) * KCHUNK], 0), c)
                 for c in range(t_all // KCHUNK)]
                + [lambda: (vtc_ref[_rows(j, HD), s_len:s_len + KCHUNK], 1)]
                + [functools.partial(lambda lo, hi: (vtc_ref[_rows(j, HD), local(start, lo, hi)], 1), lo, hi)
                   for lo, hi in spans])

    def make_rhs(u):
        j, t = _split_unit(u, n_lat)
        return [rhs_of(qa_ref, j, t), rhs_of(qc_ref, j, t)]

    def extra(u):
        j, _ = _split_unit(u, n_lat)
        return [None, _sink_row(sink_ref, j * 4, tq)]

    def finish(u, ots):
        j, t = _split_unit(u, n_lat)
        store(oa_ref, j, t, ots[0])
        store(oc_ref, j, t, ots[1])

    ctx_tiles = range(n_lat, n_lat + n_ctx)
    n = 4 * tq
    problems, targets = [], []
    for j in range(2 if n_ctx else 0):
        for q_ref, k_ref, vt_ref, o_ref, sink in ((qa_ref, ka_ref, vta_ref, oa_ref, None),
                                                  (qc_ref, kc_ref, vtc_ref, oc_ref, _sink_row(sink_ref, j * 4, tq))):
            rhs = jnp.concatenate([rhs_of(q_ref, j, t) for t in ctx_tiles], axis=1)
            sink_cols = None if sink is None else jnp.concatenate([sink] * n_ctx, axis=1)
            problems.append((k_ref[s_len:, :], vt_ref[_rows(j, HD), s_len:], rhs, sink_cols))
            targets.append((o_ref, j))
    for (o_ref, j), ot in zip(targets, _small_softmax_pv(problems)):
        for i, t in enumerate(ctx_tiles):
            store(o_ref, j, t, ot[:, i * n:(i + 1) * n])

    _pipeline(make_rhs, score_chunks, value_chunks, finish, _ac_chunk_rows(t_all, tq), [(HD, 4 * tq)] * 2, scratch,
              2 * n_lat, zero_ref[0], extra=extra, **PIPE_AC)


def _attn_b_kernel(zero_ref, lamv_ref, subln_ref, qt_ref, k_ref, vt_ref, o_ref, *scratch, tq, n_lat, n_ctx, s_len,
                   lam_init):
    lv = lamv_ref[...]
    lam = (jnp.exp(jnp.sum(lv[0:1] * lv[1:2], axis=-1, keepdims=True))
           - jnp.exp(jnp.sum(lv[2:3] * lv[3:4], axis=-1, keepdims=True)) + lam_init)
    t_all = k_ref.shape[0]
    n_chunks = t_all // KCHUNK
    hw = 2 * HD

    def rhs_of(h, t):
        qt = qt_ref[_rows(h, hw), _tile(t, tq)]
        z = jnp.zeros((HD, tq), qt.dtype)
        return jnp.concatenate([jnp.concatenate([qt[:HD], z], axis=0),
                                jnp.concatenate([z, qt[HD:]], axis=0)], axis=1)

    def store(h, t, o2):
        o = o2[:, :tq] - lam * o2[:, tq:]
        ms = jnp.mean(o * o, axis=0, keepdims=True)
        o = o * lax.rsqrt(ms + SUBLN_EPS) * subln_ref[...] * (1.0 - lam_init)
        o_ref[_tile(t, tq), _rows(h, hw)] = o.T.astype(o_ref.dtype)

    def score_chunks(u):
        h, _ = _split_unit(u, n_lat)
        return [functools.partial(lambda c: (k_ref[c * KCHUNK:(c + 1) * KCHUNK, _rows(h, hw)], None, 0), c)
                for c in range(n_chunks)]

    def value_chunks(u):
        h, _ = _split_unit(u, n_lat)
        return [functools.partial(lambda c: (vt_ref[_rows(h, hw), c * KCHUNK:(c + 1) * KCHUNK], 0), c)
                for c in range(n_chunks)]

    units = [(h, t) for h in range(4) for t in range(n_lat, n_lat + n_ctx)]
    problems = [(k_ref[s_len:, _rows(h, hw)], vt_ref[_rows(h, hw), s_len:], rhs_of(h, t), None) for h, t in units]
    for (h, t), o2 in zip(units, _small_softmax_pv(problems)):
        store(h, t, o2)

    _pipeline(lambda u: [rhs_of(*_split_unit(u, n_lat))], score_chunks, value_chunks,
              lambda u, ots: store(*_split_unit(u, n_lat), ots[0]), [KCHUNK] * n_chunks, [(hw, 2 * tq)], scratch,
              4 * n_lat, zero_ref[0], **PIPE_B)


def _window_bias(tq):
    wk = tq + 2 * WINDOW
    r = np.arange(wk)[:, None]
    c = np.arange(tq)[None, :]
    out = np.stack([np.where(np.abs(c - r + v * WINDOW) <= WINDOW, 0.0, -np.inf) for v in range(3)])
    return jnp.asarray(out, F32)


def _attn_calls(qt_all, k_all, vt_all, sink, lamv, subln_t, layer, with_ctx, s_len, lam_init,
                tq_a, tq_b):
    nb, _, t_all = qt_all.shape
    c_len = t_all - s_len
    o_rows = t_all if with_ctx else s_len
    o_shape = jax.ShapeDtypeStruct((nb, o_rows, QW), BF16)

    def steps(tq):
        n_lat = s_len // tq
        return n_lat, n_lat + (c_len // tq if with_ctx else 0)

    whole = lambda rows, cols, r, c: pl.BlockSpec((None, rows, cols), lambda b: (b, r, c))
    o_spec = pl.BlockSpec((None, o_rows, QW), lambda b: (b, 0, 0))
    zero = jnp.zeros((1,), jnp.int32)

    n_lat, n_all = steps(tq_a)
    wk = tq_a + 2 * WINDOW
    oa, oc = pl.pallas_call(
        functools.partial(_attn_ac_kernel, tq=tq_a, n_lat=n_lat, n_ctx=n_all - n_lat, s_len=s_len),
        out_shape=(o_shape, o_shape),
        grid=(nb,),
        in_specs=[
            pl.BlockSpec(memory_space=pltpu.SMEM),
            pl.BlockSpec(memory_space=pltpu.SMEM),
            pl.BlockSpec((3, wk, tq_a), lambda b: (0, 0, 0)),
            whole(QW, t_all, 0, 0), whole(QW, t_all, 2, 0),
            whole(t_all, LANES, 0, KA_TILE), whole(t_all, LANES, 0, KA_TILE + 1),
            whole(LANES, t_all, KA_TILE, 0), whole(LANES, t_all, KA_TILE + 1, 0),
        ],
        out_specs=(o_spec, o_spec),
        scratch_shapes=_score_scratch(_ac_chunk_rows(t_all, tq_a), 4 * tq_a, PIPE_AC["lead"]),
        compiler_params=_cparams(1),
        name="attn_ac",
    )(zero, sink[layer], _window_bias(tq_a), qt_all, qt_all, k_all, k_all, vt_all, vt_all)

    n_lat, n_all = steps(tq_b)
    ob = pl.pallas_call(
        functools.partial(_attn_b_kernel, tq=tq_b, n_lat=n_lat, n_ctx=n_all - n_lat, s_len=s_len,
                          lam_init=lam_init),
        out_shape=o_shape,
        grid=(nb,),
        in_specs=[
            pl.BlockSpec(memory_space=pltpu.SMEM),
            pl.BlockSpec((None, 4, HD), lambda b: (layer, 0, 0)),
            pl.BlockSpec((None, 2 * HD, tq_b), lambda b: (layer, 0, 0)),
            whole(QW, t_all, 1, 0), whole(t_all, QW, 0, 0), whole(QW, t_all, 0, 0),
        ],
        out_specs=o_spec,
        scratch_shapes=_score_scratch([KCHUNK] * (t_all // KCHUNK), 2 * tq_b, PIPE_B["lead"]),
        compiler_params=_cparams(1),
        name="attn_b",
    )(zero, lamv, subln_t, qt_all, k_all, vt_all)

    return oa, ob, oc


def _merge_kernel(*refs, sub, n_sub, ctx_step):
    x_refs = refs[:n_sub]
    rest = refs[n_sub:]
    ctx_ref = None
    if ctx_step is not None:
        ctx_ref, rest = rest[0], rest[1:]
    (oa_ref, ob_ref, oc_ref, modb_ref, modc_ref, gpre_ref, gpost_ref, wgm_ref, bmg_ref, wbr_ref, wout_ref,
     out_ref) = rest

    def gates(i):
        x, is_ctx = _sub_tile_input(x_refs, ctx_ref, i, ctx_step)
        mod = jnp.where(is_ctx, modc_ref[...], modb_ref[...])
        h = _modulated_norm(x, mod, gpre_ref[...])
        gm = jnp.dot(h.astype(BF16), wgm_ref[...], preferred_element_type=F32)
        return x, mod, gm

    def finish(i, x, mod, gm):
        rows = slice(i * sub, (i + 1) * sub)
        z = None
        for j, o_ref in enumerate((oa_ref, ob_ref, oc_ref)):
            g = gm[:, j * QW:(j + 1) * QW]
            u = (o_ref[rows, :].astype(F32) * (g * _sigmoid(g))).astype(BF16)
            p = jnp.dot(u, wbr_ref[j], preferred_element_type=F32)
            mg = _sigmoid(gm[:, N_G + j * D:N_G + (j + 1) * D] + bmg_ref[:, j * D:(j + 1) * D])
            z = mg * p if z is None else z + mg * p
        y = jnp.dot(z.astype(BF16), wout_ref[...], preferred_element_type=F32)
        ms = jnp.mean(y * y, axis=-1, keepdims=True)
        gate = mod[:, 2 * D:]
        out_ref[rows, :] = x + gate * (y * lax.rsqrt(ms + EPS) * gpost_ref[...])

    prev = gates(0)
    for i in range(1, n_sub):
        cur = gates(i)
        finish(i - 1, *prev)
        prev = cur
    finish(n_sub - 1, *prev)


def _merge_call(x_src, ctx_src, oa, ob, oc, mod4, g_pre, g_post, wgm, b_mg, wbr, wout, layer, s_len, rows_out,
                tm, sub):
    nb = x_src.shape[0]
    n_sub = tm // sub
    tok_specs, tok_args = _token_specs(x_src, ctx_src, n_sub, sub, s_len)
    ctx_step = None if ctx_src is None else rows_out // tm - 1
    tok = lambda b, t: (b, t, 0)
    lay2 = lambda b, t: (layer, 0, 0)
    const = dict(pipeline_mode=pl.Buffered(1))
    return pl.pallas_call(
        functools.partial(_merge_kernel, sub=sub, n_sub=n_sub, ctx_step=ctx_step),
        out_shape=jax.ShapeDtypeStruct((nb, rows_out, D), F32),
        grid=(nb, rows_out // tm),
        in_specs=tok_specs + [
            pl.BlockSpec((None, tm, QW), tok),
            pl.BlockSpec((None, tm, QW), tok),
            pl.BlockSpec((None, tm, QW), tok),
            pl.BlockSpec((None, None, 1, 3 * D), lambda b, t: (layer, b, 0, 0)),
            pl.BlockSpec((None, None, 1, 3 * D), lambda b, t: (layer, nb, 0, 0)),
            pl.BlockSpec((None, 1, D), lay2),
            pl.BlockSpec((None, 1, D), lay2),
            pl.BlockSpec((None, D, N_G + N_M), lay2, **const),
            pl.BlockSpec((None, 1, N_M), lay2),
            pl.BlockSpec((None, 3, QW, D), lambda b, t: (layer, 0, 0, 0), **const),
            pl.BlockSpec((None, D, D), lay2, **const),
        ],
        out_specs=pl.BlockSpec((None, tm, D), tok),
        compiler_params=_cparams(2),
        name="merge",
    )(*tok_args, oa, ob, oc, mod4, mod4, g_pre, g_post, wgm, b_mg, wbr, wout)


def _rope_tables(s_len, c_len):
    rows = s_len // GRID_W
    row = np.repeat(np.arange(rows), GRID_W).astype(np.float32)
    col = np.tile(np.arange(GRID_W), rows).astype(np.float32)
    freqs = (np.float32(ROPE_THETA) ** (-np.arange(ROPE_PAIRS, dtype=np.float32) / ROPE_PAIRS)).astype(np.float32)
    ang_r = row[:, None] * freqs
    ang_c = col[:, None] * freqs
    ang = np.concatenate([ang_r, ang_r, ang_c, ang_c], axis=-1)
    cos = np.concatenate([np.cos(ang), np.ones((c_len, HD), np.float32)], axis=0).astype(np.float32)
    sin = np.concatenate([np.sin(ang), np.zeros((c_len, HD), np.float32)], axis=0).astype(np.float32)
    first = (np.arange(HD) % 32) < 16
    sin_a = np.where(first, -sin, np.float32(0.0))
    sin_b = np.where(first, np.float32(0.0), sin)
    tile2 = lambda a: np.concatenate([a, a], axis=-1)
    cost = np.ascontiguousarray(cos.T)
    sint = np.ascontiguousarray((sin_a + sin_b).T)
    return tuple(jnp.asarray(a, F32) for a in (cost, sint, tile2(cos), tile2(sin_a), tile2(sin_b)))


def _cols(w, names):
    return jnp.concatenate([w[..., _IN[n][0]:_IN[n][1]] for n in names], axis=-1)


def kernel(x, c, ctx, c_ctx, w_ada, b_ada, g_pre, g_post, w_in, q_norm, k_norm, lam_q1, lam_k1, lam_q2,
           lam_k2, subln, sink, w_br_a, w_br_b, w_br_c, w_mg, b_mg, w_out):
    nb, s_len, _ = x.shape
    c_len = ctx.shape[1]
    depth = w_in.shape[0]
    sub = 256
    tm_all = 3 * sub
    tm_lat = 4 * sub
    tq_a, tq_b = 128, 256
    assert (s_len + c_len) % tm_all == 0 and s_len % tm_lat == 0 and s_len % sub == 0
    assert c_len == KCHUNK == sub

    wp = _cols(w_in, ("qa", "qb", "qc", "kb", "ka", "kc", "vb", "va", "vc")).astype(BF16)
    wgm = jnp.concatenate([_cols(w_in, ("ga", "gb", "gc")), w_mg], axis=-1).astype(BF16)
    wbr = jnp.stack([w_br_a, w_br_b, w_br_c], axis=1).astype(BF16)
    wout = w_out.astype(BF16)

    tabs = _rope_tables(s_len, c_len)
    gq_t = jnp.broadcast_to(q_norm[:, :, None], (depth, HD, sub))
    gk_n = jnp.concatenate([k_norm, k_norm], axis=-1)[:, None, :]
    lamv = jnp.stack([lam_q1, lam_k1, lam_q2, lam_k2], axis=1)
    subln_t = jnp.broadcast_to(subln[:, :, None], (depth, 2 * HD, tq_b))
    g_pre3 = g_pre[:, None, :]
    g_post3 = g_post[:, None, :]
    b_mg3 = b_mg[:, None, :]

    rows = ((nb + 1 + 7) // 8) * 8
    sc_in = jnp.concatenate([c, c_ctx[None, :], jnp.zeros((rows - nb - 1, D), F32)], axis=0)
    mod = _ada_call(sc_in, w_ada, b_ada)
    mod4 = mod[:, :, None, :]

    t_all = s_len + c_len
    x_src, ctx_src = x, (ctx, 0)
    for layer in range(depth):
        last = layer == depth - 1
        lam_init = 0.8 - 0.6 * math.exp(-0.3 * layer)
        qt_all, k_all, vt_all = _proj_call(x_src, ctx_src, mod4, g_pre3, wp, tabs, gq_t, gk_n, layer, s_len,
                                           t_all, tm_all, sub)
        oa, ob, oc = _attn_calls(qt_all, k_all, vt_all, sink, lamv, subln_t, layer, not last, s_len,
                                 lam_init, tq_a, tq_b)
        rows_out, tm = (s_len, tm_lat) if last else (t_all, tm_all)
        xs = _merge_call(x_src, None if last else ctx_src, oa, ob, oc, mod4, g_pre3, g_post3, wgm, b_mg3, wbr,
                         wout, layer, s_len, rows_out, tm, sub)
        x_src, ctx_src = xs, (xs, s_len // sub)
    return xs
```

```python
import functools
import math

import jax
import jax.numpy as jnp
import numpy as np
from jax import lax
from jax.experimental import pallas as pl
from jax.experimental.pallas import tpu as pltpu

F32 = jnp.float32
BF16 = jnp.bfloat16

D = 1024
HD = 64
GRID_W = 64
WINDOW = 128
ROPE_THETA = 10000.0
ROPE_PAIRS = HD // 4
EPS = 1e-6
SUBLN_EPS = 1e-5
ATTN_SCALE = HD ** -0.5
LOG2E = math.log2(math.e)

QW = 512
N_Q = 3 * QW
N_K = 512 + 128 + 128
N_V = 512 + 128 + 128
KA_TILE = 4
N_P = N_Q + N_K + N_V
N_G = 3 * QW
N_M = 3 * D

LANES = 128
VMEM_LIMIT = 56 * 1024 * 1024

_IN = dict(qa=(0, 512), ka=(512, 640), va=(640, 768), ga=(768, 1280),
           qb=(1280, 1792), kb=(1792, 2304), vb=(2304, 2816), gb=(2816, 3328),
           qc=(3328, 3840), kc=(3840, 3968), vc=(3968, 4096), gc=(4096, 4608))


def _sigmoid(v):
    return 1.0 / (1.0 + jnp.exp(-v))


def _cparams(n_axes):
    return pltpu.CompilerParams(dimension_semantics=("arbitrary",) * n_axes,
                                vmem_limit_bytes=VMEM_LIMIT)


def _ada_kernel(sc_ref, w_ref, b_ref, o_ref):
    v = sc_ref[...]
    s = (v * _sigmoid(v)).astype(BF16)
    o_ref[...] = jnp.dot(s, w_ref[...].astype(BF16), preferred_element_type=F32) + b_ref[...]


def _ada_call(sc_in, w_ada, b_ada):
    depth = w_ada.shape[0]
    rows = sc_in.shape[0]
    nblk = 3
    return pl.pallas_call(
        _ada_kernel,
        out_shape=jax.ShapeDtypeStruct((depth, rows, 3 * D), F32),
        grid=(depth, nblk),
        in_specs=[
            pl.BlockSpec((rows, D), lambda l, n: (0, 0)),
            pl.BlockSpec((None, D, D), lambda l, n: (l, 0, n)),
            pl.BlockSpec((None, 1, D), lambda l, n: (l, 0, n)),
        ],
        out_specs=pl.BlockSpec((None, rows, D), lambda l, n: (l, 0, n)),
        compiler_params=_cparams(2),
        name="adaln",
    )(sc_in, w_ada, b_ada.reshape(depth, 1, 3 * D))


def _modulated_norm(x, mod, gpre):
    shift = mod[:, :D]
    scale = mod[:, D:2 * D]
    ms = jnp.mean(x * x, axis=-1, keepdims=True)
    return x * lax.rsqrt(ms + EPS) * gpre * (1.0 + scale) + shift


def _sub_tile_input(x_refs, ctx_ref, i, ctx_step):
    x = x_refs[i][...]
    if ctx_ref is None or i != len(x_refs) - 1:
        return x, False
    is_ctx = pl.program_id(1) == ctx_step
    return jnp.where(is_ctx, ctx_ref[...], x), is_ctx


def _proj_kernel(*refs, sub, n_sub, ctx_step):
    x_refs, ctx_ref = refs[:n_sub], refs[n_sub]
    (modb_ref, modc_ref, gpre_ref, w_ref, cost_ref, sint_ref, cosn_ref, sina_ref, sinb_ref, gq_ref, gk_ref,
     qt_ref, k_ref, vt_ref) = refs[n_sub + 1:]

    def project(i):
        x, is_ctx = _sub_tile_input(x_refs, ctx_ref, i, ctx_step)
        mod = jnp.where(is_ctx, modc_ref[...], modb_ref[...])
        h = _modulated_norm(x, mod, gpre_ref[...])
        return jnp.dot(h.astype(BF16), w_ref[...], preferred_element_type=F32)

    def finish(i, y):
        rows = slice(i * sub, (i + 1) * sub)
        q3 = y[:, :N_Q].T.reshape(N_Q // HD, HD, sub)
        qa = q3[:8]
        ss = jnp.sum(qa * qa, axis=1, keepdims=True)
        qa = qa * lax.rsqrt(ss * (1.0 / HD) + EPS) * gq_ref[...][None]
        q3 = jnp.concatenate([qa, q3[8:]], axis=0)
        rot = jnp.concatenate([q3[:, 16:32], q3[:, 0:16], q3[:, 48:64], q3[:, 32:48]], axis=1)
        q3 = (q3 * cost_ref[:, rows][None] + rot * sint_ref[:, rows][None]) * (ATTN_SCALE * LOG2E)
        qt_ref[:, rows] = q3.reshape(N_Q, sub).astype(BF16)

        ka = y[:, N_Q + KA_TILE * LANES:N_Q + (KA_TILE + 1) * LANES]
        lane = lax.broadcasted_iota(jnp.int32, (1, LANES), 1)
        lo = lane < HD
        sq = ka * ka
        s_lo = jnp.sum(jnp.where(lo, sq, 0.0), axis=-1, keepdims=True)
        s_hi = jnp.sum(jnp.where(lo, 0.0, sq), axis=-1, keepdims=True)
        r = jnp.where(lo, lax.rsqrt(s_lo * (1.0 / HD) + EPS), lax.rsqrt(s_hi * (1.0 / HD) + EPS))
        ka = ka * r * gk_ref[...]
        cosn = cosn_ref[rows, :]
        sina = sina_ref[rows, :]
        sinb = sinb_ref[rows, :]
        for j in range(N_K // LANES):
            t = ka if j == KA_TILE else y[:, N_Q + j * LANES:N_Q + (j + 1) * LANES]
            t = t * cosn + pltpu.roll(t, LANES - 16, 1) * sina + pltpu.roll(t, 16, 1) * sinb
            k_ref[rows, j * LANES:(j + 1) * LANES] = t.astype(BF16)

        vt_ref[:, rows] = y[:, N_Q + N_K:].T.astype(BF16)

    y_prev = project(0)
    for i in range(1, n_sub):
        y = project(i)
        finish(i - 1, y_prev)
        y_prev = y
    finish(n_sub - 1, y_prev)


def _token_specs(x_src, ctx_src, n_sub, sub, s_len):
    last = s_len // sub - 1
    specs = [pl.BlockSpec((None, sub, D), functools.partial(lambda i, b, t: (b, jnp.minimum(n_sub * t + i, last), 0), i))
             for i in range(n_sub)]
    operands = [x_src] * n_sub
    if ctx_src is not None:
        arr, blk = ctx_src
        specs.append(pl.BlockSpec((None, sub, D), lambda b, t: (b, blk, 0)))
        operands.append(arr)
    return specs, operands


def _proj_call(x_src, ctx_src, mod4, g_pre, wp, tabs, gq_t, gk_n, layer, s_len, t_all, tm, sub):
    nb = x_src.shape[0]
    cost, sint, cosn, sina, sinb = tabs
    n_sub = tm // sub
    tok_specs, tok_args = _token_specs(x_src, ctx_src, n_sub, sub, s_len)
    return pl.pallas_call(
        functools.partial(_proj_kernel, sub=sub, n_sub=n_sub, ctx_step=t_all // tm - 1),
        out_shape=(jax.ShapeDtypeStruct((nb, N_Q, t_all), BF16),
                   jax.ShapeDtypeStruct((nb, t_all, N_K), BF16),
                   jax.ShapeDtypeStruct((nb, N_V, t_all), BF16)),
        grid=(nb, t_all // tm),
        in_specs=tok_specs + [
            pl.BlockSpec((None, None, 1, 3 * D), lambda b, t: (layer, b, 0, 0)),
            pl.BlockSpec((None, None, 1, 3 * D), lambda b, t: (layer, nb, 0, 0)),
            pl.BlockSpec((None, 1, D), lambda b, t: (layer, 0, 0)),
            pl.BlockSpec((None, D, N_P), lambda b, t: (layer, 0, 0)),
            pl.BlockSpec((HD, tm), lambda b, t: (0, t)),
            pl.BlockSpec((HD, tm), lambda b, t: (0, t)),
            pl.BlockSpec((tm, LANES), lambda b, t: (t, 0)),
            pl.BlockSpec((tm, LANES), lambda b, t: (t, 0)),
            pl.BlockSpec((tm, LANES), lambda b, t: (t, 0)),
            pl.BlockSpec((None, HD, sub), lambda b, t: (layer, 0, 0)),
            pl.BlockSpec((None, 1, LANES), lambda b, t: (layer, 0, 0)),
        ],
        out_specs=(pl.BlockSpec((None, N_Q, tm), lambda b, t: (b, 0, t)),
                   pl.BlockSpec((None, tm, N_K), lambda b, t: (b, t, 0)),
                   pl.BlockSpec((None, N_V, tm), lambda b, t: (b, 0, t))),
        compiler_params=_cparams(2),
        name="proj",
    )(*tok_args, mod4, mod4, g_pre, wp, cost, sint, cosn, sina, sinb, gq_t, gk_n)


KCHUNK = 256
ONES_ROWS = 16
UNITS_PER_BODY = 4
PIPE_AC = dict(lead=3, pace=6)
PIPE_B = dict(lead=4, pace=2)


def _small_softmax_pv(problems):
    ss = [jnp.dot(k, rhs, preferred_element_type=F32) for k, _, rhs, _ in problems]
    ms = [jnp.max(s, axis=0, keepdims=True) for s in ss]
    ms = [m if pr[3] is None else jnp.maximum(m, pr[3]) for m, pr in zip(ms, problems)]
    ps = [jnp.exp2(s - m).astype(BF16) for s, m in zip(ss, ms)]
    outs = []
    for p, m, (_, vt, _, extra) in zip(ps, ms, problems):
        dv = vt.shape[0]
        vt_aug = jnp.concatenate([vt, jnp.ones((ONES_ROWS, vt.shape[1]), BF16)], axis=0)
        pv = jnp.dot(vt_aug, p, preferred_element_type=F32)
        l = pv[dv:dv + 1]
        if extra is not None:
            l = l + jnp.exp2(extra - m)
        outs.append(pv[:dv] * (1.0 / l))
    return outs


def _gqa_rhs(qt, j, tq):
    z = jnp.zeros((HD, tq), qt.dtype)
    first = j == 0
    cols = []
    for g in range(4):
        qg = qt[g * HD:(g + 1) * HD, :]
        cols.append(jnp.concatenate([jnp.where(first, qg, z), jnp.where(first, z, qg)], axis=0))
    return jnp.concatenate(cols, axis=1)


def _store_heads(o_ref, ot, tq, nheads):
    o = jnp.concatenate([ot[:, g * tq:(g + 1) * tq] for g in range(nheads)], axis=0)
    o_ref[...] = o.T.astype(o_ref.dtype)


def _sink_row(sink_ref, base, tq):
    blk = lax.broadcasted_iota(jnp.int32, (1, 4 * tq), 1) // tq
    row = jnp.zeros((1, 4 * tq), F32)
    for g in range(4):
        row = jnp.where(blk == g, sink_ref[base + g], row)
    return row * LOG2E


def _tile(t, size):
    if isinstance(t, int):
        return pl.ds(t * size, size)
    return pl.ds(pl.multiple_of(t * size, size), size)


def _pipeline(make_rhs, score_chunks, value_chunks, finish, chunk_rows, out_shapes, scratch, n_units, zero,
              extra=None, *, lead, pace):
    main, head = scratch[:2], scratch[2:]
    n = len(chunk_rows)
    n_groups = len(out_shapes)
    last = n_units - 1
    assert n >= 2 * lead and n_units % 2 == 0
    ex = (lambda u: [None] * n_groups) if extra is None else extra

    def place(c, parity):
        if c < lead:
            return head[parity], sum(chunk_rows[:c])
        return main[parity], sum(chunk_rows[lead:c])

    def score(thunk, rhs, c, parity, cm):
        k, bias, g = thunk()
        s = jnp.dot(k, rhs[g], preferred_element_type=F32)
        if bias is not None:
            s = s + bias
        buf, row = place(c, parity)
        buf[row:row + chunk_rows[c], :] = s
        part = jnp.max(s, axis=0, keepdims=True)
        cm = list(cm)
        cm[g] = jnp.maximum(cm[g], part)
        return cm, part

    def with_extra(cm, u):
        return [m if e is None else jnp.maximum(m, e) for m, e in zip(cm, ex(u))]

    neg = [jnp.full((1, s[1]), -jnp.inf, F32) for s in out_shapes]

    rhs, sc, cm = make_rhs(0), score_chunks(0), neg
    for c in range(n):
        cm, _ = score(sc[c], rhs, c, 0, cm)
    m0 = with_extra(cm, 0)
    rhs, sc, cm1 = make_rhs(1), score_chunks(1), neg
    for c in range(lead):
        cm1, _ = score(sc[c], rhs, c, 1, cm1)

    def half(u, parity, m_u, cm1):
        u1 = jnp.minimum(u + 1, last)
        u2 = jnp.minimum(u + 2, last)
        rhs1, sc1 = make_rhs(u1), score_chunks(u1)
        sc2 = score_chunks(u2)
        vc, ex_u = value_chunks(u), ex(u)
        cm2 = neg
        acc = [None] * n_groups
        parts = []
        for c in range(n):
            if c + lead < n:
                cm1, part = score(sc1[c + lead], rhs1, c + lead, 1 - parity, cm1)
            else:
                if c + lead == n:
                    rhs2 = make_rhs(u2)
                cm2, part = score(sc2[c + lead - n], rhs2, c + lead - n, parity, cm2)
            parts.append(part)
            vt, g = vc[c]()
            m_c = m_u[g]
            if c >= pace:
                m_c = jnp.maximum(m_c, jnp.minimum(parts[c - pace], m_c))
            buf, row = place(c, parity)
            rows = pl.ds(pl.multiple_of(row + zero, LANES), chunk_rows[c])
            p = jnp.exp2(buf[rows, :] - m_c).astype(BF16)
            vt_aug = jnp.concatenate([vt, jnp.ones((ONES_ROWS, vt.shape[1]), BF16)], axis=0)
            pv = jnp.dot(vt_aug, p, preferred_element_type=F32)
            acc[g] = pv if acc[g] is None else acc[g] + pv
        ots = []
        for g in range(n_groups):
            dv = acc[g].shape[0] - ONES_ROWS
            l = acc[g][dv:dv + 1]
            if ex_u[g] is not None:
                l = l + jnp.exp2(ex_u[g] - m_u[g])
            ots.append(acc[g][:dv] * (1.0 / l))
        return with_extra(cm1, u1), cm2, ots

    def body(i, carry):
        m_u, cm_next, ot = carry
        u0 = UNITS_PER_BODY * i
        finish(jnp.maximum(u0 - 1, 0), ot)
        for h in range(UNITS_PER_BODY):
            if h:
                finish(u0 + h - 1, ot)
            m_u, cm_next, ot = half(u0 + h, h % 2, m_u, cm_next)
        return m_u, cm_next, ot

    assert n_units % UNITS_PER_BODY == 0
    carry = (m0, cm1, [jnp.zeros(s, F32) for s in out_shapes])
    _, _, ot_last = lax.fori_loop(0, n_units // UNITS_PER_BODY, body, carry)
    finish(last, ot_last)


def _score_scratch(chunk_rows, n_cols, lead):
    main = pltpu.VMEM((sum(chunk_rows[lead:]), n_cols), F32)
    head = pltpu.VMEM((sum(chunk_rows[:lead]), n_cols), F32)
    return [main, main, head, head]


def _split_unit(u, n_tiles):
    if isinstance(u, int):
        return u // n_tiles, u % n_tiles
    return lax.div(u, n_tiles), lax.rem(u, n_tiles)


def _rows(g, size):
    return _tile(g, size)


def _ac_chunk_rows(t_all, tq):
    wk = tq + 2 * WINDOW
    return ([KCHUNK] * (t_all // KCHUNK) + [KCHUNK]
            + [min(lo + KCHUNK, wk) - lo for lo in range(0, wk, KCHUNK)])


def _attn_ac_kernel(zero_ref, sink_ref, bias_ref, qa_ref, qc_ref, ka_ref, kc_ref, vta_ref, vtc_ref, oa_ref, oc_ref,
                    *scratch, tq, n_lat, n_ctx, s_len):
    t_all = ka_ref.shape[0]
    wk = tq + 2 * WINDOW
    spans = [(lo, min(lo + KCHUNK, wk)) for lo in range(0, wk, KCHUNK)]

    def rhs_of(q_ref, j, t):
        return _gqa_rhs(q_ref[_rows(j, 4 * HD), _tile(t, tq)], j, tq)

    def store(o_ref, j, t, ot):
        _store_heads(o_ref.at[_tile(t, tq), _rows(j, 4 * HD)], ot, tq, 4)

    def window(t):
        q0 = t * tq
        start = jnp.clip(q0 - WINDOW, 0, s_len - wk)
        return start, (q0 - start) // WINDOW

    def local(start, lo, hi):
        return pl.ds(pl.multiple_of(start + lo, LANES), hi - lo)

    def score_chunks(u):
        _, t = _split_unit(u, n_lat)
        start, variant = window(t)

        def loc(lo, hi):
            b = bias_ref[variant, lo:hi, :]
            return kc_ref[local(start, lo, hi), :], jnp.concatenate([b] * 4, axis=1), 1

        return ([functools.partial(lambda c: (ka_ref[c * KCHUNK:(c + 1) * KCHUNK, :], None, 0), c)
                 for c in range(t_all // KCHUNK)]
                + [lambda: (kc_ref[s_len:s_len + KCHUNK, :], None, 1)]
                + [functools.partial(loc, lo, hi) for lo, hi in spans])

    def value_chunks(u):
        j, t = _split_unit(u, n_lat)
        start, _ = window(t)
        return ([functools.partial(lambda c: (vta_ref[_rows(j, HD), c * KCHUNK:(c + 1) * KCHUNK], 0), c)
                 for c in range(t_all // KCHUNK)]
                + [lambda: (vtc_ref[_rows(j, HD), s_len:s_len + KCHUNK], 1)]
                + [functools.partial(lambda lo, hi: (vtc_ref[_rows(j, HD), local(start, lo, hi)], 1), lo, hi)
                   for lo, hi in spans])

    def make_rhs(u):
        j, t = _split_unit(u, n_lat)
        return [rhs_of(qa_ref, j, t), rhs_of(qc_ref, j, t)]

    def extra(u):
        j, _ = _split_unit(u, n_lat)
        return [None, _sink_row(sink_ref, j * 4, tq)]

    def finish(u, ots):
        j, t = _split_unit(u, n_lat)
        store(oa_ref, j, t, ots[0])
        store(oc_ref, j, t, ots[1])

    ctx_tiles = range(n_lat, n_lat + n_ctx)
    n = 4 * tq
    problems, targets = [], []
    for j in range(2 if n_ctx else 0):
        for q_ref, k_ref, vt_ref, o_ref, sink in ((qa_ref, ka_ref, vta_ref, oa_ref, None),
                                                  (qc_ref, kc_ref, vtc_ref, oc_ref, _sink_row(sink_ref, j * 4, tq))):
            rhs = jnp.concatenate([rhs_of(q_ref, j, t) for t in ctx_tiles], axis=1)
            sink_cols = None if sink is None else jnp.concatenate([sink] * n_ctx, axis=1)
            problems.append((k_ref[s_len:, :], vt_ref[_rows(j, HD), s_len:], rhs, sink_cols))
            targets.append((o_ref, j))
    for (o_ref, j), ot in zip(targets, _small_softmax_pv(problems)):
        for i, t in enumerate(ctx_tiles):
            store(o_ref, j, t, ot[:, i * n:(i + 1) * n])

    _pipeline(make_rhs, score_chunks, value_chunks, finish, _ac_chunk_rows(t_all, tq), [(HD, 4 * tq)] * 2, scratch,
              2 * n_lat, zero_ref[0], extra=extra, **PIPE_AC)


def _attn_b_kernel(zero_ref, lamv_ref, subln_ref, qt_ref, k_ref, vt_ref, o_ref, *scratch, tq, n_lat, n_ctx, s_len,
                   lam_init):
    lv = lamv_ref[...]
    lam = (jnp.exp(jnp.sum(lv[0:1] * lv[1:2], axis=-1, keepdims=True))
           - jnp.exp(jnp.sum(lv[2:3] * lv[3:4], axis=-1, keepdims=True)) + lam_init)
    t_all = k_ref.shape[0]
    n_chunks = t_all // KCHUNK
    hw = 2 * HD

    def rhs_of(h, t):
        qt = qt_ref[_rows(h, hw), _tile(t, tq)]
        z = jnp.zeros((HD, tq), qt.dtype)
        return jnp.concatenate([jnp.concatenate([qt[:HD], z], axis=0),
                                jnp.concatenate([z, qt[HD:]], axis=0)], axis=1)

    def store(h, t, o2):
        o = o2[:, :tq] - lam * o2[:, tq:]
        ms = jnp.mean(o * o, axis=0, keepdims=True)
        o = o * lax.rsqrt(ms + SUBLN_EPS) * subln_ref[...] * (1.0 - lam_init)
        o_ref[_tile(t, tq), _rows(h, hw)] = o.T.astype(o_ref.dtype)

    def score_chunks(u):
        h, _ = _split_unit(u, n_lat)
        return [functools.partial(lambda c: (k_ref[c * KCHUNK:(c + 1) * KCHUNK, _rows(h, hw)], None, 0), c)
                for c in range(n_chunks)]

    def value_chunks(u):
        h, _ = _split_unit(u, n_lat)
        return [functools.partial(lambda c: (vt_ref[_rows(h, hw), c * KCHUNK:(c + 1) * KCHUNK], 0), c)
                for c in range(n_chunks)]

    units = [(h, t) for h in range(4) for t in range(n_lat, n_lat + n_ctx)]
    problems = [(k_ref[s_len:, _rows(h, hw)], vt_ref[_rows(h, hw), s_len:], rhs_of(h, t), None) for h, t in units]
    for (h, t), o2 in zip(units, _small_softmax_pv(problems)):
        store(h, t, o2)

    _pipeline(lambda u: [rhs_of(*_split_unit(u, n_lat))], score_chunks, value_chunks,
              lambda u, ots: store(*_split_unit(u, n_lat), ots[0]), [KCHUNK] * n_chunks, [(hw, 2 * tq)], scratch,
              4 * n_lat, zero_ref[0], **PIPE_B)


def _window_bias(tq):
    wk = tq + 2 * WINDOW
    r = np.arange(wk)[:, None]
    c = np.arange(tq)[None, :]
    out = np.stack([np.where(np.abs(c - r + v * WINDOW) <= WINDOW, 0.0, -np.inf) for v in range(3)])
    return jnp.asarray(out, F32)


def _attn_calls(qt_all, k_all, vt_all, sink, lamv, subln_t, layer, with_ctx, s_len, lam_init,
                tq_a, tq_b):
    nb, _, t_all = qt_all.shape
    c_len = t_all - s_len
    o_rows = t_all if with_ctx else s_len
    o_shape = jax.ShapeDtypeStruct((nb, o_rows, QW), BF16)

    def steps(tq):
        n_lat = s_len // tq
        return n_lat, n_lat + (c_len // tq if with_ctx else 0)

    whole = lambda rows, cols, r, c: pl.BlockSpec((None, rows, cols), lambda b: (b, r, c))
    o_spec = pl.BlockSpec((None, o_rows, QW), lambda b: (b, 0, 0))
    zero = jnp.zeros((1,), jnp.int32)

    n_lat, n_all = steps(tq_a)
    wk = tq_a + 2 * WINDOW
    oa, oc = pl.pallas_call(
        functools.partial(_attn_ac_kernel, tq=tq_a, n_lat=n_lat, n_ctx=n_all - n_lat, s_len=s_len),
        out_shape=(o_shape, o_shape),
        grid=(nb,),
        in_specs=[
            pl.BlockSpec(memory_space=pltpu.SMEM),
            pl.BlockSpec(memory_space=pltpu.SMEM),
            pl.BlockSpec((3, wk, tq_a), lambda b: (0, 0, 0)),
            whole(QW, t_all, 0, 0), whole(QW, t_all, 2, 0),
            whole(t_all, LANES, 0, KA_TILE), whole(t_all, LANES, 0, KA_TILE + 1),
            whole(LANES, t_all, KA_TILE, 0), whole(LANES, t_all, KA_TILE + 1, 0),
        ],
        out_specs=(o_spec, o_spec),
        scratch_shapes=_score_scratch(_ac_chunk_rows(t_all, tq_a), 4 * tq_a, PIPE_AC["lead"]),
        compiler_params=_cparams(1),
        name="attn_ac",
    )(zero, sink[layer], _window_bias(tq_a), qt_all, qt_all, k_all, k_all, vt_all, vt_all)

    n_lat, n_all = steps(tq_b)
    ob = pl.pallas_call(
        functools.partial(_attn_b_kernel, tq=tq_b, n_lat=n_lat, n_ctx=n_all - n_lat, s_len=s_len,
                          lam_init=lam_init),
        out_shape=o_shape,
        grid=(nb,),
        in_specs=[
            pl.BlockSpec(memory_space=pltpu.SMEM),
            pl.BlockSpec((None, 4, HD), lambda b: (layer, 0, 0)),
            pl.BlockSpec((None, 2 * HD, tq_b), lambda b: (layer, 0, 0)),
            whole(QW, t_all, 1, 0), whole(t_all, QW, 0, 0), whole(QW, t_all, 0, 0),
        ],
        out_specs=o_spec,
        scratch_shapes=_score_scratch([KCHUNK] * (t_all // KCHUNK), 2 * tq_b, PIPE_B["lead"]),
        compiler_params=_cparams(1),
        name="attn_b",
    )(zero, lamv, subln_t, qt_all, k_all, vt_all)

    return oa, ob, oc


def _merge_kernel(*refs, sub, n_sub, ctx_step):
    x_refs = refs[:n_sub]
    rest = refs[n_sub:]
    ctx_ref = None
    if ctx_step is not None:
        ctx_ref, rest = rest[0], rest[1:]
    (oa_ref, ob_ref, oc_ref, modb_ref, modc_ref, gpre_ref, gpost_ref, wgm_ref, bmg_ref, wbr_ref, wout_ref,
     out_ref) = rest

    def gates(i):
        x, is_ctx = _sub_tile_input(x_refs, ctx_ref, i, ctx_step)
        mod = jnp.where(is_ctx, modc_ref[...], modb_ref[...])
        h = _modulated_norm(x, mod, gpre_ref[...])
        gm = jnp.dot(h.astype(BF16), wgm_ref[...], preferred_element_type=F32)
        return x, mod, gm

    def finish(i, x, mod, gm):
        rows = slice(i * sub, (i + 1) * sub)
        z = None
        for j, o_ref in enumerate((oa_ref, ob_ref, oc_ref)):
            g = gm[:, j * QW:(j + 1) * QW]
            u = (o_ref[rows, :].astype(F32) * (g * _sigmoid(g))).astype(BF16)
            p = jnp.dot(u, wbr_ref[j], preferred_element_type=F32)
            mg = _sigmoid(gm[:, N_G + j * D:N_G + (j + 1) * D] + bmg_ref[:, j * D:(j + 1) * D])
            z = mg * p if z is None else z + mg * p
        y = jnp.dot(z.astype(BF16), wout_ref[...], preferred_element_type=F32)
        ms = jnp.mean(y * y, axis=-1, keepdims=True)
        gate = mod[:, 2 * D:]
        out_ref[rows, :] = x + gate * (y * lax.rsqrt(ms + EPS) * gpost_ref[...])

    prev = gates(0)
    for i in range(1, n_sub):
        cur = gates(i)
        finish(i - 1, *prev)
        prev = cur
    finish(n_sub - 1, *prev)


def _merge_call(x_src, ctx_src, oa, ob, oc, mod4, g_pre, g_post, wgm, b_mg, wbr, wout, layer, s_len, rows_out,
                tm, sub):
    nb = x_src.shape[0]
    n_sub = tm // sub
    tok_specs, tok_args = _token_specs(x_src, ctx_src, n_sub, sub, s_len)
    ctx_step = None if ctx_src is None else rows_out // tm - 1
    tok = lambda b, t: (b, t, 0)
    lay2 = lambda b, t: (layer, 0, 0)
    const = dict(pipeline_mode=pl.Buffered(1))
    return pl.pallas_call(
        functools.partial(_merge_kernel, sub=sub, n_sub=n_sub, ctx_step=ctx_step),
        out_shape=jax.ShapeDtypeStruct((nb, rows_out, D), F32),
        grid=(nb, rows_out // tm),
        in_specs=tok_specs + [
            pl.BlockSpec((None, tm, QW), tok),
            pl.BlockSpec((None, tm, QW), tok),
            pl.BlockSpec((None, tm, QW), tok),
            pl.BlockSpec((None, None, 1, 3 * D), lambda b, t: (layer, b, 0, 0)),
            pl.BlockSpec((None, None, 1, 3 * D), lambda b, t: (layer, nb, 0, 0)),
            pl.BlockSpec((None, 1, D), lay2),
            pl.BlockSpec((None, 1, D), lay2),
            pl.BlockSpec((None, D, N_G + N_M), lay2, **const),
            pl.BlockSpec((None, 1, N_M), lay2),
            pl.BlockSpec((None, 3, QW, D), lambda b, t: (layer, 0, 0, 0), **const),
            pl.BlockSpec((None, D, D), lay2, **const),
        ],
        out_specs=pl.BlockSpec((None, tm, D), tok),
        compiler_params=_cparams(2),
        name="merge",
    )(*tok_args, oa, ob, oc, mod4, mod4, g_pre, g_post, wgm, b_mg, wbr, wout)


def _rope_tables(s_len, c_len):
    rows = s_len // GRID_W
    row = np.repeat(np.arange(rows), GRID_W).astype(np.float32)
    col = np.tile(np.arange(GRID_W), rows).astype(np.float32)
    freqs = (np.float32(ROPE_THETA) ** (-np.arange(ROPE_PAIRS, dtype=np.float32) / ROPE_PAIRS)).astype(np.float32)
    ang_r = row[:, None] * freqs
    ang_c = col[:, None] * freqs
    ang = np.concatenate([ang_r, ang_r, ang_c, ang_c], axis=-1)
    cos = np.concatenate([np.cos(ang), np.ones((c_len, HD), np.float32)], axis=0).astype(np.float32)
    sin = np.concatenate([np.sin(ang), np.zeros((c_len, HD), np.float32)], axis=0).astype(np.float32)
    first = (np.arange(HD) % 32) < 16
    sin_a = np.where(first, -sin, np.float32(0.0))
    sin_b = np.where(first, np.float32(0.0), sin)
    tile2 = lambda a: np.concatenate([a, a], axis=-1)
    cost = np.ascontiguousarray(cos.T)
    sint = np.ascontiguousarray((sin_a + sin_b).T)
    return tuple(jnp.asarray(a, F32) for a in (cost, sint, tile2(cos), tile2(sin_a), tile2(sin_b)))


def _cols(w, names):
    return jnp.concatenate([w[..., _IN[n][0]:_IN[n][1]] for n in names], axis=-1)


def kernel(x, c, ctx, c_ctx, w_ada, b_ada, g_pre, g_post, w_in, q_norm, k_norm, lam_q1, lam_k1, lam_q2,
           lam_k2, subln, sink, w_br_a, w_br_b, w_br_c, w_mg, b_mg, w_out):
    nb, s_len, _ = x.shape
    c_len = ctx.shape[1]
    depth = w_in.shape[0]
    sub = 256
    tm_all = 3 * sub
    tm_lat = 4 * sub
    tq_a, tq_b = 128, 256
    assert (s_len + c_len) % tm_all == 0 and s_len % tm_lat == 0 and s_len % sub == 0
    assert c_len == KCHUNK == sub

    wp = _cols(w_in, ("qa", "qb", "qc", "kb", "ka", "kc", "vb", "va", "vc")).astype(BF16)
    wgm = jnp.concatenate([_cols(w_in, ("ga", "gb", "gc")), w_mg], axis=-1).astype(BF16)
    wbr = jnp.stack([w_br_a, w_br_b, w_br_c], axis=1).astype(BF16)
    wout = w_out.astype(BF16)

    tabs = _rope_tables(s_len, c_len)
    gq_t = jnp.broadcast_to(q_norm[:, :, None], (depth, HD, sub))
    gk_n = jnp.concatenate([k_norm, k_norm], axis=-1)[:, None, :]
    lamv = jnp.stack([lam_q1, lam_k1, lam_q2, lam_k2], axis=1)
    subln_t = jnp.broadcast_to(subln[:, :, None], (depth, 2 * HD, tq_b))
    g_pre3 = g_pre[:, None, :]
    g_post3 = g_post[:, None, :]
    b_mg3 = b_mg[:, None, :]

    rows = ((nb + 1 + 7) // 8) * 8
    sc_in = jnp.concatenate([c, c_ctx[None, :], jnp.zeros((rows - nb - 1, D), F32)], axis=0)
    mod = _ada_call(sc_in, w_ada, b_ada)
    mod4 = mod[:, :, None, :]

    t_all = s_len + c_len
    x_src, ctx_src = x, (ctx, 0)
    for layer in range(depth):
        last = layer == depth - 1
        lam_init = 0.8 - 0.6 * math.exp(-0.3 * layer)
        qt_all, k_all, vt_all = _proj_call(x_src, ctx_src, mod4, g_pre3, wp, tabs, gq_t, gk_n, layer, s_len,
                                           t_all, tm_all, sub)
        oa, ob, oc = _attn_calls(qt_all, k_all, vt_all, sink, lamv, subln_t, layer, not last, s_len,
                                 lam_init, tq_a, tq_b)
        rows_out, tm = (s_len, tm_lat) if last else (t_all, tm_all)
        xs = _merge_call(x_src, None if last else ctx_src, oa, ob, oc, mod4, g_pre3, g_post3, wgm, b_mg3, wbr,
                         wout, layer, s_len, rows_out, tm, sub)
        x_src, ctx_src = xs, (xs, s_len // sub)
    return xs
```

```python
import functools
import math

import jax
import jax.numpy as jnp
import numpy as np
from jax import lax
from jax.experimental import pallas as pl
from jax.experimental.pallas import tpu as pltpu

F32 = jnp.float32
BF16 = jnp.bfloat16

D = 1024
HD = 64
GRID_W = 64
WINDOW = 128
ROPE_THETA = 10000.0
ROPE_PAIRS = HD // 4
EPS = 1e-6
SUBLN_EPS = 1e-5
ATTN_SCALE = HD ** -0.5
LOG2E = math.log2(math.e)

QW = 512
N_Q = 3 * QW
N_K = 512 + 128 + 128
N_V = 512 + 128 + 128
KA_TILE = 4
N_P = N_Q + N_K + N_V
N_G = 3 * QW
N_M = 3 * D

LANES = 128
VMEM_LIMIT = 56 * 1024 * 1024

_IN = dict(qa=(0, 512), ka=(512, 640), va=(640, 768), ga=(768, 1280),
           qb=(1280, 1792), kb=(1792, 2304), vb=(2304, 2816), gb=(2816, 3328),
           qc=(3328, 3840), kc=(3840, 3968), vc=(3968, 4096), gc=(4096, 4608))


def _sigmoid(v):
    return 1.0 / (1.0 + jnp.exp(-v))


def _cparams(n_axes):
    return pltpu.CompilerParams(dimension_semantics=("arbitrary",) * n_axes,
                                vmem_limit_bytes=VMEM_LIMIT)


def _ada_kernel(sc_ref, w_ref, b_ref, o_ref):
    v = sc_ref[...]
    s = (v * _sigmoid(v)).astype(BF16)
    o_ref[...] = jnp.dot(s, w_ref[...].astype(BF16), preferred_element_type=F32) + b_ref[...]


def _ada_call(sc_in, w_ada, b_ada):
    depth = w_ada.shape[0]
    rows = sc_in.shape[0]
    nblk = 3
    return pl.pallas_call(
        _ada_kernel,
        out_shape=jax.ShapeDtypeStruct((depth, rows, 3 * D), F32),
        grid=(depth, nblk),
        in_specs=[
            pl.BlockSpec((rows, D), lambda l, n: (0, 0)),
            pl.BlockSpec((None, D, D), lambda l, n: (l, 0, n)),
            pl.BlockSpec((None, 1, D), lambda l, n: (l, 0, n)),
        ],
        out_specs=pl.BlockSpec((None, rows, D), lambda l, n: (l, 0, n)),
        compiler_params=_cparams(2),
        name="adaln",
    )(sc_in, w_ada, b_ada.reshape(depth, 1, 3 * D))


def _modulated_norm(x, mod, gpre):
    shift = mod[:, :D]
    scale = mod[:, D:2 * D]
    ms = jnp.mean(x * x, axis=-1, keepdims=True)
    return x * lax.rsqrt(ms + EPS) * gpre * (1.0 + scale) + shift


def _sub_tile_input(x_refs, ctx_ref, i, ctx_step):
    x = x_refs[i][...]
    if ctx_ref is None or i != len(x_refs) - 1:
        return x, False
    is_ctx = pl.program_id(1) == ctx_step
    return jnp.where(is_ctx, ctx_ref[...], x), is_ctx


def _proj_kernel(*refs, sub, n_sub, ctx_step):
    x_refs, ctx_ref = refs[:n_sub], refs[n_sub]
    (modb_ref, modc_ref, gpre_ref, w_ref, cost_ref, sint_ref, cosn_ref, sina_ref, sinb_ref, gq_ref, gk_ref,
     qt_ref, k_ref, vt_ref) = refs[n_sub + 1:]

    def project(i):
        x, is_ctx = _sub_tile_input(x_refs, ctx_ref, i, ctx_step)
        mod = jnp.where(is_ctx, modc_ref[...], modb_ref[...])
        h = _modulated_norm(x, mod, gpre_ref[...])
        return jnp.dot(h.astype(BF16), w_ref[...], preferred_element_type=F32)

    def finish(i, y):
        rows = slice(i * sub, (i + 1) * sub)
        q3 = y[:, :N_Q].T.reshape(N_Q // HD, HD, sub)
        qa = q3[:8]
        ss = jnp.sum(qa * qa, axis=1, keepdims=True)
        qa = qa * lax.rsqrt(ss * (1.0 / HD) + EPS) * gq_ref[...][None]
        q3 = jnp.concatenate([qa, q3[8:]], axis=0)
        rot = jnp.concatenate([q3[:, 16:32], q3[:, 0:16], q3[:, 48:64], q3[:, 32:48]], axis=1)
        q3 = (q3 * cost_ref[:, rows][None] + rot * sint_ref[:, rows][None]) * (ATTN_SCALE * LOG2E)
        qt_ref[:, rows] = q3.reshape(N_Q, sub).astype(BF16)

        ka = y[:, N_Q + KA_TILE * LANES:N_Q + (KA_TILE + 1) * LANES]
        lane = lax.broadcasted_iota(jnp.int32, (1, LANES), 1)
        lo = lane < HD
        sq = ka * ka
        s_lo = jnp.sum(jnp.where(lo, sq, 0.0), axis=-1, keepdims=True)
        s_hi = jnp.sum(jnp.where(lo, 0.0, sq), axis=-1, keepdims=True)
        r = jnp.where(lo, lax.rsqrt(s_lo * (1.0 / HD) + EPS), lax.rsqrt(s_hi * (1.0 / HD) + EPS))
        ka = ka * r * gk_ref[...]
        cosn = cosn_ref[rows, :]
        sina = sina_ref[rows, :]
        sinb = sinb_ref[rows, :]
        for j in range(N_K // LANES):
            t = ka if j == KA_TILE else y[:, N_Q + j * LANES:N_Q + (j + 1) * LANES]
            t = t * cosn + pltpu.roll(t, LANES - 16, 1) * sina + pltpu.roll(t, 16, 1) * sinb
            k_ref[rows, j * LANES:(j + 1) * LANES] = t.astype(BF16)

        vt_ref[:, rows] = y[:, N_Q + N_K:].T.astype(BF16)

    y_prev = project(0)
    for i in range(1, n_sub):
        y = project(i)
        finish(i - 1, y_prev)
        y_prev = y
    finish(n_sub - 1, y_prev)


def _token_specs(x_src, ctx_src, n_sub, sub, s_len):
    last = s_len // sub - 1
    specs = [pl.BlockSpec((None, sub, D), functools.partial(lambda i, b, t: (b, jnp.minimum(n_sub * t + i, last), 0), i))
             for i in range(n_sub)]
    operands = [x_src] * n_sub
    if ctx_src is not None:
        arr, blk = ctx_src
        specs.append(pl.BlockSpec((None, sub, D), lambda b, t: (b, blk, 0)))
        operands.append(arr)
    return specs, operands


def _proj_call(x_src, ctx_src, mod4, g_pre, wp, tabs, gq_t, gk_n, layer, s_len, t_all, tm, sub):
    nb = x_src.shape[0]
    cost, sint, cosn, sina, sinb = tabs
    n_sub = tm // sub
    tok_specs, tok_args = _token_specs(x_src, ctx_src, n_sub, sub, s_len)
    return pl.pallas_call(
        functools.partial(_proj_kernel, sub=sub, n_sub=n_sub, ctx_step=t_all // tm - 1),
        out_shape=(jax.ShapeDtypeStruct((nb, N_Q, t_all), BF16),
                   jax.ShapeDtypeStruct((nb, t_all, N_K), BF16),
                   jax.ShapeDtypeStruct((nb, N_V, t_all), BF16)),
        grid=(nb, t_all // tm),
        in_specs=tok_specs + [
            pl.BlockSpec((None, None, 1, 3 * D), lambda b, t: (layer, b, 0, 0)),
            pl.BlockSpec((None, None, 1, 3 * D), lambda b, t: (layer, nb, 0, 0)),
            pl.BlockSpec((None, 1, D), lambda b, t: (layer, 0, 0)),
            pl.BlockSpec((None, D, N_P), lambda b, t: (layer, 0, 0), pipeline_mode=pl.Buffered(1)),
            pl.BlockSpec((HD, tm), lambda b, t: (0, t)),
            pl.BlockSpec((HD, tm), lambda b, t: (0, t)),
            pl.BlockSpec((tm, LANES), lambda b, t: (t, 0)),
            pl.BlockSpec((tm, LANES), lambda b, t: (t, 0)),
            pl.BlockSpec((tm, LANES), lambda b, t: (t, 0)),
            pl.BlockSpec((None, HD, sub), lambda b, t: (layer, 0, 0)),
            pl.BlockSpec((None, 1, LANES), lambda b, t: (layer, 0, 0)),
        ],
        out_specs=(pl.BlockSpec((None, N_Q, tm), lambda b, t: (b, 0, t)),
                   pl.BlockSpec((None, tm, N_K), lambda b, t: (b, t, 0)),
                   pl.BlockSpec((None, N_V, tm), lambda b, t: (b, 0, t))),
        compiler_params=_cparams(2),
        name="proj",
    )(*tok_args, mod4, mod4, g_pre, wp, cost, sint, cosn, sina, sinb, gq_t, gk_n)


KCHUNK = 256
ONES_ROWS = 16
UNITS_PER_BODY = 4
PIPE_AC = dict(lead=3, pace=6)
PIPE_B = dict(lead=4, pace=2)


def _small_softmax_pv(problems):
    ss = [jnp.dot(k, rhs, preferred_element_type=F32) for k, _, rhs, _ in problems]
    ms = [jnp.max(s, axis=0, keepdims=True) for s in ss]
    ms = [m if pr[3] is None else jnp.maximum(m, pr[3]) for m, pr in zip(ms, problems)]
    ps = [jnp.exp2(s - m).astype(BF16) for s, m in zip(ss, ms)]
    outs = []
    for p, m, (_, vt, _, extra) in zip(ps, ms, problems):
        dv = vt.shape[0]
        vt_aug = jnp.concatenate([vt, jnp.ones((ONES_ROWS, vt.shape[1]), BF16)], axis=0)
        pv = jnp.dot(vt_aug, p, preferred_element_type=F32)
        l = pv[dv:dv + 1]
        if extra is not None:
            l = l + jnp.exp2(extra - m)
        outs.append(pv[:dv] * (1.0 / l))
    return outs


def _gqa_rhs(qt, j, tq):
    z = jnp.zeros((HD, tq), qt.dtype)
    first = j == 0
    cols = []
    for g in range(4):
        qg = qt[g * HD:(g + 1) * HD, :]
        cols.append(jnp.concatenate([jnp.where(first, qg, z), jnp.where(first, z, qg)], axis=0))
    return jnp.concatenate(cols, axis=1)


def _store_heads(o_ref, ot, tq, nheads):
    o = jnp.concatenate([ot[:, g * tq:(g + 1) * tq] for g in range(nheads)], axis=0)
    o_ref[...] = o.T.astype(o_ref.dtype)


def _sink_row(sink_ref, base, tq):
    blk = lax.broadcasted_iota(jnp.int32, (1, 4 * tq), 1) // tq
    row = jnp.zeros((1, 4 * tq), F32)
    for g in range(4):
        row = jnp.where(blk == g, sink_ref[base + g], row)
    return row * LOG2E


def _tile(t, size):
    if isinstance(t, int):
        return pl.ds(t * size, size)
    return pl.ds(pl.multiple_of(t * size, size), size)


def _pipeline(make_rhs, score_chunks, value_chunks, finish, chunk_rows, out_shapes, scratch, n_units, zero,
              extra=None, *, lead, pace):
    main, head = scratch[:2], scratch[2:]
    n = len(chunk_rows)
    n_groups = len(out_shapes)
    last = n_units - 1
    assert n >= 2 * lead and n_units % 2 == 0
    ex = (lambda u: [None] * n_groups) if extra is None else extra

    def place(c, parity):
        if c < lead:
            return head[parity], sum(chunk_rows[:c])
        return main[parity], sum(chunk_rows[lead:c])

    def score(thunk, rhs, c, parity, cm):
        k, bias, g = thunk()
        s = jnp.dot(k, rhs[g], preferred_element_type=F32)
        if bias is not None:
            s = s + bias
        buf, row = place(c, parity)
        buf[row:row + chunk_rows[c], :] = s
        part = jnp.max(s, axis=0, keepdims=True)
        cm = list(cm)
        cm[g] = jnp.maximum(cm[g], part)
        return cm, part

    def with_extra(cm, u):
        return [m if e is None else jnp.maximum(m, e) for m, e in zip(cm, ex(u))]

    neg = [jnp.full((1, s[1]), -jnp.inf, F32) for s in out_shapes]

    rhs, sc, cm = make_rhs(0), score_chunks(0), neg
    for c in range(n):
        cm, _ = score(sc[c], rhs, c, 0, cm)
    m0 = with_extra(cm, 0)
    rhs, sc, cm1 = make_rhs(1), score_chunks(1), neg
    for c in range(lead):
        cm1, _ = score(sc[c], rhs, c, 1, cm1)

    def half(u, parity, m_u, cm1):
        u1 = jnp.minimum(u + 1, last)
        u2 = jnp.minimum(u + 2, last)
        rhs1, sc1 = make_rhs(u1), score_chunks(u1)
        rhs2, sc2 = make_rhs(u2), score_chunks(u2)
        vc, ex_u = value_chunks(u), ex(u)
        cm2 = neg
        acc = [None] * n_groups
        parts = []
        for c in range(n):
            if c + lead < n:
                cm1, part = score(sc1[c + lead], rhs1, c + lead, 1 - parity, cm1)
            else:
                cm2, part = score(sc2[c + lead - n], rhs2, c + lead - n, parity, cm2)
            parts.append(part)
            vt, g = vc[c]()
            m_c = m_u[g]
            if c >= pace:
                m_c = jnp.maximum(m_c, jnp.minimum(parts[c - pace], m_c))
            buf, row = place(c, parity)
            rows = pl.ds(pl.multiple_of(row + zero, LANES), chunk_rows[c])
            p = jnp.exp2(buf[rows, :] - m_c).astype(BF16)
            vt_aug = jnp.concatenate([vt, jnp.ones((ONES_ROWS, vt.shape[1]), BF16)], axis=0)
            pv = jnp.dot(vt_aug, p, preferred_element_type=F32)
            acc[g] = pv if acc[g] is None else acc[g] + pv
        ots = []
        for g in range(n_groups):
            dv = acc[g].shape[0] - ONES_ROWS
            l = acc[g][dv:dv + 1]
            if ex_u[g] is not None:
                l = l + jnp.exp2(ex_u[g] - m_u[g])
            ots.append(acc[g][:dv] * (1.0 / l))
        return with_extra(cm1, u1), cm2, ots

    def body(i, carry):
        m_u, cm_next, ot = carry
        u0 = UNITS_PER_BODY * i
        finish(jnp.maximum(u0 - 1, 0), ot)
        for h in range(UNITS_PER_BODY):
            if h:
                finish(u0 + h - 1, ot)
            m_u, cm_next, ot = half(u0 + h, h % 2, m_u, cm_next)
        return m_u, cm_next, ot

    assert n_units % UNITS_PER_BODY == 0
    carry = (m0, cm1, [jnp.zeros(s, F32) for s in out_shapes])
    _, _, ot_last = lax.fori_loop(0, n_units // UNITS_PER_BODY, body, carry)
    finish(last, ot_last)


def _score_scratch(chunk_rows, n_cols, lead):
    main = pltpu.VMEM((sum(chunk_rows[lead:]), n_cols), F32)
    head = pltpu.VMEM((sum(chunk_rows[:lead]), n_cols), F32)
    return [main, main, head, head]


def _split_unit(u, n_tiles):
    if isinstance(u, int):
        return u // n_tiles, u % n_tiles
    return lax.div(u, n_tiles), lax.rem(u, n_tiles)


def _rows(g, size):
    return _tile(g, size)


def _ac_chunk_rows(t_all, tq):
    wk = tq + 2 * WINDOW
    return ([KCHUNK] * (t_all // KCHUNK) + [KCHUNK]
            + [min(lo + KCHUNK, wk) - lo for lo in range(0, wk, KCHUNK)])


def _attn_ac_kernel(zero_ref, sink_ref, bias_ref, qa_ref, qc_ref, ka_ref, kc_ref, vta_ref, vtc_ref, oa_ref, oc_ref,
                    *scratch, tq, n_lat, n_ctx, s_len):
    t_all = ka_ref.shape[0]
    wk = tq + 2 * WINDOW
    spans = [(lo, min(lo + KCHUNK, wk)) for lo in range(0, wk, KCHUNK)]

    def rhs_of(q_ref, j, t):
        return _gqa_rhs(q_ref[_rows(j, 4 * HD), _tile(t, tq)], j, tq)

    def store(o_ref, j, t, ot):
        _store_heads(o_ref.at[_tile(t, tq), _rows(j, 4 * HD)], ot, tq, 4)

    def window(t):
        q0 = t * tq
        start = jnp.clip(q0 - WINDOW, 0, s_len - wk)
        return start, (q0 - start) // WINDOW

    def local(start, lo, hi):
        return pl.ds(pl.multiple_of(start + lo, LANES), hi - lo)

    def score_chunks(u):
        _, t = _split_unit(u, n_lat)
        start, variant = window(t)

        def loc(lo, hi):
            b = bias_ref[variant, lo:hi, :]
            return kc_ref[local(start, lo, hi), :], jnp.concatenate([b] * 4, axis=1), 1

        return ([functools.partial(lambda c: (ka_ref[c * KCHUNK:(c + 1) * KCHUNK, :], None, 0), c)
                 for c in range(t_all // KCHUNK)]
                + [lambda: (kc_ref[s_len:s_len + KCHUNK, :], None, 1)]
                + [functools.partial(loc, lo, hi) for lo, hi in spans])

    def value_chunks(u):
        j, t = _split_unit(u, n_lat)
        start, _ = window(t)
        return ([functools.partial(lambda c: (vta_ref[_rows(j, HD), c * KCHUNK:(c + 1) * KCHUNK], 0), c)
                 for c in range(t_all // KCHUNK)]
                + [lambda: (vtc_ref[_rows(j, HD), s_len:s_len + KCHUNK], 1)]
                + [functools.partial(lambda lo, hi: (vtc_ref[_rows(j, HD), local(start, lo, hi)], 1), lo, hi)
                   for lo, hi in spans])

    def make_rhs(u):
        j, t = _split_unit(u, n_lat)
        return [rhs_of(qa_ref, j, t), rhs_of(qc_ref, j, t)]

    def extra(u):
        j, _ = _split_unit(u, n_lat)
        return [None, _sink_row(sink_ref, j * 4, tq)]

    def finish(u, ots):
        j, t = _split_unit(u, n_lat)
        store(oa_ref, j, t, ots[0])
        store(oc_ref, j, t, ots[1])

    ctx_tiles = range(n_lat, n_lat + n_ctx)
    n = 4 * tq
    problems, targets = [], []
    for j in range(2 if n_ctx else 0):
        for q_ref, k_ref, vt_ref, o_ref, sink in ((qa_ref, ka_ref, vta_ref, oa_ref, None),
                                                  (qc_ref, kc_ref, vtc_ref, oc_ref, _sink_row(sink_ref, j * 4, tq))):
            rhs = jnp.concatenate([rhs_of(q_ref, j, t) for t in ctx_tiles], axis=1)
            sink_cols = None if sink is None else jnp.concatenate([sink] * n_ctx, axis=1)
            problems.append((k_ref[s_len:, :], vt_ref[_rows(j, HD), s_len:], rhs, sink_cols))
            targets.append((o_ref, j))
    for (o_ref, j), ot in zip(targets, _small_softmax_pv(problems)):
        for i, t in enumerate(ctx_tiles):
            store(o_ref, j, t, ot[:, i * n:(i + 1) * n])

    _pipeline(make_rhs, score_chunks, value_chunks, finish, _ac_chunk_rows(t_all, tq), [(HD, 4 * tq)] * 2, scratch,
              2 * n_lat, zero_ref[0], extra=extra, **PIPE_AC)


def _attn_b_kernel(zero_ref, lamv_ref, subln_ref, qt_ref, k_ref, vt_ref, o_ref, *scratch, tq, n_lat, n_ctx, s_len,
                   lam_init):
    lv = lamv_ref[...]
    lam = (jnp.exp(jnp.sum(lv[0:1] * lv[1:2], axis=-1, keepdims=True))
           - jnp.exp(jnp.sum(lv[2:3] * lv[3:4], axis=-1, keepdims=True)) + lam_init)
    t_all = k_ref.shape[0]
    n_chunks = t_all // KCHUNK
    hw = 2 * HD

    def rhs_of(h, t):
        qt = qt_ref[_rows(h, hw), _tile(t, tq)]
        z = jnp.zeros((HD, tq), qt.dtype)
        return jnp.concatenate([jnp.concatenate([qt[:HD], z], axis=0),
                                jnp.concatenate([z, qt[HD:]], axis=0)], axis=1)

    def store(h, t, o2):
        o = o2[:, :tq] - lam * o2[:, tq:]
        ms = jnp.mean(o * o, axis=0, keepdims=True)
        o = o * lax.rsqrt(ms + SUBLN_EPS) * subln_ref[...] * (1.0 - lam_init)
        o_ref[_tile(t, tq), _rows(h, hw)] = o.T.astype(o_ref.dtype)

    def score_chunks(u):
        h, _ = _split_unit(u, n_lat)
        return [functools.partial(lambda c: (k_ref[c * KCHUNK:(c + 1) * KCHUNK, _rows(h, hw)], None, 0), c)
                for c in range(n_chunks)]

    def value_chunks(u):
        h, _ = _split_unit(u, n_lat)
        return [functools.partial(lambda c: (vt_ref[_rows(h, hw), c * KCHUNK:(c + 1) * KCHUNK], 0), c)
                for c in range(n_chunks)]

    units = [(h, t) for h in range(4) for t in range(n_lat, n_lat + n_ctx)]
    problems = [(k_ref[s_len:, _rows(h, hw)], vt_ref[_rows(h, hw), s_len:], rhs_of(h, t), None) for h, t in units]
    for (h, t), o2 in zip(units, _small_softmax_pv(problems)):
        store(h, t, o2)

    _pipeline(lambda u: [rhs_of(*_split_unit(u, n_lat))], score_chunks, value_chunks,
              lambda u, ots: store(*_split_unit(u, n_lat), ots[0]), [KCHUNK] * n_chunks, [(hw, 2 * tq)], scratch,
              4 * n_lat, zero_ref[0], **PIPE_B)


def _window_bias(tq):
    wk = tq + 2 * WINDOW
    r = np.arange(wk)[:, None]
    c = np.arange(tq)[None, :]
    out = np.stack([np.where(np.abs(c - r + v * WINDOW) <= WINDOW, 0.0, -np.inf) for v in range(3)])
    return jnp.asarray(out, F32)


def _attn_calls(qt_all, k_all, vt_all, sink, lamv, subln_t, layer, with_ctx, s_len, lam_init,
                tq_a, tq_b):
    nb, _, t_all = qt_all.shape
    c_len = t_all - s_len
    o_rows = t_all if with_ctx else s_len
    o_shape = jax.ShapeDtypeStruct((nb, o_rows, QW), BF16)

    def steps(tq):
        n_lat = s_len // tq
        return n_lat, n_lat + (c_len // tq if with_ctx else 0)

    whole = lambda rows, cols, r, c: pl.BlockSpec((None, rows, cols), lambda b: (b, r, c))
    o_spec = pl.BlockSpec((None, o_rows, QW), lambda b: (b, 0, 0))
    zero = jnp.zeros((1,), jnp.int32)

    n_lat, n_all = steps(tq_a)
    wk = tq_a + 2 * WINDOW
    oa, oc = pl.pallas_call(
        functools.partial(_attn_ac_kernel, tq=tq_a, n_lat=n_lat, n_ctx=n_all - n_lat, s_len=s_len),
        out_shape=(o_shape, o_shape),
        grid=(nb,),
        in_specs=[
            pl.BlockSpec(memory_space=pltpu.SMEM),
            pl.BlockSpec(memory_space=pltpu.SMEM),
            pl.BlockSpec((3, wk, tq_a), lambda b: (0, 0, 0)),
            whole(QW, t_all, 0, 0), whole(QW, t_all, 2, 0),
            whole(t_all, LANES, 0, KA_TILE), whole(t_all, LANES, 0, KA_TILE + 1),
            whole(LANES, t_all, KA_TILE, 0), whole(LANES, t_all, KA_TILE + 1, 0),
        ],
        out_specs=(o_spec, o_spec),
        scratch_shapes=_score_scratch(_ac_chunk_rows(t_all, tq_a), 4 * tq_a, PIPE_AC["lead"]),
        compiler_params=_cparams(1),
        name="attn_ac",
    )(zero, sink[layer], _window_bias(tq_a), qt_all, qt_all, k_all, k_all, vt_all, vt_all)

    n_lat, n_all = steps(tq_b)
    ob = pl.pallas_call(
        functools.partial(_attn_b_kernel, tq=tq_b, n_lat=n_lat, n_ctx=n_all - n_lat, s_len=s_len,
                          lam_init=lam_init),
        out_shape=o_shape,
        grid=(nb,),
        in_specs=[
            pl.BlockSpec(memory_space=pltpu.SMEM),
            pl.BlockSpec((None, 4, HD), lambda b: (layer, 0, 0)),
            pl.BlockSpec((None, 2 * HD, tq_b), lambda b: (layer, 0, 0)),
            whole(QW, t_all, 1, 0), whole(t_all, QW, 0, 0), whole(QW, t_all, 0, 0),
        ],
        out_specs=o_spec,
        scratch_shapes=_score_scratch([KCHUNK] * (t_all // KCHUNK), 2 * tq_b, PIPE_B["lead"]),
        compiler_params=_cparams(1),
        name="attn_b",
    )(zero, lamv, subln_t, qt_all, k_all, vt_all)

    return oa, ob, oc


def _merge_kernel(*refs, sub, n_sub, ctx_step):
    x_refs = refs[:n_sub]
    rest = refs[n_sub:]
    ctx_ref = None
    if ctx_step is not None:
        ctx_ref, rest = rest[0], rest[1:]
    (oa_ref, ob_ref, oc_ref, modb_ref, modc_ref, gpre_ref, gpost_ref, wgm_ref, bmg_ref, wbr_ref, wout_ref,
     out_ref) = rest

    def gates(i):
        x, is_ctx = _sub_tile_input(x_refs, ctx_ref, i, ctx_step)
        mod = jnp.where(is_ctx, modc_ref[...], modb_ref[...])
        h = _modulated_norm(x, mod, gpre_ref[...])
        gm = jnp.dot(h.astype(BF16), wgm_ref[...], preferred_element_type=F32)
        return x, mod, gm

    def finish(i, x, mod, gm):
        rows = slice(i * sub, (i + 1) * sub)
        z = None
        for j, o_ref in enumerate((oa_ref, ob_ref, oc_ref)):
            g = gm[:, j * QW:(j + 1) * QW]
            u = (o_ref[rows, :].astype(F32) * (g * _sigmoid(g))).astype(BF16)
            p = jnp.dot(u, wbr_ref[j], preferred_element_type=F32)
            mg = _sigmoid(gm[:, N_G + j * D:N_G + (j + 1) * D] + bmg_ref[:, j * D:(j + 1) * D])
            z = mg * p if z is None else z + mg * p
        y = jnp.dot(z.astype(BF16), wout_ref[...], preferred_element_type=F32)
        ms = jnp.mean(y * y, axis=-1, keepdims=True)
        gate = mod[:, 2 * D:]
        out_ref[rows, :] = x + gate * (y * lax.rsqrt(ms + EPS) * gpost_ref[...])

    prev = gates(0)
    for i in range(1, n_sub):
        cur = gates(i)
        finish(i - 1, *prev)
        prev = cur
    finish(n_sub - 1, *prev)


def _merge_call(x_src, ctx_src, oa, ob, oc, mod4, g_pre, g_post, wgm, b_mg, wbr, wout, layer, s_len, rows_out,
                tm, sub):
    nb = x_src.shape[0]
    n_sub = tm // sub
    tok_specs, tok_args = _token_specs(x_src, ctx_src, n_sub, sub, s_len)
    ctx_step = None if ctx_src is None else rows_out // tm - 1
    tok = lambda b, t: (b, t, 0)
    lay2 = lambda b, t: (layer, 0, 0)
    const = dict(pipeline_mode=pl.Buffered(1))
    return pl.pallas_call(
        functools.partial(_merge_kernel, sub=sub, n_sub=n_sub, ctx_step=ctx_step),
        out_shape=jax.ShapeDtypeStruct((nb, rows_out, D), F32),
        grid=(nb, rows_out // tm),
        in_specs=tok_specs + [
            pl.BlockSpec((None, tm, QW), tok),
            pl.BlockSpec((None, tm, QW), tok),
            pl.BlockSpec((None, tm, QW), tok),
            pl.BlockSpec((None, None, 1, 3 * D), lambda b, t: (layer, b, 0, 0)),
            pl.BlockSpec((None, None, 1, 3 * D), lambda b, t: (layer, nb, 0, 0)),
            pl.BlockSpec((None, 1, D), lay2),
            pl.BlockSpec((None, 1, D), lay2),
            pl.BlockSpec((None, D, N_G + N_M), lay2, **const),
            pl.BlockSpec((None, 1, N_M), lay2),
            pl.BlockSpec((None, 3, QW, D), lambda b, t: (layer, 0, 0, 0), **const),
            pl.BlockSpec((None, D, D), lay2, **const),
        ],
        out_specs=pl.BlockSpec((None, tm, D), tok),
        compiler_params=_cparams(2),
        name="merge",
    )(*tok_args, oa, ob, oc, mod4, mod4, g_pre, g_post, wgm, b_mg, wbr, wout)


def _rope_tables(s_len, c_len):
    rows = s_len // GRID_W
    row = np.repeat(np.arange(rows), GRID_W).astype(np.float32)
    col = np.tile(np.arange(GRID_W), rows).astype(np.float32)
    freqs = (np.float32(ROPE_THETA) ** (-np.arange(ROPE_PAIRS, dtype=np.float32) / ROPE_PAIRS)).astype(np.float32)
    ang_r = row[:, None] * freqs
    ang_c = col[:, None] * freqs
    ang = np.concatenate([ang_r, ang_r, ang_c, ang_c], axis=-1)
    cos = np.concatenate([np.cos(ang), np.ones((c_len, HD), np.float32)], axis=0).astype(np.float32)
    sin = np.concatenate([np.sin(ang), np.zeros((c_len, HD), np.float32)], axis=0).astype(np.float32)
    first = (np.arange(HD) % 32) < 16
    sin_a = np.where(first, -sin, np.float32(0.0))
    sin_b = np.where(first, np.float32(0.0), sin)
    tile2 = lambda a: np.concatenate([a, a], axis=-1)
    cost = np.ascontiguousarray(cos.T)
    sint = np.ascontiguousarray((sin_a + sin_b).T)
    return tuple(jnp.asarray(a, F32) for a in (cost, sint, tile2(cos), tile2(sin_a), tile2(sin_b)))


def _cols(w, names):
    return jnp.concatenate([w[..., _IN[n][0]:_IN[n][1]] for n in names], axis=-1)


def kernel(x, c, ctx, c_ctx, w_ada, b_ada, g_pre, g_post, w_in, q_norm, k_norm, lam_q1, lam_k1, lam_q2,
           lam_k2, subln, sink, w_br_a, w_br_b, w_br_c, w_mg, b_mg, w_out):
    nb, s_len, _ = x.shape
    c_len = ctx.shape[1]
    depth = w_in.shape[0]
    sub = 256
    tm_all = 3 * sub
    tm_lat = 4 * sub
    tq_a, tq_b = 128, 256
    assert (s_len + c_len) % tm_all == 0 and s_len % tm_lat == 0 and s_len % sub == 0
    assert c_len == KCHUNK == sub

    wp = _cols(w_in, ("qa", "qb", "qc", "kb", "ka", "kc", "vb", "va", "vc")).astype(BF16)
    wgm = jnp.concatenate([_cols(w_in, ("ga", "gb", "gc")), w_mg], axis=-1).astype(BF16)
    wbr = jnp.stack([w_br_a, w_br_b, w_br_c], axis=1).astype(BF16)
    wout = w_out.astype(BF16)

    tabs = _rope_tables(s_len, c_len)
    gq_t = jnp.broadcast_to(q_norm[:, :, None], (depth, HD, sub))
    gk_n = jnp.concatenate([k_norm, k_norm], axis=-1)[:, None, :]
    lamv = jnp.stack([lam_q1, lam_k1, lam_q2, lam_k2], axis=1)
    subln_t = jnp.broadcast_to(subln[:, :, None], (depth, 2 * HD, tq_b))
    g_pre3 = g_pre[:, None, :]
    g_post3 = g_post[:, None, :]
    b_mg3 = b_mg[:, None, :]

    rows = ((nb + 1 + 7) // 8) * 8
    sc_in = jnp.concatenate([c, c_ctx[None, :], jnp.zeros((rows - nb - 1, D), F32)], axis=0)
    mod = _ada_call(sc_in, w_ada, b_ada)
    mod4 = mod[:, :, None, :]

    t_all = s_len + c_len
    x_src, ctx_src = x, (ctx, 0)
    for layer in range(depth):
        last = layer == depth - 1
        lam_init = 0.8 - 0.6 * math.exp(-0.3 * layer)
        qt_all, k_all, vt_all = _proj_call(x_src, ctx_src, mod4, g_pre3, wp, tabs, gq_t, gk_n, layer, s_len,
                                           t_all, tm_all, sub)
        oa, ob, oc = _attn_calls(qt_all, k_all, vt_all, sink, lamv, subln_t, layer, not last, s_len,
                                 lam_init, tq_a, tq_b)
        rows_out, tm = (s_len, tm_lat) if last else (t_all, tm_all)
        xs = _merge_call(x_src, None if last else ctx_src, oa, ob, oc, mod4, g_pre3, g_post3, wgm, b_mg3, wbr,
                         wout, layer, s_len, rows_out, tm, sub)
        x_src, ctx_src = xs, (xs, s_len // sub)
    return xs
```

```python
import functools
import math

import jax
import jax.numpy as jnp
import numpy as np
from jax import lax
from jax.experimental import pallas as pl
from jax.experimental.pallas import tpu as pltpu

F32 = jnp.float32
BF16 = jnp.bfloat16

D = 1024
HD = 64
GRID_W = 64
WINDOW = 128
ROPE_THETA = 10000.0
ROPE_PAIRS = HD // 4
EPS = 1e-6
SUBLN_EPS = 1e-5
ATTN_SCALE = HD ** -0.5
LOG2E = math.log2(math.e)

QW = 512
N_Q = 3 * QW
N_K = 512 + 128 + 128
N_V = 512 + 128 + 128
KA_TILE = 4
N_P = N_Q + N_K + N_V
N_G = 3 * QW
N_M = 3 * D

LANES = 128
VMEM_LIMIT = 56 * 1024 * 1024

_IN = dict(qa=(0, 512), ka=(512, 640), va=(640, 768), ga=(768, 1280),
           qb=(1280, 1792), kb=(1792, 2304), vb=(2304, 2816), gb=(2816, 3328),
           qc=(3328, 3840), kc=(3840, 3968), vc=(3968, 4096), gc=(4096, 4608))


def _sigmoid(v):
    return 1.0 / (1.0 + jnp.exp(-v))


def _cparams(n_axes):
    return pltpu.CompilerParams(dimension_semantics=("arbitrary",) * n_axes,
                                vmem_limit_bytes=VMEM_LIMIT)


def _ada_kernel(sc_ref, w_ref, b_ref, o_ref):
    v = sc_ref[...]
    s = (v * _sigmoid(v)).astype(BF16)
    o_ref[...] = jnp.dot(s, w_ref[...].astype(BF16), preferred_element_type=F32) + b_ref[...]


def _ada_call(sc_in, w_ada, b_ada):
    depth = w_ada.shape[0]
    rows = sc_in.shape[0]
    nblk = 3
    return pl.pallas_call(
        _ada_kernel,
        out_shape=jax.ShapeDtypeStruct((depth, rows, 3 * D), F32),
        grid=(depth, nblk),
        in_specs=[
            pl.BlockSpec((rows, D), lambda l, n: (0, 0)),
            pl.BlockSpec((None, D, D), lambda l, n: (l, 0, n)),
            pl.BlockSpec((None, 1, D), lambda l, n: (l, 0, n)),
        ],
        out_specs=pl.BlockSpec((None, rows, D), lambda l, n: (l, 0, n)),
        compiler_params=_cparams(2),
        name="adaln",
    )(sc_in, w_ada, b_ada.reshape(depth, 1, 3 * D))


def _modulated_norm(x, mod, gpre):
    shift = mod[:, :D]
    scale = mod[:, D:2 * D]
    ms = jnp.mean(x * x, axis=-1, keepdims=True)
    return x * lax.rsqrt(ms + EPS) * gpre * (1.0 + scale) + shift


def _sub_tile_input(x_refs, ctx_ref, i, ctx_step):
    x = x_refs[i][...]
    if ctx_ref is None or i != len(x_refs) - 1:
        return x, False
    is_ctx = pl.program_id(1) == ctx_step
    return jnp.where(is_ctx, ctx_ref[...], x), is_ctx


def _proj_kernel(*refs, sub, n_sub, ctx_step):
    x_refs, ctx_ref = refs[:n_sub], refs[n_sub]
    (modb_ref, modc_ref, gpre_ref, w_ref, cost_ref, sint_ref, cosn_ref, sina_ref, sinb_ref, gq_ref, gk_ref,
     qt_ref, k_ref, vt_ref) = refs[n_sub + 1:]

    def project(i):
        x, is_ctx = _sub_tile_input(x_refs, ctx_ref, i, ctx_step)
        mod = jnp.where(is_ctx, modc_ref[...], modb_ref[...])
        h = _modulated_norm(x, mod, gpre_ref[...])
        return jnp.dot(h.astype(BF16), w_ref[...], preferred_element_type=F32)

    def finish(i, y):
        rows = slice(i * sub, (i + 1) * sub)
        for mixer in range(N_Q // QW):
            q3 = y[:, mixer * QW:(mixer + 1) * QW].T.reshape(QW // HD, HD, sub)
            if mixer == 0:
                ss = jnp.sum(q3 * q3, axis=1, keepdims=True)
                q3 = q3 * lax.rsqrt(ss * (1.0 / HD) + EPS) * gq_ref[...][None]
            rot = jnp.concatenate([q3[:, 16:32], q3[:, 0:16], q3[:, 48:64], q3[:, 32:48]], axis=1)
            q3 = (q3 * cost_ref[:, rows][None] + rot * sint_ref[:, rows][None]) * (ATTN_SCALE * LOG2E)
            qt_ref[mixer * QW:(mixer + 1) * QW, rows] = q3.reshape(QW, sub).astype(BF16)

        ka = y[:, N_Q + KA_TILE * LANES:N_Q + (KA_TILE + 1) * LANES]
        lane = lax.broadcasted_iota(jnp.int32, (1, LANES), 1)
        lo = lane < HD
        sq = ka * ka
        s_lo = jnp.sum(jnp.where(lo, sq, 0.0), axis=-1, keepdims=True)
        s_hi = jnp.sum(jnp.where(lo, 0.0, sq), axis=-1, keepdims=True)
        r = jnp.where(lo, lax.rsqrt(s_lo * (1.0 / HD) + EPS), lax.rsqrt(s_hi * (1.0 / HD) + EPS))
        ka = ka * r * gk_ref[...]
        cosn = cosn_ref[rows, :]
        sina = sina_ref[rows, :]
        sinb = sinb_ref[rows, :]
        for j in range(N_K // LANES):
            t = ka if j == KA_TILE else y[:, N_Q + j * LANES:N_Q + (j + 1) * LANES]
            t = t * cosn + pltpu.roll(t, LANES - 16, 1) * sina + pltpu.roll(t, 16, 1) * sinb
            k_ref[rows, j * LANES:(j + 1) * LANES] = t.astype(BF16)

        vt_ref[:, rows] = y[:, N_Q + N_K:].T.astype(BF16)

    y_prev = project(0)
    for i in range(1, n_sub):
        y = project(i)
        finish(i - 1, y_prev)
        y_prev = y
    finish(n_sub - 1, y_prev)


def _token_specs(x_src, ctx_src, n_sub, sub, s_len):
    last = s_len // sub - 1
    specs = [pl.BlockSpec((None, sub, D), functools.partial(lambda i, b, t: (b, jnp.minimum(n_sub * t + i, last), 0), i))
             for i in range(n_sub)]
    operands = [x_src] * n_sub
    if ctx_src is not None:
        arr, blk = ctx_src
        specs.append(pl.BlockSpec((None, sub, D), lambda b, t: (b, blk, 0)))
        operands.append(arr)
    return specs, operands


def _proj_call(x_src, ctx_src, mod4, g_pre, wp, tabs, gq_t, gk_n, layer, s_len, t_all, tm, sub):
    nb = x_src.shape[0]
    cost, sint, cosn, sina, sinb = tabs
    n_sub = tm // sub
    tok_specs, tok_args = _token_specs(x_src, ctx_src, n_sub, sub, s_len)
    return pl.pallas_call(
        functools.partial(_proj_kernel, sub=sub, n_sub=n_sub, ctx_step=t_all // tm - 1),
        out_shape=(jax.ShapeDtypeStruct((nb, N_Q, t_all), BF16),
                   jax.ShapeDtypeStruct((nb, t_all, N_K), BF16),
                   jax.ShapeDtypeStruct((nb, N_V, t_all), BF16)),
        grid=(nb, t_all // tm),
        in_specs=tok_specs + [
            pl.BlockSpec((None, None, 1, 3 * D), lambda b, t: (layer, b, 0, 0)),
            pl.BlockSpec((None, None, 1, 3 * D), lambda b, t: (layer, nb, 0, 0)),
            pl.BlockSpec((None, 1, D), lambda b, t: (layer, 0, 0)),
            pl.BlockSpec((None, D, N_P), lambda b, t: (layer, 0, 0)),
            pl.BlockSpec((HD, tm), lambda b, t: (0, t)),
            pl.BlockSpec((HD, tm), lambda b, t: (0, t)),
            pl.BlockSpec((tm, LANES), lambda b, t: (t, 0)),
            pl.BlockSpec((tm, LANES), lambda b, t: (t, 0)),
            pl.BlockSpec((tm, LANES), lambda b, t: (t, 0)),
            pl.BlockSpec((None, HD, sub), lambda b, t: (layer, 0, 0)),
            pl.BlockSpec((None, 1, LANES), lambda b, t: (layer, 0, 0)),
        ],
        out_specs=(pl.BlockSpec((None, N_Q, tm), lambda b, t: (b, 0, t)),
                   pl.BlockSpec((None, tm, N_K), lambda b, t: (b, t, 0)),
                   pl.BlockSpec((None, N_V, tm), lambda b, t: (b, 0, t))),
        compiler_params=_cparams(2),
        name="proj",
    )(*tok_args, mod4, mod4, g_pre, wp, cost, sint, cosn, sina, sinb, gq_t, gk_n)


KCHUNK = 256
ONES_ROWS = 16
UNITS_PER_BODY = 4
PIPE_AC = dict(lead=3, pace=6)
PIPE_B = dict(lead=4, pace=2)


def _small_softmax_pv(problems):
    ss = [jnp.dot(k, rhs, preferred_element_type=F32) for k, _, rhs, _ in problems]
    ms = [jnp.max(s, axis=0, keepdims=True) for s in ss]
    ms = [m if pr[3] is None else jnp.maximum(m, pr[3]) for m, pr in zip(ms, problems)]
    ps = [jnp.exp2(s - m).astype(BF16) for s, m in zip(ss, ms)]
    outs = []
    for p, m, (_, vt, _, extra) in zip(ps, ms, problems):
        dv = vt.shape[0]
        vt_aug = jnp.concatenate([vt, jnp.ones((ONES_ROWS, vt.shape[1]), BF16)], axis=0)
        pv = jnp.dot(vt_aug, p, preferred_element_type=F32)
        l = pv[dv:dv + 1]
        if extra is not None:
            l = l + jnp.exp2(extra - m)
        outs.append(pv[:dv] * (1.0 / l))
    return outs


def _gqa_rhs(qt, j, tq):
    z = jnp.zeros((HD, tq), qt.dtype)
    first = j == 0
    cols = []
    for g in range(4):
        qg = qt[g * HD:(g + 1) * HD, :]
        cols.append(jnp.concatenate([jnp.where(first, qg, z), jnp.where(first, z, qg)], axis=0))
    return jnp.concatenate(cols, axis=1)


def _store_heads(o_ref, ot, tq, nheads):
    o = jnp.concatenate([ot[:, g * tq:(g + 1) * tq] for g in range(nheads)], axis=0)
    o_ref[...] = o.T.astype(o_ref.dtype)


def _sink_row(sink_ref, base, tq):
    blk = lax.broadcasted_iota(jnp.int32, (1, 4 * tq), 1) // tq
    row = jnp.zeros((1, 4 * tq), F32)
    for g in range(4):
        row = jnp.where(blk == g, sink_ref[base + g], row)
    return row * LOG2E


def _tile(t, size):
    if isinstance(t, int):
        return pl.ds(t * size, size)
    return pl.ds(pl.multiple_of(t * size, size), size)


def _pipeline(make_rhs, score_chunks, value_chunks, finish, chunk_rows, out_shapes, scratch, n_units, zero,
              extra=None, *, lead, pace):
    main, head = scratch[:2], scratch[2:]
    n = len(chunk_rows)
    n_groups = len(out_shapes)
    last = n_units - 1
    assert n >= 2 * lead and n_units % 2 == 0
    ex = (lambda u: [None] * n_groups) if extra is None else extra

    def place(c, parity):
        if c < lead:
            return head[parity], sum(chunk_rows[:c])
        return main[parity], sum(chunk_rows[lead:c])

    def score(thunk, rhs, c, parity, cm):
        k, bias, g = thunk()
        s = jnp.dot(k, rhs[g], preferred_element_type=F32)
        if bias is not None:
            s = s + bias
        buf, row = place(c, parity)
        buf[row:row + chunk_rows[c], :] = s
        part = jnp.max(s, axis=0, keepdims=True)
        cm = list(cm)
        cm[g] = jnp.maximum(cm[g], part)
        return cm, part

    def with_extra(cm, u):
        return [m if e is None else jnp.maximum(m, e) for m, e in zip(cm, ex(u))]

    neg = [jnp.full((1, s[1]), -jnp.inf, F32) for s in out_shapes]

    rhs, sc, cm = make_rhs(0), score_chunks(0), neg
    for c in range(n):
        cm, _ = score(sc[c], rhs, c, 0, cm)
    m0 = with_extra(cm, 0)
    rhs, sc, cm1 = make_rhs(1), score_chunks(1), neg
    for c in range(lead):
        cm1, _ = score(sc[c], rhs, c, 1, cm1)

    def half(u, parity, m_u, cm1):
        u1 = jnp.minimum(u + 1, last)
        u2 = jnp.minimum(u + 2, last)
        rhs1, sc1 = make_rhs(u1), score_chunks(u1)
        rhs2, sc2 = make_rhs(u2), score_chunks(u2)
        vc, ex_u = value_chunks(u), ex(u)
        cm2 = neg
        acc = [None] * n_groups
        parts = []
        for c in range(n):
            if c + lead < n:
                cm1, part = score(sc1[c + lead], rhs1, c + lead, 1 - parity, cm1)
            else:
                cm2, part = score(sc2[c + lead - n], rhs2, c + lead - n, parity, cm2)
            parts.append(part)
            vt, g = vc[c]()
            m_c = m_u[g]
            if c >= pace:
                m_c = jnp.maximum(m_c, jnp.minimum(parts[c - pace], m_c))
            buf, row = place(c, parity)
            rows = pl.ds(pl.multiple_of(row + zero, LANES), chunk_rows[c])
            p = jnp.exp2(buf[rows, :] - m_c).astype(BF16)
            vt_aug = jnp.concatenate([vt, jnp.ones((ONES_ROWS, vt.shape[1]), BF16)], axis=0)
            pv = jnp.dot(vt_aug, p, preferred_element_type=F32)
            acc[g] = pv if acc[g] is None else acc[g] + pv
        ots = []
        for g in range(n_groups):
            dv = acc[g].shape[0] - ONES_ROWS
            l = acc[g][dv:dv + 1]
            if ex_u[g] is not None:
                l = l + jnp.exp2(ex_u[g] - m_u[g])
            ots.append(acc[g][:dv] * (1.0 / l))
        return with_extra(cm1, u1), cm2, ots

    def body(i, carry):
        m_u, cm_next, ot = carry
        u0 = UNITS_PER_BODY * i
        finish(jnp.maximum(u0 - 1, 0), ot)
        for h in range(UNITS_PER_BODY):
            if h:
                finish(u0 + h - 1, ot)
            m_u, cm_next, ot = half(u0 + h, h % 2, m_u, cm_next)
        return m_u, cm_next, ot

    assert n_units % UNITS_PER_BODY == 0
    carry = (m0, cm1, [jnp.zeros(s, F32) for s in out_shapes])
    _, _, ot_last = lax.fori_loop(0, n_units // UNITS_PER_BODY, body, carry)
    finish(last, ot_last)


def _score_scratch(chunk_rows, n_cols, lead):
    main = pltpu.VMEM((sum(chunk_rows[lead:]), n_cols), F32)
    head = pltpu.VMEM((sum(chunk_rows[:lead]), n_cols), F32)
    return [main, main, head, head]


def _split_unit(u, n_tiles):
    if isinstance(u, int):
        return u // n_tiles, u % n_tiles
    return lax.div(u, n_tiles), lax.rem(u, n_tiles)


def _rows(g, size):
    return _tile(g, size)


def _ac_chunk_rows(t_all, tq):
    wk = tq + 2 * WINDOW
    return ([KCHUNK] * (t_all // KCHUNK) + [KCHUNK]
            + [min(lo + KCHUNK, wk) - lo for lo in range(0, wk, KCHUNK)])


def _attn_ac_kernel(zero_ref, sink_ref, bias_ref, qa_ref, qc_ref, ka_ref, kc_ref, vta_ref, vtc_ref, oa_ref, oc_ref,
                    *scratch, tq, n_lat, n_ctx, s_len):
    t_all = ka_ref.shape[0]
    wk = tq + 2 * WINDOW
    spans = [(lo, min(lo + KCHUNK, wk)) for lo in range(0, wk, KCHUNK)]

    def rhs_of(q_ref, j, t):
        return _gqa_rhs(q_ref[_rows(j, 4 * HD), _tile(t, tq)], j, tq)

    def store(o_ref, j, t, ot):
        _store_heads(o_ref.at[_tile(t, tq), _rows(j, 4 * HD)], ot, tq, 4)

    def window(t):
        q0 = t * tq
        start = jnp.clip(q0 - WINDOW, 0, s_len - wk)
        return start, (q0 - start) // WINDOW

    def local(start, lo, hi):
        return pl.ds(pl.multiple_of(start + lo, LANES), hi - lo)

    def score_chunks(u):
        _, t = _split_unit(u, n_lat)
        start, variant = window(t)

        def loc(lo, hi):
            b = bias_ref[variant, lo:hi, :]
            return kc_ref[local(start, lo, hi), :], jnp.concatenate([b] * 4, axis=1), 1

        return ([functools.partial(lambda c: (ka_ref[c * KCHUNK:(c + 1) * KCHUNK, :], None, 0), c)
                 for c in range(t_all // KCHUNK)]
                + [lambda: (kc_ref[s_len:s_len + KCHUNK, :], None, 1)]
                + [functools.partial(loc, lo, hi) for lo, hi in spans])

    def value_chunks(u):
        j, t = _split_unit(u, n_lat)
        start, _ = window(t)
        return ([functools.partial(lambda c: (vta_ref[_rows(j, HD), c * KCHUNK:(c + 1) * KCHUNK], 0), c)
                 for c in range(t_all // KCHUNK)]
                + [lambda: (vtc_ref[_rows(j, HD), s_len:s_len + KCHUNK], 1)]
                + [functools.partial(lambda lo, hi: (vtc_ref[_rows(j, HD), local(start, lo, hi)], 1), lo, hi)
                   for lo, hi in spans])

    def make_rhs(u):
        j, t = _split_unit(u, n_lat)
        return [rhs_of(qa_ref, j, t), rhs_of(qc_ref, j, t)]

    def extra(u):
        j, _ = _split_unit(u, n_lat)
        return [None, _sink_row(sink_ref, j * 4, tq)]

    def finish(u, ots):
        j, t = _split_unit(u, n_lat)
        store(oa_ref, j, t, ots[0])
        store(oc_ref, j, t, ots[1])

    ctx_tiles = range(n_lat, n_lat + n_ctx)
    n = 4 * tq
    problems, targets = [], []
    for j in range(2 if n_ctx else 0):
        for q_ref, k_ref, vt_ref, o_ref, sink in ((qa_ref, ka_ref, vta_ref, oa_ref, None),
                                                  (qc_ref, kc_ref, vtc_ref, oc_ref, _sink_row(sink_ref, j * 4, tq))):
            rhs = jnp.concatenate([rhs_of(q_ref, j, t) for t in ctx_tiles], axis=1)
            sink_cols = None if sink is None else jnp.concatenate([sink] * n_ctx, axis=1)
            problems.append((k_ref[s_len:, :], vt_ref[_rows(j, HD), s_len:], rhs, sink_cols))
            targets.append((o_ref, j))
    for (o_ref, j), ot in zip(targets, _small_softmax_pv(problems)):
        for i, t in enumerate(ctx_tiles):
            store(o_ref, j, t, ot[:, i * n:(i + 1) * n])

    _pipeline(make_rhs, score_chunks, value_chunks, finish, _ac_chunk_rows(t_all, tq), [(HD, 4 * tq)] * 2, scratch,
              2 * n_lat, zero_ref[0], extra=extra, **PIPE_AC)


def _attn_b_kernel(zero_ref, lamv_ref, subln_ref, qt_ref, k_ref, vt_ref, o_ref, *scratch, tq, n_lat, n_ctx, s_len,
                   lam_init):
    lv = lamv_ref[...]
    lam = (jnp.exp(jnp.sum(lv[0:1] * lv[1:2], axis=-1, keepdims=True))
           - jnp.exp(jnp.sum(lv[2:3] * lv[3:4], axis=-1, keepdims=True)) + lam_init)
    t_all = k_ref.shape[0]
    n_chunks = t_all // KCHUNK
    hw = 2 * HD

    def rhs_of(h, t):
        qt = qt_ref[_rows(h, hw), _tile(t, tq)]
        z = jnp.zeros((HD, tq), qt.dtype)
        return jnp.concatenate([jnp.concatenate([qt[:HD], z], axis=0),
                                jnp.concatenate([z, qt[HD:]], axis=0)], axis=1)

    def store(h, t, o2):
        o = o2[:, :tq] - lam * o2[:, tq:]
        ms = jnp.mean(o * o, axis=0, keepdims=True)
        o = o * lax.rsqrt(ms + SUBLN_EPS) * subln_ref[...] * (1.0 - lam_init)
        o_ref[_tile(t, tq), _rows(h, hw)] = o.T.astype(o_ref.dtype)

    def score_chunks(u):
        h, _ = _split_unit(u, n_lat)
        return [functools.partial(lambda c: (k_ref[c * KCHUNK:(c + 1) * KCHUNK, _rows(h, hw)], None, 0), c)
                for c in range(n_chunks)]

    def value_chunks(u):
        h, _ = _split_unit(u, n_lat)
        return [functools.partial(lambda c: (vt_ref[_rows(h, hw), c * KCHUNK:(c + 1) * KCHUNK], 0), c)
                for c in range(n_chunks)]

    units = [(h, t) for h in range(4) for t in range(n_lat, n_lat + n_ctx)]
    problems = [(k_ref[s_len:, _rows(h, hw)], vt_ref[_rows(h, hw), s_len:], rhs_of(h, t), None) for h, t in units]
    for (h, t), o2 in zip(units, _small_softmax_pv(problems)):
        store(h, t, o2)

    _pipeline(lambda u: [rhs_of(*_split_unit(u, n_lat))], score_chunks, value_chunks,
              lambda u, ots: store(*_split_unit(u, n_lat), ots[0]), [KCHUNK] * n_chunks, [(hw, 2 * tq)], scratch,
              4 * n_lat, zero_ref[0], **PIPE_B)


def _window_bias(tq):
    wk = tq + 2 * WINDOW
    r = np.arange(wk)[:, None]
    c = np.arange(tq)[None, :]
    out = np.stack([np.where(np.abs(c - r + v * WINDOW) <= WINDOW, 0.0, -np.inf) for v in range(3)])
    return jnp.asarray(out, F32)


def _attn_calls(qt_all, k_all, vt_all, sink, lamv, subln_t, layer, with_ctx, s_len, lam_init,
                tq_a, tq_b):
    nb, _, t_all = qt_all.shape
    c_len = t_all - s_len
    o_rows = t_all if with_ctx else s_len
    o_shape = jax.ShapeDtypeStruct((nb, o_rows, QW), BF16)

    def steps(tq):
        n_lat = s_len // tq
        return n_lat, n_lat + (c_len // tq if with_ctx else 0)

    whole = lambda rows, cols, r, c: pl.BlockSpec((None, rows, cols), lambda b: (b, r, c))
    o_spec = pl.BlockSpec((None, o_rows, QW), lambda b: (b, 0, 0))
    zero = jnp.zeros((1,), jnp.int32)

    n_lat, n_all = steps(tq_a)
    wk = tq_a + 2 * WINDOW
    oa, oc = pl.pallas_call(
        functools.partial(_attn_ac_kernel, tq=tq_a, n_lat=n_lat, n_ctx=n_all - n_lat, s_len=s_len),
        out_shape=(o_shape, o_shape),
        grid=(nb,),
        in_specs=[
            pl.BlockSpec(memory_space=pltpu.SMEM),
            pl.BlockSpec(memory_space=pltpu.SMEM),
            pl.BlockSpec((3, wk, tq_a), lambda b: (0, 0, 0)),
            whole(QW, t_all, 0, 0), whole(QW, t_all, 2, 0),
            whole(t_all, LANES, 0, KA_TILE), whole(t_all, LANES, 0, KA_TILE + 1),
            whole(LANES, t_all, KA_TILE, 0), whole(LANES, t_all, KA_TILE + 1, 0),
        ],
        out_specs=(o_spec, o_spec),
        scratch_shapes=_score_scratch(_ac_chunk_rows(t_all, tq_a), 4 * tq_a, PIPE_AC["lead"]),
        compiler_params=_cparams(1),
        name="attn_ac",
    )(zero, sink[layer], _window_bias(tq_a), qt_all, qt_all, k_all, k_all, vt_all, vt_all)

    n_lat, n_all = steps(tq_b)
    ob = pl.pallas_call(
        functools.partial(_attn_b_kernel, tq=tq_b, n_lat=n_lat, n_ctx=n_all - n_lat, s_len=s_len,
                          lam_init=lam_init),
        out_shape=o_shape,
        grid=(nb,),
        in_specs=[
            pl.BlockSpec(memory_space=pltpu.SMEM),
            pl.BlockSpec((None, 4, HD), lambda b: (layer, 0, 0)),
            pl.BlockSpec((None, 2 * HD, tq_b), lambda b: (layer, 0, 0)),
            whole(QW, t_all, 1, 0), whole(t_all, QW, 0, 0), whole(QW, t_all, 0, 0),
        ],
        out_specs=o_spec,
        scratch_shapes=_score_scratch([KCHUNK] * (t_all // KCHUNK), 2 * tq_b, PIPE_B["lead"]),
        compiler_params=_cparams(1),
        name="attn_b",
    )(zero, lamv, subln_t, qt_all, k_all, vt_all)

    return oa, ob, oc


def _merge_kernel(*refs, sub, n_sub, ctx_step):
    x_refs = refs[:n_sub]
    rest = refs[n_sub:]
    ctx_ref = None
    if ctx_step is not None:
        ctx_ref, rest = rest[0], rest[1:]
    (oa_ref, ob_ref, oc_ref, modb_ref, modc_ref, gpre_ref, gpost_ref, wgm_ref, bmg_ref, wbr_ref, wout_ref,
     out_ref) = rest

    def gates(i):
        x, is_ctx = _sub_tile_input(x_refs, ctx_ref, i, ctx_step)
        mod = jnp.where(is_ctx, modc_ref[...], modb_ref[...])
        h = _modulated_norm(x, mod, gpre_ref[...])
        gm = jnp.dot(h.astype(BF16), wgm_ref[...], preferred_element_type=F32)
        return x, mod, gm

    def finish(i, x, mod, gm):
        rows = slice(i * sub, (i + 1) * sub)
        z = None
        for j, o_ref in enumerate((oa_ref, ob_ref, oc_ref)):
            g = gm[:, j * QW:(j + 1) * QW]
            u = (o_ref[rows, :].astype(F32) * (g * _sigmoid(g))).astype(BF16)
            p = jnp.dot(u, wbr_ref[j], preferred_element_type=F32)
            mg = _sigmoid(gm[:, N_G + j * D:N_G + (j + 1) * D] + bmg_ref[:, j * D:(j + 1) * D])
            z = mg * p if z is None else z + mg * p
        y = jnp.dot(z.astype(BF16), wout_ref[...], preferred_element_type=F32)
        ms = jnp.mean(y * y, axis=-1, keepdims=True)
        gate = mod[:, 2 * D:]
        out_ref[rows, :] = x + gate * (y * lax.rsqrt(ms + EPS) * gpost_ref[...])

    prev = gates(0)
    for i in range(1, n_sub):
        cur = gates(i)
        finish(i - 1, *prev)
        prev = cur
    finish(n_sub - 1, *prev)


def _merge_call(x_src, ctx_src, oa, ob, oc, mod4, g_pre, g_post, wgm, b_mg, wbr, wout, layer, s_len, rows_out,
                tm, sub):
    nb = x_src.shape[0]
    n_sub = tm // sub
    tok_specs, tok_args = _token_specs(x_src, ctx_src, n_sub, sub, s_len)
    ctx_step = None if ctx_src is None else rows_out // tm - 1
    tok = lambda b, t: (b, t, 0)
    lay2 = lambda b, t: (layer, 0, 0)
    const = dict(pipeline_mode=pl.Buffered(1))
    return pl.pallas_call(
        functools.partial(_merge_kernel, sub=sub, n_sub=n_sub, ctx_step=ctx_step),
        out_shape=jax.ShapeDtypeStruct((nb, rows_out, D), F32),
        grid=(nb, rows_out // tm),
        in_specs=tok_specs + [
            pl.BlockSpec((None, tm, QW), tok),
            pl.BlockSpec((None, tm, QW), tok),
            pl.BlockSpec((None, tm, QW), tok),
            pl.BlockSpec((None, None, 1, 3 * D), lambda b, t: (layer, b, 0, 0)),
            pl.BlockSpec((None, None, 1, 3 * D), lambda b, t: (layer, nb, 0, 0)),
            pl.BlockSpec((None, 1, D), lay2),
            pl.BlockSpec((None, 1, D), lay2),
            pl.BlockSpec((None, D, N_G + N_M), lay2, **const),
            pl.BlockSpec((None, 1, N_M), lay2),
            pl.BlockSpec((None, 3, QW, D), lambda b, t: (layer, 0, 0, 0), **const),
            pl.BlockSpec((None, D, D), lay2, **const),
        ],
        out_specs=pl.BlockSpec((None, tm, D), tok),
        compiler_params=_cparams(2),
        name="merge",
    )(*tok_args, oa, ob, oc, mod4, mod4, g_pre, g_post, wgm, b_mg, wbr, wout)


def _rope_tables(s_len, c_len):
    rows = s_len // GRID_W
    row = np.repeat(np.arange(rows), GRID_W).astype(np.float32)
    col = np.tile(np.arange(GRID_W), rows).astype(np.float32)
    freqs = (np.float32(ROPE_THETA) ** (-np.arange(ROPE_PAIRS, dtype=np.float32) / ROPE_PAIRS)).astype(np.float32)
    ang_r = row[:, None] * freqs
    ang_c = col[:, None] * freqs
    ang = np.concatenate([ang_r, ang_r, ang_c, ang_c], axis=-1)
    cos = np.concatenate([np.cos(ang), np.ones((c_len, HD), np.float32)], axis=0).astype(np.float32)
    sin = np.concatenate([np.sin(ang), np.zeros((c_len, HD), np.float32)], axis=0).astype(np.float32)
    first = (np.arange(HD) % 32) < 16
    sin_a = np.where(first, -sin, np.float32(0.0))
    sin_b = np.where(first, np.float32(0.0), sin)
    tile2 = lambda a: np.concatenate([a, a], axis=-1)
    cost = np.ascontiguousarray(cos.T)
    sint = np.ascontiguousarray((sin_a + sin_b).T)
    return tuple(jnp.asarray(a, F32) for a in (cost, sint, tile2(cos), tile2(sin_a), tile2(sin_b)))


def _cols(w, names):
    return jnp.concatenate([w[..., _IN[n][0]:_IN[n][1]] for n in names], axis=-1)


def kernel(x, c, ctx, c_ctx, w_ada, b_ada, g_pre, g_post, w_in, q_norm, k_norm, lam_q1, lam_k1, lam_q2,
           lam_k2, subln, sink, w_br_a, w_br_b, w_br_c, w_mg, b_mg, w_out):
    nb, s_len, _ = x.shape
    c_len = ctx.shape[1]
    depth = w_in.shape[0]
    sub = 256
    tm_all = 3 * sub
    tm_lat = 4 * sub
    tq_a, tq_b = 128, 256
    assert (s_len + c_len) % tm_all == 0 and s_len % tm_lat == 0 and s_len % sub == 0
    assert c_len == KCHUNK == sub

    wp = _cols(w_in, ("qa", "qb", "qc", "kb", "ka", "kc", "vb", "va", "vc")).astype(BF16)
    wgm = jnp.concatenate([_cols(w_in, ("ga", "gb", "gc")), w_mg], axis=-1).astype(BF16)
    wbr = jnp.stack([w_br_a, w_br_b, w_br_c], axis=1).astype(BF16)
    wout = w_out.astype(BF16)

    tabs = _rope_tables(s_len, c_len)
    gq_t = jnp.broadcast_to(q_norm[:, :, None], (depth, HD, sub))
    gk_n = jnp.concatenate([k_norm, k_norm], axis=-1)[:, None, :]
    lamv = jnp.stack([lam_q1, lam_k1, lam_q2, lam_k2], axis=1)
    subln_t = jnp.broadcast_to(subln[:, :, None], (depth, 2 * HD, tq_b))
    g_pre3 = g_pre[:, None, :]
    g_post3 = g_post[:, None, :]
    b_mg3 = b_mg[:, None, :]

    rows = ((nb + 1 + 7) // 8) * 8
    sc_in = jnp.concatenate([c, c_ctx[None, :], jnp.zeros((rows - nb - 1, D), F32)], axis=0)
    mod = _ada_call(sc_in, w_ada, b_ada)
    mod4 = mod[:, :, None, :]

    t_all = s_len + c_len
    x_src, ctx_src = x, (ctx, 0)
    for layer in range(depth):
        last = layer == depth - 1
        lam_init = 0.8 - 0.6 * math.exp(-0.3 * layer)
        qt_all, k_all, vt_all = _proj_call(x_src, ctx_src, mod4, g_pre3, wp, tabs, gq_t, gk_n, layer, s_len,
                                           t_all, tm_all, sub)
        oa, ob, oc = _attn_calls(qt_all, k_all, vt_all, sink, lamv, subln_t, layer, not last, s_len,
                                 lam_init, tq_a, tq_b)
        rows_out, tm = (s_len, tm_lat) if last else (t_all, tm_all)
        xs = _merge_call(x_src, None if last else ctx_src, oa, ob, oc, mod4, g_pre3, g_post3, wgm, b_mg3, wbr,
                         wout, layer, s_len, rows_out, tm, sub)
        x_src, ctx_src = xs, (xs, s_len // sub)
    return xs
```
